```python
import math, functools
import jax, jax.numpy as jnp
from jax import lax
import numpy as np

D_MODEL = 1024
BATCH = 8
SEQ = 2048
DEPTH = 1
DEC_BATCH = 16
DEC_SEQ = 64
PAST_LEN = 2048

CHUNK = 64
A_HEADS = 8
A_HEAD_DIM = 64
A_PAST_CHUNKS = 8
A_BAND = A_PAST_CHUNKS * CHUNK
A_REL_CLIP = 128
B_HEADS = 8
B_HEAD_DIM = 64
Q_BLOCK = 128
D_FF = 2816
CONV_W = 3
EPS = 1e-6
A_WIDTH = A_HEADS * A_HEAD_DIM
B_WIDTH = B_HEADS * B_HEAD_DIM
IN_SIZES = [A_WIDTH, A_WIDTH, A_WIDTH, B_WIDTH, B_WIDTH, B_WIDTH, B_HEADS, D_MODEL, D_MODEL]
IN_COLS = sum(IN_SIZES)
IN_SPLITS = [int(o) for o in np.cumsum(IN_SIZES)[:-1]]

kernel_name = 'streaming_band_fox_hybrid_step'


def rmsnorm(x, g):
    xf = x.astype(jnp.float32)
    y = xf * lax.rsqrt(jnp.mean(xf * xf, axis=-1, keepdims=True) + EPS) * g.astype(jnp.float32)
    return y.astype(x.dtype)


def split_in_proj(z, b_f):
    lead = z.shape[:-1]
    parts = jnp.split(z, IN_SPLITS, axis=-1)
    q_a, k_a, v_a = [p.reshape(lead + (A_HEADS, A_HEAD_DIM)) for p in parts[0:3]]
    q_b, k_b, v_b = [p.reshape(lead + (B_HEADS, B_HEAD_DIM)) for p in parts[3:6]]
    log_f = jax.nn.log_sigmoid((parts[6] + b_f).astype(jnp.float32))
    return q_a, k_a, v_a, q_b, k_b, v_b, log_f, parts[7], parts[8]


def rel_bias(rel_table, dist):
    idx = jnp.clip(dist, -A_REL_CLIP, A_REL_CLIP) + A_REL_CLIP
    return rel_table[:, idx].astype(jnp.float32)


def band_attn_prompt(q, k, v, rel_table):
    b, s, h, d = q.shape
    nc = s // CHUNK
    band = (A_PAST_CHUNKS + 1) * CHUNK
    pad = jnp.zeros((b, A_BAND, h, d), k.dtype)
    kp = jnp.concatenate([pad, k], axis=1).reshape(b, nc + A_PAST_CHUNKS, CHUNK, h, d)
    vp = jnp.concatenate([pad.astype(v.dtype), v], axis=1).reshape(b, nc + A_PAST_CHUNKS, CHUNK, h, d)
    kb = jnp.concatenate([kp[:, j:j + nc] for j in range(A_PAST_CHUNKS + 1)], axis=2)
    vb = jnp.concatenate([vp[:, j:j + nc] for j in range(A_PAST_CHUNKS + 1)], axis=2)
    qc = q.reshape(b, nc, CHUNK, h, d)
    scores = jnp.einsum('bcqhd,bckhd->bhcqk', qc, kb).astype(jnp.float32) * (d ** -0.5)
    koff = jnp.arange(band) - A_BAND
    dist = jnp.arange(CHUNK)[:, None] - koff[None, :]
    bias = rel_bias(rel_table, dist)
    kpos = jnp.arange(nc)[:, None] * CHUNK + koff[None, :]
    scores = jnp.where((kpos >= 0)[None, None, :, None, :], scores + bias[None, :, None], -jnp.inf)
    p = jax.nn.softmax(scores, axis=-1)
    out = jnp.einsum('bhcqk,bckhd->bcqhd', p.astype(v.dtype), vb)
    return out.reshape(b, s, h * d)


def band_attn_sample(q, k, v, k_cache, v_cache, rel_table):
    b, t, h, d = q.shape
    L = k_cache.shape[1]
    kk = jnp.concatenate([k_cache.astype(k.dtype), k], axis=1)
    vv = jnp.concatenate([v_cache.astype(v.dtype), v], axis=1)
    scores = jnp.einsum('bqhd,bkhd->bhqk', q, kk).astype(jnp.float32) * (d ** -0.5)
    kpos = jnp.concatenate([jnp.arange(-L, 0), jnp.arange(t)])
    dist = jnp.arange(t)[:, None] - kpos[None, :]
    p = jax.nn.softmax(scores + rel_bias(rel_table, dist)[None], axis=-1)
    out = jnp.einsum('bhqk,bkhd->bqhd', p.astype(v.dtype), vv)
    return out.reshape(b, t, h * d)


def forget_attn_prompt(q, k, v, log_f):
    b, s, h, d = q.shape
    nb = s // Q_BLOCK
    c = jnp.cumsum(log_f, axis=1).transpose(0, 2, 1)
    qb = q.reshape(b, nb, Q_BLOCK, h, d).transpose(1, 0, 2, 3, 4)
    cb = c.reshape(b, h, nb, Q_BLOCK).transpose(2, 0, 1, 3)
    kpos = jnp.arange(s)

    def one_block(args):
        qi, ci, i = args
        sc = jnp.einsum('bqhd,bkhd->bhqk', qi, k).astype(jnp.float32) * (d ** -0.5)
        sc = sc + ci[..., None] - c[:, :, None, :]
        qpos = i * Q_BLOCK + jnp.arange(Q_BLOCK)
        sc = jnp.where((kpos[None, :] <= qpos[:, None])[None, None], sc, -jnp.inf)
        p = jax.nn.softmax(sc, axis=-1)
        return jnp.einsum('bhqk,bkhd->bqhd', p.astype(v.dtype), v)

    out = lax.map(one_block, (qb, cb, jnp.arange(nb)))
    return out.transpose(1, 0, 2, 3, 4).reshape(b, s, h * d)


def forget_attn_sample(q, k, v, log_f, k_cache, v_cache, logf_cache):
    b, t, h, d = q.shape
    L = k_cache.shape[1]
    kk = jnp.concatenate([k_cache.astype(k.dtype), k], axis=1)
    vv = jnp.concatenate([v_cache.astype(v.dtype), v], axis=1)
    c = jnp.cumsum(jnp.concatenate([logf_cache.astype(jnp.float32), log_f], axis=1), axis=1).transpose(0, 2, 1)
    sc = jnp.einsum('bqhd,bkhd->bhqk', q, kk).astype(jnp.float32) * (d ** -0.5)
    sc = sc + c[:, :, L:, None] - c[:, :, None, :]
    kpos = jnp.arange(L + t)
    qpos = L + jnp.arange(t)
    sc = jnp.where((kpos[None, :] <= qpos[:, None])[None, None], sc, -jnp.inf)
    p = jax.nn.softmax(sc, axis=-1)
    out = jnp.einsum('bhqk,bkhd->bqhd', p.astype(v.dtype), vv)
    return out.reshape(b, t, h * d)


def conv_ffn(h, buf, w_up, conv_w, conv_b, w_down):
    t = h.shape[1]
    u = h @ w_up
    ext = jnp.concatenate([buf.astype(u.dtype), u], axis=1)
    y = conv_b
    for j in range(CONV_W):
        y = y + ext[:, j:j + t] * conv_w[j]
    gate, val = jnp.split(y, 2, axis=-1)
    return (jax.nn.gelu(gate) * val) @ w_down, ext[:, -(CONV_W - 1):]


def layer(x, attn_a, attn_b, conv_buf, g_pre_mix, g_post_mix, g_pre_ffn, g_post_ffn,
          w_in, b_f, w_proj_a, w_proj_b, w_out, w_up, conv_w, conv_b, w_down):
    h = rmsnorm(x, g_pre_mix)
    q_a, k_a, v_a, q_b, k_b, v_b, log_f, g_a, g_b = split_in_proj(h @ w_in, b_f)
    o_a = attn_a(q_a, k_a, v_a)
    o_b = attn_b(q_b, k_b, v_b, log_f)
    merged = jax.nn.sigmoid(g_a) * (o_a @ w_proj_a) + jax.nn.sigmoid(g_b) * (o_b @ w_proj_b)
    x = x + rmsnorm(merged @ w_out, g_post_mix)
    f, new_buf = conv_ffn(rmsnorm(x, g_pre_ffn), conv_buf, w_up, conv_w, conv_b, w_down)
    x = x + rmsnorm(f, g_post_ffn)
    return x, (k_a, v_a, k_b, v_b, log_f, new_buf)


def setup_inputs(seed: int = 0) -> dict:
    key = jax.random.key(seed)
    ks = jax.random.split(key, 24)
    nrm = jax.random.normal
    a_len = min(A_BAND, PAST_LEN)
    f32 = jnp.float32
    return {
        'x_prompt': nrm(ks[0], (BATCH, SEQ, D_MODEL), f32),
        'x_sample': nrm(ks[1], (DEC_BATCH, DEC_SEQ, D_MODEL), f32),
        'cache_k_a': nrm(ks[2], (DEPTH, DEC_BATCH, a_len, A_HEADS, A_HEAD_DIM), f32),
        'cache_v_a': nrm(ks[3], (DEPTH, DEC_BATCH, a_len, A_HEADS, A_HEAD_DIM), f32),
        'cache_k_b': nrm(ks[4], (DEPTH, DEC_BATCH, PAST_LEN, B_HEADS, B_HEAD_DIM), f32),
        'cache_v_b': nrm(ks[5], (DEPTH, DEC_BATCH, PAST_LEN, B_HEADS, B_HEAD_DIM), f32),
        'cache_logf_b': jax.nn.log_sigmoid(2.0 + nrm(ks[6], (DEPTH, DEC_BATCH, PAST_LEN, B_HEADS), f32)),
        'state_conv_ffn': nrm(ks[7], (DEPTH, DEC_BATCH, CONV_W - 1, 2 * D_FF), f32),
        'g_pre_mix': 1.0 + 0.05 * nrm(ks[8], (DEPTH, D_MODEL), f32),
        'g_post_mix': 1.0 + 0.05 * nrm(ks[9], (DEPTH, D_MODEL), f32),
        'g_pre_ffn': 1.0 + 0.05 * nrm(ks[10], (DEPTH, D_MODEL), f32),
        'g_post_ffn': 1.0 + 0.05 * nrm(ks[11], (DEPTH, D_MODEL), f32),
        'w_in': nrm(ks[12], (DEPTH, D_MODEL, IN_COLS), f32) * D_MODEL ** -0.5,
        'b_f': 2.0 + 0.5 * nrm(ks[13], (DEPTH, B_HEADS), f32),
        'rel_table': 0.5 * nrm(ks[14], (DEPTH, A_HEADS, 2 * A_REL_CLIP + 1), f32),
        'w_proj_a': nrm(ks[15], (DEPTH, A_WIDTH, D_MODEL), f32) * A_WIDTH ** -0.5,
        'w_proj_b': nrm(ks[16], (DEPTH, B_WIDTH, D_MODEL), f32) * B_WIDTH ** -0.5,
        'w_out': nrm(ks[17], (DEPTH, D_MODEL, D_MODEL), f32) * D_MODEL ** -0.5,
        'w_up': nrm(ks[18], (DEPTH, D_MODEL, 2 * D_FF), f32) * D_MODEL ** -0.5,
        'conv_w': nrm(ks[19], (DEPTH, CONV_W, 2 * D_FF), f32) * CONV_W ** -0.5,
        'conv_b': 0.02 * nrm(ks[20], (DEPTH, 2 * D_FF), f32),
        'w_down': nrm(ks[21], (DEPTH, D_FF, D_MODEL), f32) * D_FF ** -0.5,
    }


def reference(x_prompt, x_sample, cache_k_a, cache_v_a, cache_k_b, cache_v_b, cache_logf_b,
              state_conv_ffn, g_pre_mix, g_post_mix, g_pre_ffn, g_post_ffn, w_in, b_f, rel_table,
              w_proj_a, w_proj_b, w_out, w_up, conv_w, conv_b, w_down):
    x_p, x_s = x_prompt, x_sample
    p_states, s_states = [], []
    for l in range(DEPTH):
        w = (g_pre_mix[l], g_post_mix[l], g_pre_ffn[l], g_post_ffn[l], w_in[l], b_f[l],
             w_proj_a[l], w_proj_b[l], w_out[l], w_up[l], conv_w[l], conv_b[l], w_down[l])
        buf0 = jnp.zeros((x_p.shape[0], CONV_W - 1, 2 * D_FF), x_p.dtype)
        x_p, (ka, va, kb, vb, lf, cv) = layer(
            x_p, functools.partial(band_attn_prompt, rel_table=rel_table[l]),
            forget_attn_prompt, buf0, *w)
        keep = min(A_BAND, x_p.shape[1])
        p_states.append((ka[:, -keep:], va[:, -keep:], kb, vb, lf, cv))
        x_s, st = layer(
            x_s,
            functools.partial(band_attn_sample, k_cache=cache_k_a[l], v_cache=cache_v_a[l],
                              rel_table=rel_table[l]),
            functools.partial(forget_attn_sample, k_cache=cache_k_b[l], v_cache=cache_v_b[l],
                              logf_cache=cache_logf_b[l]),
            state_conv_ffn[l], *w)
        s_states.append(st)
    ka_p, va_p, kb_p, vb_p, lf_p, cv_p = [jnp.stack(s) for s in zip(*p_states)]
    ka_s, va_s, kb_s, vb_s, lf_s, cv_s = [jnp.stack(s) for s in zip(*s_states)]
    return (x_p, x_s, ka_p, va_p, kb_p, vb_p, lf_p, cv_p, ka_s, va_s, kb_s, vb_s, lf_s, cv_s)
```

```python
import functools

import jax
import jax.numpy as jnp
from jax import lax
from jax.experimental import pallas as pl
from jax.experimental.pallas import tpu as pltpu

F32, BF16 = jnp.float32, jnp.bfloat16

HEADS = 8
HEAD_DIM = 64
WIDTH = HEADS * HEAD_DIM
CHUNK = 64
PAST_CHUNKS = 8
BAND = PAST_CHUNKS * CHUNK
REL_CLIP = 128
REL_SIZE = 2 * REL_CLIP + 1
CONV_W = 3
EPS = 1e-6
NEG = -1e30

ATT_BLK = 256
BAND_KEYS = 3 * ATT_BLK
ROW_TILE = 512
FFN_COLS = 256
CUMSUM_BLK = 256
VMEM_LIMIT = 48 * 1024 * 1024


def _params(*sem):
    return pltpu.CompilerParams(dimension_semantics=sem, vmem_limit_bytes=VMEM_LIMIT)


def _resident(shape, index_map):
    return pl.BlockSpec(shape, index_map, pipeline_mode=pl.Buffered(1))


def _rmsnorm(x, g):
    return x * lax.rsqrt(jnp.mean(x * x, axis=-1, keepdims=True) + EPS) * g


def _dot(a, b):
    return jnp.dot(a, b, preferred_element_type=F32)


def _dot_nt(a, b):
    return lax.dot_general(a, b, (((1,), (1,)), ((), ())), preferred_element_type=F32)


def _dot_tn(a, b):
    return lax.dot_general(a, b, (((0,), (0,)), ((), ())), preferred_element_type=F32)


def _split3(x):
    hi = x.astype(BF16)
    r = x - hi.astype(F32)
    mid = r.astype(BF16)
    lo = (r - mid.astype(F32)).astype(BF16)
    return hi, mid, lo


def _inproj_kernel(x_ref, g_ref, w_ref, wf_ref, bf_ref,
                   qa_ref, ka_ref, va_ref, qb_ref, kb_ref, vb_ref,
                   kab_ref, vab_ref, kbb_ref, vbb_ref, lf_ref):
    h = _rmsnorm(x_ref[...], g_ref[...]).astype(BF16)
    scale = HEAD_DIM ** -0.5

    def proj(c):
        return _dot(h, w_ref[:, c * WIDTH:(c + 1) * WIDTH])

    qa_ref[...] = (proj(0) * scale).astype(BF16)
    qb_ref[...] = (proj(3) * scale).astype(BF16)
    for c, full_ref, half_ref in ((1, ka_ref, kab_ref), (2, va_ref, vab_ref),
                                  (4, kb_ref, kbb_ref), (5, vb_ref, vbb_ref)):
        z = proj(c)
        full_ref[...] = z
        half_ref[...] = z.astype(BF16)
    lf_ref[...] = jax.nn.log_sigmoid(_dot(h, wf_ref[...]) + bf_ref[...])


def _inproj(x, g, w_qkv, w_f, b_f):
    n, d = x.shape
    tm = min(ROW_TILE, n)
    row = lambda i: (i, 0)
    fixed = lambda i: (0, 0)
    wide = lambda dt: jax.ShapeDtypeStruct((n, WIDTH), dt)
    blk = pl.BlockSpec((tm, WIDTH), row)
    out_dtypes = (BF16, F32, F32, BF16, F32, F32, BF16, BF16, BF16, BF16)
    return pl.pallas_call(
        _inproj_kernel,
        grid=(n // tm,),
        in_specs=[pl.BlockSpec((tm, d), row),
                  pl.BlockSpec((1, d), fixed),
                  _resident(w_qkv.shape, fixed),
                  pl.BlockSpec(w_f.shape, fixed),
                  pl.BlockSpec((1, HEADS), fixed)],
        out_specs=[blk] * 10 + [pl.BlockSpec((tm, HEADS), row)],
        out_shape=[wide(dt) for dt in out_dtypes] + [jax.ShapeDtypeStruct((n, HEADS), F32)],
        compiler_params=_params("parallel"),
        name="inproj",
    )(x, g, w_qkv, w_f, b_f)


def _cumsum_kernel(*refs, seg_lens):
    seg_refs = refs[:len(seg_lens)]
    ccol_ref, crow_ref = refs[len(seg_lens):]
    carry_c = jnp.zeros((1, HEADS), F32)
    carry_r = jnp.zeros((HEADS, 1), F32)
    off = 0
    for ref, n in zip(seg_refs, seg_lens):
        for o in range(0, n, CUMSUM_BLK):
            b = min(CUMSUM_BLK, n - o)
            parts = _split3(ref[0, o:o + b, :])
            r = lax.broadcasted_iota(jnp.int32, (b, b), 0)
            c = lax.broadcasted_iota(jnp.int32, (b, b), 1)
            lower = jnp.where(r >= c, 1.0, 0.0).astype(BF16)
            upper = jnp.where(r <= c, 1.0, 0.0).astype(BF16)
            cc = carry_c
            cr = carry_r
            for p in parts:
                cc = cc + _dot(lower, p)
                cr = cr + _dot_tn(p, upper)
            ccol_ref[0, off + o:off + o + b, :] = cc
            crow_ref[0, :, off + o:off + o + b] = cr
            carry_c = cc[b - 1:b, :]
            carry_r = cr[:, b - 1:b]
        off += n


def _cumsum(*segs):
    bsz = segs[0].shape[0]
    seg_lens = tuple(s.shape[1] for s in segs)
    total = sum(seg_lens)
    return pl.pallas_call(
        functools.partial(_cumsum_kernel, seg_lens=seg_lens),
        grid=(bsz,),
        in_specs=[pl.BlockSpec((1, n, HEADS), lambda b: (b, 0, 0)) for n in seg_lens],
        out_specs=[pl.BlockSpec((1, total, HEADS), lambda b: (b, 0, 0)),
                   pl.BlockSpec((1, HEADS, total), lambda b: (b, 0, 0))],
        out_shape=[jax.ShapeDtypeStruct((bsz, total, HEADS), F32),
                   jax.ShapeDtypeStruct((bsz, HEADS, total), F32)],
        compiler_params=_params("parallel"),
        name="cumsum_logf",
    )(*segs)


REL_PAD = 384
DIST_SPAN = 1024


def _band_bias_kernel(tbl_ref, o_ref):
    j = lax.broadcasted_iota(jnp.int32, (REL_PAD, DIST_SPAN), 1)
    r = lax.broadcasted_iota(jnp.int32, (REL_PAD, DIST_SPAN), 0)
    idx = jnp.clip((BAND_KEYS - 1) - j, -REL_CLIP, REL_CLIP) + REL_CLIP
    onehot = jnp.where(idx == r, 1.0, 0.0).astype(BF16)
    e = jnp.zeros((HEADS, DIST_SPAN), F32)
    for p in _split3(tbl_ref[...]):
        e = e + _dot(p, onehot)
    qc = lax.broadcasted_iota(jnp.int32, (ATT_BLK, BAND_KEYS), 0) // CHUNK
    kc = lax.broadcasted_iota(jnp.int32, (ATT_BLK, BAND_KEYS), 1) // CHUNK
    visible = (kc >= qc) & (kc <= qc + PAST_CHUNKS)
    for h in range(HEADS):
        rows = jnp.broadcast_to(e[h:h + 1, :], (ATT_BLK, DIST_SPAN))
        skew = pltpu.roll(rows, DIST_SPAN - (ATT_BLK - 1), 1, stride=1, stride_axis=0)
        o_ref[h] = jnp.where(visible, skew[:, :BAND_KEYS], NEG)


def _band_bias(rel_table):
    tbl = jnp.pad(rel_table, ((0, 0), (0, REL_PAD - REL_SIZE)))
    return pl.pallas_call(
        _band_bias_kernel,
        out_shape=jax.ShapeDtypeStruct((HEADS, ATT_BLK, BAND_KEYS), F32),
        compiler_params=pltpu.CompilerParams(vmem_limit_bytes=VMEM_LIMIT),
        name="band_bias",
    )(tbl)


def _softmax_pv(scores, values):
    m = functools.reduce(jnp.maximum, [jnp.max(s, axis=-1, keepdims=True) for s in scores])
    l = 0.0
    acc = 0.0
    for s, v in zip(scores, values):
        p = jnp.exp(s - m)
        l = l + jnp.sum(p, axis=-1, keepdims=True)
        acc = acc + _dot(p.astype(BF16), v)
    return acc / l


def _band_attn_prompt_kernel(q_ref, k_ref, v_ref, bias_ref, o_ref):
    i = pl.program_id(1)
    for h in range(HEADS):
        sl = slice(h * HEAD_DIM, (h + 1) * HEAD_DIM)
        q = q_ref[0, :, sl]
        scores, values = [], []
        for g in range(3):
            j = i - 2 + g
            start = pl.multiple_of(jnp.maximum(j, 0) * ATT_BLK, ATT_BLK)
            s = _dot_nt(q, k_ref[0, pl.ds(start, ATT_BLK), sl]) + bias_ref[h, :, g * ATT_BLK:(g + 1) * ATT_BLK]
            if g < 2:
                s = s + jnp.where(j >= 0, 0.0, NEG)
            scores.append(s)
            values.append(v_ref[0, pl.ds(start, ATT_BLK), sl])
        o_ref[0, :, sl] = _softmax_pv(scores, values).astype(BF16)


def _band_attn_prompt(q, k, v, bias):
    bsz, s, _ = q.shape
    qblk = pl.BlockSpec((1, ATT_BLK, WIDTH), lambda b, i: (b, i, 0))
    seq = pl.BlockSpec((1, s, WIDTH), lambda b, i: (b, 0, 0))
    return pl.pallas_call(
        _band_attn_prompt_kernel,
        grid=(bsz, s // ATT_BLK),
        in_specs=[qblk, seq, seq, _resident(bias.shape, lambda b, i: (0, 0, 0))],
        out_specs=qblk,
        out_shape=jax.ShapeDtypeStruct(q.shape, BF16),
        compiler_params=_params("parallel", "parallel"),
        name="band_attn_prompt",
    )(q, k, v, bias)


def _fox_attn_prompt_kernel(q_ref, k_ref, v_ref, ccol_ref, crow_ref, o_ref):
    i = pl.program_id(1)
    row = lax.broadcasted_iota(jnp.int32, (ATT_BLK, ATT_BLK), 0)
    col = lax.broadcasted_iota(jnp.int32, (ATT_BLK, ATT_BLK), 1)
    causal = col <= row
    for h in range(HEADS):
        sl = slice(h * HEAD_DIM, (h + 1) * HEAD_DIM)
        q = q_ref[0, :, sl]
        cq = ccol_ref[0, :, h:h + 1]

        def step(j, carry, masked):
            m, l, acc = carry
            start = pl.multiple_of(j * ATT_BLK, ATT_BLK)
            s = _dot_nt(q, k_ref[0, pl.ds(start, ATT_BLK), sl]) + (cq - crow_ref[0, h:h + 1, pl.ds(start, ATT_BLK)])
            if masked:
                s = jnp.where(causal, s, NEG)
            m_new = jnp.maximum(m, jnp.max(s, axis=-1, keepdims=True))
            alpha = jnp.exp(m - m_new)
            p = jnp.exp(s - m_new)
            l = alpha * l + jnp.sum(p, axis=-1, keepdims=True)
            acc = alpha * acc + _dot(p.astype(BF16), v_ref[0, pl.ds(start, ATT_BLK), sl])
            return m_new, l, acc

        init = (jnp.full((ATT_BLK, 1), NEG, F32), jnp.zeros((ATT_BLK, 1), F32),
                jnp.zeros((ATT_BLK, HEAD_DIM), F32))
        carry = lax.fori_loop(0, i, functools.partial(step, masked=False), init)
        _, l, acc = step(i, carry, masked=True)
        o_ref[0, :, sl] = (acc / l).astype(BF16)


def _fox_attn_prompt(q, k, v, ccol, crow):
    bsz, s, _ = q.shape
    qblk = pl.BlockSpec((1, ATT_BLK, WIDTH), lambda b, i: (b, i, 0))
    seq = pl.BlockSpec((1, s, WIDTH), lambda b, i: (b, 0, 0))
    return pl.pallas_call(
        _fox_attn_prompt_kernel,
        grid=(bsz, s // ATT_BLK),
        in_specs=[qblk, seq, seq,
                  pl.BlockSpec((1, ATT_BLK, HEADS), lambda b, i: (b, i, 0)),
                  pl.BlockSpec((1, HEADS, s), lambda b, i: (b, 0, 0))],
        out_specs=qblk,
        out_shape=jax.ShapeDtypeStruct(q.shape, BF16),
        compiler_params=_params("parallel", "parallel"),
        name="fox_attn_prompt",
    )(q, k, v, ccol, crow)


def _band_attn_sample_kernel(q_ref, kc_ref, vc_ref, kn_ref, vn_ref, bias_ref, o_ref):
    t = q_ref.shape[1]
    past = kc_ref.shape[1]
    for h in range(HEADS):
        sl = slice(h * HEAD_DIM, (h + 1) * HEAD_DIM)
        q = q_ref[0, :, sl]
        s_past = _dot_nt(q, kc_ref[0, :, sl].astype(BF16)) + bias_ref[h, 0:t, BAND - past:BAND]
        s_new = _dot_nt(q, kn_ref[0, :, sl]) + bias_ref[h, 0:t, BAND:BAND + t]
        out = _softmax_pv([s_past, s_new], [vc_ref[0, :, sl].astype(BF16), vn_ref[0, :, sl]])
        o_ref[0, :, sl] = out.astype(BF16)


def _band_attn_sample(q, k_cache, v_cache, k_new, v_new, bias):
    bsz, t, _ = q.shape
    past = k_cache.shape[1]
    new = pl.BlockSpec((1, t, WIDTH), lambda b: (b, 0, 0))
    old = pl.BlockSpec((1, past, WIDTH), lambda b: (b, 0, 0))
    return pl.pallas_call(
        _band_attn_sample_kernel,
        grid=(bsz,),
        in_specs=[new, old, old, new, new, _resident(bias.shape, lambda b: (0, 0, 0))],
        out_specs=new,
        out_shape=jax.ShapeDtypeStruct(q.shape, BF16),
        compiler_params=_params("parallel"),
        name="band_attn_sample",
    )(q, k_cache, v_cache, k_new, v_new, bias)


def _fox_attn_sample_kernel(q_ref, kc_ref, vc_ref, kn_ref, vn_ref, ccol_ref, crow_ref, o_ref):
    t = q_ref.shape[1]
    past = kc_ref.shape[1]
    row = lax.broadcasted_iota(jnp.int32, (t, t), 0)
    col = lax.broadcasted_iota(jnp.int32, (t, t), 1)
    causal = col <= row
    for h in range(HEADS):
        sl = slice(h * HEAD_DIM, (h + 1) * HEAD_DIM)
        q = q_ref[0, :, sl]
        cq = ccol_ref[0, :, h:h + 1]
        s_past = _dot_nt(q, kc_ref[0, :, sl].astype(BF16)) + (cq - crow_ref[0, h:h + 1, 0:past])
        s_new = _dot_nt(q, kn_ref[0, :, sl]) + (cq - crow_ref[0, h:h + 1, past:past + t])
        s_new = jnp.where(causal, s_new, NEG)
        out = _softmax_pv([s_past, s_new], [vc_ref[0, :, sl].astype(BF16), vn_ref[0, :, sl]])
        o_ref[0, :, sl] = out.astype(BF16)


def _fox_attn_sample(q, k_cache, v_cache, k_new, v_new, ccol, crow):
    bsz, t, _ = q.shape
    past = k_cache.shape[1]
    assert past % t == 0
    new = pl.BlockSpec((1, t, WIDTH), lambda b: (b, 0, 0))
    old = pl.BlockSpec((1, past, WIDTH), lambda b: (b, 0, 0))
    return pl.pallas_call(
        _fox_attn_sample_kernel,
        grid=(bsz,),
        in_specs=[new, old, old, new, new,
                  pl.BlockSpec((1, t, HEADS), lambda b: (b, past // t, 0)),
                  pl.BlockSpec((1, HEADS, past + t), lambda b: (b, 0, 0))],
        out_specs=new,
        out_shape=jax.ShapeDtypeStruct(q.shape, BF16),
        compiler_params=_params("parallel"),
        name="fox_attn_sample",
    )(q, k_cache, v_cache, k_new, v_new, ccol, crow)


def _postmix_kernel(x_ref, oa_ref, ob_ref, gpre_ref, gpost_ref, wg_ref, wpa_ref, wpb_ref, wout_ref, y_ref):
    x = x_ref[...]
    d = x.shape[-1]
    h = _rmsnorm(x, gpre_ref[...]).astype(BF16)
    gate_a = jax.nn.sigmoid(_dot(h, wg_ref[:, :d]))
    gate_b = jax.nn.sigmoid(_dot(h, wg_ref[:, d:]))
    merged = gate_a * _dot(oa_ref[...], wpa_ref[...]) + gate_b * _dot(ob_ref[...], wpb_ref[...])
    y_ref[...] = x + _rmsnorm(_dot(merged.astype(BF16), wout_ref[...]), gpost_ref[...])


def _postmix(x, oa, ob, g_pre, g_post, w_g, w_pa, w_pb, w_out):
    n, d = x.shape
    tm = min(ROW_TILE, n)
    row = lambda i: (i, 0)
    fixed = lambda i: (0, 0)
    return pl.pallas_call(
        _postmix_kernel,
        grid=(n // tm,),
        in_specs=[pl.BlockSpec((tm, d), row),
                  pl.BlockSpec((tm, WIDTH), row),
                  pl.BlockSpec((tm, WIDTH), row),
                  pl.BlockSpec((1, d), fixed),
                  pl.BlockSpec((1, d), fixed),
                  _resident(w_g.shape, fixed),
                  _resident(w_pa.shape, fixed),
                  _resident(w_pb.shape, fixed),
                  _resident(w_out.shape, fixed)],
        out_specs=pl.BlockSpec((tm, d), row),
        out_shape=jax.ShapeDtypeStruct((n, d), F32),
        compiler_params=_params("parallel"),
        name="postmix",
    )(x, oa, ob, g_pre, g_post, w_g, w_pa, w_pb, w_out)


HEADER = 8


def _ffn_kernel(x_ref, st_ref, gpre_ref, gpost_ref, wup_ref, cw_ref, cb_ref, wdn_ref,
                y_ref, nst_ref, hist_ref, ext_ref, *, nseg, seg_len):
    @pl.when(pl.program_id(1) == 0)
    def _():
        hist_ref[...] = st_ref[...]

    x = x_ref[...]
    d_ff = wdn_ref.shape[0]
    h = _rmsnorm(x, gpre_ref[...]).astype(BF16)
    lo, hi = HEADER - (CONV_W - 1), HEADER
    f = jnp.zeros(x.shape, F32)
    for c in range(d_ff // FFN_COLS):
        halves = []
        for part in range(2):
            cols = slice(part * d_ff + c * FFN_COLS, part * d_ff + (c + 1) * FFN_COLS)
            dst = slice(part * FFN_COLS, (part + 1) * FFN_COLS)
            u = _dot(h, wup_ref[:, cols])
            ext_ref[:, lo:hi, dst] = hist_ref[:, :, cols]
            for s in range(nseg):
                ext_ref[s, hi:hi + seg_len, dst] = u[s * seg_len:(s + 1) * seg_len, :]
            hist_ref[:, :, cols] = ext_ref[:, lo + seg_len:hi + seg_len, dst]
            y = cb_ref[:, cols]
            for tap in range(CONV_W):
                y = y + ext_ref[:, lo + tap:lo + tap + seg_len, dst] * cw_ref[tap:tap + 1, cols]
            halves.append(y)
        act = (jax.nn.gelu(halves[0]) * halves[1]).reshape(nseg * seg_len, FFN_COLS)
        f = f + _dot(act.astype(BF16), wdn_ref[c * FFN_COLS:(c + 1) * FFN_COLS, :])
    y_ref[...] = x + _rmsnorm(f, gpost_ref[...])
    nst_ref[...] = hist_ref[...]


def _ffn(x, state, g_pre, g_post, w_up, conv_w, conv_b, w_down):
    bsz, s, d = x.shape
    up = w_up.shape[1]
    tm = min(ROW_TILE, bsz * s)
    if s >= tm:
        nseg, seg_len, tiles = 1, tm, s // tm
    else:
        nseg, seg_len, tiles = tm // s, s, 1
    outer = bsz // nseg
    row = lambda o, t: (o * tiles + t, 0)
    fixed = lambda o, t: (0, 0)
    st_spec = pl.BlockSpec((nseg, CONV_W - 1, up), lambda o, t: (o, 0, 0))
    y, new_state = pl.pallas_call(
        functools.partial(_ffn_kernel, nseg=nseg, seg_len=seg_len),
        grid=(outer, tiles),
        in_specs=[pl.BlockSpec((tm, d), row),
                  st_spec,
                  pl.BlockSpec((1, d), fixed),
                  pl.BlockSpec((1, d), fixed),
                  _resident(w_up.shape, fixed),
                  pl.BlockSpec(conv_w.shape, fixed),
                  pl.BlockSpec((1, up), fixed),
                  _resident(w_down.shape, fixed)],
        out_specs=[pl.BlockSpec((tm, d), row), st_spec],
        out_shape=[jax.ShapeDtypeStruct((bsz * s, d), F32),
                   jax.ShapeDtypeStruct(state.shape, F32)],
        scratch_shapes=[pltpu.VMEM((nseg, CONV_W - 1, up), F32),
                        pltpu.VMEM((nseg, HEADER + seg_len, 2 * FFN_COLS), F32)],
        compiler_params=_params("arbitrary", "arbitrary"),
        name="conv_ffn",
    )(x.reshape(bsz * s, d), state, g_pre, g_post, w_up, conv_w, conv_b, w_down)
    return y.reshape(bsz, s, d), new_state


def _layer(x, caches, conv_state, bias, w):
    bsz, s, d = x.shape
    n = bsz * s
    x2 = x.reshape(n, d)
    (qa, ka, va, qb, kb, vb, ka16, va16, kb16, vb16, logf) = _inproj(
        x2, w["g_pre_mix"], w["w_qkv"], w["w_f"], w["b_f"])
    seq = lambda a: a.reshape(bsz, s, a.shape[-1])
    logf = seq(logf)
    if caches is None:
        ccol, crow = _cumsum(logf)
        oa = _band_attn_prompt(seq(qa), seq(ka16), seq(va16), bias)
        ob = _fox_attn_prompt(seq(qb), seq(kb16), seq(vb16), ccol, crow)
    else:
        cka, cva, ckb, cvb, clf = caches
        flat = lambda a: a.reshape(a.shape[0], a.shape[1], WIDTH)
        ccol, crow = _cumsum(clf, logf)
        oa = _band_attn_sample(seq(qa), flat(cka), flat(cva), seq(ka16), seq(va16), bias)
        ob = _fox_attn_sample(seq(qb), flat(ckb), flat(cvb), seq(kb16), seq(vb16), ccol, crow)
    x1 = _postmix(x2, oa.reshape(n, WIDTH), ob.reshape(n, WIDTH), w["g_pre_mix"], w["g_post_mix"],
                  w["w_g"], w["w_pa"], w["w_pb"], w["w_out"])
    y, new_conv = _ffn(x1.reshape(bsz, s, d), conv_state, w["g_pre_ffn"], w["g_post_ffn"],
                       w["w_up"], w["conv_w"], w["conv_b"], w["w_down"])
    heads = lambda a: a.reshape(bsz, s, HEADS, HEAD_DIM)
    return y, (heads(ka), heads(va), heads(kb), heads(vb), logf, new_conv)


def kernel(x_prompt, x_sample, cache_k_a, cache_v_a, cache_k_b, cache_v_b, cache_logf_b, state_conv_ffn,
           g_pre_mix, g_post_mix, g_pre_ffn, g_post_ffn, w_in, b_f, rel_table, w_proj_a, w_proj_b, w_out,
           w_up, conv_w, conv_b, w_down):
    depth = w_in.shape[0]
    up = w_up.shape[-1]
    x_p, x_s = x_prompt, x_sample
    p_states, s_states = [], []
    for l in range(depth):
        w = {
            "g_pre_mix": g_pre_mix[l][None], "g_post_mix": g_post_mix[l][None],
            "g_pre_ffn": g_pre_ffn[l][None], "g_post_ffn": g_post_ffn[l][None],
            "w_qkv": w_in[l][:, :6 * WIDTH].astype(BF16),
            "w_f": w_in[l][:, 6 * WIDTH:6 * WIDTH + HEADS].astype(BF16),
            "w_g": w_in[l][:, 6 * WIDTH + HEADS:].astype(BF16),
            "b_f": b_f[l][None],
            "w_pa": w_proj_a[l].astype(BF16), "w_pb": w_proj_b[l].astype(BF16),
            "w_out": w_out[l].astype(BF16), "w_up": w_up[l].astype(BF16),
            "conv_w": conv_w[l], "conv_b": conv_b[l][None], "w_down": w_down[l].astype(BF16),
        }
        bias = _band_bias(rel_table[l])
        zero_state = jnp.zeros((x_p.shape[0], CONV_W - 1, up), F32)
        x_p, (ka, va, kb, vb, lf, cv) = _layer(x_p, None, zero_state, bias, w)
        keep = min(BAND, x_p.shape[1])
        p_states.append((ka[:, -keep:], va[:, -keep:], kb, vb, lf, cv))
        caches = (cache_k_a[l], cache_v_a[l], cache_k_b[l], cache_v_b[l], cache_logf_b[l])
        x_s, st = _layer(x_s, caches, state_conv_ffn[l], bias, w)
        s_states.append(st)
    stack = lambda states: [jnp.stack(s) for s in zip(*states)]
    return (x_p, x_s, *stack(p_states), *stack(s_states))
```

```python
import functools

import jax
import jax.numpy as jnp
from jax import lax
from jax.experimental import pallas as pl
from jax.experimental.pallas import tpu as pltpu

F32, BF16 = jnp.float32, jnp.bfloat16

HEADS = 8
HEAD_DIM = 64
WIDTH = HEADS * HEAD_DIM
CHUNK = 64
PAST_CHUNKS = 8
BAND = PAST_CHUNKS * CHUNK
REL_CLIP = 128
REL_SIZE = 2 * REL_CLIP + 1
CONV_W = 3
EPS = 1e-6
NEG = -1e30
LOG2E = 1.4426950408889634

ATT_BLK = 256
BAND_KEYS = 3 * ATT_BLK
ROW_TILE = 512
FFN_COLS = 256
CUMSUM_BLK = 256
VMEM_LIMIT = 48 * 1024 * 1024


def _params(*sem):
    return pltpu.CompilerParams(dimension_semantics=sem, vmem_limit_bytes=VMEM_LIMIT)


def _resident(shape, index_map):
    return pl.BlockSpec(shape, index_map, pipeline_mode=pl.Buffered(1))


def _rmsnorm(x, g):
    return x * lax.rsqrt(jnp.mean(x * x, axis=-1, keepdims=True) + EPS) * g


def _dot(a, b):
    return jnp.dot(a, b, preferred_element_type=F32)


def _dot_nt(a, b):
    return lax.dot_general(a, b, (((1,), (1,)), ((), ())), preferred_element_type=F32)


def _dot_tn(a, b):
    return lax.dot_general(a, b, (((0,), (0,)), ((), ())), preferred_element_type=F32)


def _split3(x):
    hi = x.astype(BF16)
    r = x - hi.astype(F32)
    mid = r.astype(BF16)
    lo = (r - mid.astype(F32)).astype(BF16)
    return hi, mid, lo


def _inproj_kernel(x_ref, g_ref, w_ref, wf_ref, bf_ref,
                   qa_ref, ka_ref, va_ref, qb_ref, kb_ref, vb_ref,
                   kab_ref, vab_ref, kbb_ref, vbb_ref, lf_ref):
    h = _rmsnorm(x_ref[...], g_ref[...]).astype(BF16)
    scale = HEAD_DIM ** -0.5

    def proj(c):
        return _dot(h, w_ref[:, c * WIDTH:(c + 1) * WIDTH])

    qa_ref[...] = (proj(0) * scale).astype(BF16)
    qb_ref[...] = (proj(3) * (scale * LOG2E)).astype(BF16)
    for c, full_ref, half_ref in ((1, ka_ref, kab_ref), (2, va_ref, vab_ref),
                                  (4, kb_ref, kbb_ref), (5, vb_ref, vbb_ref)):
        z = proj(c)
        full_ref[...] = z
        half_ref[...] = z.astype(BF16)
    lf_ref[...] = jax.nn.log_sigmoid(_dot(h, wf_ref[...]) + bf_ref[...])


def _inproj(x, g, w_qkv, w_f, b_f):
    n, d = x.shape
    tm = min(ROW_TILE, n)
    row = lambda i: (i, 0)
    fixed = lambda i: (0, 0)
    wide = lambda dt: jax.ShapeDtypeStruct((n, WIDTH), dt)
    blk = pl.BlockSpec((tm, WIDTH), row)
    out_dtypes = (BF16, F32, F32, BF16, F32, F32, BF16, BF16, BF16, BF16)
    return pl.pallas_call(
        _inproj_kernel,
        grid=(n // tm,),
        in_specs=[pl.BlockSpec((tm, d), row),
                  pl.BlockSpec((1, d), fixed),
                  _resident(w_qkv.shape, fixed),
                  pl.BlockSpec(w_f.shape, fixed),
                  pl.BlockSpec((1, HEADS), fixed)],
        out_specs=[blk] * 10 + [pl.BlockSpec((tm, HEADS), row)],
        out_shape=[wide(dt) for dt in out_dtypes] + [jax.ShapeDtypeStruct((n, HEADS), F32)],
        compiler_params=_params("parallel"),
        name="inproj",
    )(x, g, w_qkv, w_f, b_f)


def _cumsum_kernel(*refs, seg_lens):
    seg_refs = refs[:len(seg_lens)]
    ccol_ref, crow_ref = refs[len(seg_lens):]
    carry_c = jnp.zeros((1, HEADS), F32)
    carry_r = jnp.zeros((HEADS, 1), F32)
    off = 0
    for ref, n in zip(seg_refs, seg_lens):
        for o in range(0, n, CUMSUM_BLK):
            b = min(CUMSUM_BLK, n - o)
            parts = _split3(ref[0, o:o + b, :])
            r = lax.broadcasted_iota(jnp.int32, (b, b), 0)
            c = lax.broadcasted_iota(jnp.int32, (b, b), 1)
            lower = jnp.where(r >= c, 1.0, 0.0).astype(BF16)
            upper = jnp.where(r <= c, 1.0, 0.0).astype(BF16)
            cc = carry_c
            cr = carry_r
            for p in parts:
                cc = cc + _dot(lower, p)
                cr = cr + _dot_tn(p, upper)
            ccol_ref[0, off + o:off + o + b, :] = cc
            crow_ref[0, :, off + o:off + o + b] = cr
            carry_c = cc[b - 1:b, :]
            carry_r = cr[:, b - 1:b]
        off += n


def _cumsum(*segs):
    bsz = segs[0].shape[0]
    seg_lens = tuple(s.shape[1] for s in segs)
    total = sum(seg_lens)
    return pl.pallas_call(
        functools.partial(_cumsum_kernel, seg_lens=seg_lens),
        grid=(bsz,),
        in_specs=[pl.BlockSpec((1, n, HEADS), lambda b: (b, 0, 0)) for n in seg_lens],
        out_specs=[pl.BlockSpec((1, total, HEADS), lambda b: (b, 0, 0)),
                   pl.BlockSpec((1, HEADS, total), lambda b: (b, 0, 0))],
        out_shape=[jax.ShapeDtypeStruct((bsz, total, HEADS), F32),
                   jax.ShapeDtypeStruct((bsz, HEADS, total), F32)],
        compiler_params=_params("parallel"),
        name="cumsum_logf",
    )(*segs)


REL_PAD = 384
DIST_SPAN = 1024


def _band_bias_kernel(tbl_ref, o_ref):
    j = lax.broadcasted_iota(jnp.int32, (REL_PAD, DIST_SPAN), 1)
    r = lax.broadcasted_iota(jnp.int32, (REL_PAD, DIST_SPAN), 0)
    idx = jnp.clip((BAND_KEYS - 1) - j, -REL_CLIP, REL_CLIP) + REL_CLIP
    onehot = jnp.where(idx == r, 1.0, 0.0).astype(BF16)
    e = jnp.zeros((HEADS, DIST_SPAN), F32)
    for p in _split3(tbl_ref[...]):
        e = e + _dot(p, onehot)
    qc = lax.broadcasted_iota(jnp.int32, (ATT_BLK, BAND_KEYS), 0) // CHUNK
    kc = lax.broadcasted_iota(jnp.int32, (ATT_BLK, BAND_KEYS), 1) // CHUNK
    visible = (kc >= qc) & (kc <= qc + PAST_CHUNKS)
    for h in range(HEADS):
        rows = jnp.broadcast_to(e[h:h + 1, :], (ATT_BLK, DIST_SPAN))
        skew = pltpu.roll(rows, DIST_SPAN - (ATT_BLK - 1), 1, stride=1, stride_axis=0)
        o_ref[h] = jnp.where(visible, skew[:, :BAND_KEYS], NEG)


def _band_bias(rel_table):
    tbl = jnp.pad(rel_table, ((0, 0), (0, REL_PAD - REL_SIZE)))
    return pl.pallas_call(
        _band_bias_kernel,
        out_shape=jax.ShapeDtypeStruct((HEADS, ATT_BLK, BAND_KEYS), F32),
        compiler_params=pltpu.CompilerParams(vmem_limit_bytes=VMEM_LIMIT),
        name="band_bias",
    )(tbl)


def _softmax_pv(scores, values, exp=jnp.exp):
    m = functools.reduce(jnp.maximum, [jnp.max(s, axis=-1, keepdims=True) for s in scores])
    l = 0.0
    acc = 0.0
    for s, v in zip(scores, values):
        p = exp(s - m)
        l = l + jnp.sum(p, axis=-1, keepdims=True)
        acc = acc + _dot(p.astype(BF16), v)
    return acc / l


def _band_attn_prompt_kernel(q_ref, k_ref, v_ref, bias_ref, o_ref):
    i = pl.program_id(1)
    for h in range(HEADS):
        sl = slice(h * HEAD_DIM, (h + 1) * HEAD_DIM)
        q = q_ref[0, :, sl]
        scores, values = [], []
        for g in range(3):
            j = i - 2 + g
            start = pl.multiple_of(jnp.maximum(j, 0) * ATT_BLK, ATT_BLK)
            s = _dot_nt(q, k_ref[0, pl.ds(start, ATT_BLK), sl]) + bias_ref[h, :, g * ATT_BLK:(g + 1) * ATT_BLK]
            if g < 2:
                s = s + jnp.where(j >= 0, 0.0, NEG)
            scores.append(s)
            values.append(v_ref[0, pl.ds(start, ATT_BLK), sl])
        o_ref[0, :, sl] = _softmax_pv(scores, values).astype(BF16)


def _band_attn_prompt(q, k, v, bias):
    bsz, s, _ = q.shape
    qblk = pl.BlockSpec((1, ATT_BLK, WIDTH), lambda b, i: (b, i, 0))
    seq = pl.BlockSpec((1, s, WIDTH), lambda b, i: (b, 0, 0))
    return pl.pallas_call(
        _band_attn_prompt_kernel,
        grid=(bsz, s // ATT_BLK),
        in_specs=[qblk, seq, seq, _resident(bias.shape, lambda b, i: (0, 0, 0))],
        out_specs=qblk,
        out_shape=jax.ShapeDtypeStruct(q.shape, BF16),
        compiler_params=_params("parallel", "parallel"),
        name="band_attn_prompt",
    )(q, k, v, bias)


def _decay_lanes(c, key_side):
    n = c.shape[0]
    lane = lax.broadcasted_iota(jnp.int32, (n, HEAD_DIM), 1)
    p0, p1, p2 = [p.astype(F32) for p in _split3(jnp.broadcast_to(c, (n, HEAD_DIM)))]
    if key_side:
        out = jnp.where(lane == 0, -p0, jnp.where(lane == 1, -p1, jnp.where(lane == 2, -p2,
              jnp.where(lane < 6, 1.0, 0.0))))
    else:
        out = jnp.where(lane < 3, 1.0, jnp.where(lane == 3, p0, jnp.where(lane == 4, p1,
              jnp.where(lane == 5, p2, 0.0))))
    return out.astype(BF16)


PAD_DIM = 2 * HEAD_DIM


def _fox_attn_prompt_kernel(q_ref, k_ref, v_ref, ccol_ref, o_ref,
                            kp_ref, vt_ref, qp_ref, s_ref, m_ref, l_ref, acc_ref):
    i = pl.program_id(1)
    qstart = pl.multiple_of(i * ATT_BLK, ATT_BLK)

    @pl.when(i == 0)
    def _():
        vt_ref[...] = v_ref[0].T
        for h in range(HEADS):
            sl = slice(h * HEAD_DIM, (h + 1) * HEAD_DIM)
            extra = _decay_lanes(ccol_ref[0, :, h:h + 1] * LOG2E, key_side=True)
            kp_ref[:, h * PAD_DIM:(h + 1) * PAD_DIM] = jnp.concatenate([k_ref[0, :, sl], extra], axis=1)

    for h in range(HEADS):
        sl = slice(h * HEAD_DIM, (h + 1) * HEAD_DIM)
        extra = _decay_lanes(ccol_ref[0, pl.ds(qstart, ATT_BLK), h:h + 1] * LOG2E, key_side=False)
        qp_ref[:, h * PAD_DIM:(h + 1) * PAD_DIM] = jnp.concatenate([q_ref[0, :, sl], extra], axis=1)

    key = lax.broadcasted_iota(jnp.int32, (ATT_BLK, ATT_BLK), 0)
    qry = lax.broadcasted_iota(jnp.int32, (ATT_BLK, ATT_BLK), 1)
    causal = key <= qry
    m_ref[...] = jnp.full(m_ref.shape, NEG, F32)
    l_ref[...] = jnp.zeros(l_ref.shape, F32)
    acc_ref[...] = jnp.zeros(acc_ref.shape, F32)

    def scores(j, h):
        start = pl.multiple_of(j * ATT_BLK, ATT_BLK)
        pad = slice(h * PAD_DIM, (h + 1) * PAD_DIM)
        return _dot_nt(kp_ref[pl.ds(start, ATT_BLK), pad], qp_ref[:, pad])

    def update(j, h, s, masked):
        start = pl.multiple_of(j * ATT_BLK, ATT_BLK)
        if masked:
            s = jnp.where(causal, s, NEG)
        m_old = m_ref[h]
        m_new = jnp.maximum(m_old, jnp.max(s, axis=0, keepdims=True))
        alpha = jnp.exp2(m_old - m_new)
        p = jnp.exp2(s - m_new)
        l_ref[h] = alpha * l_ref[h] + jnp.sum(p, axis=0, keepdims=True)
        vt = vt_ref[h * HEAD_DIM:(h + 1) * HEAD_DIM, pl.ds(start, ATT_BLK)]
        acc_ref[h] = alpha * acc_ref[h] + _dot(vt, p.astype(BF16))
        m_ref[h] = m_new

    for h in range(HEADS):
        s_ref[h] = scores(0, h)

    def body(j, carry):
        for h in range(HEADS):
            s = s_ref[h]
            s_ref[h] = scores(j + 1, h)
            update(j, h, s, masked=False)
        return carry

    lax.fori_loop(0, i, body, 0)
    for h in range(HEADS):
        update(i, h, s_ref[h], masked=True)
    for h in range(HEADS):
        sl = slice(h * HEAD_DIM, (h + 1) * HEAD_DIM)
        o_ref[0, :, sl] = (acc_ref[h] / l_ref[h]).T.astype(BF16)


def _fox_attn_prompt(q, k, v, ccol):
    bsz, s, _ = q.shape
    qblk = pl.BlockSpec((1, ATT_BLK, WIDTH), lambda b, i: (b, i, 0))
    seq = pl.BlockSpec((1, s, WIDTH), lambda b, i: (b, 0, 0))
    return pl.pallas_call(
        _fox_attn_prompt_kernel,
        grid=(bsz, s // ATT_BLK),
        in_specs=[qblk, seq, seq, pl.BlockSpec((1, s, HEADS), lambda b, i: (b, 0, 0))],
        out_specs=qblk,
        out_shape=jax.ShapeDtypeStruct(q.shape, BF16),
        scratch_shapes=[pltpu.VMEM((s, HEADS * PAD_DIM), BF16),
                        pltpu.VMEM((WIDTH, s), BF16),
                        pltpu.VMEM((ATT_BLK, HEADS * PAD_DIM), BF16),
                        pltpu.VMEM((HEADS, ATT_BLK, ATT_BLK), F32),
                        pltpu.VMEM((HEADS, 1, ATT_BLK), F32),
                        pltpu.VMEM((HEADS, 1, ATT_BLK), F32),
                        pltpu.VMEM((HEADS, HEAD_DIM, ATT_BLK), F32)],
        compiler_params=_params("parallel", "arbitrary"),
        name="fox_attn_prompt",
    )(q, k, v, ccol)


def _band_attn_sample_kernel(q_ref, kc_ref, vc_ref, kn_ref, vn_ref, bias_ref, o_ref):
    t = q_ref.shape[1]
    past = kc_ref.shape[1]
    for h in range(HEADS):
        sl = slice(h * HEAD_DIM, (h + 1) * HEAD_DIM)
        q = q_ref[0, :, sl]
        s_past = _dot_nt(q, kc_ref[0, :, sl].astype(BF16)) + bias_ref[h, 0:t, BAND - past:BAND]
        s_new = _dot_nt(q, kn_ref[0, :, sl]) + bias_ref[h, 0:t, BAND:BAND + t]
        out = _softmax_pv([s_past, s_new], [vc_ref[0, :, sl].astype(BF16), vn_ref[0, :, sl]])
        o_ref[0, :, sl] = out.astype(BF16)


def _band_attn_sample(q, k_cache, v_cache, k_new, v_new, bias):
    bsz, t, _ = q.shape
    past = k_cache.shape[1]
    new = pl.BlockSpec((1, t, WIDTH), lambda b: (b, 0, 0))
    old = pl.BlockSpec((1, past, WIDTH), lambda b: (b, 0, 0))
    return pl.pallas_call(
        _band_attn_sample_kernel,
        grid=(bsz,),
        in_specs=[new, old, old, new, new, _resident(bias.shape, lambda b: (0, 0, 0))],
        out_specs=new,
        out_shape=jax.ShapeDtypeStruct(q.shape, BF16),
        compiler_params=_params("parallel"),
        name="band_attn_sample",
    )(q, k_cache, v_cache, k_new, v_new, bias)


def _fox_attn_sample_kernel(q_ref, kc_ref, vc_ref, kn_ref, vn_ref, ccol_ref, crow_ref, o_ref):
    t = q_ref.shape[1]
    past = kc_ref.shape[1]
    row = lax.broadcasted_iota(jnp.int32, (t, t), 0)
    col = lax.broadcasted_iota(jnp.int32, (t, t), 1)
    causal = col <= row
    for h in range(HEADS):
        sl = slice(h * HEAD_DIM, (h + 1) * HEAD_DIM)
        q = q_ref[0, :, sl]
        cq = ccol_ref[0, :, h:h + 1]
        s_past = _dot_nt(q, kc_ref[0, :, sl].astype(BF16)) + (cq - crow_ref[0, h:h + 1, 0:past]) * LOG2E
        s_new = _dot_nt(q, kn_ref[0, :, sl]) + (cq - crow_ref[0, h:h + 1, past:past + t]) * LOG2E
        s_new = jnp.where(causal, s_new, NEG)
        out = _softmax_pv([s_past, s_new], [vc_ref[0, :, sl].astype(BF16), vn_ref[0, :, sl]], exp=jnp.exp2)
        o_ref[0, :, sl] = out.astype(BF16)


def _fox_attn_sample(q, k_cache, v_cache, k_new, v_new, ccol, crow):
    bsz, t, _ = q.shape
    past = k_cache.shape[1]
    assert past % t == 0
    new = pl.BlockSpec((1, t, WIDTH), lambda b: (b, 0, 0))
    old = pl.BlockSpec((1, past, WIDTH), lambda b: (b, 0, 0))
    return pl.pallas_call(
        _fox_attn_sample_kernel,
        grid=(bsz,),
        in_specs=[new, old, old, new, new,
                  pl.BlockSpec((1, t, HEADS), lambda b: (b, past // t, 0)),
                  pl.BlockSpec((1, HEADS, past + t), lambda b: (b, 0, 0))],
        out_specs=new,
        out_shape=jax.ShapeDtypeStruct(q.shape, BF16),
        compiler_params=_params("parallel"),
        name="fox_attn_sample",
    )(q, k_cache, v_cache, k_new, v_new, ccol, crow)


def _postmix_kernel(x_ref, oa_ref, ob_ref, gpre_ref, gpost_ref, wg_ref, wpa_ref, wpb_ref, wout_ref, y_ref):
    x = x_ref[...]
    d = x.shape[-1]
    h = _rmsnorm(x, gpre_ref[...]).astype(BF16)
    gate_a = jax.nn.sigmoid(_dot(h, wg_ref[:, :d]))
    gate_b = jax.nn.sigmoid(_dot(h, wg_ref[:, d:]))
    merged = gate_a * _dot(oa_ref[...], wpa_ref[...]) + gate_b * _dot(ob_ref[...], wpb_ref[...])
    y_ref[...] = x + _rmsnorm(_dot(merged.astype(BF16), wout_ref[...]), gpost_ref[...])


def _postmix(x, oa, ob, g_pre, g_post, w_g, w_pa, w_pb, w_out):
    n, d = x.shape
    tm = min(ROW_TILE, n)
    row = lambda i: (i, 0)
    fixed = lambda i: (0, 0)
    return pl.pallas_call(
        _postmix_kernel,
        grid=(n // tm,),
        in_specs=[pl.BlockSpec((tm, d), row),
                  pl.BlockSpec((tm, WIDTH), row),
                  pl.BlockSpec((tm, WIDTH), row),
                  pl.BlockSpec((1, d), fixed),
                  pl.BlockSpec((1, d), fixed),
                  _resident(w_g.shape, fixed),
                  _resident(w_pa.shape, fixed),
                  _resident(w_pb.shape, fixed),
                  _resident(w_out.shape, fixed)],
        out_specs=pl.BlockSpec((tm, d), row),
        out_shape=jax.ShapeDtypeStruct((n, d), F32),
        compiler_params=_params("parallel"),
        name="postmix",
    )(x, oa, ob, g_pre, g_post, w_g, w_pa, w_pb, w_out)


HEADER = 8


def _ffn_kernel(x_ref, st_ref, gpre_ref, gpost_ref, wup_ref, cw_ref, cb_ref, wdn_ref,
                y_ref, nst_ref, hist_ref, ext_ref, *, nseg, seg_len):
    @pl.when(pl.program_id(1) == 0)
    def _():
        hist_ref[...] = st_ref[...]

    x = x_ref[...]
    d_ff = wdn_ref.shape[0]
    h = _rmsnorm(x, gpre_ref[...]).astype(BF16)
    lo, hi = HEADER - (CONV_W - 1), HEADER
    f = jnp.zeros(x.shape, F32)
    for c in range(d_ff // FFN_COLS):
        halves = []
        for part in range(2):
            cols = slice(part * d_ff + c * FFN_COLS, part * d_ff + (c + 1) * FFN_COLS)
            dst = slice(part * FFN_COLS, (part + 1) * FFN_COLS)
            u = _dot(h, wup_ref[:, cols])
            ext_ref[:, lo:hi, dst] = hist_ref[:, :, cols]
            for s in range(nseg):
                ext_ref[s, hi:hi + seg_len, dst] = u[s * seg_len:(s + 1) * seg_len, :]
            hist_ref[:, :, cols] = ext_ref[:, lo + seg_len:hi + seg_len, dst]
            y = cb_ref[:, cols]
            for tap in range(CONV_W):
                y = y + ext_ref[:, lo + tap:lo + tap + seg_len, dst] * cw_ref[tap:tap + 1, cols]
            halves.append(y)
        act = (jax.nn.gelu(halves[0]) * halves[1]).reshape(nseg * seg_len, FFN_COLS)
        f = f + _dot(act.astype(BF16), wdn_ref[c * FFN_COLS:(c + 1) * FFN_COLS, :])
    y_ref[...] = x + _rmsnorm(f, gpost_ref[...])
    nst_ref[...] = hist_ref[...]


def _ffn(x, state, g_pre, g_post, w_up, conv_w, conv_b, w_down):
    bsz, s, d = x.shape
    up = w_up.shape[1]
    tm = min(ROW_TILE, bsz * s)
    if s >= tm:
        nseg, seg_len, tiles = 1, tm, s // tm
    else:
        nseg, seg_len, tiles = tm // s, s, 1
    outer = bsz // nseg
    row = lambda o, t: (o * tiles + t, 0)
    fixed = lambda o, t: (0, 0)
    st_spec = pl.BlockSpec((nseg, CONV_W - 1, up), lambda o, t: (o, 0, 0))
    y, new_state = pl.pallas_call(
        functools.partial(_ffn_kernel, nseg=nseg, seg_len=seg_len),
        grid=(outer, tiles),
        in_specs=[pl.BlockSpec((tm, d), row),
                  st_spec,
                  pl.BlockSpec((1, d), fixed),
                  pl.BlockSpec((1, d), fixed),
                  _resident(w_up.shape, fixed),
                  pl.BlockSpec(conv_w.shape, fixed),
                  pl.BlockSpec((1, up), fixed),
                  _resident(w_down.shape, fixed)],
        out_specs=[pl.BlockSpec((tm, d), row), st_spec],
        out_shape=[jax.ShapeDtypeStruct((bsz * s, d), F32),
                   jax.ShapeDtypeStruct(state.shape, F32)],
        scratch_shapes=[pltpu.VMEM((nseg, CONV_W - 1, up), F32),
                        pltpu.VMEM((nseg, HEADER + seg_len, 2 * FFN_COLS), F32)],
        compiler_params=_params("arbitrary", "arbitrary"),
        name="conv_ffn",
    )(x.reshape(bsz * s, d), state, g_pre, g_post, w_up, conv_w, conv_b, w_down)
    return y.reshape(bsz, s, d), new_state


def _layer(x, caches, conv_state, bias, w):
    bsz, s, d = x.shape
    n = bsz * s
    x2 = x.reshape(n, d)
    (qa, ka, va, qb, kb, vb, ka16, va16, kb16, vb16, logf) = _inproj(
        x2, w["g_pre_mix"], w["w_qkv"], w["w_f"], w["b_f"])
    seq = lambda a: a.reshape(bsz, s, a.shape[-1])
    logf = seq(logf)
    if caches is None:
        ccol, _ = _cumsum(logf)
        oa = _band_attn_prompt(seq(qa), seq(ka16), seq(va16), bias)
        ob = _fox_attn_prompt(seq(qb), seq(kb16), seq(vb16), ccol)
    else:
        cka, cva, ckb, cvb, clf = caches
        flat = lambda a: a.reshape(a.shape[0], a.shape[1], WIDTH)
        ccol, crow = _cumsum(clf, logf)
        oa = _band_attn_sample(seq(qa), flat(cka), flat(cva), seq(ka16), seq(va16), bias)
        ob = _fox_attn_sample(seq(qb), flat(ckb), flat(cvb), seq(kb16), seq(vb16), ccol, crow)
    x1 = _postmix(x2, oa.reshape(n, WIDTH), ob.reshape(n, WIDTH), w["g_pre_mix"], w["g_post_mix"],
                  w["w_g"], w["w_pa"], w["w_pb"], w["w_out"])
    y, new_conv = _ffn(x1.reshape(bsz, s, d), conv_state, w["g_pre_ffn"], w["g_post_ffn"],
                       w["w_up"], w["conv_w"], w["conv_b"], w["w_down"])
    heads = lambda a: a.reshape(bsz, s, HEADS, HEAD_DIM)
    return y, (heads(ka), heads(va), heads(kb), heads(vb), logf, new_conv)


def kernel(x_prompt, x_sample, cache_k_a, cache_v_a, cache_k_b, cache_v_b, cache_logf_b, state_conv_ffn,
           g_pre_mix, g_post_mix, g_pre_ffn, g_post_ffn, w_in, b_f, rel_table, w_proj_a, w_proj_b, w_out,
           w_up, conv_w, conv_b, w_down):
    depth = w_in.shape[0]
    up = w_up.shape[-1]
    x_p, x_s = x_prompt, x_sample
    p_states, s_states = [], []
    for l in range(depth):
        w = {
            "g_pre_mix": g_pre_mix[l][None], "g_post_mix": g_post_mix[l][None],
            "g_pre_ffn": g_pre_ffn[l][None], "g_post_ffn": g_post_ffn[l][None],
            "w_qkv": w_in[l][:, :6 * WIDTH].astype(BF16),
            "w_f": w_in[l][:, 6 * WIDTH:6 * WIDTH + HEADS].astype(BF16),
            "w_g": w_in[l][:, 6 * WIDTH + HEADS:].astype(BF16),
            "b_f": b_f[l][None],
            "w_pa": w_proj_a[l].astype(BF16), "w_pb": w_proj_b[l].astype(BF16),
            "w_out": w_out[l].astype(BF16), "w_up": w_up[l].astype(BF16),
            "conv_w": conv_w[l], "conv_b": conv_b[l][None], "w_down": w_down[l].astype(BF16),
        }
        bias = _band_bias(rel_table[l])
        zero_state = jnp.zeros((x_p.shape[0], CONV_W - 1, up), F32)
        x_p, (ka, va, kb, vb, lf, cv) = _layer(x_p, None, zero_state, bias, w)
        keep = min(BAND, x_p.shape[1])
        p_states.append((ka[:, -keep:], va[:, -keep:], kb, vb, lf, cv))
        caches = (cache_k_a[l], cache_v_a[l], cache_k_b[l], cache_v_b[l], cache_logf_b[l])
        x_s, st = _layer(x_s, caches, state_conv_ffn[l], bias, w)
        s_states.append(st)
    stack = lambda states: [jnp.stack(s) for s in zip(*states)]
    return (x_p, x_s, *stack(p_states), *stack(s_states))
```

```python
import functools

import jax
import jax.numpy as jnp
from jax import lax
from jax.experimental import pallas as pl
from jax.experimental.pallas import tpu as pltpu

F32, BF16 = jnp.float32, jnp.bfloat16

HEADS = 8
HEAD_DIM = 64
WIDTH = HEADS * HEAD_DIM
CHUNK = 64
PAST_CHUNKS = 8
BAND = PAST_CHUNKS * CHUNK
REL_CLIP = 128
REL_SIZE = 2 * REL_CLIP + 1
CONV_W = 3
EPS = 1e-6
NEG = -1e30
LOG2E = 1.4426950408889634

ATT_BLK = 256
BAND_KEYS = 3 * ATT_BLK
ROW_TILE = 512
FFN_COLS = 256
CUMSUM_BLK = 256
VMEM_LIMIT = 48 * 1024 * 1024


def _params(*sem):
    return pltpu.CompilerParams(dimension_semantics=sem, vmem_limit_bytes=VMEM_LIMIT)


def _resident(shape, index_map):
    return pl.BlockSpec(shape, index_map, pipeline_mode=pl.Buffered(1))


def _rmsnorm(x, g):
    return x * lax.rsqrt(jnp.mean(x * x, axis=-1, keepdims=True) + EPS) * g


def _dot(a, b):
    return jnp.dot(a, b, preferred_element_type=F32)


def _dot_nt(a, b):
    return lax.dot_general(a, b, (((1,), (1,)), ((), ())), preferred_element_type=F32)


def _dot_tn(a, b):
    return lax.dot_general(a, b, (((0,), (0,)), ((), ())), preferred_element_type=F32)


def _split3(x):
    hi = x.astype(BF16)
    r = x - hi.astype(F32)
    mid = r.astype(BF16)
    lo = (r - mid.astype(F32)).astype(BF16)
    return hi, mid, lo


def _inproj_kernel(x_ref, g_ref, w_ref, wf_ref, bf_ref,
                   qa_ref, ka_ref, va_ref, qb_ref, kb_ref, vb_ref,
                   kab_ref, vab_ref, kbb_ref, vbb_ref, lf_ref):
    h = _rmsnorm(x_ref[...], g_ref[...]).astype(BF16)
    scale = HEAD_DIM ** -0.5

    def proj(c):
        return _dot(h, w_ref[:, c * WIDTH:(c + 1) * WIDTH])

    qa_ref[...] = (proj(0) * (scale * LOG2E)).astype(BF16)
    qb_ref[...] = (proj(3) * (scale * LOG2E)).astype(BF16)
    for c, full_ref, half_ref in ((1, ka_ref, kab_ref), (2, va_ref, vab_ref),
                                  (4, kb_ref, kbb_ref), (5, vb_ref, vbb_ref)):
        z = proj(c)
        full_ref[...] = z
        half_ref[...] = z.astype(BF16)
    lf_ref[...] = jax.nn.log_sigmoid(_dot(h, wf_ref[...]) + bf_ref[...])


def _inproj(x, g, w_qkv, w_f, b_f):
    n, d = x.shape
    tm = min(ROW_TILE, n)
    row = lambda i: (i, 0)
    fixed = lambda i: (0, 0)
    wide = lambda dt: jax.ShapeDtypeStruct((n, WIDTH), dt)
    blk = pl.BlockSpec((tm, WIDTH), row)
    out_dtypes = (BF16, F32, F32, BF16, F32, F32, BF16, BF16, BF16, BF16)
    return pl.pallas_call(
        _inproj_kernel,
        grid=(n // tm,),
        in_specs=[pl.BlockSpec((tm, d), row),
                  pl.BlockSpec((1, d), fixed),
                  _resident(w_qkv.shape, fixed),
                  pl.BlockSpec(w_f.shape, fixed),
                  pl.BlockSpec((1, HEADS), fixed)],
        out_specs=[blk] * 10 + [pl.BlockSpec((tm, HEADS), row)],
        out_shape=[wide(dt) for dt in out_dtypes] + [jax.ShapeDtypeStruct((n, HEADS), F32)],
        compiler_params=_params("parallel"),
        name="inproj",
    )(x, g, w_qkv, w_f, b_f)


def _cumsum_kernel(*refs, seg_lens):
    seg_refs = refs[:len(seg_lens)]
    ccol_ref, crow_ref = refs[len(seg_lens):]
    carry_c = jnp.zeros((1, HEADS), F32)
    carry_r = jnp.zeros((HEADS, 1), F32)
    off = 0
    for ref, n in zip(seg_refs, seg_lens):
        for o in range(0, n, CUMSUM_BLK):
            b = min(CUMSUM_BLK, n - o)
            parts = _split3(ref[0, o:o + b, :])
            r = lax.broadcasted_iota(jnp.int32, (b, b), 0)
            c = lax.broadcasted_iota(jnp.int32, (b, b), 1)
            lower = jnp.where(r >= c, 1.0, 0.0).astype(BF16)
            upper = jnp.where(r <= c, 1.0, 0.0).astype(BF16)
            cc = carry_c
            cr = carry_r
            for p in parts:
                cc = cc + _dot(lower, p)
                cr = cr + _dot_tn(p, upper)
            ccol_ref[0, off + o:off + o + b, :] = cc
            crow_ref[0, :, off + o:off + o + b] = cr
            carry_c = cc[b - 1:b, :]
            carry_r = cr[:, b - 1:b]
        off += n


def _cumsum(*segs):
    bsz = segs[0].shape[0]
    seg_lens = tuple(s.shape[1] for s in segs)
    total = sum(seg_lens)
    return pl.pallas_call(
        functools.partial(_cumsum_kernel, seg_lens=seg_lens),
        grid=(bsz,),
        in_specs=[pl.BlockSpec((1, n, HEADS), lambda b: (b, 0, 0)) for n in seg_lens],
        out_specs=[pl.BlockSpec((1, total, HEADS), lambda b: (b, 0, 0)),
                   pl.BlockSpec((1, HEADS, total), lambda b: (b, 0, 0))],
        out_shape=[jax.ShapeDtypeStruct((bsz, total, HEADS), F32),
                   jax.ShapeDtypeStruct((bsz, HEADS, total), F32)],
        compiler_params=_params("parallel"),
        name="cumsum_logf",
    )(*segs)


REL_PAD = 384
DIST_SPAN = 1024


def _band_bias_kernel(tbl_ref, qk_ref, kq_ref):
    j = lax.broadcasted_iota(jnp.int32, (REL_PAD, DIST_SPAN), 1)
    r = lax.broadcasted_iota(jnp.int32, (REL_PAD, DIST_SPAN), 0)
    parts = _split3(tbl_ref[...])

    def by_offset(dist):
        onehot = jnp.where(jnp.clip(dist, -REL_CLIP, REL_CLIP) + REL_CLIP == r, 1.0, 0.0).astype(BF16)
        return sum(_dot(p, onehot) for p in parts) * LOG2E

    def skewed(e_row, rows, cols, shift):
        wide = jnp.broadcast_to(e_row, (rows, DIST_SPAN))
        return pltpu.roll(wide, shift % DIST_SPAN, 1, stride=1, stride_axis=0)[:, :cols]

    e_qk = by_offset((BAND_KEYS - 1) - j)
    e_kq = by_offset(j - (ATT_BLK - 1))
    qc = lax.broadcasted_iota(jnp.int32, (ATT_BLK, BAND_KEYS), 0) // CHUNK
    kc = lax.broadcasted_iota(jnp.int32, (ATT_BLK, BAND_KEYS), 1) // CHUNK
    vis_qk = (kc >= qc) & (kc <= qc + PAST_CHUNKS)
    kc = lax.broadcasted_iota(jnp.int32, (BAND_KEYS, ATT_BLK), 0) // CHUNK
    qc = lax.broadcasted_iota(jnp.int32, (BAND_KEYS, ATT_BLK), 1) // CHUNK
    vis_kq = (kc >= qc) & (kc <= qc + PAST_CHUNKS)
    for h in range(HEADS):
        qk_ref[h] = jnp.where(vis_qk, skewed(e_qk[h:h + 1, :], ATT_BLK, BAND_KEYS, -(ATT_BLK - 1)), NEG)
        kq_ref[h] = jnp.where(vis_kq, skewed(e_kq[h:h + 1, :], BAND_KEYS, ATT_BLK, -(BAND_KEYS - 1)), NEG)


def _band_bias(rel_table):
    tbl = jnp.pad(rel_table, ((0, 0), (0, REL_PAD - REL_SIZE)))
    return pl.pallas_call(
        _band_bias_kernel,
        out_shape=[jax.ShapeDtypeStruct((HEADS, ATT_BLK, BAND_KEYS), F32),
                   jax.ShapeDtypeStruct((HEADS, BAND_KEYS, ATT_BLK), F32)],
        compiler_params=pltpu.CompilerParams(vmem_limit_bytes=VMEM_LIMIT),
        name="band_bias",
    )(tbl)


def _softmax_pv(scores, values, exp=jnp.exp):
    m = functools.reduce(jnp.maximum, [jnp.max(s, axis=-1, keepdims=True) for s in scores])
    l = 0.0
    acc = 0.0
    for s, v in zip(scores, values):
        p = exp(s - m)
        l = l + jnp.sum(p, axis=-1, keepdims=True)
        acc = acc + _dot(p.astype(BF16), v)
    return acc / l


BAND_BLOCKS = BAND_KEYS // ATT_BLK


def _band_attn_prompt_kernel(q_ref, k_ref, v_ref, bias_ref, o_ref, vt_ref, s_ref):
    i = pl.program_id(1)

    @pl.when(i == 0)
    def _():
        vt_ref[...] = v_ref[0].T

    blocks = []
    for g in range(BAND_BLOCKS):
        j = i - (BAND_BLOCKS - 1) + g
        start = pl.multiple_of(jnp.maximum(j, 0) * ATT_BLK, ATT_BLK)
        blocks.append((start, jnp.where(j >= 0, 0.0, NEG) if g < BAND_BLOCKS - 1 else None))

    def scores(h):
        sl = slice(h * HEAD_DIM, (h + 1) * HEAD_DIM)
        q = q_ref[0, :, sl]
        for g, (start, penalty) in enumerate(blocks):
            s = _dot_nt(k_ref[0, pl.ds(start, ATT_BLK), sl], q) + bias_ref[h, g * ATT_BLK:(g + 1) * ATT_BLK, :]
            s_ref[h % 2, g] = s if penalty is None else s + penalty

    def finish(h):
        sl = slice(h * HEAD_DIM, (h + 1) * HEAD_DIM)
        s = [s_ref[h % 2, g] for g in range(BAND_BLOCKS)]
        m = functools.reduce(jnp.maximum, [jnp.max(x, axis=0, keepdims=True) for x in s])
        l = 0.0
        acc = 0.0
        for x, (start, _) in zip(s, blocks):
            p = jnp.exp2(x - m)
            l = l + jnp.sum(p, axis=0, keepdims=True)
            acc = acc + _dot(vt_ref[sl, pl.ds(start, ATT_BLK)], p.astype(BF16))
        o_ref[0, :, sl] = (acc / l).T.astype(BF16)

    scores(0)
    for h in range(HEADS):
        if h + 1 < HEADS:
            scores(h + 1)
        finish(h)


def _band_attn_prompt(q, k, v, bias_kq):
    bsz, s, _ = q.shape
    qblk = pl.BlockSpec((1, ATT_BLK, WIDTH), lambda b, i: (b, i, 0))
    seq = pl.BlockSpec((1, s, WIDTH), lambda b, i: (b, 0, 0))
    return pl.pallas_call(
        _band_attn_prompt_kernel,
        grid=(bsz, s // ATT_BLK),
        in_specs=[qblk, seq, seq, _resident(bias_kq.shape, lambda b, i: (0, 0, 0))],
        out_specs=qblk,
        out_shape=jax.ShapeDtypeStruct(q.shape, BF16),
        scratch_shapes=[pltpu.VMEM((WIDTH, s), BF16),
                        pltpu.VMEM((2, BAND_BLOCKS, ATT_BLK, ATT_BLK), F32)],
        compiler_params=_params("parallel", "arbitrary"),
        name="band_attn_prompt",
    )(q, k, v, bias_kq)


def _decay_lanes(c, key_side):
    n = c.shape[0]
    lane = lax.broadcasted_iota(jnp.int32, (n, HEAD_DIM), 1)
    p0, p1, p2 = [p.astype(F32) for p in _split3(jnp.broadcast_to(c, (n, HEAD_DIM)))]
    if key_side:
        out = jnp.where(lane == 0, -p0, jnp.where(lane == 1, -p1, jnp.where(lane == 2, -p2,
              jnp.where(lane < 6, 1.0, 0.0))))
    else:
        out = jnp.where(lane < 3, 1.0, jnp.where(lane == 3, p0, jnp.where(lane == 4, p1,
              jnp.where(lane == 5, p2, 0.0))))
    return out.astype(BF16)


PAD_DIM = 2 * HEAD_DIM


def _fox_attn_prompt_kernel(q_ref, k_ref, v_ref, ccol_ref, o_ref,
                            kp_ref, vt_ref, qp_ref, s_ref, m_ref, l_ref, acc_ref):
    i = pl.program_id(1)
    qstart = pl.multiple_of(i * ATT_BLK, ATT_BLK)

    @pl.when(i == 0)
    def _():
        vt_ref[...] = v_ref[0].T
        for h in range(HEADS):
            sl = slice(h * HEAD_DIM, (h + 1) * HEAD_DIM)
            extra = _decay_lanes(ccol_ref[0, :, h:h + 1] * LOG2E, key_side=True)
            kp_ref[:, h * PAD_DIM:(h + 1) * PAD_DIM] = jnp.concatenate([k_ref[0, :, sl], extra], axis=1)

    for h in range(HEADS):
        sl = slice(h * HEAD_DIM, (h + 1) * HEAD_DIM)
        extra = _decay_lanes(ccol_ref[0, pl.ds(qstart, ATT_BLK), h:h + 1] * LOG2E, key_side=False)
        qp_ref[:, h * PAD_DIM:(h + 1) * PAD_DIM] = jnp.concatenate([q_ref[0, :, sl], extra], axis=1)

    key = lax.broadcasted_iota(jnp.int32, (ATT_BLK, ATT_BLK), 0)
    qry = lax.broadcasted_iota(jnp.int32, (ATT_BLK, ATT_BLK), 1)
    causal = key <= qry
    m_ref[...] = jnp.full(m_ref.shape, NEG, F32)
    l_ref[...] = jnp.zeros(l_ref.shape, F32)
    acc_ref[...] = jnp.zeros(acc_ref.shape, F32)

    def scores(j, h):
        start = pl.multiple_of(j * ATT_BLK, ATT_BLK)
        pad = slice(h * PAD_DIM, (h + 1) * PAD_DIM)
        return _dot_nt(kp_ref[pl.ds(start, ATT_BLK), pad], qp_ref[:, pad])

    def update(j, h, s, masked):
        start = pl.multiple_of(j * ATT_BLK, ATT_BLK)
        if masked:
            s = jnp.where(causal, s, NEG)
        m_old = m_ref[h]
        m_new = jnp.maximum(m_old, jnp.max(s, axis=0, keepdims=True))
        alpha = jnp.exp2(m_old - m_new)
        p = jnp.exp2(s - m_new)
        l_ref[h] = alpha * l_ref[h] + jnp.sum(p, axis=0, keepdims=True)
        vt = vt_ref[h * HEAD_DIM:(h + 1) * HEAD_DIM, pl.ds(start, ATT_BLK)]
        acc_ref[h] = alpha * acc_ref[h] + _dot(vt, p.astype(BF16))
        m_ref[h] = m_new

    for h in range(HEADS):
        s_ref[h] = scores(0, h)

    def body(j, carry):
        for h in range(HEADS):
            s = s_ref[h]
            s_ref[h] = scores(j + 1, h)
            update(j, h, s, masked=False)
        return carry

    lax.fori_loop(0, i, body, 0)
    for h in range(HEADS):
        update(i, h, s_ref[h], masked=True)
    for h in range(HEADS):
        sl = slice(h * HEAD_DIM, (h + 1) * HEAD_DIM)
        o_ref[0, :, sl] = (acc_ref[h] / l_ref[h]).T.astype(BF16)


def _fox_attn_prompt(q, k, v, ccol):
    bsz, s, _ = q.shape
    qblk = pl.BlockSpec((1, ATT_BLK, WIDTH), lambda b, i: (b, i, 0))
    seq = pl.BlockSpec((1, s, WIDTH), lambda b, i: (b, 0, 0))
    return pl.pallas_call(
        _fox_attn_prompt_kernel,
        grid=(bsz, s // ATT_BLK),
        in_specs=[qblk, seq, seq, pl.BlockSpec((1, s, HEADS), lambda b, i: (b, 0, 0))],
        out_specs=qblk,
        out_shape=jax.ShapeDtypeStruct(q.shape, BF16),
        scratch_shapes=[pltpu.VMEM((s, HEADS * PAD_DIM), BF16),
                        pltpu.VMEM((WIDTH, s), BF16),
                        pltpu.VMEM((ATT_BLK, HEADS * PAD_DIM), BF16),
                        pltpu.VMEM((HEADS, ATT_BLK, ATT_BLK), F32),
                        pltpu.VMEM((HEADS, 1, ATT_BLK), F32),
                        pltpu.VMEM((HEADS, 1, ATT_BLK), F32),
                        pltpu.VMEM((HEADS, HEAD_DIM, ATT_BLK), F32)],
        compiler_params=_params("parallel", "arbitrary"),
        name="fox_attn_prompt",
    )(q, k, v, ccol)


def _band_attn_sample_kernel(q_ref, kc_ref, vc_ref, kn_ref, vn_ref, bias_ref, o_ref):
    t = q_ref.shape[1]
    past = kc_ref.shape[1]
    for h in range(HEADS):
        sl = slice(h * HEAD_DIM, (h + 1) * HEAD_DIM)
        q = q_ref[0, :, sl]
        s_past = _dot_nt(q, kc_ref[0, :, sl].astype(BF16)) + bias_ref[h, 0:t, BAND - past:BAND]
        s_new = _dot_nt(q, kn_ref[0, :, sl]) + bias_ref[h, 0:t, BAND:BAND + t]
        out = _softmax_pv([s_past, s_new], [vc_ref[0, :, sl].astype(BF16), vn_ref[0, :, sl]], exp=jnp.exp2)
        o_ref[0, :, sl] = out.astype(BF16)


def _band_attn_sample(q, k_cache, v_cache, k_new, v_new, bias):
    bsz, t, _ = q.shape
    past = k_cache.shape[1]
    new = pl.BlockSpec((1, t, WIDTH), lambda b: (b, 0, 0))
    old = pl.BlockSpec((1, past, WIDTH), lambda b: (b, 0, 0))
    return pl.pallas_call(
        _band_attn_sample_kernel,
        grid=(bsz,),
        in_specs=[new, old, old, new, new, _resident(bias.shape, lambda b: (0, 0, 0))],
        out_specs=new,
        out_shape=jax.ShapeDtypeStruct(q.shape, BF16),
        compiler_params=_params("parallel"),
        name="band_attn_sample",
    )(q, k_cache, v_cache, k_new, v_new, bias)


def _fox_attn_sample_kernel(q_ref, kc_ref, vc_ref, kn_ref, vn_ref, ccol_ref, crow_ref, o_ref):
    t = q_ref.shape[1]
    past = kc_ref.shape[1]
    row = lax.broadcasted_iota(jnp.int32, (t, t), 0)
    col = lax.broadcasted_iota(jnp.int32, (t, t), 1)
    causal = col <= row
    for h in range(HEADS):
        sl = slice(h * HEAD_DIM, (h + 1) * HEAD_DIM)
        q = q_ref[0, :, sl]
        cq = ccol_ref[0, :, h:h + 1]
        s_past = _dot_nt(q, kc_ref[0, :, sl].astype(BF16)) + (cq - crow_ref[0, h:h + 1, 0:past]) * LOG2E
        s_new = _dot_nt(q, kn_ref[0, :, sl]) + (cq - crow_ref[0, h:h + 1, past:past + t]) * LOG2E
        s_new = jnp.where(causal, s_new, NEG)
        out = _softmax_pv([s_past, s_new], [vc_ref[0, :, sl].astype(BF16), vn_ref[0, :, sl]], exp=jnp.exp2)
        o_ref[0, :, sl] = out.astype(BF16)


def _fox_attn_sample(q, k_cache, v_cache, k_new, v_new, ccol, crow):
    bsz, t, _ = q.shape
    past = k_cache.shape[1]
    assert past % t == 0
    new = pl.BlockSpec((1, t, WIDTH), lambda b: (b, 0, 0))
    old = pl.BlockSpec((1, past, WIDTH), lambda b: (b, 0, 0))
    return pl.pallas_call(
        _fox_attn_sample_kernel,
        grid=(bsz,),
        in_specs=[new, old, old, new, new,
                  pl.BlockSpec((1, t, HEADS), lambda b: (b, past // t, 0)),
                  pl.BlockSpec((1, HEADS, past + t), lambda b: (b, 0, 0))],
        out_specs=new,
        out_shape=jax.ShapeDtypeStruct(q.shape, BF16),
        compiler_params=_params("parallel"),
        name="fox_attn_sample",
    )(q, k_cache, v_cache, k_new, v_new, ccol, crow)


def _postmix_kernel(x_ref, oa_ref, ob_ref, gpre_ref, gpost_ref, wg_ref, wpa_ref, wpb_ref, wout_ref, y_ref):
    x = x_ref[...]
    d = x.shape[-1]
    h = _rmsnorm(x, gpre_ref[...]).astype(BF16)
    gate_a = jax.nn.sigmoid(_dot(h, wg_ref[:, :d]))
    gate_b = jax.nn.sigmoid(_dot(h, wg_ref[:, d:]))
    merged = gate_a * _dot(oa_ref[...], wpa_ref[...]) + gate_b * _dot(ob_ref[...], wpb_ref[...])
    y_ref[...] = x + _rmsnorm(_dot(merged.astype(BF16), wout_ref[...]), gpost_ref[...])


def _postmix(x, oa, ob, g_pre, g_post, w_g, w_pa, w_pb, w_out):
    n, d = x.shape
    tm = min(ROW_TILE, n)
    row = lambda i: (i, 0)
    fixed = lambda i: (0, 0)
    return pl.pallas_call(
        _postmix_kernel,
        grid=(n // tm,),
        in_specs=[pl.BlockSpec((tm, d), row),
                  pl.BlockSpec((tm, WIDTH), row),
                  pl.BlockSpec((tm, WIDTH), row),
                  pl.BlockSpec((1, d), fixed),
                  pl.BlockSpec((1, d), fixed),
                  _resident(w_g.shape, fixed),
                  _resident(w_pa.shape, fixed),
                  _resident(w_pb.shape, fixed),
                  _resident(w_out.shape, fixed)],
        out_specs=pl.BlockSpec((tm, d), row),
        out_shape=jax.ShapeDtypeStruct((n, d), F32),
        compiler_params=_params("parallel"),
        name="postmix",
    )(x, oa, ob, g_pre, g_post, w_g, w_pa, w_pb, w_out)


HEADER = 8


def _ffn_kernel(x_ref, st_ref, gpre_ref, gpost_ref, wup_ref, cw_ref, cb_ref, wdn_ref,
                y_ref, nst_ref, hist_ref, ext_ref, *, nseg, seg_len):
    @pl.when(pl.program_id(1) == 0)
    def _():
        hist_ref[...] = st_ref[...]

    x = x_ref[...]
    d_ff = wdn_ref.shape[0]
    h = _rmsnorm(x, gpre_ref[...]).astype(BF16)
    lo, hi = HEADER - (CONV_W - 1), HEADER
    f = jnp.zeros(x.shape, F32)
    for c in range(d_ff // FFN_COLS):
        halves = []
        for part in range(2):
            cols = slice(part * d_ff + c * FFN_COLS, part * d_ff + (c + 1) * FFN_COLS)
            dst = slice(part * FFN_COLS, (part + 1) * FFN_COLS)
            u = _dot(h, wup_ref[:, cols])
            ext_ref[:, lo:hi, dst] = hist_ref[:, :, cols]
            for s in range(nseg):
                ext_ref[s, hi:hi + seg_len, dst] = u[s * seg_len:(s + 1) * seg_len, :]
            hist_ref[:, :, cols] = ext_ref[:, lo + seg_len:hi + seg_len, dst]
            y = cb_ref[:, cols]
            for tap in range(CONV_W):
                y = y + ext_ref[:, lo + tap:lo + tap + seg_len, dst] * cw_ref[tap:tap + 1, cols]
            halves.append(y)
        act = (jax.nn.gelu(halves[0]) * halves[1]).reshape(nseg * seg_len, FFN_COLS)
        f = f + _dot(act.astype(BF16), wdn_ref[c * FFN_COLS:(c + 1) * FFN_COLS, :])
    y_ref[...] = x + _rmsnorm(f, gpost_ref[...])
    nst_ref[...] = hist_ref[...]


def _ffn(x, state, g_pre, g_post, w_up, conv_w, conv_b, w_down):
    bsz, s, d = x.shape
    up = w_up.shape[1]
    tm = min(ROW_TILE, bsz * s)
    if s >= tm:
        nseg, seg_len, tiles = 1, tm, s // tm
    else:
        nseg, seg_len, tiles = tm // s, s, 1
    outer = bsz // nseg
    row = lambda o, t: (o * tiles + t, 0)
    fixed = lambda o, t: (0, 0)
    st_spec = pl.BlockSpec((nseg, CONV_W - 1, up), lambda o, t: (o, 0, 0))
    y, new_state = pl.pallas_call(
        functools.partial(_ffn_kernel, nseg=nseg, seg_len=seg_len),
        grid=(outer, tiles),
        in_specs=[pl.BlockSpec((tm, d), row),
                  st_spec,
                  pl.BlockSpec((1, d), fixed),
                  pl.BlockSpec((1, d), fixed),
                  _resident(w_up.shape, fixed),
                  pl.BlockSpec(conv_w.shape, fixed),
                  pl.BlockSpec((1, up), fixed),
                  _resident(w_down.shape, fixed)],
        out_specs=[pl.BlockSpec((tm, d), row), st_spec],
        out_shape=[jax.ShapeDtypeStruct((bsz * s, d), F32),
                   jax.ShapeDtypeStruct(state.shape, F32)],
        scratch_shapes=[pltpu.VMEM((nseg, CONV_W - 1, up), F32),
                        pltpu.VMEM((nseg, HEADER + seg_len, 2 * FFN_COLS), F32)],
        compiler_params=_params("arbitrary", "arbitrary"),
        name="conv_ffn",
    )(x.reshape(bsz * s, d), state, g_pre, g_post, w_up, conv_w, conv_b, w_down)
    return y.reshape(bsz, s, d), new_state


def _layer(x, caches, conv_state, bias, w):
    bsz, s, d = x.shape
    n = bsz * s
    x2 = x.reshape(n, d)
    (qa, ka, va, qb, kb, vb, ka16, va16, kb16, vb16, logf) = _inproj(
        x2, w["g_pre_mix"], w["w_qkv"], w["w_f"], w["b_f"])
    seq = lambda a: a.reshape(bsz, s, a.shape[-1])
    logf = seq(logf)
    if caches is None:
        ccol, _ = _cumsum(logf)
        oa = _band_attn_prompt(seq(qa), seq(ka16), seq(va16), bias[1])
        ob = _fox_attn_prompt(seq(qb), seq(kb16), seq(vb16), ccol)
    else:
        cka, cva, ckb, cvb, clf = caches
        flat = lambda a: a.reshape(a.shape[0], a.shape[1], WIDTH)
        ccol, crow = _cumsum(clf, logf)
        oa = _band_attn_sample(seq(qa), flat(cka), flat(cva), seq(ka16), seq(va16), bias[0])
        ob = _fox_attn_sample(seq(qb), flat(ckb), flat(cvb), seq(kb16), seq(vb16), ccol, crow)
    x1 = _postmix(x2, oa.reshape(n, WIDTH), ob.reshape(n, WIDTH), w["g_pre_mix"], w["g_post_mix"],
                  w["w_g"], w["w_pa"], w["w_pb"], w["w_out"])
    y, new_conv = _ffn(x1.reshape(bsz, s, d), conv_state, w["g_pre_ffn"], w["g_post_ffn"],
                       w["w_up"], w["conv_w"], w["conv_b"], w["w_down"])
    heads = lambda a: a.reshape(bsz, s, HEADS, HEAD_DIM)
    return y, (heads(ka), heads(va), heads(kb), heads(vb), logf, new_conv)


def kernel(x_prompt, x_sample, cache_k_a, cache_v_a, cache_k_b, cache_v_b, cache_logf_b, state_conv_ffn,
           g_pre_mix, g_post_mix, g_pre_ffn, g_post_ffn, w_in, b_f, rel_table, w_proj_a, w_proj_b, w_out,
           w_up, conv_w, conv_b, w_down):
    depth = w_in.shape[0]
    up = w_up.shape[-1]
    x_p, x_s = x_prompt, x_sample
    p_states, s_states = [], []
    for l in range(depth):
        w = {
            "g_pre_mix": g_pre_mix[l][None], "g_post_mix": g_post_mix[l][None],
            "g_pre_ffn": g_pre_ffn[l][None], "g_post_ffn": g_post_ffn[l][None],
            "w_qkv": w_in[l][:, :6 * WIDTH].astype(BF16),
            "w_f": w_in[l][:, 6 * WIDTH:6 * WIDTH + HEADS].astype(BF16),
            "w_g": w_in[l][:, 6 * WIDTH + HEADS:].astype(BF16),
            "b_f": b_f[l][None],
            "w_pa": w_proj_a[l].astype(BF16), "w_pb": w_proj_b[l].astype(BF16),
            "w_out": w_out[l].astype(BF16), "w_up": w_up[l].astype(BF16),
            "conv_w": conv_w[l], "conv_b": conv_b[l][None], "w_down": w_down[l].astype(BF16),
        }
        bias = _band_bias(rel_table[l])
        zero_state = jnp.zeros((x_p.shape[0], CONV_W - 1, up), F32)
        x_p, (ka, va, kb, vb, lf, cv) = _layer(x_p, None, zero_state, bias, w)
        keep = min(BAND, x_p.shape[1])
        p_states.append((ka[:, -keep:], va[:, -keep:], kb, vb, lf, cv))
        caches = (cache_k_a[l], cache_v_a[l], cache_k_b[l], cache_v_b[l], cache_logf_b[l])
        x_s, st = _layer(x_s, caches, state_conv_ffn[l], bias, w)
        s_states.append(st)
    stack = lambda states: [jnp.stack(s) for s in zip(*states)]
    return (x_p, x_s, *stack(p_states), *stack(s_states))
```

```python
import functools

import jax
import jax.numpy as jnp
from jax import lax
from jax.experimental import pallas as pl
from jax.experimental.pallas import tpu as pltpu

F32, BF16 = jnp.float32, jnp.bfloat16

HEADS = 8
HEAD_DIM = 64
WIDTH = HEADS * HEAD_DIM
CHUNK = 64
PAST_CHUNKS = 8
BAND = PAST_CHUNKS * CHUNK
REL_CLIP = 128
REL_SIZE = 2 * REL_CLIP + 1
CONV_W = 3
EPS = 1e-6
NEG = -1e30
LOG2E = 1.4426950408889634

ATT_BLK = 256
BAND_KEYS = 3 * ATT_BLK
ROW_TILE = 512
FFN_COLS = 256
CUMSUM_BLK = 256
VMEM_LIMIT = 56 * 1024 * 1024


def _params(*sem):
    return pltpu.CompilerParams(dimension_semantics=sem, vmem_limit_bytes=VMEM_LIMIT)


def _resident(shape, index_map):
    return pl.BlockSpec(shape, index_map, pipeline_mode=pl.Buffered(1))


def _rmsnorm(x, g):
    return x * lax.rsqrt(jnp.mean(x * x, axis=-1, keepdims=True) + EPS) * g


def _dot(a, b):
    return jnp.dot(a, b, preferred_element_type=F32)


def _dot_nt(a, b):
    return lax.dot_general(a, b, (((1,), (1,)), ((), ())), preferred_element_type=F32)


def _split3(x):
    hi = x.astype(BF16)
    r = x - hi.astype(F32)
    mid = r.astype(BF16)
    lo = (r - mid.astype(F32)).astype(BF16)
    return hi, mid, lo


def _inproj_kernel(x_ref, g_ref, w_ref, wft_ref, bf_ref,
                   qa_ref, qb_ref, ka_ref, kb_ref, vat16_ref, vbt16_ref,
                   kat_ref, vat_ref, kbt_ref, vbt_ref, lft_ref):
    h = _rmsnorm(x_ref[...], g_ref[...]).astype(BF16)
    scale = HEAD_DIM ** -0.5

    def proj(c):
        return _dot(h, w_ref[:, c * WIDTH:(c + 1) * WIDTH])

    def put(ref, zt):
        per_seq = zt.shape[1] // ref.shape[0]
        for sq in range(ref.shape[0]):
            ref[sq] = zt[:, sq * per_seq:(sq + 1) * per_seq]

    qa_ref[...] = (proj(0) * (scale * LOG2E)).astype(BF16)
    qb_ref[...] = (proj(3) * (scale * LOG2E)).astype(BF16)
    for c, row_ref, t_ref in ((1, ka_ref, kat_ref), (4, kb_ref, kbt_ref)):
        z = proj(c)
        row_ref[...] = z.astype(BF16)
        put(t_ref, z.T)
    for c, t16_ref, t_ref in ((2, vat16_ref, vat_ref), (5, vbt16_ref, vbt_ref)):
        zt = proj(c).T
        put(t_ref, zt)
        put(t16_ref, zt.astype(BF16))
    put(lft_ref, jax.nn.log_sigmoid(_dot_nt(wft_ref[...], h) + bf_ref[...]))


def _inproj(x, g, w_qkv, w_ft, b_f, seq_len, band_keep):
    n, d = x.shape
    bsz = n // seq_len
    tm = min(ROW_TILE, n)
    tiles_per_seq = max(seq_len // tm, 1)
    seqs_per_tile = max(tm // seq_len, 1)
    cols = tm // seqs_per_tile
    assert band_keep == cols
    row = lambda i: (i, 0)
    fixed = lambda i: (0, 0)
    along = lambda i: (i // tiles_per_seq, 0, i % tiles_per_seq)
    kept = lambda i: (i // tiles_per_seq, 0, 0)
    rows16 = (jax.ShapeDtypeStruct((n, WIDTH), BF16), pl.BlockSpec((tm, WIDTH), row))

    def chan(channels, length, dtype, index_map):
        return (jax.ShapeDtypeStruct((bsz, channels, length), dtype),
                pl.BlockSpec((seqs_per_tile, channels, cols), index_map))

    outs = [rows16] * 4
    outs += [chan(WIDTH, seq_len, BF16, along)] * 2
    outs += [chan(WIDTH, band_keep, F32, kept)] * 2
    outs += [chan(WIDTH, seq_len, F32, along)] * 2
    outs += [chan(HEADS, seq_len, F32, along)]
    return pl.pallas_call(
        _inproj_kernel,
        grid=(n // tm,),
        in_specs=[pl.BlockSpec((tm, d), row),
                  pl.BlockSpec((1, d), fixed),
                  _resident(w_qkv.shape, fixed),
                  pl.BlockSpec(w_ft.shape, fixed),
                  pl.BlockSpec((HEADS, 1), fixed)],
        out_specs=[spec for _, spec in outs],
        out_shape=[shape for shape, _ in outs],
        compiler_params=_params("arbitrary"),
        name="inproj",
    )(x, g, w_qkv, w_ft, b_f)


def _cumsum_kernel(*refs, seg_lens):
    seg_refs = refs[:len(seg_lens)]
    ccol_ref, crow_ref = refs[len(seg_lens):]
    carry_c = jnp.zeros((1, HEADS), F32)
    carry_r = jnp.zeros((HEADS, 1), F32)
    off = 0
    for ref, n in zip(seg_refs, seg_lens):
        for o in range(0, n, CUMSUM_BLK):
            b = min(CUMSUM_BLK, n - o)
            parts = _split3(ref[0, :, o:o + b])
            r = lax.broadcasted_iota(jnp.int32, (b, b), 0)
            c = lax.broadcasted_iota(jnp.int32, (b, b), 1)
            lower = jnp.where(r >= c, 1.0, 0.0).astype(BF16)
            upper = jnp.where(r <= c, 1.0, 0.0).astype(BF16)
            cc = carry_c
            cr = carry_r
            for p in parts:
                cc = cc + _dot_nt(lower, p)
                cr = cr + _dot(p, upper)
            ccol_ref[0, off + o:off + o + b, :] = cc
            crow_ref[0, :, off + o:off + o + b] = cr
            carry_c = cc[b - 1:b, :]
            carry_r = cr[:, b - 1:b]
        off += n


def _cumsum(*segs):
    bsz = segs[0].shape[0]
    seg_lens = tuple(s.shape[2] for s in segs)
    total = sum(seg_lens)
    return pl.pallas_call(
        functools.partial(_cumsum_kernel, seg_lens=seg_lens),
        grid=(bsz,),
        in_specs=[pl.BlockSpec((1, HEADS, n), lambda b: (b, 0, 0)) for n in seg_lens],
        out_specs=[pl.BlockSpec((1, total, HEADS), lambda b: (b, 0, 0)),
                   pl.BlockSpec((1, HEADS, total), lambda b: (b, 0, 0))],
        out_shape=[jax.ShapeDtypeStruct((bsz, total, HEADS), F32),
                   jax.ShapeDtypeStruct((bsz, HEADS, total), F32)],
        compiler_params=_params("parallel"),
        name="cumsum_logf",
    )(*segs)


REL_PAD = 384
DIST_SPAN = 1024


def _band_bias_kernel(tbl_ref, qk_ref, kq_ref):
    j = lax.broadcasted_iota(jnp.int32, (REL_PAD, DIST_SPAN), 1)
    r = lax.broadcasted_iota(jnp.int32, (REL_PAD, DIST_SPAN), 0)
    parts = _split3(tbl_ref[...])

    def by_offset(dist):
        onehot = jnp.where(jnp.clip(dist, -REL_CLIP, REL_CLIP) + REL_CLIP == r, 1.0, 0.0).astype(BF16)
        return sum(_dot(p, onehot) for p in parts) * LOG2E

    def skewed(e_row, rows, cols, shift):
        wide = jnp.broadcast_to(e_row, (rows, DIST_SPAN))
        return pltpu.roll(wide, shift % DIST_SPAN, 1, stride=1, stride_axis=0)[:, :cols]

    e_qk = by_offset((BAND_KEYS - 1) - j)
    e_kq = by_offset(j - (ATT_BLK - 1))
    qc = lax.broadcasted_iota(jnp.int32, (ATT_BLK, BAND_KEYS), 0) // CHUNK
    kc = lax.broadcasted_iota(jnp.int32, (ATT_BLK, BAND_KEYS), 1) // CHUNK
    vis_qk = (kc >= qc) & (kc <= qc + PAST_CHUNKS)
    kc = lax.broadcasted_iota(jnp.int32, (BAND_KEYS, ATT_BLK), 0) // CHUNK
    qc = lax.broadcasted_iota(jnp.int32, (BAND_KEYS, ATT_BLK), 1) // CHUNK
    vis_kq = (kc >= qc) & (kc <= qc + PAST_CHUNKS)
    for h in range(HEADS):
        qk_ref[h] = jnp.where(vis_qk, skewed(e_qk[h:h + 1, :], ATT_BLK, BAND_KEYS, -(ATT_BLK - 1)), NEG)
        kq_ref[h] = jnp.where(vis_kq, skewed(e_kq[h:h + 1, :], BAND_KEYS, ATT_BLK, -(BAND_KEYS - 1)), NEG)


def _band_bias(rel_table):
    tbl = jnp.pad(rel_table, ((0, 0), (0, REL_PAD - REL_SIZE)))
    return pl.pallas_call(
        _band_bias_kernel,
        out_shape=[jax.ShapeDtypeStruct((HEADS, ATT_BLK, BAND_KEYS), F32),
                   jax.ShapeDtypeStruct((HEADS, BAND_KEYS, ATT_BLK), F32)],
        compiler_params=pltpu.CompilerParams(vmem_limit_bytes=VMEM_LIMIT),
        name="band_bias",
    )(tbl)


BAND_BLOCKS = BAND_KEYS // ATT_BLK


def _band_attn_prompt_kernel(q_ref, k_ref, vt_ref, bias_ref, o_ref, s_ref):
    i = pl.program_id(1)

    blocks = []
    for g in range(BAND_BLOCKS):
        j = i - (BAND_BLOCKS - 1) + g
        start = pl.multiple_of(jnp.maximum(j, 0) * ATT_BLK, ATT_BLK)
        blocks.append((start, jnp.where(j >= 0, 0.0, NEG) if g < BAND_BLOCKS - 1 else None))

    def scores(h):
        sl = slice(h * HEAD_DIM, (h + 1) * HEAD_DIM)
        q = q_ref[0, :, sl]
        for g, (start, penalty) in enumerate(blocks):
            s = _dot_nt(k_ref[0, pl.ds(start, ATT_BLK), sl], q) + bias_ref[h, g * ATT_BLK:(g + 1) * ATT_BLK, :]
            s_ref[h % 2, g] = s if penalty is None else s + penalty

    def finish(h):
        sl = slice(h * HEAD_DIM, (h + 1) * HEAD_DIM)
        s = [s_ref[h % 2, g] for g in range(BAND_BLOCKS)]
        m = functools.reduce(jnp.maximum, [jnp.max(x, axis=0, keepdims=True) for x in s])
        l = 0.0
        acc = 0.0
        for x, (start, _) in zip(s, blocks):
            p = jnp.exp2(x - m)
            l = l + jnp.sum(p, axis=0, keepdims=True)
            acc = acc + _dot(vt_ref[0, sl, pl.ds(start, ATT_BLK)], p.astype(BF16))
        o_ref[0, :, sl] = (acc / l).T.astype(BF16)

    scores(0)
    for h in range(HEADS):
        if h + 1 < HEADS:
            scores(h + 1)
        finish(h)


def _band_attn_prompt(q, k, vt, bias_kq):
    bsz, s, _ = q.shape
    qblk = pl.BlockSpec((1, ATT_BLK, WIDTH), lambda b, i: (b, i, 0))
    return pl.pallas_call(
        _band_attn_prompt_kernel,
        grid=(bsz, s // ATT_BLK),
        in_specs=[qblk,
                  pl.BlockSpec((1, s, WIDTH), lambda b, i: (b, 0, 0)),
                  pl.BlockSpec((1, WIDTH, s), lambda b, i: (b, 0, 0)),
                  _resident(bias_kq.shape, lambda b, i: (0, 0, 0))],
        out_specs=qblk,
        out_shape=jax.ShapeDtypeStruct(q.shape, BF16),
        scratch_shapes=[pltpu.VMEM((2, BAND_BLOCKS, ATT_BLK, ATT_BLK), F32)],
        compiler_params=_params("parallel", "parallel"),
        name="band_attn_prompt",
    )(q, k, vt, bias_kq)


def _decay_lanes(c, key_side):
    n = c.shape[0]
    lane = lax.broadcasted_iota(jnp.int32, (n, HEAD_DIM), 1)
    p0, p1, p2 = [p.astype(F32) for p in _split3(jnp.broadcast_to(c, (n, HEAD_DIM)))]
    if key_side:
        out = jnp.where(lane == 0, -p0, jnp.where(lane == 1, -p1, jnp.where(lane == 2, -p2,
              jnp.where(lane < 6, 1.0, 0.0))))
    else:
        out = jnp.where(lane < 3, 1.0, jnp.where(lane == 3, p0, jnp.where(lane == 4, p1,
              jnp.where(lane == 5, p2, 0.0))))
    return out.astype(BF16)


PAD_DIM = 2 * HEAD_DIM


def _fox_attn_prompt_kernel(q_ref, k_ref, vt_ref, ccol_ref, o_ref,
                            kp_ref, qp_ref, s_ref, m_ref, l_ref, acc_ref):
    i = pl.program_id(1)
    qstart = pl.multiple_of(i * ATT_BLK, ATT_BLK)

    @pl.when(i == 0)
    def _():
        for h in range(HEADS):
            sl = slice(h * HEAD_DIM, (h + 1) * HEAD_DIM)
            extra = _decay_lanes(ccol_ref[0, :, h:h + 1] * LOG2E, key_side=True)
            kp_ref[:, h * PAD_DIM:(h + 1) * PAD_DIM] = jnp.concatenate([k_ref[0, :, sl], extra], axis=1)

    for h in range(HEADS):
        sl = slice(h * HEAD_DIM, (h + 1) * HEAD_DIM)
        extra = _decay_lanes(ccol_ref[0, pl.ds(qstart, ATT_BLK), h:h + 1] * LOG2E, key_side=False)
        qp_ref[:, h * PAD_DIM:(h + 1) * PAD_DIM] = jnp.concatenate([q_ref[0, :, sl], extra], axis=1)

    key = lax.broadcasted_iota(jnp.int32, (ATT_BLK, ATT_BLK), 0)
    qry = lax.broadcasted_iota(jnp.int32, (ATT_BLK, ATT_BLK), 1)
    causal = key <= qry
    m_ref[...] = jnp.full(m_ref.shape, NEG, F32)
    l_ref[...] = jnp.zeros(l_ref.shape, F32)
    acc_ref[...] = jnp.zeros(acc_ref.shape, F32)

    def scores(j, h):
        start = pl.multiple_of(j * ATT_BLK, ATT_BLK)
        pad = slice(h * PAD_DIM, (h + 1) * PAD_DIM)
        return _dot_nt(kp_ref[pl.ds(start, ATT_BLK), pad], qp_ref[:, pad])

    def update(j, h, s, masked):
        start = pl.multiple_of(j * ATT_BLK, ATT_BLK)
        if masked:
            s = jnp.where(causal, s, NEG)
        m_old = m_ref[h]
        m_new = jnp.maximum(m_old, jnp.max(s, axis=0, keepdims=True))
        alpha = jnp.exp2(m_old - m_new)
        p = jnp.exp2(s - m_new)
        l_ref[h] = alpha * l_ref[h] + jnp.sum(p, axis=0, keepdims=True)
        vt = vt_ref[0, h * HEAD_DIM:(h + 1) * HEAD_DIM, pl.ds(start, ATT_BLK)]
        acc_ref[h] = alpha * acc_ref[h] + _dot(vt, p.astype(BF16))
        m_ref[h] = m_new

    for h in range(HEADS):
        s_ref[h] = scores(0, h)

    def body(j, carry):
        for h in range(HEADS):
            s = s_ref[h]
            s_ref[h] = scores(j + 1, h)
            update(j, h, s, masked=False)
        return carry

    lax.fori_loop(0, i, body, 0)
    for h in range(HEADS):
        update(i, h, s_ref[h], masked=True)
    for h in range(HEADS):
        sl = slice(h * HEAD_DIM, (h + 1) * HEAD_DIM)
        o_ref[0, :, sl] = (acc_ref[h] / l_ref[h]).T.astype(BF16)


def _fox_attn_prompt(q, k, vt, ccol):
    bsz, s, _ = q.shape
    qblk = pl.BlockSpec((1, ATT_BLK, WIDTH), lambda b, i: (b, i, 0))
    return pl.pallas_call(
        _fox_attn_prompt_kernel,
        grid=(bsz, s // ATT_BLK),
        in_specs=[qblk,
                  pl.BlockSpec((1, s, WIDTH), lambda b, i: (b, 0, 0)),
                  pl.BlockSpec((1, WIDTH, s), lambda b, i: (b, 0, 0)),
                  pl.BlockSpec((1, s, HEADS), lambda b, i: (b, 0, 0))],
        out_specs=qblk,
        out_shape=jax.ShapeDtypeStruct(q.shape, BF16),
        scratch_shapes=[pltpu.VMEM((s, HEADS * PAD_DIM), BF16),
                        pltpu.VMEM((ATT_BLK, HEADS * PAD_DIM), BF16),
                        pltpu.VMEM((HEADS, ATT_BLK, ATT_BLK), F32),
                        pltpu.VMEM((HEADS, 1, ATT_BLK), F32),
                        pltpu.VMEM((HEADS, 1, ATT_BLK), F32),
                        pltpu.VMEM((HEADS, HEAD_DIM, ATT_BLK), F32)],
        compiler_params=_params("parallel", "arbitrary"),
        name="fox_attn_prompt",
    )(q, k, vt, ccol)


def _sample_attention(q, s_past, s_new, vct, vnt):
    m = jnp.maximum(jnp.max(s_past, axis=-1, keepdims=True), jnp.max(s_new, axis=-1, keepdims=True))
    p_past = jnp.exp2(s_past - m)
    p_new = jnp.exp2(s_new - m)
    l = jnp.sum(p_past, axis=-1, keepdims=True) + jnp.sum(p_new, axis=-1, keepdims=True)
    acc = _dot_nt(p_past.astype(BF16), vct) + _dot_nt(p_new.astype(BF16), vnt)
    return (acc / l).astype(BF16)


def _band_attn_sample_kernel(q_ref, kct_ref, vct_ref, kn_ref, vnt_ref, bias_ref, o_ref):
    t = q_ref.shape[1]
    past = kct_ref.shape[2]
    for h in range(HEADS):
        sl = slice(h * HEAD_DIM, (h + 1) * HEAD_DIM)
        q = q_ref[0, :, sl]
        s_past = _dot(q, kct_ref[0, sl, :].astype(BF16)) + bias_ref[h, 0:t, BAND - past:BAND]
        s_new = _dot_nt(q, kn_ref[0, :, sl]) + bias_ref[h, 0:t, BAND:BAND + t]
        o_ref[0, :, sl] = _sample_attention(q, s_past, s_new, vct_ref[0, sl, :].astype(BF16), vnt_ref[0, sl, :])


def _band_attn_sample(q, k_cache_t, v_cache_t, k_new, v_new_t, bias):
    bsz, t, _ = q.shape
    past = k_cache_t.shape[2]
    new = pl.BlockSpec((1, t, WIDTH), lambda b: (b, 0, 0))
    new_t = pl.BlockSpec((1, WIDTH, t), lambda b: (b, 0, 0))
    old_t = pl.BlockSpec((1, WIDTH, past), lambda b: (b, 0, 0))
    return pl.pallas_call(
        _band_attn_sample_kernel,
        grid=(bsz,),
        in_specs=[new, old_t, old_t, new, new_t, _resident(bias.shape, lambda b: (0, 0, 0))],
        out_specs=new,
        out_shape=jax.ShapeDtypeStruct(q.shape, BF16),
        compiler_params=_params("parallel"),
        name="band_attn_sample",
    )(q, k_cache_t, v_cache_t, k_new, v_new_t, bias)


def _fox_attn_sample_kernel(q_ref, kct_ref, vct_ref, kn_ref, vnt_ref, ccol_ref, crow_ref, o_ref):
    t = q_ref.shape[1]
    past = kct_ref.shape[2]
    row = lax.broadcasted_iota(jnp.int32, (t, t), 0)
    col = lax.broadcasted_iota(jnp.int32, (t, t), 1)
    causal = col <= row
    for h in range(HEADS):
        sl = slice(h * HEAD_DIM, (h + 1) * HEAD_DIM)
        q = q_ref[0, :, sl]
        cq = ccol_ref[0, :, h:h + 1]
        s_past = _dot(q, kct_ref[0, sl, :].astype(BF16)) + (cq - crow_ref[0, h:h + 1, 0:past]) * LOG2E
        s_new = _dot_nt(q, kn_ref[0, :, sl]) + (cq - crow_ref[0, h:h + 1, past:past + t]) * LOG2E
        s_new = jnp.where(causal, s_new, NEG)
        o_ref[0, :, sl] = _sample_attention(q, s_past, s_new, vct_ref[0, sl, :].astype(BF16), vnt_ref[0, sl, :])


def _fox_attn_sample(q, k_cache_t, v_cache_t, k_new, v_new_t, ccol, crow):
    bsz, t, _ = q.shape
    past = k_cache_t.shape[2]
    assert past % t == 0
    new = pl.BlockSpec((1, t, WIDTH), lambda b: (b, 0, 0))
    new_t = pl.BlockSpec((1, WIDTH, t), lambda b: (b, 0, 0))
    old_t = pl.BlockSpec((1, WIDTH, past), lambda b: (b, 0, 0))
    return pl.pallas_call(
        _fox_attn_sample_kernel,
        grid=(bsz,),
        in_specs=[new, old_t, old_t, new, new_t,
                  pl.BlockSpec((1, t, HEADS), lambda b: (b, past // t, 0)),
                  pl.BlockSpec((1, HEADS, past + t), lambda b: (b, 0, 0))],
        out_specs=new,
        out_shape=jax.ShapeDtypeStruct(q.shape, BF16),
        compiler_params=_params("parallel"),
        name="fox_attn_sample",
    )(q, k_cache_t, v_cache_t, k_new, v_new_t, ccol, crow)


def _postmix_kernel(x_ref, oa_ref, ob_ref, gpre_ref, gpost_ref, wg_ref, wpa_ref, wpb_ref, wout_ref, y_ref):
    x = x_ref[...]
    d = x.shape[-1]
    h = _rmsnorm(x, gpre_ref[...]).astype(BF16)
    gate_a = jax.nn.sigmoid(_dot(h, wg_ref[:, :d]))
    gate_b = jax.nn.sigmoid(_dot(h, wg_ref[:, d:]))
    merged = gate_a * _dot(oa_ref[...], wpa_ref[...]) + gate_b * _dot(ob_ref[...], wpb_ref[...])
    y_ref[...] = x + _rmsnorm(_dot(merged.astype(BF16), wout_ref[...]), gpost_ref[...])


def _postmix(x, oa, ob, g_pre, g_post, w_g, w_pa, w_pb, w_out):
    n, d = x.shape
    tm = min(ROW_TILE, n)
    row = lambda i: (i, 0)
    fixed = lambda i: (0, 0)
    return pl.pallas_call(
        _postmix_kernel,
        grid=(n // tm,),
        in_specs=[pl.BlockSpec((tm, d), row),
                  pl.BlockSpec((tm, WIDTH), row),
                  pl.BlockSpec((tm, WIDTH), row),
                  pl.BlockSpec((1, d), fixed),
                  pl.BlockSpec((1, d), fixed),
                  _resident(w_g.shape, fixed),
                  _resident(w_pa.shape, fixed),
                  _resident(w_pb.shape, fixed),
                  _resident(w_out.shape, fixed)],
        out_specs=pl.BlockSpec((tm, d), row),
        out_shape=jax.ShapeDtypeStruct((n, d), F32),
        compiler_params=_params("parallel"),
        name="postmix",
    )(x, oa, ob, g_pre, g_post, w_g, w_pa, w_pb, w_out)


HEADER = 8


def _ffn_kernel(x_ref, st_ref, gpre_ref, gpost_ref, wup_ref, cw_ref, cb_ref, wdn_ref,
                y_ref, nst_ref, hist_ref, ext_ref, h_ref, f_ref, *, nseg, seg_len):
    @pl.when(pl.program_id(1) == 0)
    def _():
        hist_ref[...] = st_ref[...]

    d_ff = wdn_ref.shape[0]
    h_ref[...] = _rmsnorm(x_ref[...], gpre_ref[...]).astype(BF16)
    lo, hi = HEADER - (CONV_W - 1), HEADER
    n_chunks = d_ff // FFN_COLS

    def parts(c):
        for part in range(2):
            yield (slice(part * d_ff + c * FFN_COLS, part * d_ff + (c + 1) * FFN_COLS),
                   slice(part * FFN_COLS, (part + 1) * FFN_COLS))

    def up(c):
        ext = ext_ref.at[c % 2]
        for cols, dst in parts(c):
            u = _dot(h_ref[...], wup_ref[:, cols])
            ext[:, lo:hi, dst] = hist_ref[:, :, cols]
            for s in range(nseg):
                ext[s, hi:hi + seg_len, dst] = u[s * seg_len:(s + 1) * seg_len, :]
            hist_ref[:, :, cols] = ext[:, lo + seg_len:hi + seg_len, dst]

    def down(c):
        ext = ext_ref.at[c % 2]
        halves = []
        for cols, dst in parts(c):
            y = cb_ref[:, cols]
            for tap in range(CONV_W):
                y = y + ext[:, lo + tap:lo + tap + seg_len, dst] * cw_ref[tap:tap + 1, cols]
            halves.append(y)
        act = (jax.nn.gelu(halves[0]) * halves[1]).reshape(nseg * seg_len, FFN_COLS)
        return _dot(act.astype(BF16), wdn_ref[c * FFN_COLS:(c + 1) * FFN_COLS, :])

    up(0)
    for c in range(n_chunks):
        if c + 1 < n_chunks:
            up(c + 1)
        if c == 0:
            f_ref[...] = down(c)
        else:
            f_ref[...] += down(c)
    y_ref[...] = x_ref[...] + _rmsnorm(f_ref[...], gpost_ref[...])
    nst_ref[...] = hist_ref[...]


def _ffn(x, state, g_pre, g_post, w_up, conv_w, conv_b, w_down):
    bsz, s, d = x.shape
    up = w_up.shape[1]
    tm = min(ROW_TILE, bsz * s)
    if s >= tm:
        nseg, seg_len, tiles = 1, tm, s // tm
    else:
        nseg, seg_len, tiles = tm // s, s, 1
    outer = bsz // nseg
    row = lambda o, t: (o * tiles + t, 0)
    fixed = lambda o, t: (0, 0)
    st_spec = pl.BlockSpec((nseg, CONV_W - 1, up), lambda o, t: (o, 0, 0))
    y, new_state = pl.pallas_call(
        functools.partial(_ffn_kernel, nseg=nseg, seg_len=seg_len),
        grid=(outer, tiles),
        in_specs=[pl.BlockSpec((tm, d), row),
                  st_spec,
                  pl.BlockSpec((1, d), fixed),
                  pl.BlockSpec((1, d), fixed),
                  _resident(w_up.shape, fixed),
                  pl.BlockSpec(conv_w.shape, fixed),
                  pl.BlockSpec((1, up), fixed),
                  _resident(w_down.shape, fixed)],
        out_specs=[pl.BlockSpec((tm, d), row), st_spec],
        out_shape=[jax.ShapeDtypeStruct((bsz * s, d), F32),
                   jax.ShapeDtypeStruct(state.shape, F32)],
        scratch_shapes=[pltpu.VMEM((nseg, CONV_W - 1, up), F32),
                        pltpu.VMEM((2, nseg, HEADER + seg_len, 2 * FFN_COLS), F32),
                        pltpu.VMEM((tm, d), BF16),
                        pltpu.VMEM((tm, d), F32)],
        compiler_params=_params("arbitrary", "arbitrary"),
        name="conv_ffn",
    )(x.reshape(bsz * s, d), state, g_pre, g_post, w_up, conv_w, conv_b, w_down)
    return y.reshape(bsz, s, d), new_state


def _layer(x, caches, conv_state, bias, w):
    bsz, s, d = x.shape
    n = bsz * s
    x2 = x.reshape(n, d)
    keep = min(BAND, s)
    (qa, qb, ka16, kb16, vat16, vbt16, kat, vat, kbt, vbt, lft) = _inproj(
        x2, w["g_pre_mix"], w["w_qkv"], w["w_ft"], w["b_f"], seq_len=s, band_keep=keep)
    seq = lambda a: a.reshape(bsz, s, a.shape[-1])
    if caches is None:
        ccol, _ = _cumsum(lft)
        oa = _band_attn_prompt(seq(qa), seq(ka16), vat16, bias[1])
        ob = _fox_attn_prompt(seq(qb), seq(kb16), vbt16, ccol)
    else:
        ckat, cvat, ckbt, cvbt, clft = caches
        ccol, crow = _cumsum(clft, lft)
        oa = _band_attn_sample(seq(qa), ckat, cvat, seq(ka16), vat16, bias[0])
        ob = _fox_attn_sample(seq(qb), ckbt, cvbt, seq(kb16), vbt16, ccol, crow)
    x1 = _postmix(x2, oa.reshape(n, WIDTH), ob.reshape(n, WIDTH), w["g_pre_mix"], w["g_post_mix"],
                  w["w_g"], w["w_pa"], w["w_pb"], w["w_out"])
    y, new_conv = _ffn(x1.reshape(bsz, s, d), conv_state, w["g_pre_ffn"], w["g_post_ffn"],
                       w["w_up"], w["conv_w"], w["conv_b"], w["w_down"])
    heads = lambda a: a.reshape(bsz, HEADS, HEAD_DIM, a.shape[-1]).transpose(0, 3, 1, 2)
    return y, (heads(kat), heads(vat), heads(kbt), heads(vbt), lft.transpose(0, 2, 1), new_conv)


def _channel_major(cache):
    bsz, past = cache.shape[:2]
    return cache.transpose(0, 2, 3, 1).reshape(bsz, WIDTH, past)


def kernel(x_prompt, x_sample, cache_k_a, cache_v_a, cache_k_b, cache_v_b, cache_logf_b, state_conv_ffn,
           g_pre_mix, g_post_mix, g_pre_ffn, g_post_ffn, w_in, b_f, rel_table, w_proj_a, w_proj_b, w_out,
           w_up, conv_w, conv_b, w_down):
    depth = w_in.shape[0]
    up = w_up.shape[-1]
    x_p, x_s = x_prompt, x_sample
    p_states, s_states = [], []
    for l in range(depth):
        w = {
            "g_pre_mix": g_pre_mix[l][None], "g_post_mix": g_post_mix[l][None],
            "g_pre_ffn": g_pre_ffn[l][None], "g_post_ffn": g_post_ffn[l][None],
            "w_qkv": w_in[l][:, :6 * WIDTH].astype(BF16),
            "w_ft": w_in[l][:, 6 * WIDTH:6 * WIDTH + HEADS].T.astype(BF16),
            "w_g": w_in[l][:, 6 * WIDTH + HEADS:].astype(BF16),
            "b_f": b_f[l][:, None],
            "w_pa": w_proj_a[l].astype(BF16), "w_pb": w_proj_b[l].astype(BF16),
            "w_out": w_out[l].astype(BF16), "w_up": w_up[l].astype(BF16),
            "conv_w": conv_w[l], "conv_b": conv_b[l][None], "w_down": w_down[l].astype(BF16),
        }
        bias = _band_bias(rel_table[l])
        zero_state = jnp.zeros((x_p.shape[0], CONV_W - 1, up), F32)
        x_p, (ka, va, kb, vb, lf, cv) = _layer(x_p, None, zero_state, bias, w)
        p_states.append((ka, va, kb, vb, lf, cv))
        caches = (_channel_major(cache_k_a[l]), _channel_major(cache_v_a[l]), _channel_major(cache_k_b[l]),
                  _channel_major(cache_v_b[l]), cache_logf_b[l].transpose(0, 2, 1))
        x_s, st = _layer(x_s, caches, state_conv_ffn[l], bias, w)
        s_states.append(st)
    stack = lambda states: [jnp.stack(s) for s in zip(*states)]
    return (x_p, x_s, *stack(p_states), *stack(s_states))
```

```python
import functools

import jax
import jax.numpy as jnp
from jax import lax
from jax.experimental import pallas as pl
from jax.experimental.pallas import tpu as pltpu

F32, BF16 = jnp.float32, jnp.bfloat16

HEADS = 8
HEAD_DIM = 64
WIDTH = HEADS * HEAD_DIM
CHUNK = 64
PAST_CHUNKS = 8
BAND = PAST_CHUNKS * CHUNK
REL_CLIP = 128
REL_SIZE = 2 * REL_CLIP + 1
CONV_W = 3
EPS = 1e-6
NEG = -1e30
LOG2E = 1.4426950408889634

SUBLANES = 8
ATT_BLK = 256
BAND_KEYS = 3 * ATT_BLK
ROW_TILE = 512
FFN_COLS = 256
CUMSUM_BLK = 256
VMEM_LIMIT = 56 * 1024 * 1024


def _params(*sem):
    return pltpu.CompilerParams(dimension_semantics=sem, vmem_limit_bytes=VMEM_LIMIT)


def _resident(shape, index_map):
    return pl.BlockSpec(shape, index_map, pipeline_mode=pl.Buffered(1))


def _rmsnorm(x, g):
    return x * lax.rsqrt(jnp.mean(x * x, axis=-1, keepdims=True) + EPS) * g


def _dot(a, b):
    return jnp.dot(a, b, preferred_element_type=F32)


def _dot_nt(a, b):
    return lax.dot_general(a, b, (((1,), (1,)), ((), ())), preferred_element_type=F32)


def _split3(x):
    hi = x.astype(BF16)
    r = x - hi.astype(F32)
    mid = r.astype(BF16)
    lo = (r - mid.astype(F32)).astype(BF16)
    return hi, mid, lo


def _inproj_kernel(x_ref, g_ref, w_ref, wft_ref, bf_ref,
                   qa_ref, qb_ref, ka_ref, kb_ref, vat16_ref, vbt16_ref,
                   kat_ref, vat_ref, kbt_ref, vbt_ref, lft_ref):
    h = _rmsnorm(x_ref[...], g_ref[...]).astype(BF16)
    scale = HEAD_DIM ** -0.5

    def proj(c):
        return _dot(h, w_ref[:, c * WIDTH:(c + 1) * WIDTH])

    def put(ref, zt):
        per_seq = zt.shape[1] // ref.shape[0]
        for sq in range(ref.shape[0]):
            ref[sq] = zt[:, sq * per_seq:(sq + 1) * per_seq]

    qa_ref[...] = (proj(0) * (scale * LOG2E)).astype(BF16)
    qb_ref[...] = (proj(3) * (scale * LOG2E)).astype(BF16)
    for c, row_ref, t_ref in ((1, ka_ref, kat_ref), (4, kb_ref, kbt_ref)):
        z = proj(c)
        row_ref[...] = z.astype(BF16)
        put(t_ref, z.T)
    for c, t16_ref, t_ref in ((2, vat16_ref, vat_ref), (5, vbt16_ref, vbt_ref)):
        zt = proj(c).T
        put(t_ref, zt)
        put(t16_ref, zt.astype(BF16))
    put(lft_ref, jax.nn.log_sigmoid(_dot_nt(wft_ref[...], h) + bf_ref[...]))


def _inproj(x, g, w_qkv, w_ft, b_f, seq_len, band_keep):
    n, d = x.shape
    bsz = n // seq_len
    tm = min(ROW_TILE, n)
    tiles_per_seq = max(seq_len // tm, 1)
    seqs_per_tile = max(tm // seq_len, 1)
    cols = tm // seqs_per_tile
    assert band_keep == cols
    row = lambda i: (i, 0)
    fixed = lambda i: (0, 0)
    along = lambda i: (i // tiles_per_seq, 0, i % tiles_per_seq)
    kept = lambda i: (i // tiles_per_seq, 0, 0)
    rows16 = (jax.ShapeDtypeStruct((n, WIDTH), BF16), pl.BlockSpec((tm, WIDTH), row))

    def chan(channels, length, dtype, index_map):
        return (jax.ShapeDtypeStruct((bsz, channels, length), dtype),
                pl.BlockSpec((seqs_per_tile, channels, cols), index_map))

    outs = [rows16] * 4
    outs += [chan(WIDTH, seq_len, BF16, along)] * 2
    outs += [chan(WIDTH, band_keep, F32, kept)] * 2
    outs += [chan(WIDTH, seq_len, F32, along)] * 2
    outs += [chan(HEADS, seq_len, F32, along)]
    return pl.pallas_call(
        _inproj_kernel,
        grid=(n // tm,),
        in_specs=[pl.BlockSpec((tm, d), row),
                  pl.BlockSpec((1, d), fixed),
                  _resident(w_qkv.shape, fixed),
                  pl.BlockSpec(w_ft.shape, fixed),
                  pl.BlockSpec((HEADS, 1), fixed)],
        out_specs=[spec for _, spec in outs],
        out_shape=[shape for shape, _ in outs],
        compiler_params=_params("arbitrary"),
        name="inproj",
    )(x, g, w_qkv, w_ft, b_f)


def _cumsum_kernel(*refs, seg_lens):
    seg_refs = refs[:len(seg_lens)]
    ccol_ref, crow_ref = refs[len(seg_lens):]
    carry_c = jnp.zeros((1, HEADS), F32)
    carry_r = jnp.zeros((HEADS, 1), F32)
    off = 0
    for ref, n in zip(seg_refs, seg_lens):
        for o in range(0, n, CUMSUM_BLK):
            b = min(CUMSUM_BLK, n - o)
            parts = _split3(ref[0, :, o:o + b])
            r = lax.broadcasted_iota(jnp.int32, (b, b), 0)
            c = lax.broadcasted_iota(jnp.int32, (b, b), 1)
            lower = jnp.where(r >= c, 1.0, 0.0).astype(BF16)
            upper = jnp.where(r <= c, 1.0, 0.0).astype(BF16)
            cc = carry_c
            cr = carry_r
            for p in parts:
                cc = cc + _dot_nt(lower, p)
                cr = cr + _dot(p, upper)
            ccol_ref[0, off + o:off + o + b, :] = cc
            crow_ref[0, :, off + o:off + o + b] = cr
            carry_c = cc[b - 1:b, :]
            carry_r = cr[:, b - 1:b]
        off += n


def _cumsum(*segs):
    bsz = segs[0].shape[0]
    seg_lens = tuple(s.shape[2] for s in segs)
    total = sum(seg_lens)
    return pl.pallas_call(
        functools.partial(_cumsum_kernel, seg_lens=seg_lens),
        grid=(bsz,),
        in_specs=[pl.BlockSpec((1, HEADS, n), lambda b: (b, 0, 0)) for n in seg_lens],
        out_specs=[pl.BlockSpec((1, total, HEADS), lambda b: (b, 0, 0)),
                   pl.BlockSpec((1, HEADS, total), lambda b: (b, 0, 0))],
        out_shape=[jax.ShapeDtypeStruct((bsz, total, HEADS), F32),
                   jax.ShapeDtypeStruct((bsz, HEADS, total), F32)],
        compiler_params=_params("parallel"),
        name="cumsum_logf",
    )(*segs)


REL_PAD = 384
DIST_SPAN = 1024


def _band_bias_kernel(tbl_ref, qk_ref, kq_ref):
    j = lax.broadcasted_iota(jnp.int32, (REL_PAD, DIST_SPAN), 1)
    r = lax.broadcasted_iota(jnp.int32, (REL_PAD, DIST_SPAN), 0)
    parts = _split3(tbl_ref[...])

    def by_offset(dist):
        onehot = jnp.where(jnp.clip(dist, -REL_CLIP, REL_CLIP) + REL_CLIP == r, 1.0, 0.0).astype(BF16)
        return sum(_dot(p, onehot) for p in parts) * LOG2E

    def skewed(e_row, rows, cols, shift):
        wide = jnp.broadcast_to(e_row, (rows, DIST_SPAN))
        return pltpu.roll(wide, shift % DIST_SPAN, 1, stride=1, stride_axis=0)[:, :cols]

    e_qk = by_offset((BAND_KEYS - 1) - j)
    e_kq = by_offset(j - (ATT_BLK - 1))
    qc = lax.broadcasted_iota(jnp.int32, (ATT_BLK, BAND_KEYS), 0) // CHUNK
    kc = lax.broadcasted_iota(jnp.int32, (ATT_BLK, BAND_KEYS), 1) // CHUNK
    vis_qk = (kc >= qc) & (kc <= qc + PAST_CHUNKS)
    kc = lax.broadcasted_iota(jnp.int32, (BAND_KEYS, ATT_BLK), 0) // CHUNK
    qc = lax.broadcasted_iota(jnp.int32, (BAND_KEYS, ATT_BLK), 1) // CHUNK
    vis_kq = (kc >= qc) & (kc <= qc + PAST_CHUNKS)
    for h in range(HEADS):
        qk_ref[h] = jnp.where(vis_qk, skewed(e_qk[h:h + 1, :], ATT_BLK, BAND_KEYS, -(ATT_BLK - 1)), NEG)
        kq_ref[h] = jnp.where(vis_kq, skewed(e_kq[h:h + 1, :], BAND_KEYS, ATT_BLK, -(BAND_KEYS - 1)), NEG)


def _band_bias(rel_table):
    tbl = jnp.pad(rel_table, ((0, 0), (0, REL_PAD - REL_SIZE)))
    return pl.pallas_call(
        _band_bias_kernel,
        out_shape=[jax.ShapeDtypeStruct((HEADS, ATT_BLK, BAND_KEYS), F32),
                   jax.ShapeDtypeStruct((HEADS, BAND_KEYS, ATT_BLK), F32)],
        compiler_params=pltpu.CompilerParams(vmem_limit_bytes=VMEM_LIMIT),
        name="band_bias",
    )(tbl)


BAND_BLOCKS = BAND_KEYS // ATT_BLK


def _band_attn_prompt_kernel(q_ref, k_ref, vt_ref, bias_ref, o_ref, s_ref):
    i = pl.program_id(1)

    blocks = []
    for g in range(BAND_BLOCKS):
        j = i - (BAND_BLOCKS - 1) + g
        start = pl.multiple_of(jnp.maximum(j, 0) * ATT_BLK, ATT_BLK)
        blocks.append((start, jnp.where(j >= 0, 0.0, NEG) if g < BAND_BLOCKS - 1 else None))

    def scores(h):
        sl = slice(h * HEAD_DIM, (h + 1) * HEAD_DIM)
        q = q_ref[0, :, sl]
        for g, (start, penalty) in enumerate(blocks):
            s = _dot_nt(k_ref[0, pl.ds(start, ATT_BLK), sl], q) + bias_ref[h, g * ATT_BLK:(g + 1) * ATT_BLK, :]
            s_ref[h % 2, g] = s if penalty is None else s + penalty

    def finish(h):
        sl = slice(h * HEAD_DIM, (h + 1) * HEAD_DIM)
        s = [s_ref[h % 2, g] for g in range(BAND_BLOCKS)]
        m = functools.reduce(jnp.maximum, [jnp.max(x, axis=0, keepdims=True) for x in s])
        l = 0.0
        acc = 0.0
        for x, (start, _) in zip(s, blocks):
            p = jnp.exp2(x - m)
            l = l + jnp.sum(p, axis=0, keepdims=True)
            acc = acc + _dot(vt_ref[0, sl, pl.ds(start, ATT_BLK)], p.astype(BF16))
        o_ref[0, :, sl] = (acc / l).T.astype(BF16)

    scores(0)
    for h in range(HEADS):
        if h + 1 < HEADS:
            scores(h + 1)
        finish(h)


def _band_attn_prompt(q, k, vt, bias_kq):
    bsz, s, _ = q.shape
    qblk = pl.BlockSpec((1, ATT_BLK, WIDTH), lambda b, i: (b, i, 0))
    return pl.pallas_call(
        _band_attn_prompt_kernel,
        grid=(bsz, s // ATT_BLK),
        in_specs=[qblk,
                  pl.BlockSpec((1, s, WIDTH), lambda b, i: (b, 0, 0)),
                  pl.BlockSpec((1, WIDTH, s), lambda b, i: (b, 0, 0)),
                  _resident(bias_kq.shape, lambda b, i: (0, 0, 0))],
        out_specs=qblk,
        out_shape=jax.ShapeDtypeStruct(q.shape, BF16),
        scratch_shapes=[pltpu.VMEM((2, BAND_BLOCKS, ATT_BLK, ATT_BLK), F32)],
        compiler_params=_params("parallel", "parallel"),
        name="band_attn_prompt",
    )(q, k, vt, bias_kq)


def _decay_lanes(c, key_side):
    n = c.shape[0]
    lane = lax.broadcasted_iota(jnp.int32, (n, HEAD_DIM), 1)
    p0, p1, p2 = [p.astype(F32) for p in _split3(jnp.broadcast_to(c, (n, HEAD_DIM)))]
    if key_side:
        out = jnp.where(lane == 0, -p0, jnp.where(lane == 1, -p1, jnp.where(lane == 2, -p2,
              jnp.where(lane < 6, 1.0, 0.0))))
    else:
        out = jnp.where(lane < 3, 1.0, jnp.where(lane == 3, p0, jnp.where(lane == 4, p1,
              jnp.where(lane == 5, p2, 0.0))))
    return out.astype(BF16)


PAD_DIM = 2 * HEAD_DIM


def _fox_attn_prompt_kernel(q_ref, k_ref, vt_ref, ccol_ref, o_ref,
                            kp_ref, qp_ref, s_ref, m_ref, l_ref, acc_ref):
    i = pl.program_id(1)
    qstart = pl.multiple_of(i * ATT_BLK, ATT_BLK)

    @pl.when(i == 0)
    def _():
        for h in range(HEADS):
            sl = slice(h * HEAD_DIM, (h + 1) * HEAD_DIM)
            extra = _decay_lanes(ccol_ref[0, :, h:h + 1] * LOG2E, key_side=True)
            kp_ref[:, h * PAD_DIM:(h + 1) * PAD_DIM] = jnp.concatenate([k_ref[0, :, sl], extra], axis=1)

    for h in range(HEADS):
        sl = slice(h * HEAD_DIM, (h + 1) * HEAD_DIM)
        extra = _decay_lanes(ccol_ref[0, pl.ds(qstart, ATT_BLK), h:h + 1] * LOG2E, key_side=False)
        qp_ref[:, h * PAD_DIM:(h + 1) * PAD_DIM] = jnp.concatenate([q_ref[0, :, sl], extra], axis=1)

    key = lax.broadcasted_iota(jnp.int32, (ATT_BLK, ATT_BLK), 0)
    qry = lax.broadcasted_iota(jnp.int32, (ATT_BLK, ATT_BLK), 1)
    causal = key <= qry
    m_ref[...] = jnp.full(m_ref.shape, NEG, F32)
    l_ref[...] = jnp.zeros(l_ref.shape, F32)
    acc_ref[...] = jnp.zeros(acc_ref.shape, F32)

    def scores(j, h):
        start = pl.multiple_of(j * ATT_BLK, ATT_BLK)
        pad = slice(h * PAD_DIM, (h + 1) * PAD_DIM)
        return _dot_nt(kp_ref[pl.ds(start, ATT_BLK), pad], qp_ref[:, pad])

    def update(j, h, s, masked):
        start = pl.multiple_of(j * ATT_BLK, ATT_BLK)
        if masked:
            s = jnp.where(causal, s, NEG)
        m_old = m_ref[h]
        m_new = jnp.maximum(m_old, jnp.max(s, axis=0, keepdims=True))
        alpha = jnp.exp2(m_old - m_new)
        p = jnp.exp2(s - m_new)
        l_ref[h] = alpha * l_ref[h] + jnp.sum(p, axis=0, keepdims=True)
        vt = vt_ref[0, h * HEAD_DIM:(h + 1) * HEAD_DIM, pl.ds(start, ATT_BLK)]
        acc_ref[h] = alpha * acc_ref[h] + _dot(vt, p.astype(BF16))
        m_ref[h] = m_new

    for h in range(HEADS):
        s_ref[h] = scores(0, h)

    def body(j, carry):
        for h in range(HEADS):
            s = s_ref[h]
            s_ref[h] = scores(j + 1, h)
            update(j, h, s, masked=False)
        return carry

    lax.fori_loop(0, i, body, 0)
    for h in range(HEADS):
        update(i, h, s_ref[h], masked=True)
    for h in range(HEADS):
        sl = slice(h * HEAD_DIM, (h + 1) * HEAD_DIM)
        o_ref[0, :, sl] = (acc_ref[h] / l_ref[h]).T.astype(BF16)


def _fox_attn_prompt(q, k, vt, ccol):
    bsz, s, _ = q.shape
    qblk = pl.BlockSpec((1, ATT_BLK, WIDTH), lambda b, i: (b, i, 0))
    return pl.pallas_call(
        _fox_attn_prompt_kernel,
        grid=(bsz, s // ATT_BLK),
        in_specs=[qblk,
                  pl.BlockSpec((1, s, WIDTH), lambda b, i: (b, 0, 0)),
                  pl.BlockSpec((1, WIDTH, s), lambda b, i: (b, 0, 0)),
                  pl.BlockSpec((1, s, HEADS), lambda b, i: (b, 0, 0))],
        out_specs=qblk,
        out_shape=jax.ShapeDtypeStruct(q.shape, BF16),
        scratch_shapes=[pltpu.VMEM((s, HEADS * PAD_DIM), BF16),
                        pltpu.VMEM((ATT_BLK, HEADS * PAD_DIM), BF16),
                        pltpu.VMEM((HEADS, ATT_BLK, ATT_BLK), F32),
                        pltpu.VMEM((HEADS, 1, ATT_BLK), F32),
                        pltpu.VMEM((HEADS, 1, ATT_BLK), F32),
                        pltpu.VMEM((HEADS, HEAD_DIM, ATT_BLK), F32)],
        compiler_params=_params("parallel", "arbitrary"),
        name="fox_attn_prompt",
    )(q, k, vt, ccol)


def _sample_attention(q, s_past, s_new, vct, vnt):
    m = jnp.maximum(jnp.max(s_past, axis=-1, keepdims=True), jnp.max(s_new, axis=-1, keepdims=True))
    p_past = jnp.exp2(s_past - m)
    p_new = jnp.exp2(s_new - m)
    l = jnp.sum(p_past, axis=-1, keepdims=True) + jnp.sum(p_new, axis=-1, keepdims=True)
    acc = _dot_nt(p_past.astype(BF16), vct) + _dot_nt(p_new.astype(BF16), vnt)
    return (acc / l).astype(BF16)


def _band_attn_sample_kernel(q_ref, kct_ref, vct_ref, kn_ref, vnt_ref, bias_ref, o_ref):
    t = q_ref.shape[1]
    past = kct_ref.shape[2]
    for h in range(HEADS):
        sl = slice(h * HEAD_DIM, (h + 1) * HEAD_DIM)
        q = q_ref[0, :, sl]
        s_past = _dot(q, kct_ref[0, sl, :].astype(BF16)) + bias_ref[h, 0:t, BAND - past:BAND]
        s_new = _dot_nt(q, kn_ref[0, :, sl]) + bias_ref[h, 0:t, BAND:BAND + t]
        o_ref[0, :, sl] = _sample_attention(q, s_past, s_new, vct_ref[0, sl, :].astype(BF16), vnt_ref[0, sl, :])


def _band_attn_sample(q, k_cache_t, v_cache_t, k_new, v_new_t, bias):
    bsz, t, _ = q.shape
    past = k_cache_t.shape[2]
    new = pl.BlockSpec((1, t, WIDTH), lambda b: (b, 0, 0))
    new_t = pl.BlockSpec((1, WIDTH, t), lambda b: (b, 0, 0))
    old_t = pl.BlockSpec((1, WIDTH, past), lambda b: (b, 0, 0))
    return pl.pallas_call(
        _band_attn_sample_kernel,
        grid=(bsz,),
        in_specs=[new, old_t, old_t, new, new_t, _resident(bias.shape, lambda b: (0, 0, 0))],
        out_specs=new,
        out_shape=jax.ShapeDtypeStruct(q.shape, BF16),
        compiler_params=_params("parallel"),
        name="band_attn_sample",
    )(q, k_cache_t, v_cache_t, k_new, v_new_t, bias)


def _fox_attn_sample_kernel(q_ref, kct_ref, vct_ref, kn_ref, vnt_ref, ccol_ref, crow_ref, o_ref):
    t = q_ref.shape[1]
    past = kct_ref.shape[2]
    row = lax.broadcasted_iota(jnp.int32, (t, t), 0)
    col = lax.broadcasted_iota(jnp.int32, (t, t), 1)
    causal = col <= row
    for h in range(HEADS):
        sl = slice(h * HEAD_DIM, (h + 1) * HEAD_DIM)
        q = q_ref[0, :, sl]
        cq = ccol_ref[0, :, h:h + 1]
        s_past = _dot(q, kct_ref[0, sl, :].astype(BF16)) + (cq - crow_ref[0, h:h + 1, 0:past]) * LOG2E
        s_new = _dot_nt(q, kn_ref[0, :, sl]) + (cq - crow_ref[0, h:h + 1, past:past + t]) * LOG2E
        s_new = jnp.where(causal, s_new, NEG)
        o_ref[0, :, sl] = _sample_attention(q, s_past, s_new, vct_ref[0, sl, :].astype(BF16), vnt_ref[0, sl, :])


def _fox_attn_sample(q, k_cache_t, v_cache_t, k_new, v_new_t, ccol, crow):
    bsz, t, _ = q.shape
    past = k_cache_t.shape[2]
    assert past % t == 0
    new = pl.BlockSpec((1, t, WIDTH), lambda b: (b, 0, 0))
    new_t = pl.BlockSpec((1, WIDTH, t), lambda b: (b, 0, 0))
    old_t = pl.BlockSpec((1, WIDTH, past), lambda b: (b, 0, 0))
    return pl.pallas_call(
        _fox_attn_sample_kernel,
        grid=(bsz,),
        in_specs=[new, old_t, old_t, new, new_t,
                  pl.BlockSpec((1, t, HEADS), lambda b: (b, past // t, 0)),
                  pl.BlockSpec((1, HEADS, past + t), lambda b: (b, 0, 0))],
        out_specs=new,
        out_shape=jax.ShapeDtypeStruct(q.shape, BF16),
        compiler_params=_params("parallel"),
        name="fox_attn_sample",
    )(q, k_cache_t, v_cache_t, k_new, v_new_t, ccol, crow)


def _postmix_kernel(x_ref, oa_ref, ob_ref, gpre_ref, gpost_ref, wg_ref, wpa_ref, wpb_ref, wout_ref, y_ref):
    x = x_ref[...]
    d = x.shape[-1]
    h = _rmsnorm(x, gpre_ref[...]).astype(BF16)
    gate_a = jax.nn.sigmoid(_dot(h, wg_ref[:, :d]))
    gate_b = jax.nn.sigmoid(_dot(h, wg_ref[:, d:]))
    merged = gate_a * _dot(oa_ref[...], wpa_ref[...]) + gate_b * _dot(ob_ref[...], wpb_ref[...])
    y = x + _rmsnorm(_dot(merged.astype(BF16), wout_ref[...]), gpost_ref[...])
    y_ref[...] = _interleave(y)


def _interleave(rows):
    n, d = rows.shape
    return jnp.swapaxes(rows.reshape(SUBLANES, n // SUBLANES, d), 0, 1)


def _deinterleave(planes):
    p, s, d = planes.shape
    return jnp.swapaxes(planes, 0, 1).reshape(s * p, d)


def _postmix(x, oa, ob, g_pre, g_post, w_g, w_pa, w_pb, w_out):
    n, d = x.shape
    tm = min(ROW_TILE, n)
    row = lambda i: (i, 0)
    fixed = lambda i: (0, 0)
    return pl.pallas_call(
        _postmix_kernel,
        grid=(n // tm,),
        in_specs=[pl.BlockSpec((tm, d), row),
                  pl.BlockSpec((tm, WIDTH), row),
                  pl.BlockSpec((tm, WIDTH), row),
                  pl.BlockSpec((1, d), fixed),
                  pl.BlockSpec((1, d), fixed),
                  _resident(w_g.shape, fixed),
                  _resident(w_pa.shape, fixed),
                  _resident(w_pb.shape, fixed),
                  _resident(w_out.shape, fixed)],
        out_specs=pl.BlockSpec((tm // SUBLANES, SUBLANES, d), lambda i: (i, 0, 0)),
        out_shape=jax.ShapeDtypeStruct((n // SUBLANES, SUBLANES, d), F32),
        compiler_params=_params("parallel"),
        name="postmix",
    )(x, oa, ob, g_pre, g_post, w_g, w_pa, w_pb, w_out)


def _ffn_kernel(x_ref, st_ref, gpre_ref, gpost_ref, wup_ref, cw_ref, cb_ref, wdn_ref,
                y_ref, nst_ref, hist_ref, ext_ref, h_ref, f_ref, *, nseg):
    @pl.when(pl.program_id(1) == 0)
    def _():
        hist_ref[...] = st_ref[...]

    planes, _, d = x_ref.shape
    tm = planes * SUBLANES
    d_ff = wdn_ref.shape[0]
    h_ref[...] = _rmsnorm(x_ref[...].reshape(tm, d), gpre_ref[...]).astype(BF16)
    n_chunks = d_ff // FFN_COLS
    first_sublane = lax.broadcasted_iota(jnp.int32, (SUBLANES, FFN_COLS), 0) == 0

    def parts(c):
        for part in range(2):
            yield (slice(part * d_ff + c * FFN_COLS, part * d_ff + (c + 1) * FFN_COLS),
                   slice(part * FFN_COLS, (part + 1) * FFN_COLS))

    def up(c):
        ext = ext_ref.at[c % 2]
        for cols, dst in parts(c):
            u = _dot(h_ref[...], wup_ref[:, cols]).reshape(planes, SUBLANES, FFN_COLS)
            ext[CONV_W - 1:, :, dst] = u
            for k in range(CONV_W - 1):
                last = u[planes - (CONV_W - 1) + k]
                if nseg == 1:
                    ext[k, :, dst] = jnp.where(first_sublane, hist_ref[0, k:k + 1, cols], pltpu.roll(last, 1, 0))
                    hist_ref[0, k:k + 1, cols] = last[SUBLANES - 1:, :]
                else:
                    ext[k, :, dst] = hist_ref[:, k, cols]
                    hist_ref[:, k, cols] = last

    def down(c):
        ext = ext_ref.at[c % 2]
        halves = []
        for cols, dst in parts(c):
            y = cb_ref[:, cols]
            for tap in range(CONV_W):
                y = y + ext[tap:tap + planes, :, dst] * cw_ref[tap:tap + 1, cols]
            halves.append(y)
        act = (jax.nn.gelu(halves[0]) * halves[1]).reshape(tm, FFN_COLS)
        return _dot(act.astype(BF16), wdn_ref[c * FFN_COLS:(c + 1) * FFN_COLS, :])

    up(0)
    for c in range(n_chunks):
        if c + 1 < n_chunks:
            up(c + 1)
        if c == 0:
            f_ref[...] = down(c)
        else:
            f_ref[...] += down(c)
    y = x_ref[...].reshape(tm, d) + _rmsnorm(f_ref[...], gpost_ref[...])
    y_ref[...] = _deinterleave(y.reshape(planes, SUBLANES, d))
    nst_ref[...] = hist_ref[...]


def _ffn(x, bsz, s, state, g_pre, g_post, w_up, conv_w, conv_b, w_down):
    d = x.shape[-1]
    up = w_up.shape[1]
    tm = min(ROW_TILE, bsz * s)
    planes = tm // SUBLANES
    if s >= tm:
        nseg, tiles = 1, s // tm
    else:
        nseg, tiles = tm // s, 1
        assert nseg == SUBLANES
    outer = bsz // nseg
    row = lambda o, t: (o * tiles + t, 0)
    fixed = lambda o, t: (0, 0)
    st_spec = pl.BlockSpec((nseg, CONV_W - 1, up), lambda o, t: (o, 0, 0))
    y, new_state = pl.pallas_call(
        functools.partial(_ffn_kernel, nseg=nseg),
        grid=(outer, tiles),
        in_specs=[pl.BlockSpec((planes, SUBLANES, d), lambda o, t: (o * tiles + t, 0, 0)),
                  st_spec,
                  pl.BlockSpec((1, d), fixed),
                  pl.BlockSpec((1, d), fixed),
                  _resident(w_up.shape, fixed),
                  pl.BlockSpec(conv_w.shape, fixed),
                  pl.BlockSpec((1, up), fixed),
                  _resident(w_down.shape, fixed)],
        out_specs=[pl.BlockSpec((tm, d), row), st_spec],
        out_shape=[jax.ShapeDtypeStruct((bsz * s, d), F32),
                   jax.ShapeDtypeStruct(state.shape, F32)],
        scratch_shapes=[pltpu.VMEM((nseg, CONV_W - 1, up), F32),
                        pltpu.VMEM((2, CONV_W - 1 + planes, SUBLANES, 2 * FFN_COLS), F32),
                        pltpu.VMEM((tm, d), BF16),
                        pltpu.VMEM((tm, d), F32)],
        compiler_params=_params("arbitrary", "arbitrary"),
        name="conv_ffn",
    )(x, state, g_pre, g_post, w_up, conv_w, conv_b, w_down)
    return y.reshape(bsz, s, d), new_state


def _layer(x, caches, conv_state, bias, w):
    bsz, s, d = x.shape
    n = bsz * s
    x2 = x.reshape(n, d)
    keep = min(BAND, s)
    (qa, qb, ka16, kb16, vat16, vbt16, kat, vat, kbt, vbt, lft) = _inproj(
        x2, w["g_pre_mix"], w["w_qkv"], w["w_ft"], w["b_f"], seq_len=s, band_keep=keep)
    seq = lambda a: a.reshape(bsz, s, a.shape[-1])
    if caches is None:
        ccol, _ = _cumsum(lft)
        oa = _band_attn_prompt(seq(qa), seq(ka16), vat16, bias[1])
        ob = _fox_attn_prompt(seq(qb), seq(kb16), vbt16, ccol)
    else:
        ckat, cvat, ckbt, cvbt, clft = caches
        ccol, crow = _cumsum(clft, lft)
        oa = _band_attn_sample(seq(qa), ckat, cvat, seq(ka16), vat16, bias[0])
        ob = _fox_attn_sample(seq(qb), ckbt, cvbt, seq(kb16), vbt16, ccol, crow)
    x1 = _postmix(x2, oa.reshape(n, WIDTH), ob.reshape(n, WIDTH), w["g_pre_mix"], w["g_post_mix"],
                  w["w_g"], w["w_pa"], w["w_pb"], w["w_out"])
    y, new_conv = _ffn(x1, bsz, s, conv_state, w["g_pre_ffn"], w["g_post_ffn"],
                       w["w_up"], w["conv_w"], w["conv_b"], w["w_down"])
    heads = lambda a: a.reshape(bsz, HEADS, HEAD_DIM, a.shape[-1]).transpose(0, 3, 1, 2)
    return y, (heads(kat), heads(vat), heads(kbt), heads(vbt), lft.transpose(0, 2, 1), new_conv)


def _channel_major(cache):
    bsz, past = cache.shape[:2]
    return cache.transpose(0, 2, 3, 1).reshape(bsz, WIDTH, past)


def kernel(x_prompt, x_sample, cache_k_a, cache_v_a, cache_k_b, cache_v_b, cache_logf_b, state_conv_ffn,
           g_pre_mix, g_post_mix, g_pre_ffn, g_post_ffn, w_in, b_f, rel_table, w_proj_a, w_proj_b, w_out,
           w_up, conv_w, conv_b, w_down):
    depth = w_in.shape[0]
    up = w_up.shape[-1]
    x_p, x_s = x_prompt, x_sample
    p_states, s_states = [], []
    for l in range(depth):
        w = {
            "g_pre_mix": g_pre_mix[l][None], "g_post_mix": g_post_mix[l][None],
            "g_pre_ffn": g_pre_ffn[l][None], "g_post_ffn": g_post_ffn[l][None],
            "w_qkv": w_in[l][:, :6 * WIDTH].astype(BF16),
            "w_ft": w_in[l][:, 6 * WIDTH:6 * WIDTH + HEADS].T.astype(BF16),
            "w_g": w_in[l][:, 6 * WIDTH + HEADS:].astype(BF16),
            "b_f": b_f[l][:, None],
            "w_pa": w_proj_a[l].astype(BF16), "w_pb": w_proj_b[l].astype(BF16),
            "w_out": w_out[l].astype(BF16), "w_up": w_up[l].astype(BF16),
            "conv_w": conv_w[l], "conv_b": conv_b[l][None], "w_down": w_down[l].astype(BF16),
        }
        bias = _band_bias(rel_table[l])
        zero_state = jnp.zeros((x_p.shape[0], CONV_W - 1, up), F32)
        x_p, (ka, va, kb, vb, lf, cv) = _layer(x_p, None, zero_state, bias, w)
        p_states.append((ka, va, kb, vb, lf, cv))
        caches = (_channel_major(cache_k_a[l]), _channel_major(cache_v_a[l]), _channel_major(cache_k_b[l]),
                  _channel_major(cache_v_b[l]), cache_logf_b[l].transpose(0, 2, 1))
        x_s, st = _layer(x_s, caches, state_conv_ffn[l], bias, w)
        s_states.append(st)
    stack = lambda states: [jnp.stack(s) for s in zip(*states)]
    return (x_p, x_s, *stack(p_states), *stack(s_states))
```

```python
import functools

import jax
import jax.numpy as jnp
from jax import lax
from jax.experimental import pallas as pl
from jax.experimental.pallas import tpu as pltpu

F32, BF16 = jnp.float32, jnp.bfloat16

HEADS = 8
HEAD_DIM = 64
WIDTH = HEADS * HEAD_DIM
CHUNK = 64
PAST_CHUNKS = 8
BAND = PAST_CHUNKS * CHUNK
REL_CLIP = 128
REL_SIZE = 2 * REL_CLIP + 1
CONV_W = 3
EPS = 1e-6
NEG = -1e30
LOG2E = 1.4426950408889634

SUBLANES = 8
ATT_BLK = 256
BAND_KEYS = 3 * ATT_BLK
ROW_TILE = 512
FFN_COLS = 256
CUMSUM_BLK = 256
VMEM_LIMIT = 56 * 1024 * 1024


def _params(*sem):
    return pltpu.CompilerParams(dimension_semantics=sem, vmem_limit_bytes=VMEM_LIMIT)


def _resident(shape, index_map):
    return pl.BlockSpec(shape, index_map, pipeline_mode=pl.Buffered(1))


def _rmsnorm(x, g):
    return x * lax.rsqrt(jnp.mean(x * x, axis=-1, keepdims=True) + EPS) * g


def _dot(a, b):
    return jnp.dot(a, b, preferred_element_type=F32)


def _dot_nt(a, b):
    return lax.dot_general(a, b, (((1,), (1,)), ((), ())), preferred_element_type=F32)


def _split3(x):
    hi = x.astype(BF16)
    r = x - hi.astype(F32)
    mid = r.astype(BF16)
    lo = (r - mid.astype(F32)).astype(BF16)
    return hi, mid, lo


def _inproj_kernel(x_ref, g_ref, w_ref, wft_ref, bf_ref,
                   qa_ref, qb_ref, ka_ref, kb_ref, vat16_ref, vbt16_ref,
                   kat_ref, vat_ref, kbt_ref, vbt_ref, lft_ref):
    h = _rmsnorm(x_ref[...], g_ref[...]).astype(BF16)
    scale = HEAD_DIM ** -0.5

    def proj(c):
        return _dot(h, w_ref[:, c * WIDTH:(c + 1) * WIDTH])

    def put(ref, zt):
        per_seq = zt.shape[1] // ref.shape[0]
        for sq in range(ref.shape[0]):
            ref[sq] = zt[:, sq * per_seq:(sq + 1) * per_seq]

    qa_ref[...] = (proj(0) * (scale * LOG2E)).astype(BF16)
    qb_ref[...] = (proj(3) * (scale * LOG2E)).astype(BF16)
    for c, row_ref, t_ref in ((1, ka_ref, kat_ref), (4, kb_ref, kbt_ref)):
        z = proj(c)
        row_ref[...] = z.astype(BF16)
        put(t_ref, z.T)
    for c, t16_ref, t_ref in ((2, vat16_ref, vat_ref), (5, vbt16_ref, vbt_ref)):
        zt = proj(c).T
        put(t_ref, zt)
        put(t16_ref, zt.astype(BF16))
    put(lft_ref, jax.nn.log_sigmoid(_dot_nt(wft_ref[...], h) + bf_ref[...]))


def _inproj(x, g, w_qkv, w_ft, b_f, seq_len, band_keep):
    n, d = x.shape
    bsz = n // seq_len
    tm = min(ROW_TILE, n)
    tiles_per_seq = max(seq_len // tm, 1)
    seqs_per_tile = max(tm // seq_len, 1)
    cols = tm // seqs_per_tile
    assert band_keep == cols
    row = lambda i: (i, 0)
    fixed = lambda i: (0, 0)
    along = lambda i: (i // tiles_per_seq, 0, i % tiles_per_seq)
    kept = lambda i: (i // tiles_per_seq, 0, 0)
    rows16 = (jax.ShapeDtypeStruct((n, WIDTH), BF16), pl.BlockSpec((tm, WIDTH), row))

    def chan(channels, length, dtype, index_map):
        return (jax.ShapeDtypeStruct((bsz, channels, length), dtype),
                pl.BlockSpec((seqs_per_tile, channels, cols), index_map))

    outs = [rows16] * 4
    outs += [chan(WIDTH, seq_len, BF16, along)] * 2
    outs += [chan(WIDTH, band_keep, F32, kept)] * 2
    outs += [chan(WIDTH, seq_len, F32, along)] * 2
    outs += [chan(HEADS, seq_len, F32, along)]
    return pl.pallas_call(
        _inproj_kernel,
        grid=(n // tm,),
        in_specs=[pl.BlockSpec((tm, d), row),
                  pl.BlockSpec((1, d), fixed),
                  _resident(w_qkv.shape, fixed),
                  pl.BlockSpec(w_ft.shape, fixed),
                  pl.BlockSpec((HEADS, 1), fixed)],
        out_specs=[spec for _, spec in outs],
        out_shape=[shape for shape, _ in outs],
        compiler_params=_params("arbitrary"),
        name="inproj",
    )(x, g, w_qkv, w_ft, b_f)


def _cumsum_kernel(*refs, seg_lens):
    seg_refs = refs[:len(seg_lens)]
    ccol_ref, crow_ref = refs[len(seg_lens):]
    carry_c = jnp.zeros((1, HEADS), F32)
    carry_r = jnp.zeros((HEADS, 1), F32)
    off = 0
    for ref, n in zip(seg_refs, seg_lens):
        for o in range(0, n, CUMSUM_BLK):
            b = min(CUMSUM_BLK, n - o)
            parts = _split3(ref[0, :, o:o + b])
            r = lax.broadcasted_iota(jnp.int32, (b, b), 0)
            c = lax.broadcasted_iota(jnp.int32, (b, b), 1)
            lower = jnp.where(r >= c, 1.0, 0.0).astype(BF16)
            upper = jnp.where(r <= c, 1.0, 0.0).astype(BF16)
            cc = carry_c
            cr = carry_r
            for p in parts:
                cc = cc + _dot_nt(lower, p)
                cr = cr + _dot(p, upper)
            ccol_ref[0, off + o:off + o + b, :] = cc
            crow_ref[0, :, off + o:off + o + b] = cr
            carry_c = cc[b - 1:b, :]
            carry_r = cr[:, b - 1:b]
        off += n


def _cumsum(*segs):
    bsz = segs[0].shape[0]
    seg_lens = tuple(s.shape[2] for s in segs)
    total = sum(seg_lens)
    return pl.pallas_call(
        functools.partial(_cumsum_kernel, seg_lens=seg_lens),
        grid=(bsz,),
        in_specs=[pl.BlockSpec((1, HEADS, n), lambda b: (b, 0, 0)) for n in seg_lens],
        out_specs=[pl.BlockSpec((1, total, HEADS), lambda b: (b, 0, 0)),
                   pl.BlockSpec((1, HEADS, total), lambda b: (b, 0, 0))],
        out_shape=[jax.ShapeDtypeStruct((bsz, total, HEADS), F32),
                   jax.ShapeDtypeStruct((bsz, HEADS, total), F32)],
        compiler_params=_params("parallel"),
        name="cumsum_logf",
    )(*segs)


REL_PAD = 384
DIST_SPAN = 1024


def _band_bias_kernel(tbl_ref, qk_ref, kq_ref):
    j = lax.broadcasted_iota(jnp.int32, (REL_PAD, DIST_SPAN), 1)
    r = lax.broadcasted_iota(jnp.int32, (REL_PAD, DIST_SPAN), 0)
    parts = _split3(tbl_ref[...])

    def by_offset(dist):
        onehot = jnp.where(jnp.clip(dist, -REL_CLIP, REL_CLIP) + REL_CLIP == r, 1.0, 0.0).astype(BF16)
        return sum(_dot(p, onehot) for p in parts) * LOG2E

    def skewed(e_row, rows, cols, shift):
        wide = jnp.broadcast_to(e_row, (rows, DIST_SPAN))
        return pltpu.roll(wide, shift % DIST_SPAN, 1, stride=1, stride_axis=0)[:, :cols]

    e_qk = by_offset((BAND_KEYS - 1) - j)
    e_kq = by_offset(j - (ATT_BLK - 1))
    qc = lax.broadcasted_iota(jnp.int32, (ATT_BLK, BAND_KEYS), 0) // CHUNK
    kc = lax.broadcasted_iota(jnp.int32, (ATT_BLK, BAND_KEYS), 1) // CHUNK
    vis_qk = (kc >= qc) & (kc <= qc + PAST_CHUNKS)
    kc = lax.broadcasted_iota(jnp.int32, (BAND_KEYS, ATT_BLK), 0) // CHUNK
    qc = lax.broadcasted_iota(jnp.int32, (BAND_KEYS, ATT_BLK), 1) // CHUNK
    vis_kq = (kc >= qc) & (kc <= qc + PAST_CHUNKS)
    for h in range(HEADS):
        qk_ref[h] = jnp.where(vis_qk, skewed(e_qk[h:h + 1, :], ATT_BLK, BAND_KEYS, -(ATT_BLK - 1)), NEG)
        kq_ref[h] = jnp.where(vis_kq, skewed(e_kq[h:h + 1, :], BAND_KEYS, ATT_BLK, -(BAND_KEYS - 1)), NEG)


def _band_bias(rel_table):
    tbl = jnp.pad(rel_table, ((0, 0), (0, REL_PAD - REL_SIZE)))
    return pl.pallas_call(
        _band_bias_kernel,
        out_shape=[jax.ShapeDtypeStruct((HEADS, ATT_BLK, BAND_KEYS), F32),
                   jax.ShapeDtypeStruct((HEADS, BAND_KEYS, ATT_BLK), F32)],
        compiler_params=pltpu.CompilerParams(vmem_limit_bytes=VMEM_LIMIT),
        name="band_bias",
    )(tbl)


BAND_BLOCKS = BAND_KEYS // ATT_BLK


def _band_attn_prompt_kernel(q_ref, k_ref, vt_ref, bias_ref, o_ref, s_ref):
    i = pl.program_id(1)
    ones = jnp.ones((SUM_ROWS, ATT_BLK), BF16)

    def run(clipped):
        blocks = []
        for g in range(BAND_BLOCKS):
            j = i - (BAND_BLOCKS - 1) + g
            start = pl.multiple_of(jnp.maximum(j, 0) * ATT_BLK, ATT_BLK)
            masks = clipped and g < BAND_BLOCKS - 1
            blocks.append((start, jnp.where(j >= 0, 0.0, NEG) if masks else None))

        def scores(h):
            sl = slice(h * HEAD_DIM, (h + 1) * HEAD_DIM)
            q = q_ref[0, :, sl]
            for g, (start, penalty) in enumerate(blocks):
                s = _dot_nt(k_ref[0, pl.ds(start, ATT_BLK), sl], q) + bias_ref[h, g * ATT_BLK:(g + 1) * ATT_BLK, :]
                s_ref[h % 2, g] = s if penalty is None else s + penalty

        def finish(h):
            sl = slice(h * HEAD_DIM, (h + 1) * HEAD_DIM)
            s = [s_ref[h % 2, g] for g in range(BAND_BLOCKS)]
            m = functools.reduce(jnp.maximum, [jnp.max(x, axis=0, keepdims=True) for x in s])
            acc = 0.0
            for x, (start, _) in zip(s, blocks):
                vt = jnp.concatenate([vt_ref[0, sl, pl.ds(start, ATT_BLK)], ones], axis=0)
                acc = acc + _dot(vt, jnp.exp2(x - m).astype(BF16))
            o_ref[0, :, sl] = (acc[:HEAD_DIM] / acc[HEAD_DIM:HEAD_DIM + 1]).T.astype(BF16)

        scores(0)
        for h in range(HEADS):
            if h + 1 < HEADS:
                scores(h + 1)
            finish(h)

    pl.when(i >= BAND_BLOCKS - 1)(lambda: run(clipped=False))
    pl.when(i < BAND_BLOCKS - 1)(lambda: run(clipped=True))


def _band_attn_prompt(q, k, vt, bias_kq):
    bsz, s, _ = q.shape
    qblk = pl.BlockSpec((1, ATT_BLK, WIDTH), lambda b, i: (b, i, 0))
    return pl.pallas_call(
        _band_attn_prompt_kernel,
        grid=(bsz, s // ATT_BLK),
        in_specs=[qblk,
                  pl.BlockSpec((1, s, WIDTH), lambda b, i: (b, 0, 0)),
                  pl.BlockSpec((1, WIDTH, s), lambda b, i: (b, 0, 0)),
                  _resident(bias_kq.shape, lambda b, i: (0, 0, 0))],
        out_specs=qblk,
        out_shape=jax.ShapeDtypeStruct(q.shape, BF16),
        scratch_shapes=[pltpu.VMEM((2, BAND_BLOCKS, ATT_BLK, ATT_BLK), F32)],
        compiler_params=_params("parallel", "parallel"),
        name="band_attn_prompt",
    )(q, k, vt, bias_kq)


def _decay_lanes(c, key_side):
    slot = (lax.broadcasted_iota(jnp.int32, (HEADS, WIDTH), 1)
            - HEAD_DIM * lax.broadcasted_iota(jnp.int32, (HEADS, WIDTH), 0))
    in_group = lax.broadcasted_iota(jnp.int32, (1, WIDTH), 1) % HEAD_DIM
    first_one, first_term, sign = (3, 0, -1.0) if key_side else (0, 3, 1.0)
    out = jnp.where((in_group >= first_one) & (in_group < first_one + 3), 1.0, 0.0)
    for t, part in enumerate(_split3(c)):
        out = out + _dot(part, jnp.where(slot == first_term + t, sign, 0.0).astype(BF16))
    return out.astype(BF16)


PAD_DIM = 2 * HEAD_DIM
SUM_ROWS = 16


def _fox_attn_prompt_kernel(q_ref, k_ref, vt_ref, ccol_ref, o_ref,
                            kp_ref, qp_ref, s_ref, m_ref, acc_ref):
    i = pl.program_id(1)
    qstart = pl.multiple_of(i * ATT_BLK, ATT_BLK)

    @pl.when(i == 0)
    def _():
        extra = _decay_lanes(ccol_ref[0] * LOG2E, key_side=True)
        for h in range(HEADS):
            sl = slice(h * HEAD_DIM, (h + 1) * HEAD_DIM)
            kp_ref[:, h * PAD_DIM:(h + 1) * PAD_DIM] = jnp.concatenate([k_ref[0, :, sl], extra[:, sl]], axis=1)

    extra = _decay_lanes(ccol_ref[0, pl.ds(qstart, ATT_BLK), :] * LOG2E, key_side=False)
    for h in range(HEADS):
        sl = slice(h * HEAD_DIM, (h + 1) * HEAD_DIM)
        qp_ref[:, h * PAD_DIM:(h + 1) * PAD_DIM] = jnp.concatenate([q_ref[0, :, sl], extra[:, sl]], axis=1)

    key = lax.broadcasted_iota(jnp.int32, (ATT_BLK, ATT_BLK), 0)
    qry = lax.broadcasted_iota(jnp.int32, (ATT_BLK, ATT_BLK), 1)
    causal = key <= qry
    ones = jnp.ones((SUM_ROWS, ATT_BLK), BF16)
    m_ref[...] = jnp.full(m_ref.shape, NEG, F32)
    acc_ref[...] = jnp.zeros(acc_ref.shape, F32)

    def scores(j, h):
        start = pl.multiple_of(j * ATT_BLK, ATT_BLK)
        pad = slice(h * PAD_DIM, (h + 1) * PAD_DIM)
        return _dot_nt(kp_ref[pl.ds(start, ATT_BLK), pad], qp_ref[:, pad])

    def update(j, h, s, masked):
        start = pl.multiple_of(j * ATT_BLK, ATT_BLK)
        if masked:
            s = jnp.where(causal, s, NEG)
        m_old = m_ref[h]
        m_new = jnp.maximum(m_old, jnp.max(s, axis=0, keepdims=True))
        alpha = jnp.exp2(m_old - m_new)
        p = jnp.exp2(s - m_new)
        vt = vt_ref[0, h * HEAD_DIM:(h + 1) * HEAD_DIM, pl.ds(start, ATT_BLK)]
        acc_ref[h] = alpha * acc_ref[h] + _dot(jnp.concatenate([vt, ones], axis=0), p.astype(BF16))
        m_ref[h] = m_new

    def step(j, cur, masked=False, prefetch=True):
        if prefetch:
            for h in range(HEADS):
                s_ref[1 - cur, h] = scores(j + 1, h)
        for h in range(HEADS):
            update(j, h, s_ref[cur, h], masked)

    for h in range(HEADS):
        s_ref[0, h] = scores(0, h)

    def pair(t, carry):
        step(2 * t, 0)
        step(2 * t + 1, 1)
        return carry

    lax.fori_loop(0, i // 2, pair, 0)

    @pl.when(i % 2 == 1)
    def _():
        step(i - 1, 0)
        step(i, 1, masked=True, prefetch=False)

    @pl.when(i % 2 == 0)
    def _():
        step(i, 0, masked=True, prefetch=False)

    for h in range(HEADS):
        sl = slice(h * HEAD_DIM, (h + 1) * HEAD_DIM)
        out = acc_ref[h, :HEAD_DIM, :] / acc_ref[h, HEAD_DIM:HEAD_DIM + 1, :]
        o_ref[0, :, sl] = out.T.astype(BF16)


def _fox_attn_prompt(q, k, vt, ccol):
    bsz, s, _ = q.shape
    qblk = pl.BlockSpec((1, ATT_BLK, WIDTH), lambda b, i: (b, i, 0))
    return pl.pallas_call(
        _fox_attn_prompt_kernel,
        grid=(bsz, s // ATT_BLK),
        in_specs=[qblk,
                  pl.BlockSpec((1, s, WIDTH), lambda b, i: (b, 0, 0)),
                  pl.BlockSpec((1, WIDTH, s), lambda b, i: (b, 0, 0)),
                  pl.BlockSpec((1, s, HEADS), lambda b, i: (b, 0, 0))],
        out_specs=qblk,
        out_shape=jax.ShapeDtypeStruct(q.shape, BF16),
        scratch_shapes=[pltpu.VMEM((s, HEADS * PAD_DIM), BF16),
                        pltpu.VMEM((ATT_BLK, HEADS * PAD_DIM), BF16),
                        pltpu.VMEM((2, HEADS, ATT_BLK, ATT_BLK), F32),
                        pltpu.VMEM((HEADS, 1, ATT_BLK), F32),
                        pltpu.VMEM((HEADS, HEAD_DIM + SUM_ROWS, ATT_BLK), F32)],
        compiler_params=_params("parallel", "arbitrary"),
        name="fox_attn_prompt",
    )(q, k, vt, ccol)


def _sample_attention(q, s_past, s_new, vct, vnt):
    m = jnp.maximum(jnp.max(s_past, axis=-1, keepdims=True), jnp.max(s_new, axis=-1, keepdims=True))
    p_past = jnp.exp2(s_past - m)
    p_new = jnp.exp2(s_new - m)
    l = jnp.sum(p_past, axis=-1, keepdims=True) + jnp.sum(p_new, axis=-1, keepdims=True)
    acc = _dot_nt(p_past.astype(BF16), vct) + _dot_nt(p_new.astype(BF16), vnt)
    return (acc / l).astype(BF16)


def _band_attn_sample_kernel(q_ref, kct_ref, vct_ref, kn_ref, vnt_ref, bias_ref, o_ref):
    t = q_ref.shape[1]
    past = kct_ref.shape[2]
    for h in range(HEADS):
        sl = slice(h * HEAD_DIM, (h + 1) * HEAD_DIM)
        q = q_ref[0, :, sl]
        s_past = _dot(q, kct_ref[0, sl, :].astype(BF16)) + bias_ref[h, 0:t, BAND - past:BAND]
        s_new = _dot_nt(q, kn_ref[0, :, sl]) + bias_ref[h, 0:t, BAND:BAND + t]
        o_ref[0, :, sl] = _sample_attention(q, s_past, s_new, vct_ref[0, sl, :].astype(BF16), vnt_ref[0, sl, :])


def _band_attn_sample(q, k_cache_t, v_cache_t, k_new, v_new_t, bias):
    bsz, t, _ = q.shape
    past = k_cache_t.shape[2]
    new = pl.BlockSpec((1, t, WIDTH), lambda b: (b, 0, 0))
    new_t = pl.BlockSpec((1, WIDTH, t), lambda b: (b, 0, 0))
    old_t = pl.BlockSpec((1, WIDTH, past), lambda b: (b, 0, 0))
    return pl.pallas_call(
        _band_attn_sample_kernel,
        grid=(bsz,),
        in_specs=[new, old_t, old_t, new, new_t, _resident(bias.shape, lambda b: (0, 0, 0))],
        out_specs=new,
        out_shape=jax.ShapeDtypeStruct(q.shape, BF16),
        compiler_params=_params("parallel"),
        name="band_attn_sample",
    )(q, k_cache_t, v_cache_t, k_new, v_new_t, bias)


def _fox_attn_sample_kernel(q_ref, kct_ref, vct_ref, kn_ref, vnt_ref, ccol_ref, crow_ref, o_ref):
    t = q_ref.shape[1]
    past = kct_ref.shape[2]
    row = lax.broadcasted_iota(jnp.int32, (t, t), 0)
    col = lax.broadcasted_iota(jnp.int32, (t, t), 1)
    causal = col <= row
    for h in range(HEADS):
        sl = slice(h * HEAD_DIM, (h + 1) * HEAD_DIM)
        q = q_ref[0, :, sl]
        cq = ccol_ref[0, :, h:h + 1]
        s_past = _dot(q, kct_ref[0, sl, :].astype(BF16)) + (cq - crow_ref[0, h:h + 1, 0:past]) * LOG2E
        s_new = _dot_nt(q, kn_ref[0, :, sl]) + (cq - crow_ref[0, h:h + 1, past:past + t]) * LOG2E
        s_new = jnp.where(causal, s_new, NEG)
        o_ref[0, :, sl] = _sample_attention(q, s_past, s_new, vct_ref[0, sl, :].astype(BF16), vnt_ref[0, sl, :])


def _fox_attn_sample(q, k_cache_t, v_cache_t, k_new, v_new_t, ccol, crow):
    bsz, t, _ = q.shape
    past = k_cache_t.shape[2]
    assert past % t == 0
    new = pl.BlockSpec((1, t, WIDTH), lambda b: (b, 0, 0))
    new_t = pl.BlockSpec((1, WIDTH, t), lambda b: (b, 0, 0))
    old_t = pl.BlockSpec((1, WIDTH, past), lambda b: (b, 0, 0))
    return pl.pallas_call(
        _fox_attn_sample_kernel,
        grid=(bsz,),
        in_specs=[new, old_t, old_t, new, new_t,
                  pl.BlockSpec((1, t, HEADS), lambda b: (b, past // t, 0)),
                  pl.BlockSpec((1, HEADS, past + t), lambda b: (b, 0, 0))],
        out_specs=new,
        out_shape=jax.ShapeDtypeStruct(q.shape, BF16),
        compiler_params=_params("parallel"),
        name="fox_attn_sample",
    )(q, k_cache_t, v_cache_t, k_new, v_new_t, ccol, crow)


def _postmix_kernel(x_ref, oa_ref, ob_ref, gpre_ref, gpost_ref, wg_ref, wpa_ref, wpb_ref, wout_ref, y_ref):
    x = x_ref[...]
    d = x.shape[-1]
    h = _rmsnorm(x, gpre_ref[...]).astype(BF16)
    gate_a = jax.nn.sigmoid(_dot(h, wg_ref[:, :d]))
    gate_b = jax.nn.sigmoid(_dot(h, wg_ref[:, d:]))
    merged = gate_a * _dot(oa_ref[...], wpa_ref[...]) + gate_b * _dot(ob_ref[...], wpb_ref[...])
    y = x + _rmsnorm(_dot(merged.astype(BF16), wout_ref[...]), gpost_ref[...])
    y_ref[...] = _interleave(y)


def _interleave(rows):
    n, d = rows.shape
    return jnp.swapaxes(rows.reshape(SUBLANES, n // SUBLANES, d), 0, 1)


def _deinterleave(planes):
    p, s, d = planes.shape
    return jnp.swapaxes(planes, 0, 1).reshape(s * p, d)


def _postmix(x, oa, ob, g_pre, g_post, w_g, w_pa, w_pb, w_out):
    n, d = x.shape
    tm = min(ROW_TILE, n)
    row = lambda i: (i, 0)
    fixed = lambda i: (0, 0)
    return pl.pallas_call(
        _postmix_kernel,
        grid=(n // tm,),
        in_specs=[pl.BlockSpec((tm, d), row),
                  pl.BlockSpec((tm, WIDTH), row),
                  pl.BlockSpec((tm, WIDTH), row),
                  pl.BlockSpec((1, d), fixed),
                  pl.BlockSpec((1, d), fixed),
                  _resident(w_g.shape, fixed),
                  _resident(w_pa.shape, fixed),
                  _resident(w_pb.shape, fixed),
                  _resident(w_out.shape, fixed)],
        out_specs=pl.BlockSpec((tm // SUBLANES, SUBLANES, d), lambda i: (i, 0, 0)),
        out_shape=jax.ShapeDtypeStruct((n // SUBLANES, SUBLANES, d), F32),
        compiler_params=_params("parallel"),
        name="postmix",
    )(x, oa, ob, g_pre, g_post, w_g, w_pa, w_pb, w_out)


def _ffn_kernel(x_ref, st_ref, gpre_ref, gpost_ref, wup_ref, cw_ref, cb_ref, wdn_ref,
                y_ref, nst_ref, hist_ref, ext_ref, h_ref, f_ref, *, nseg):
    @pl.when(pl.program_id(1) == 0)
    def _():
        hist_ref[...] = st_ref[...]

    planes, _, d = x_ref.shape
    tm = planes * SUBLANES
    d_ff = wdn_ref.shape[0]
    h_ref[...] = _rmsnorm(x_ref[...].reshape(tm, d), gpre_ref[...]).astype(BF16)
    n_chunks = d_ff // FFN_COLS
    first_sublane = lax.broadcasted_iota(jnp.int32, (SUBLANES, FFN_COLS), 0) == 0

    def parts(c):
        for part in range(2):
            yield (slice(part * d_ff + c * FFN_COLS, part * d_ff + (c + 1) * FFN_COLS),
                   slice(part * FFN_COLS, (part + 1) * FFN_COLS))

    def up(c):
        ext = ext_ref.at[c % 2]
        for cols, dst in parts(c):
            u = _dot(h_ref[...], wup_ref[:, cols]).reshape(planes, SUBLANES, FFN_COLS)
            ext[CONV_W - 1:, :, dst] = u
            for k in range(CONV_W - 1):
                last = u[planes - (CONV_W - 1) + k]
                if nseg == 1:
                    ext[k, :, dst] = jnp.where(first_sublane, hist_ref[0, k:k + 1, cols], pltpu.roll(last, 1, 0))
                    hist_ref[0, k:k + 1, cols] = last[SUBLANES - 1:, :]
                else:
                    ext[k, :, dst] = hist_ref[:, k, cols]
                    hist_ref[:, k, cols] = last

    def down(c):
        ext = ext_ref.at[c % 2]
        halves = []
        for cols, dst in parts(c):
            y = cb_ref[:, cols]
            for tap in range(CONV_W):
                y = y + ext[tap:tap + planes, :, dst] * cw_ref[tap:tap + 1, cols]
            halves.append(y)
        act = (jax.nn.gelu(halves[0]) * halves[1]).reshape(tm, FFN_COLS)
        return _dot(act.astype(BF16), wdn_ref[c * FFN_COLS:(c + 1) * FFN_COLS, :])

    up(0)
    for c in range(n_chunks):
        if c + 1 < n_chunks:
            up(c + 1)
        if c == 0:
            f_ref[...] = down(c)
        else:
            f_ref[...] += down(c)
    y = x_ref[...].reshape(tm, d) + _rmsnorm(f_ref[...], gpost_ref[...])
    y_ref[...] = _deinterleave(y.reshape(planes, SUBLANES, d))
    nst_ref[...] = hist_ref[...]


def _ffn(x, bsz, s, state, g_pre, g_post, w_up, conv_w, conv_b, w_down):
    d = x.shape[-1]
    up = w_up.shape[1]
    tm = min(ROW_TILE, bsz * s)
    planes = tm // SUBLANES
    if s >= tm:
        nseg, tiles = 1, s // tm
    else:
        nseg, tiles = tm // s, 1
        assert nseg == SUBLANES
    outer = bsz // nseg
    row = lambda o, t: (o * tiles + t, 0)
    fixed = lambda o, t: (0, 0)
    st_spec = pl.BlockSpec((nseg, CONV_W - 1, up), lambda o, t: (o, 0, 0))
    y, new_state = pl.pallas_call(
        functools.partial(_ffn_kernel, nseg=nseg),
        grid=(outer, tiles),
        in_specs=[pl.BlockSpec((planes, SUBLANES, d), lambda o, t: (o * tiles + t, 0, 0)),
                  st_spec,
                  pl.BlockSpec((1, d), fixed),
                  pl.BlockSpec((1, d), fixed),
                  _resident(w_up.shape, fixed),
                  pl.BlockSpec(conv_w.shape, fixed),
                  pl.BlockSpec((1, up), fixed),
                  _resident(w_down.shape, fixed)],
        out_specs=[pl.BlockSpec((tm, d), row), st_spec],
        out_shape=[jax.ShapeDtypeStruct((bsz * s, d), F32),
                   jax.ShapeDtypeStruct(state.shape, F32)],
        scratch_shapes=[pltpu.VMEM((nseg, CONV_W - 1, up), F32),
                        pltpu.VMEM((2, CONV_W - 1 + planes, SUBLANES, 2 * FFN_COLS), F32),
                        pltpu.VMEM((tm, d), BF16),
                        pltpu.VMEM((tm, d), F32)],
        compiler_params=_params("arbitrary", "arbitrary"),
        name="conv_ffn",
    )(x, state, g_pre, g_post, w_up, conv_w, conv_b, w_down)
    return y.reshape(bsz, s, d), new_state


def _layer(x, caches, conv_state, bias, w):
    bsz, s, d = x.shape
    n = bsz * s
    x2 = x.reshape(n, d)
    keep = min(BAND, s)
    (qa, qb, ka16, kb16, vat16, vbt16, kat, vat, kbt, vbt, lft) = _inproj(
        x2, w["g_pre_mix"], w["w_qkv"], w["w_ft"], w["b_f"], seq_len=s, band_keep=keep)
    seq = lambda a: a.reshape(bsz, s, a.shape[-1])
    if caches is None:
        ccol, _ = _cumsum(lft)
        oa = _band_attn_prompt(seq(qa), seq(ka16), vat16, bias[1])
        ob = _fox_attn_prompt(seq(qb), seq(kb16), vbt16, ccol)
    else:
        ckat, cvat, ckbt, cvbt, clft = caches
        ccol, crow = _cumsum(clft, lft)
        oa = _band_attn_sample(seq(qa), ckat, cvat, seq(ka16), vat16, bias[0])
        ob = _fox_attn_sample(seq(qb), ckbt, cvbt, seq(kb16), vbt16, ccol, crow)
    x1 = _postmix(x2, oa.reshape(n, WIDTH), ob.reshape(n, WIDTH), w["g_pre_mix"], w["g_post_mix"],
                  w["w_g"], w["w_pa"], w["w_pb"], w["w_out"])
    y, new_conv = _ffn(x1, bsz, s, conv_state, w["g_pre_ffn"], w["g_post_ffn"],
                       w["w_up"], w["conv_w"], w["conv_b"], w["w_down"])
    heads = lambda a: a.reshape(bsz, HEADS, HEAD_DIM, a.shape[-1]).transpose(0, 3, 1, 2)
    return y, (heads(kat), heads(vat), heads(kbt), heads(vbt), lft.transpose(0, 2, 1), new_conv)


def _channel_major(cache):
    bsz, past = cache.shape[:2]
    return cache.transpose(0, 2, 3, 1).reshape(bsz, WIDTH, past)


def kernel(x_prompt, x_sample, cache_k_a, cache_v_a, cache_k_b, cache_v_b, cache_logf_b, state_conv_ffn,
           g_pre_mix, g_post_mix, g_pre_ffn, g_post_ffn, w_in, b_f, rel_table, w_proj_a, w_proj_b, w_out,
           w_up, conv_w, conv_b, w_down):
    depth = w_in.shape[0]
    up = w_up.shape[-1]
    x_p, x_s = x_prompt, x_sample
    p_states, s_states = [], []
    for l in range(depth):
        w = {
            "g_pre_mix": g_pre_mix[l][None], "g_post_mix": g_post_mix[l][None],
            "g_pre_ffn": g_pre_ffn[l][None], "g_post_ffn": g_post_ffn[l][None],
            "w_qkv": w_in[l][:, :6 * WIDTH].astype(BF16),
            "w_ft": w_in[l][:, 6 * WIDTH:6 * WIDTH + HEADS].T.astype(BF16),
            "w_g": w_in[l][:, 6 * WIDTH + HEADS:].astype(BF16),
            "b_f": b_f[l][:, None],
            "w_pa": w_proj_a[l].astype(BF16), "w_pb": w_proj_b[l].astype(BF16),
            "w_out": w_out[l].astype(BF16), "w_up": w_up[l].astype(BF16),
            "conv_w": conv_w[l], "conv_b": conv_b[l][None], "w_down": w_down[l].astype(BF16),
        }
        bias = _band_bias(rel_table[l])
        zero_state = jnp.zeros((x_p.shape[0], CONV_W - 1, up), F32)
        x_p, (ka, va, kb, vb, lf, cv) = _layer(x_p, None, zero_state, bias, w)
        p_states.append((ka, va, kb, vb, lf, cv))
        caches = (_channel_major(cache_k_a[l]), _channel_major(cache_v_a[l]), _channel_major(cache_k_b[l]),
                  _channel_major(cache_v_b[l]), cache_logf_b[l].transpose(0, 2, 1))
        x_s, st = _layer(x_s, caches, state_conv_ffn[l], bias, w)
        s_states.append(st)
    stack = lambda states: [jnp.stack(s) for s in zip(*states)]
    return (x_p, x_s, *stack(p_states), *stack(s_states))
```

```python
import functools

import jax
import jax.numpy as jnp
from jax import lax
from jax.experimental import pallas as pl
from jax.experimental.pallas import tpu as pltpu

F32, BF16 = jnp.float32, jnp.bfloat16

HEADS = 8
HEAD_DIM = 64
WIDTH = HEADS * HEAD_DIM
CHUNK = 64
PAST_CHUNKS = 8
BAND = PAST_CHUNKS * CHUNK
REL_CLIP = 128
REL_SIZE = 2 * REL_CLIP + 1
CONV_W = 3
EPS = 1e-6
NEG = -1e30
LOG2E = 1.4426950408889634

SUBLANES = 8
ATT_BLK = 256
BAND_KEYS = 3 * ATT_BLK
ROW_TILE = 512
FFN_COLS = 256
CUMSUM_BLK = 256
VMEM_LIMIT = 56 * 1024 * 1024


def _params(*sem):
    return pltpu.CompilerParams(dimension_semantics=sem, vmem_limit_bytes=VMEM_LIMIT)


def _resident(shape, index_map):
    return pl.BlockSpec(shape, index_map, pipeline_mode=pl.Buffered(1))


def _rmsnorm(x, g):
    return x * lax.rsqrt(jnp.mean(x * x, axis=-1, keepdims=True) + EPS) * g


def _dot(a, b):
    return jnp.dot(a, b, preferred_element_type=F32)


def _dot_nt(a, b):
    return lax.dot_general(a, b, (((1,), (1,)), ((), ())), preferred_element_type=F32)


def _split3(x):
    hi = x.astype(BF16)
    r = x - hi.astype(F32)
    mid = r.astype(BF16)
    lo = (r - mid.astype(F32)).astype(BF16)
    return hi, mid, lo


def _inproj_kernel(x_ref, g_ref, w_ref, wft_ref, bf_ref,
                   qa_ref, qb_ref, ka_ref, kb_ref, vat16_ref, vbt16_ref,
                   kat_ref, vat_ref, kbt_ref, vbt_ref, lft_ref):
    h = _rmsnorm(x_ref[...], g_ref[...]).astype(BF16)
    scale = HEAD_DIM ** -0.5

    def proj(c):
        return _dot(h, w_ref[:, c * WIDTH:(c + 1) * WIDTH])

    def put(ref, zt):
        per_seq = zt.shape[1] // ref.shape[0]
        for sq in range(ref.shape[0]):
            ref[sq] = zt[:, sq * per_seq:(sq + 1) * per_seq]

    qa_ref[...] = (proj(0) * (scale * LOG2E)).astype(BF16)
    qb_ref[...] = (proj(3) * (scale * LOG2E)).astype(BF16)
    for c, row_ref, t_ref in ((1, ka_ref, kat_ref), (4, kb_ref, kbt_ref)):
        z = proj(c)
        row_ref[...] = z.astype(BF16)
        put(t_ref, z.T)
    for c, t16_ref, t_ref in ((2, vat16_ref, vat_ref), (5, vbt16_ref, vbt_ref)):
        zt = proj(c).T
        put(t_ref, zt)
        put(t16_ref, zt.astype(BF16))
    put(lft_ref, jax.nn.log_sigmoid(_dot_nt(wft_ref[...], h) + bf_ref[...]))


def _inproj(x, g, w_qkv, w_ft, b_f, seq_len, band_keep):
    n, d = x.shape
    bsz = n // seq_len
    tm = min(ROW_TILE, n)
    tiles_per_seq = max(seq_len // tm, 1)
    seqs_per_tile = max(tm // seq_len, 1)
    cols = tm // seqs_per_tile
    assert band_keep == cols
    row = lambda i: (i, 0)
    fixed = lambda i: (0, 0)
    along = lambda i: (i // tiles_per_seq, 0, i % tiles_per_seq)
    kept = lambda i: (i // tiles_per_seq, 0, 0)
    rows16 = (jax.ShapeDtypeStruct((n, WIDTH), BF16), pl.BlockSpec((tm, WIDTH), row))

    def chan(channels, length, dtype, index_map):
        return (jax.ShapeDtypeStruct((bsz, channels, length), dtype),
                pl.BlockSpec((seqs_per_tile, channels, cols), index_map))

    outs = [rows16] * 4
    outs += [chan(WIDTH, seq_len, BF16, along)] * 2
    outs += [chan(WIDTH, band_keep, F32, kept)] * 2
    outs += [chan(WIDTH, seq_len, F32, along)] * 2
    outs += [chan(HEADS, seq_len, F32, along)]
    return pl.pallas_call(
        _inproj_kernel,
        grid=(n // tm,),
        in_specs=[pl.BlockSpec((tm, d), row),
                  pl.BlockSpec((1, d), fixed),
                  _resident(w_qkv.shape, fixed),
                  pl.BlockSpec(w_ft.shape, fixed),
                  pl.BlockSpec((HEADS, 1), fixed)],
        out_specs=[spec for _, spec in outs],
        out_shape=[shape for shape, _ in outs],
        compiler_params=_params("arbitrary"),
        name="inproj",
    )(x, g, w_qkv, w_ft, b_f)


def _cumsum_kernel(*refs, seg_lens):
    seg_refs = refs[:len(seg_lens)]
    ccol_ref, crow_ref = refs[len(seg_lens):]
    carry_c = jnp.zeros((1, HEADS), F32)
    carry_r = jnp.zeros((HEADS, 1), F32)
    off = 0
    for ref, n in zip(seg_refs, seg_lens):
        for o in range(0, n, CUMSUM_BLK):
            b = min(CUMSUM_BLK, n - o)
            parts = _split3(ref[0, :, o:o + b])
            r = lax.broadcasted_iota(jnp.int32, (b, b), 0)
            c = lax.broadcasted_iota(jnp.int32, (b, b), 1)
            lower = jnp.where(r >= c, 1.0, 0.0).astype(BF16)
            upper = jnp.where(r <= c, 1.0, 0.0).astype(BF16)
            cc = carry_c
            cr = carry_r
            for p in parts:
                cc = cc + _dot_nt(lower, p)
                cr = cr + _dot(p, upper)
            ccol_ref[0, off + o:off + o + b, :] = cc
            crow_ref[0, :, off + o:off + o + b] = cr
            carry_c = cc[b - 1:b, :]
            carry_r = cr[:, b - 1:b]
        off += n


def _cumsum(*segs):
    bsz = segs[0].shape[0]
    seg_lens = tuple(s.shape[2] for s in segs)
    total = sum(seg_lens)
    return pl.pallas_call(
        functools.partial(_cumsum_kernel, seg_lens=seg_lens),
        grid=(bsz,),
        in_specs=[pl.BlockSpec((1, HEADS, n), lambda b: (b, 0, 0)) for n in seg_lens],
        out_specs=[pl.BlockSpec((1, total, HEADS), lambda b: (b, 0, 0)),
                   pl.BlockSpec((1, HEADS, total), lambda b: (b, 0, 0))],
        out_shape=[jax.ShapeDtypeStruct((bsz, total, HEADS), F32),
                   jax.ShapeDtypeStruct((bsz, HEADS, total), F32)],
        compiler_params=_params("parallel"),
        name="cumsum_logf",
    )(*segs)


REL_PAD = 384
DIST_SPAN = 1024


def _band_bias_kernel(tbl_ref, qk_ref, kq_ref):
    j = lax.broadcasted_iota(jnp.int32, (REL_PAD, DIST_SPAN), 1)
    r = lax.broadcasted_iota(jnp.int32, (REL_PAD, DIST_SPAN), 0)
    parts = _split3(tbl_ref[...])

    def by_offset(dist):
        onehot = jnp.where(jnp.clip(dist, -REL_CLIP, REL_CLIP) + REL_CLIP == r, 1.0, 0.0).astype(BF16)
        return sum(_dot(p, onehot) for p in parts) * LOG2E

    def skewed(e_row, rows, cols, shift):
        wide = jnp.broadcast_to(e_row, (rows, DIST_SPAN))
        return pltpu.roll(wide, shift % DIST_SPAN, 1, stride=1, stride_axis=0)[:, :cols]

    e_qk = by_offset((BAND_KEYS - 1) - j)
    e_kq = by_offset(j - (ATT_BLK - 1))
    qc = lax.broadcasted_iota(jnp.int32, (ATT_BLK, BAND_KEYS), 0) // CHUNK
    kc = lax.broadcasted_iota(jnp.int32, (ATT_BLK, BAND_KEYS), 1) // CHUNK
    vis_qk = (kc >= qc) & (kc <= qc + PAST_CHUNKS)
    kc = lax.broadcasted_iota(jnp.int32, (BAND_KEYS, ATT_BLK), 0) // CHUNK
    qc = lax.broadcasted_iota(jnp.int32, (BAND_KEYS, ATT_BLK), 1) // CHUNK
    vis_kq = (kc >= qc) & (kc <= qc + PAST_CHUNKS)
    for h in range(HEADS):
        qk_ref[h] = jnp.where(vis_qk, skewed(e_qk[h:h + 1, :], ATT_BLK, BAND_KEYS, -(ATT_BLK - 1)), NEG)
        kq_ref[h] = jnp.where(vis_kq, skewed(e_kq[h:h + 1, :], BAND_KEYS, ATT_BLK, -(BAND_KEYS - 1)), NEG)


def _band_bias(rel_table):
    tbl = jnp.pad(rel_table, ((0, 0), (0, REL_PAD - REL_SIZE)))
    return pl.pallas_call(
        _band_bias_kernel,
        out_shape=[jax.ShapeDtypeStruct((HEADS, ATT_BLK, BAND_KEYS), F32),
                   jax.ShapeDtypeStruct((HEADS, BAND_KEYS, ATT_BLK), F32)],
        compiler_params=pltpu.CompilerParams(vmem_limit_bytes=VMEM_LIMIT),
        name="band_bias",
    )(tbl)


BAND_BLOCKS = BAND_KEYS // ATT_BLK


def _band_attn_prompt_kernel(q_ref, k_ref, vt_ref, bias_ref, o_ref, s_ref):
    i = pl.program_id(1)
    ones = jnp.ones((SUM_ROWS, ATT_BLK), BF16)

    def run(clipped):
        blocks = []
        for g in range(BAND_BLOCKS):
            j = i - (BAND_BLOCKS - 1) + g
            start = pl.multiple_of(jnp.maximum(j, 0) * ATT_BLK, ATT_BLK)
            masks = clipped and g < BAND_BLOCKS - 1
            blocks.append((start, jnp.where(j >= 0, 0.0, NEG) if masks else None))

        def scores(h):
            sl = slice(h * HEAD_DIM, (h + 1) * HEAD_DIM)
            q = q_ref[0, :, sl]
            for g, (start, penalty) in enumerate(blocks):
                s = _dot_nt(k_ref[0, pl.ds(start, ATT_BLK), sl], q) + bias_ref[h, g * ATT_BLK:(g + 1) * ATT_BLK, :]
                s_ref[h % 2, g] = s if penalty is None else s + penalty

        def finish(h):
            sl = slice(h * HEAD_DIM, (h + 1) * HEAD_DIM)
            s = [s_ref[h % 2, g] for g in range(BAND_BLOCKS)]
            m = functools.reduce(jnp.maximum, [jnp.max(x, axis=0, keepdims=True) for x in s])
            acc = 0.0
            for x, (start, _) in zip(s, blocks):
                vt = jnp.concatenate([vt_ref[0, sl, pl.ds(start, ATT_BLK)], ones], axis=0)
                acc = acc + _dot(vt, jnp.exp2(x - m).astype(BF16))
            o_ref[0, :, sl] = (acc[:HEAD_DIM] / acc[HEAD_DIM:HEAD_DIM + 1]).T.astype(BF16)

        scores(0)
        for h in range(HEADS):
            if h + 1 < HEADS:
                scores(h + 1)
            finish(h)

    pl.when(i >= BAND_BLOCKS - 1)(lambda: run(clipped=False))
    pl.when(i < BAND_BLOCKS - 1)(lambda: run(clipped=True))


def _band_attn_prompt(q, k, vt, bias_kq):
    bsz, s, _ = q.shape
    qblk = pl.BlockSpec((1, ATT_BLK, WIDTH), lambda b, i: (b, i, 0))
    return pl.pallas_call(
        _band_attn_prompt_kernel,
        grid=(bsz, s // ATT_BLK),
        in_specs=[qblk,
                  pl.BlockSpec((1, s, WIDTH), lambda b, i: (b, 0, 0)),
                  pl.BlockSpec((1, WIDTH, s), lambda b, i: (b, 0, 0)),
                  _resident(bias_kq.shape, lambda b, i: (0, 0, 0))],
        out_specs=qblk,
        out_shape=jax.ShapeDtypeStruct(q.shape, BF16),
        scratch_shapes=[pltpu.VMEM((2, BAND_BLOCKS, ATT_BLK, ATT_BLK), F32)],
        compiler_params=_params("parallel", "parallel"),
        name="band_attn_prompt",
    )(q, k, vt, bias_kq)


def _decay_lanes(c, key_side):
    slot = (lax.broadcasted_iota(jnp.int32, (HEADS, WIDTH), 1)
            - HEAD_DIM * lax.broadcasted_iota(jnp.int32, (HEADS, WIDTH), 0))
    in_group = lax.broadcasted_iota(jnp.int32, (1, WIDTH), 1) % HEAD_DIM
    first_one, first_term, sign = (3, 0, -1.0) if key_side else (0, 3, 1.0)
    out = jnp.where((in_group >= first_one) & (in_group < first_one + 3), 1.0, 0.0)
    for t, part in enumerate(_split3(c)):
        out = out + _dot(part, jnp.where(slot == first_term + t, sign, 0.0).astype(BF16))
    return out.astype(BF16)


PAD_DIM = 2 * HEAD_DIM
SUM_ROWS = 16


def _fox_attn_prompt_kernel(q_ref, k_ref, vt_ref, ccol_ref, o_ref,
                            kp_ref, qp_ref, s_ref, m_ref, acc_ref):
    i = pl.program_id(1)
    qstart = pl.multiple_of(i * ATT_BLK, ATT_BLK)

    @pl.when(i == 0)
    def _():
        extra = _decay_lanes(ccol_ref[0] * LOG2E, key_side=True)
        for h in range(HEADS):
            sl = slice(h * HEAD_DIM, (h + 1) * HEAD_DIM)
            kp_ref[:, h * PAD_DIM:(h + 1) * PAD_DIM] = jnp.concatenate([k_ref[0, :, sl], extra[:, sl]], axis=1)

    extra = _decay_lanes(ccol_ref[0, pl.ds(qstart, ATT_BLK), :] * LOG2E, key_side=False)
    for h in range(HEADS):
        sl = slice(h * HEAD_DIM, (h + 1) * HEAD_DIM)
        qp_ref[:, h * PAD_DIM:(h + 1) * PAD_DIM] = jnp.concatenate([q_ref[0, :, sl], extra[:, sl]], axis=1)

    key = lax.broadcasted_iota(jnp.int32, (ATT_BLK, ATT_BLK), 0)
    qry = lax.broadcasted_iota(jnp.int32, (ATT_BLK, ATT_BLK), 1)
    causal = key <= qry
    ones = jnp.ones((SUM_ROWS, ATT_BLK), BF16)
    m_ref[...] = jnp.full(m_ref.shape, NEG, F32)
    acc_ref[...] = jnp.zeros(acc_ref.shape, F32)

    def scores(j, h):
        start = pl.multiple_of(j * ATT_BLK, ATT_BLK)
        pad = slice(h * PAD_DIM, (h + 1) * PAD_DIM)
        return _dot_nt(kp_ref[pl.ds(start, ATT_BLK), pad], qp_ref[:, pad])

    def update(j, h, s, masked):
        start = pl.multiple_of(j * ATT_BLK, ATT_BLK)
        if masked:
            s = jnp.where(causal, s, NEG)
        m_old = m_ref[h]
        m_new = jnp.maximum(m_old, jnp.max(s, axis=0, keepdims=True))
        alpha = jnp.exp2(m_old - m_new)
        p = jnp.exp2(s - m_new)
        vt = vt_ref[0, h * HEAD_DIM:(h + 1) * HEAD_DIM, pl.ds(start, ATT_BLK)]
        acc_ref[h] = alpha * acc_ref[h] + _dot(jnp.concatenate([vt, ones], axis=0), p.astype(BF16))
        m_ref[h] = m_new

    def step(j, cur, masked=False, prefetch=True):
        if prefetch:
            for h in range(HEADS):
                s_ref[1 - cur, h] = scores(j + 1, h)
        for h in range(HEADS):
            update(j, h, s_ref[cur, h], masked)

    for h in range(HEADS):
        s_ref[0, h] = scores(0, h)

    def pair(t, carry):
        step(2 * t, 0)
        step(2 * t + 1, 1)
        return carry

    lax.fori_loop(0, i // 2, pair, 0)

    @pl.when(i % 2 == 1)
    def _():
        step(i - 1, 0)
        step(i, 1, masked=True, prefetch=False)

    @pl.when(i % 2 == 0)
    def _():
        step(i, 0, masked=True, prefetch=False)

    for h in range(HEADS):
        sl = slice(h * HEAD_DIM, (h + 1) * HEAD_DIM)
        out = acc_ref[h, :HEAD_DIM, :] / acc_ref[h, HEAD_DIM:HEAD_DIM + 1, :]
        o_ref[0, :, sl] = out.T.astype(BF16)


def _fox_attn_prompt(q, k, vt, ccol):
    bsz, s, _ = q.shape
    qblk = pl.BlockSpec((1, ATT_BLK, WIDTH), lambda b, i: (b, i, 0))
    return pl.pallas_call(
        _fox_attn_prompt_kernel,
        grid=(bsz, s // ATT_BLK),
        in_specs=[qblk,
                  pl.BlockSpec((1, s, WIDTH), lambda b, i: (b, 0, 0)),
                  pl.BlockSpec((1, WIDTH, s), lambda b, i: (b, 0, 0)),
                  pl.BlockSpec((1, s, HEADS), lambda b, i: (b, 0, 0))],
        out_specs=qblk,
        out_shape=jax.ShapeDtypeStruct(q.shape, BF16),
        scratch_shapes=[pltpu.VMEM((s, HEADS * PAD_DIM), BF16),
                        pltpu.VMEM((ATT_BLK, HEADS * PAD_DIM), BF16),
                        pltpu.VMEM((2, HEADS, ATT_BLK, ATT_BLK), F32),
                        pltpu.VMEM((HEADS, 1, ATT_BLK), F32),
                        pltpu.VMEM((HEADS, HEAD_DIM + SUM_ROWS, ATT_BLK), F32)],
        compiler_params=_params("parallel", "arbitrary"),
        name="fox_attn_prompt",
    )(q, k, vt, ccol)


def _pipelined_heads(scores, finish):
    ahead = scores(0)
    for h in range(HEADS):
        current = ahead
        if h + 1 < HEADS:
            ahead = scores(h + 1)
        finish(h, current)


def _sample_attention(s_past, s_new, vct, vnt):
    m = jnp.maximum(jnp.max(s_past, axis=-1, keepdims=True), jnp.max(s_new, axis=-1, keepdims=True))
    p_past = jnp.exp2(s_past - m)
    p_new = jnp.exp2(s_new - m)
    l = jnp.sum(p_past, axis=-1, keepdims=True) + jnp.sum(p_new, axis=-1, keepdims=True)
    acc = _dot_nt(p_past.astype(BF16), vct) + _dot_nt(p_new.astype(BF16), vnt)
    return (acc / l).astype(BF16)


def _band_attn_sample_kernel(q_ref, kct_ref, vct_ref, kn_ref, vnt_ref, bias_ref, o_ref):
    t = q_ref.shape[1]
    past = kct_ref.shape[2]

    def scores(h):
        sl = slice(h * HEAD_DIM, (h + 1) * HEAD_DIM)
        q = q_ref[0, :, sl]
        s_past = _dot(q, kct_ref[0, sl, :].astype(BF16)) + bias_ref[h, 0:t, BAND - past:BAND]
        s_new = _dot_nt(q, kn_ref[0, :, sl]) + bias_ref[h, 0:t, BAND:BAND + t]
        return s_past, s_new

    def finish(h, s):
        sl = slice(h * HEAD_DIM, (h + 1) * HEAD_DIM)
        o_ref[0, :, sl] = _sample_attention(*s, vct_ref[0, sl, :].astype(BF16), vnt_ref[0, sl, :])

    _pipelined_heads(scores, finish)


def _band_attn_sample(q, k_cache_t, v_cache_t, k_new, v_new_t, bias):
    bsz, t, _ = q.shape
    past = k_cache_t.shape[2]
    new = pl.BlockSpec((1, t, WIDTH), lambda b: (b, 0, 0))
    new_t = pl.BlockSpec((1, WIDTH, t), lambda b: (b, 0, 0))
    old_t = pl.BlockSpec((1, WIDTH, past), lambda b: (b, 0, 0))
    return pl.pallas_call(
        _band_attn_sample_kernel,
        grid=(bsz,),
        in_specs=[new, old_t, old_t, new, new_t, _resident(bias.shape, lambda b: (0, 0, 0))],
        out_specs=new,
        out_shape=jax.ShapeDtypeStruct(q.shape, BF16),
        compiler_params=_params("parallel"),
        name="band_attn_sample",
    )(q, k_cache_t, v_cache_t, k_new, v_new_t, bias)


def _fox_attn_sample_kernel(q_ref, kct_ref, vct_ref, kn_ref, vnt_ref, ccol_ref, crow_ref, o_ref):
    t = q_ref.shape[1]
    past = kct_ref.shape[2]
    row = lax.broadcasted_iota(jnp.int32, (t, t), 0)
    col = lax.broadcasted_iota(jnp.int32, (t, t), 1)
    causal = col <= row

    def scores(h):
        sl = slice(h * HEAD_DIM, (h + 1) * HEAD_DIM)
        q = q_ref[0, :, sl]
        cq = ccol_ref[0, :, h:h + 1]
        s_past = _dot(q, kct_ref[0, sl, :].astype(BF16)) + (cq - crow_ref[0, h:h + 1, 0:past]) * LOG2E
        s_new = _dot_nt(q, kn_ref[0, :, sl]) + (cq - crow_ref[0, h:h + 1, past:past + t]) * LOG2E
        return s_past, jnp.where(causal, s_new, NEG)

    def finish(h, s):
        sl = slice(h * HEAD_DIM, (h + 1) * HEAD_DIM)
        o_ref[0, :, sl] = _sample_attention(*s, vct_ref[0, sl, :].astype(BF16), vnt_ref[0, sl, :])

    _pipelined_heads(scores, finish)


def _fox_attn_sample(q, k_cache_t, v_cache_t, k_new, v_new_t, ccol, crow):
    bsz, t, _ = q.shape
    past = k_cache_t.shape[2]
    assert past % t == 0
    new = pl.BlockSpec((1, t, WIDTH), lambda b: (b, 0, 0))
    new_t = pl.BlockSpec((1, WIDTH, t), lambda b: (b, 0, 0))
    old_t = pl.BlockSpec((1, WIDTH, past), lambda b: (b, 0, 0))
    return pl.pallas_call(
        _fox_attn_sample_kernel,
        grid=(bsz,),
        in_specs=[new, old_t, old_t, new, new_t,
                  pl.BlockSpec((1, t, HEADS), lambda b: (b, past // t, 0)),
                  pl.BlockSpec((1, HEADS, past + t), lambda b: (b, 0, 0))],
        out_specs=new,
        out_shape=jax.ShapeDtypeStruct(q.shape, BF16),
        compiler_params=_params("parallel"),
        name="fox_attn_sample",
    )(q, k_cache_t, v_cache_t, k_new, v_new_t, ccol, crow)


def _postmix_kernel(x_ref, oa_ref, ob_ref, gpre_ref, gpost_ref, wg_ref, wpa_ref, wpb_ref, wout_ref, y_ref):
    x = x_ref[...]
    d = x.shape[-1]
    h = _rmsnorm(x, gpre_ref[...]).astype(BF16)
    gate_a = jax.nn.sigmoid(_dot(h, wg_ref[:, :d]))
    gate_b = jax.nn.sigmoid(_dot(h, wg_ref[:, d:]))
    merged = gate_a * _dot(oa_ref[...], wpa_ref[...]) + gate_b * _dot(ob_ref[...], wpb_ref[...])
    y = x + _rmsnorm(_dot(merged.astype(BF16), wout_ref[...]), gpost_ref[...])
    y_ref[...] = _interleave(y)


def _interleave(rows):
    n, d = rows.shape
    return jnp.swapaxes(rows.reshape(SUBLANES, n // SUBLANES, d), 0, 1)


def _deinterleave(planes):
    p, s, d = planes.shape
    return jnp.swapaxes(planes, 0, 1).reshape(s * p, d)


def _postmix(x, oa, ob, g_pre, g_post, w_g, w_pa, w_pb, w_out):
    n, d = x.shape
    tm = min(ROW_TILE, n)
    row = lambda i: (i, 0)
    fixed = lambda i: (0, 0)
    return pl.pallas_call(
        _postmix_kernel,
        grid=(n // tm,),
        in_specs=[pl.BlockSpec((tm, d), row),
                  pl.BlockSpec((tm, WIDTH), row),
                  pl.BlockSpec((tm, WIDTH), row),
                  pl.BlockSpec((1, d), fixed),
                  pl.BlockSpec((1, d), fixed),
                  _resident(w_g.shape, fixed),
                  _resident(w_pa.shape, fixed),
                  _resident(w_pb.shape, fixed),
                  _resident(w_out.shape, fixed)],
        out_specs=pl.BlockSpec((tm // SUBLANES, SUBLANES, d), lambda i: (i, 0, 0)),
        out_shape=jax.ShapeDtypeStruct((n // SUBLANES, SUBLANES, d), F32),
        compiler_params=_params("parallel"),
        name="postmix",
    )(x, oa, ob, g_pre, g_post, w_g, w_pa, w_pb, w_out)


def _ffn_kernel(x_ref, st_ref, gpre_ref, gpost_ref, wup_ref, cw_ref, cb_ref, wdn_ref,
                y_ref, nst_ref, hist_ref, ext_ref, h_ref, f_ref, *, nseg):
    @pl.when(pl.program_id(1) == 0)
    def _():
        hist_ref[...] = st_ref[...]

    planes, _, d = x_ref.shape
    tm = planes * SUBLANES
    d_ff = wdn_ref.shape[0]
    h_ref[...] = _rmsnorm(x_ref[...].reshape(tm, d), gpre_ref[...]).astype(BF16)
    n_chunks = d_ff // FFN_COLS
    first_sublane = lax.broadcasted_iota(jnp.int32, (SUBLANES, FFN_COLS), 0) == 0

    def parts(c):
        for part in range(2):
            yield (slice(part * d_ff + c * FFN_COLS, part * d_ff + (c + 1) * FFN_COLS),
                   slice(part * FFN_COLS, (part + 1) * FFN_COLS))

    def up(c):
        ext = ext_ref.at[c % 2]
        for cols, dst in parts(c):
            u = _dot(h_ref[...], wup_ref[:, cols]).reshape(planes, SUBLANES, FFN_COLS)
            ext[CONV_W - 1:, :, dst] = u
            for k in range(CONV_W - 1):
                last = u[planes - (CONV_W - 1) + k]
                if nseg == 1:
                    ext[k, :, dst] = jnp.where(first_sublane, hist_ref[0, k:k + 1, cols], pltpu.roll(last, 1, 0))
                    hist_ref[0, k:k + 1, cols] = last[SUBLANES - 1:, :]
                else:
                    ext[k, :, dst] = hist_ref[:, k, cols]
                    hist_ref[:, k, cols] = last

    def down(c):
        ext = ext_ref.at[c % 2]
        halves = []
        for cols, dst in parts(c):
            y = cb_ref[:, cols]
            for tap in range(CONV_W):
                y = y + ext[tap:tap + planes, :, dst] * cw_ref[tap:tap + 1, cols]
            halves.append(y)
        act = (jax.nn.gelu(halves[0]) * halves[1]).reshape(tm, FFN_COLS)
        f_ref[:, c * FFN_COLS:(c + 1) * FFN_COLS] = act.astype(BF16)

    up(0)
    for c in range(n_chunks):
        if c + 1 < n_chunks:
            up(c + 1)
        down(c)
    f = _dot(f_ref[...], wdn_ref[...])
    y = x_ref[...].reshape(tm, d) + _rmsnorm(f, gpost_ref[...])
    y_ref[...] = _deinterleave(y.reshape(planes, SUBLANES, d))
    nst_ref[...] = hist_ref[...]


def _ffn(x, bsz, s, state, g_pre, g_post, w_up, conv_w, conv_b, w_down):
    d = x.shape[-1]
    up = w_up.shape[1]
    tm = min(ROW_TILE, bsz * s)
    planes = tm // SUBLANES
    if s >= tm:
        nseg, tiles = 1, s // tm
    else:
        nseg, tiles = tm // s, 1
        assert nseg == SUBLANES
    outer = bsz // nseg
    row = lambda o, t: (o * tiles + t, 0)
    fixed = lambda o, t: (0, 0)
    st_spec = pl.BlockSpec((nseg, CONV_W - 1, up), lambda o, t: (o, 0, 0))
    y, new_state = pl.pallas_call(
        functools.partial(_ffn_kernel, nseg=nseg),
        grid=(outer, tiles),
        in_specs=[pl.BlockSpec((planes, SUBLANES, d), lambda o, t: (o * tiles + t, 0, 0)),
                  st_spec,
                  pl.BlockSpec((1, d), fixed),
                  pl.BlockSpec((1, d), fixed),
                  _resident(w_up.shape, fixed),
                  pl.BlockSpec(conv_w.shape, fixed),
                  pl.BlockSpec((1, up), fixed),
                  _resident(w_down.shape, fixed)],
        out_specs=[pl.BlockSpec((tm, d), row), st_spec],
        out_shape=[jax.ShapeDtypeStruct((bsz * s, d), F32),
                   jax.ShapeDtypeStruct(state.shape, F32)],
        scratch_shapes=[pltpu.VMEM((nseg, CONV_W - 1, up), F32),
                        pltpu.VMEM((2, CONV_W - 1 + planes, SUBLANES, 2 * FFN_COLS), F32),
                        pltpu.VMEM((tm, d), BF16),
                        pltpu.VMEM((tm, w_down.shape[0]), BF16)],
        compiler_params=_params("arbitrary", "arbitrary"),
        name="conv_ffn",
    )(x, state, g_pre, g_post, w_up, conv_w, conv_b, w_down)
    return y.reshape(bsz, s, d), new_state


def _layer(x, caches, conv_state, bias, w):
    bsz, s, d = x.shape
    n = bsz * s
    x2 = x.reshape(n, d)
    keep = min(BAND, s)
    (qa, qb, ka16, kb16, vat16, vbt16, kat, vat, kbt, vbt, lft) = _inproj(
        x2, w["g_pre_mix"], w["w_qkv"], w["w_ft"], w["b_f"], seq_len=s, band_keep=keep)
    seq = lambda a: a.reshape(bsz, s, a.shape[-1])
    if caches is None:
        ccol, _ = _cumsum(lft)
        oa = _band_attn_prompt(seq(qa), seq(ka16), vat16, bias[1])
        ob = _fox_attn_prompt(seq(qb), seq(kb16), vbt16, ccol)
    else:
        ckat, cvat, ckbt, cvbt, clft = caches
        ccol, crow = _cumsum(clft, lft)
        oa = _band_attn_sample(seq(qa), ckat, cvat, seq(ka16), vat16, bias[0])
        ob = _fox_attn_sample(seq(qb), ckbt, cvbt, seq(kb16), vbt16, ccol, crow)
    x1 = _postmix(x2, oa.reshape(n, WIDTH), ob.reshape(n, WIDTH), w["g_pre_mix"], w["g_post_mix"],
                  w["w_g"], w["w_pa"], w["w_pb"], w["w_out"])
    y, new_conv = _ffn(x1, bsz, s, conv_state, w["g_pre_ffn"], w["g_post_ffn"],
                       w["w_up"], w["conv_w"], w["conv_b"], w["w_down"])
    heads = lambda a: a.reshape(bsz, HEADS, HEAD_DIM, a.shape[-1]).transpose(0, 3, 1, 2)
    return y, (heads(kat), heads(vat), heads(kbt), heads(vbt), lft.transpose(0, 2, 1), new_conv)


def _channel_major(cache):
    bsz, past = cache.shape[:2]
    return cache.transpose(0, 2, 3, 1).reshape(bsz, WIDTH, past)


def kernel(x_prompt, x_sample, cache_k_a, cache_v_a, cache_k_b, cache_v_b, cache_logf_b, state_conv_ffn,
           g_pre_mix, g_post_mix, g_pre_ffn, g_post_ffn, w_in, b_f, rel_table, w_proj_a, w_proj_b, w_out,
           w_up, conv_w, conv_b, w_down):
    depth = w_in.shape[0]
    up = w_up.shape[-1]
    x_p, x_s = x_prompt, x_sample
    p_states, s_states = [], []
    for l in range(depth):
        w = {
            "g_pre_mix": g_pre_mix[l][None], "g_post_mix": g_post_mix[l][None],
            "g_pre_ffn": g_pre_ffn[l][None], "g_post_ffn": g_post_ffn[l][None],
            "w_qkv": w_in[l][:, :6 * WIDTH].astype(BF16),
            "w_ft": w_in[l][:, 6 * WIDTH:6 * WIDTH + HEADS].T.astype(BF16),
            "w_g": w_in[l][:, 6 * WIDTH + HEADS:].astype(BF16),
            "b_f": b_f[l][:, None],
            "w_pa": w_proj_a[l].astype(BF16), "w_pb": w_proj_b[l].astype(BF16),
            "w_out": w_out[l].astype(BF16), "w_up": w_up[l].astype(BF16),
            "conv_w": conv_w[l], "conv_b": conv_b[l][None], "w_down": w_down[l].astype(BF16),
        }
        bias = _band_bias(rel_table[l])
        zero_state = jnp.zeros((x_p.shape[0], CONV_W - 1, up), F32)
        x_p, (ka, va, kb, vb, lf, cv) = _layer(x_p, None, zero_state, bias, w)
        p_states.append((ka, va, kb, vb, lf, cv))
        caches = (_channel_major(cache_k_a[l]), _channel_major(cache_v_a[l]), _channel_major(cache_k_b[l]),
                  _channel_major(cache_v_b[l]), cache_logf_b[l].transpose(0, 2, 1))
        x_s, st = _layer(x_s, caches, state_conv_ffn[l], bias, w)
        s_states.append(st)
    stack = lambda states: [jnp.stack(s) for s in zip(*states)]
    return (x_p, x_s, *stack(p_states), *stack(s_states))
```

```python
import functools

import jax
import jax.numpy as jnp
from jax import lax
from jax.experimental import pallas as pl
from jax.experimental.pallas import tpu as pltpu

F32, BF16 = jnp.float32, jnp.bfloat16

HEADS = 8
HEAD_DIM = 64
WIDTH = HEADS * HEAD_DIM
CHUNK = 64
PAST_CHUNKS = 8
BAND = PAST_CHUNKS * CHUNK
REL_CLIP = 128
REL_SIZE = 2 * REL_CLIP + 1
CONV_W = 3
EPS = 1e-6
NEG = -1e30
LOG2E = 1.4426950408889634

SUBLANES = 8
ATT_BLK = 256
BAND_KEYS = 3 * ATT_BLK
ROW_TILE = 512
FFN_COLS = 256
CUMSUM_BLK = 256
VMEM_LIMIT = 56 * 1024 * 1024


def _params(*sem):
    return pltpu.CompilerParams(dimension_semantics=sem, vmem_limit_bytes=VMEM_LIMIT)


def _resident(shape, index_map):
    return pl.BlockSpec(shape, index_map, pipeline_mode=pl.Buffered(1))


def _rmsnorm(x, g):
    return x * lax.rsqrt(jnp.mean(x * x, axis=-1, keepdims=True) + EPS) * g


def _dot(a, b):
    return jnp.dot(a, b, preferred_element_type=F32)


def _dot_nt(a, b):
    return lax.dot_general(a, b, (((1,), (1,)), ((), ())), preferred_element_type=F32)


def _split3(x):
    hi = x.astype(BF16)
    r = x - hi.astype(F32)
    mid = r.astype(BF16)
    lo = (r - mid.astype(F32)).astype(BF16)
    return hi, mid, lo


def _inproj_kernel(x_ref, g_ref, w_ref, wft_ref, bf_ref,
                   qa_ref, qb_ref, ka_ref, kb_ref, vat16_ref, vbt16_ref,
                   kat_ref, vat_ref, kbt_ref, vbt_ref, lft_ref):
    h = _rmsnorm(x_ref[...], g_ref[...]).astype(BF16)
    scale = HEAD_DIM ** -0.5

    def proj(c):
        return _dot(h, w_ref[:, c * WIDTH:(c + 1) * WIDTH])

    def put(ref, zt):
        per_seq = zt.shape[1] // ref.shape[0]
        for sq in range(ref.shape[0]):
            ref[sq] = zt[:, sq * per_seq:(sq + 1) * per_seq]

    qa_ref[...] = (proj(0) * (scale * LOG2E)).astype(BF16)
    qb_ref[...] = (proj(3) * (scale * LOG2E)).astype(BF16)
    for c, row_ref, t_ref in ((1, ka_ref, kat_ref), (4, kb_ref, kbt_ref)):
        z = proj(c)
        row_ref[...] = z.astype(BF16)
        put(t_ref, z.T)
    for c, t16_ref, t_ref in ((2, vat16_ref, vat_ref), (5, vbt16_ref, vbt_ref)):
        zt = proj(c).T
        put(t_ref, zt)
        put(t16_ref, zt.astype(BF16))
    put(lft_ref, jax.nn.log_sigmoid(_dot_nt(wft_ref[...], h) + bf_ref[...]))


def _inproj(x, g, w_qkv, w_ft, b_f, seq_len, band_keep):
    n, d = x.shape
    bsz = n // seq_len
    tm = min(ROW_TILE, n)
    tiles_per_seq = max(seq_len // tm, 1)
    seqs_per_tile = max(tm // seq_len, 1)
    cols = tm // seqs_per_tile
    assert band_keep == cols
    row = lambda i: (i, 0)
    fixed = lambda i: (0, 0)
    along = lambda i: (i // tiles_per_seq, 0, i % tiles_per_seq)
    kept = lambda i: (i // tiles_per_seq, 0, 0)
    rows16 = (jax.ShapeDtypeStruct((n, WIDTH), BF16), pl.BlockSpec((tm, WIDTH), row))

    def chan(channels, length, dtype, index_map):
        return (jax.ShapeDtypeStruct((bsz, channels, length), dtype),
                pl.BlockSpec((seqs_per_tile, channels, cols), index_map))

    outs = [rows16] * 4
    outs += [chan(WIDTH, seq_len, BF16, along)] * 2
    outs += [chan(WIDTH, band_keep, F32, kept)] * 2
    outs += [chan(WIDTH, seq_len, F32, along)] * 2
    outs += [chan(HEADS, seq_len, F32, along)]
    return pl.pallas_call(
        _inproj_kernel,
        grid=(n // tm,),
        in_specs=[pl.BlockSpec((tm, d), row),
                  pl.BlockSpec((1, d), fixed),
                  _resident(w_qkv.shape, fixed),
                  pl.BlockSpec(w_ft.shape, fixed),
                  pl.BlockSpec((HEADS, 1), fixed)],
        out_specs=[spec for _, spec in outs],
        out_shape=[shape for shape, _ in outs],
        compiler_params=_params("arbitrary"),
        name="inproj",
    )(x, g, w_qkv, w_ft, b_f)


def _cumsum_kernel(*refs, seg_lens):
    seg_refs = refs[:len(seg_lens)]
    ccol_ref, crow_ref = refs[len(seg_lens):]
    carry_c = jnp.zeros((1, HEADS), F32)
    carry_r = jnp.zeros((HEADS, 1), F32)
    off = 0
    for ref, n in zip(seg_refs, seg_lens):
        for o in range(0, n, CUMSUM_BLK):
            b = min(CUMSUM_BLK, n - o)
            parts = _split3(ref[0, :, o:o + b])
            r = lax.broadcasted_iota(jnp.int32, (b, b), 0)
            c = lax.broadcasted_iota(jnp.int32, (b, b), 1)
            lower = jnp.where(r >= c, 1.0, 0.0).astype(BF16)
            upper = jnp.where(r <= c, 1.0, 0.0).astype(BF16)
            cc = carry_c
            cr = carry_r
            for p in parts:
                cc = cc + _dot_nt(lower, p)
                cr = cr + _dot(p, upper)
            ccol_ref[0, off + o:off + o + b, :] = cc
            crow_ref[0, :, off + o:off + o + b] = cr
            carry_c = cc[b - 1:b, :]
            carry_r = cr[:, b - 1:b]
        off += n


def _cumsum(*segs):
    bsz = segs[0].shape[0]
    seg_lens = tuple(s.shape[2] for s in segs)
    total = sum(seg_lens)
    return pl.pallas_call(
        functools.partial(_cumsum_kernel, seg_lens=seg_lens),
        grid=(bsz,),
        in_specs=[pl.BlockSpec((1, HEADS, n), lambda b: (b, 0, 0)) for n in seg_lens],
        out_specs=[pl.BlockSpec((1, total, HEADS), lambda b: (b, 0, 0)),
                   pl.BlockSpec((1, HEADS, total), lambda b: (b, 0, 0))],
        out_shape=[jax.ShapeDtypeStruct((bsz, total, HEADS), F32),
                   jax.ShapeDtypeStruct((bsz, HEADS, total), F32)],
        compiler_params=_params("parallel"),
        name="cumsum_logf",
    )(*segs)


REL_PAD = 384
DIST_SPAN = 1024


def _band_bias_kernel(tbl_ref, qk_ref, kq_ref):
    j = lax.broadcasted_iota(jnp.int32, (REL_PAD, DIST_SPAN), 1)
    r = lax.broadcasted_iota(jnp.int32, (REL_PAD, DIST_SPAN), 0)
    parts = _split3(tbl_ref[...])

    def by_offset(dist):
        onehot = jnp.where(jnp.clip(dist, -REL_CLIP, REL_CLIP) + REL_CLIP == r, 1.0, 0.0).astype(BF16)
        return sum(_dot(p, onehot) for p in parts) * LOG2E

    def skewed(e_row, rows, cols, shift):
        wide = jnp.broadcast_to(e_row, (rows, DIST_SPAN))
        return pltpu.roll(wide, shift % DIST_SPAN, 1, stride=1, stride_axis=0)[:, :cols]

    e_qk = by_offset((BAND_KEYS - 1) - j)
    e_kq = by_offset(j - (ATT_BLK - 1))
    qc = lax.broadcasted_iota(jnp.int32, (ATT_BLK, BAND_KEYS), 0) // CHUNK
    kc = lax.broadcasted_iota(jnp.int32, (ATT_BLK, BAND_KEYS), 1) // CHUNK
    vis_qk = (kc >= qc) & (kc <= qc + PAST_CHUNKS)
    kc = lax.broadcasted_iota(jnp.int32, (BAND_KEYS, ATT_BLK), 0) // CHUNK
    qc = lax.broadcasted_iota(jnp.int32, (BAND_KEYS, ATT_BLK), 1) // CHUNK
    vis_kq = (kc >= qc) & (kc <= qc + PAST_CHUNKS)
    for h in range(HEADS):
        qk_ref[h] = jnp.where(vis_qk, skewed(e_qk[h:h + 1, :], ATT_BLK, BAND_KEYS, -(ATT_BLK - 1)), NEG)
        kq_ref[h] = jnp.where(vis_kq, skewed(e_kq[h:h + 1, :], BAND_KEYS, ATT_BLK, -(BAND_KEYS - 1)), NEG)


def _band_bias(rel_table):
    tbl = jnp.pad(rel_table, ((0, 0), (0, REL_PAD - REL_SIZE)))
    return pl.pallas_call(
        _band_bias_kernel,
        out_shape=[jax.ShapeDtypeStruct((HEADS, ATT_BLK, BAND_KEYS), F32),
                   jax.ShapeDtypeStruct((HEADS, BAND_KEYS, ATT_BLK), F32)],
        compiler_params=pltpu.CompilerParams(vmem_limit_bytes=VMEM_LIMIT),
        name="band_bias",
    )(tbl)


BAND_BLOCKS = BAND_KEYS // ATT_BLK


def _band_attn_prompt_kernel(q_ref, k_ref, vt_ref, bias_ref, o_ref, s_ref):
    i = pl.program_id(1)
    ones = jnp.ones((SUM_ROWS, ATT_BLK), BF16)

    def run(clipped):
        blocks = []
        for g in range(BAND_BLOCKS):
            j = i - (BAND_BLOCKS - 1) + g
            start = pl.multiple_of(jnp.maximum(j, 0) * ATT_BLK, ATT_BLK)
            masks = clipped and g < BAND_BLOCKS - 1
            blocks.append((start, jnp.where(j >= 0, 0.0, NEG) if masks else None))

        def scores(h):
            sl = slice(h * HEAD_DIM, (h + 1) * HEAD_DIM)
            q = q_ref[0, :, sl]
            for g, (start, penalty) in enumerate(blocks):
                s = _dot_nt(k_ref[0, pl.ds(start, ATT_BLK), sl], q) + bias_ref[h, g * ATT_BLK:(g + 1) * ATT_BLK, :]
                s_ref[h % 2, g] = s if penalty is None else s + penalty

        def finish(h):
            sl = slice(h * HEAD_DIM, (h + 1) * HEAD_DIM)
            s = [s_ref[h % 2, g] for g in range(BAND_BLOCKS)]
            m = functools.reduce(jnp.maximum, [jnp.max(x, axis=0, keepdims=True) for x in s])
            acc = 0.0
            for x, (start, _) in zip(s, blocks):
                vt = jnp.concatenate([vt_ref[0, sl, pl.ds(start, ATT_BLK)], ones], axis=0)
                acc = acc + _dot(vt, jnp.exp2(x - m).astype(BF16))
            o_ref[0, sl, :] = (acc[:HEAD_DIM] / acc[HEAD_DIM:HEAD_DIM + 1]).astype(BF16)

        scores(0)
        for h in range(HEADS):
            if h + 1 < HEADS:
                scores(h + 1)
            finish(h)

    pl.when(i >= BAND_BLOCKS - 1)(lambda: run(clipped=False))
    pl.when(i < BAND_BLOCKS - 1)(lambda: run(clipped=True))


def _band_attn_prompt(q, k, vt, bias_kq):
    bsz, s, _ = q.shape
    qblk = pl.BlockSpec((1, ATT_BLK, WIDTH), lambda b, i: (b, i, 0))
    return pl.pallas_call(
        _band_attn_prompt_kernel,
        grid=(bsz, s // ATT_BLK),
        in_specs=[qblk,
                  pl.BlockSpec((1, s, WIDTH), lambda b, i: (b, 0, 0)),
                  pl.BlockSpec((1, WIDTH, s), lambda b, i: (b, 0, 0)),
                  _resident(bias_kq.shape, lambda b, i: (0, 0, 0))],
        out_specs=pl.BlockSpec((1, WIDTH, ATT_BLK), lambda b, i: (b, 0, i)),
        out_shape=jax.ShapeDtypeStruct((bsz, WIDTH, s), BF16),
        scratch_shapes=[pltpu.VMEM((2, BAND_BLOCKS, ATT_BLK, ATT_BLK), F32)],
        compiler_params=_params("parallel", "parallel"),
        name="band_attn_prompt",
    )(q, k, vt, bias_kq)


def _decay_lanes(c, key_side):
    slot = (lax.broadcasted_iota(jnp.int32, (HEADS, WIDTH), 1)
            - HEAD_DIM * lax.broadcasted_iota(jnp.int32, (HEADS, WIDTH), 0))
    in_group = lax.broadcasted_iota(jnp.int32, (1, WIDTH), 1) % HEAD_DIM
    first_one, first_term, sign = (3, 0, -1.0) if key_side else (0, 3, 1.0)
    out = jnp.where((in_group >= first_one) & (in_group < first_one + 3), 1.0, 0.0)
    for t, part in enumerate(_split3(c)):
        out = out + _dot(part, jnp.where(slot == first_term + t, sign, 0.0).astype(BF16))
    return out.astype(BF16)


PAD_DIM = 2 * HEAD_DIM
SUM_ROWS = 16


def _fox_attn_prompt_kernel(q_ref, k_ref, vt_ref, ccol_ref, o_ref,
                            kp_ref, qp_ref, s_ref, m_ref, acc_ref):
    i = pl.program_id(1)
    qstart = pl.multiple_of(i * ATT_BLK, ATT_BLK)

    @pl.when(i == 0)
    def _():
        extra = _decay_lanes(ccol_ref[0] * LOG2E, key_side=True)
        for h in range(HEADS):
            sl = slice(h * HEAD_DIM, (h + 1) * HEAD_DIM)
            kp_ref[:, h * PAD_DIM:(h + 1) * PAD_DIM] = jnp.concatenate([k_ref[0, :, sl], extra[:, sl]], axis=1)

    extra = _decay_lanes(ccol_ref[0, pl.ds(qstart, ATT_BLK), :] * LOG2E, key_side=False)
    for h in range(HEADS):
        sl = slice(h * HEAD_DIM, (h + 1) * HEAD_DIM)
        qp_ref[:, h * PAD_DIM:(h + 1) * PAD_DIM] = jnp.concatenate([q_ref[0, :, sl], extra[:, sl]], axis=1)

    key = lax.broadcasted_iota(jnp.int32, (ATT_BLK, ATT_BLK), 0)
    qry = lax.broadcasted_iota(jnp.int32, (ATT_BLK, ATT_BLK), 1)
    causal = key <= qry
    ones = jnp.ones((SUM_ROWS, ATT_BLK), BF16)
    m_ref[...] = jnp.full(m_ref.shape, NEG, F32)
    acc_ref[...] = jnp.zeros(acc_ref.shape, F32)

    def scores(j, h):
        start = pl.multiple_of(j * ATT_BLK, ATT_BLK)
        pad = slice(h * PAD_DIM, (h + 1) * PAD_DIM)
        return _dot_nt(kp_ref[pl.ds(start, ATT_BLK), pad], qp_ref[:, pad])

    def update(j, h, s, masked):
        start = pl.multiple_of(j * ATT_BLK, ATT_BLK)
        if masked:
            s = jnp.where(causal, s, NEG)
        m_old = m_ref[h]
        m_new = jnp.maximum(m_old, jnp.max(s, axis=0, keepdims=True))
        alpha = jnp.exp2(m_old - m_new)
        p = jnp.exp2(s - m_new)
        vt = vt_ref[0, h * HEAD_DIM:(h + 1) * HEAD_DIM, pl.ds(start, ATT_BLK)]
        acc_ref[h] = alpha * acc_ref[h] + _dot(jnp.concatenate([vt, ones], axis=0), p.astype(BF16))
        m_ref[h] = m_new

    def step(j, cur, masked=False, prefetch=True):
        if prefetch:
            for h in range(HEADS):
                s_ref[1 - cur, h] = scores(j + 1, h)
        for h in range(HEADS):
            update(j, h, s_ref[cur, h], masked)

    for h in range(HEADS):
        s_ref[0, h] = scores(0, h)

    def pair(t, carry):
        step(2 * t, 0)
        step(2 * t + 1, 1)
        return carry

    lax.fori_loop(0, i // 2, pair, 0)

    @pl.when(i % 2 == 1)
    def _():
        step(i - 1, 0)
        step(i, 1, masked=True, prefetch=False)

    @pl.when(i % 2 == 0)
    def _():
        step(i, 0, masked=True, prefetch=False)

    for h in range(HEADS):
        sl = slice(h * HEAD_DIM, (h + 1) * HEAD_DIM)
        out = acc_ref[h, :HEAD_DIM, :] / acc_ref[h, HEAD_DIM:HEAD_DIM + 1, :]
        o_ref[0, sl, :] = out.astype(BF16)


def _fox_attn_prompt(q, k, vt, ccol):
    bsz, s, _ = q.shape
    qblk = pl.BlockSpec((1, ATT_BLK, WIDTH), lambda b, i: (b, i, 0))
    return pl.pallas_call(
        _fox_attn_prompt_kernel,
        grid=(bsz, s // ATT_BLK),
        in_specs=[qblk,
                  pl.BlockSpec((1, s, WIDTH), lambda b, i: (b, 0, 0)),
                  pl.BlockSpec((1, WIDTH, s), lambda b, i: (b, 0, 0)),
                  pl.BlockSpec((1, s, HEADS), lambda b, i: (b, 0, 0))],
        out_specs=pl.BlockSpec((1, WIDTH, ATT_BLK), lambda b, i: (b, 0, i)),
        out_shape=jax.ShapeDtypeStruct((bsz, WIDTH, s), BF16),
        scratch_shapes=[pltpu.VMEM((s, HEADS * PAD_DIM), BF16),
                        pltpu.VMEM((ATT_BLK, HEADS * PAD_DIM), BF16),
                        pltpu.VMEM((2, HEADS, ATT_BLK, ATT_BLK), F32),
                        pltpu.VMEM((HEADS, 1, ATT_BLK), F32),
                        pltpu.VMEM((HEADS, HEAD_DIM + SUM_ROWS, ATT_BLK), F32)],
        compiler_params=_params("parallel", "arbitrary"),
        name="fox_attn_prompt",
    )(q, k, vt, ccol)


def _pipelined_heads(scores, finish):
    ahead = scores(0)
    for h in range(HEADS):
        current = ahead
        if h + 1 < HEADS:
            ahead = scores(h + 1)
        finish(h, current)


def _sample_attention(s_past, s_new, vct, vnt):
    m = jnp.maximum(jnp.max(s_past, axis=-1, keepdims=True), jnp.max(s_new, axis=-1, keepdims=True))
    p_past = jnp.exp2(s_past - m)
    p_new = jnp.exp2(s_new - m)
    l = jnp.sum(p_past, axis=-1, keepdims=True) + jnp.sum(p_new, axis=-1, keepdims=True)
    acc = _dot_nt(p_past.astype(BF16), vct) + _dot_nt(p_new.astype(BF16), vnt)
    return (acc / l).astype(BF16)


def _band_attn_sample_kernel(q_ref, kct_ref, vct_ref, kn_ref, vnt_ref, bias_ref, o_ref):
    t = q_ref.shape[1]
    past = kct_ref.shape[2]

    def scores(h):
        sl = slice(h * HEAD_DIM, (h + 1) * HEAD_DIM)
        q = q_ref[0, :, sl]
        s_past = _dot(q, kct_ref[0, sl, :].astype(BF16)) + bias_ref[h, 0:t, BAND - past:BAND]
        s_new = _dot_nt(q, kn_ref[0, :, sl]) + bias_ref[h, 0:t, BAND:BAND + t]
        return s_past, s_new

    def finish(h, s):
        sl = slice(h * HEAD_DIM, (h + 1) * HEAD_DIM)
        o_ref[0, :, sl] = _sample_attention(*s, vct_ref[0, sl, :].astype(BF16), vnt_ref[0, sl, :])

    _pipelined_heads(scores, finish)


def _band_attn_sample(q, k_cache_t, v_cache_t, k_new, v_new_t, bias):
    bsz, t, _ = q.shape
    past = k_cache_t.shape[2]
    new = pl.BlockSpec((1, t, WIDTH), lambda b: (b, 0, 0))
    new_t = pl.BlockSpec((1, WIDTH, t), lambda b: (b, 0, 0))
    old_t = pl.BlockSpec((1, WIDTH, past), lambda b: (b, 0, 0))
    return pl.pallas_call(
        _band_attn_sample_kernel,
        grid=(bsz,),
        in_specs=[new, old_t, old_t, new, new_t, _resident(bias.shape, lambda b: (0, 0, 0))],
        out_specs=new,
        out_shape=jax.ShapeDtypeStruct(q.shape, BF16),
        compiler_params=_params("parallel"),
        name="band_attn_sample",
    )(q, k_cache_t, v_cache_t, k_new, v_new_t, bias)


def _fox_attn_sample_kernel(q_ref, kct_ref, vct_ref, kn_ref, vnt_ref, ccol_ref, crow_ref, o_ref):
    t = q_ref.shape[1]
    past = kct_ref.shape[2]
    row = lax.broadcasted_iota(jnp.int32, (t, t), 0)
    col = lax.broadcasted_iota(jnp.int32, (t, t), 1)
    causal = col <= row

    def scores(h):
        sl = slice(h * HEAD_DIM, (h + 1) * HEAD_DIM)
        q = q_ref[0, :, sl]
        cq = ccol_ref[0, :, h:h + 1]
        s_past = _dot(q, kct_ref[0, sl, :].astype(BF16)) + (cq - crow_ref[0, h:h + 1, 0:past]) * LOG2E
        s_new = _dot_nt(q, kn_ref[0, :, sl]) + (cq - crow_ref[0, h:h + 1, past:past + t]) * LOG2E
        return s_past, jnp.where(causal, s_new, NEG)

    def finish(h, s):
        sl = slice(h * HEAD_DIM, (h + 1) * HEAD_DIM)
        o_ref[0, :, sl] = _sample_attention(*s, vct_ref[0, sl, :].astype(BF16), vnt_ref[0, sl, :])

    _pipelined_heads(scores, finish)


def _fox_attn_sample(q, k_cache_t, v_cache_t, k_new, v_new_t, ccol, crow):
    bsz, t, _ = q.shape
    past = k_cache_t.shape[2]
    assert past % t == 0
    new = pl.BlockSpec((1, t, WIDTH), lambda b: (b, 0, 0))
    new_t = pl.BlockSpec((1, WIDTH, t), lambda b: (b, 0, 0))
    old_t = pl.BlockSpec((1, WIDTH, past), lambda b: (b, 0, 0))
    return pl.pallas_call(
        _fox_attn_sample_kernel,
        grid=(bsz,),
        in_specs=[new, old_t, old_t, new, new_t,
                  pl.BlockSpec((1, t, HEADS), lambda b: (b, past // t, 0)),
                  pl.BlockSpec((1, HEADS, past + t), lambda b: (b, 0, 0))],
        out_specs=new,
        out_shape=jax.ShapeDtypeStruct(q.shape, BF16),
        compiler_params=_params("parallel"),
        name="fox_attn_sample",
    )(q, k_cache_t, v_cache_t, k_new, v_new_t, ccol, crow)


def _postmix_kernel(x_ref, oa_ref, ob_ref, gpre_ref, gpost_ref, wg_ref, wpa_ref, wpb_ref, wout_ref, y_ref,
                    *, channel_major):
    x = x_ref[...]
    d = x.shape[-1]
    h = _rmsnorm(x, gpre_ref[...]).astype(BF16)
    gate_a = jax.nn.sigmoid(_dot(h, wg_ref[:, :d]))
    gate_b = jax.nn.sigmoid(_dot(h, wg_ref[:, d:]))
    if channel_major:
        proj = lambda o_ref, w_ref: lax.dot_general(o_ref[0], w_ref[...], (((0,), (0,)), ((), ())),
                                                    preferred_element_type=F32)
    else:
        proj = lambda o_ref, w_ref: _dot(o_ref[...], w_ref[...])
    merged = gate_a * proj(oa_ref, wpa_ref) + gate_b * proj(ob_ref, wpb_ref)
    y = x + _rmsnorm(_dot(merged.astype(BF16), wout_ref[...]), gpost_ref[...])
    y_ref[...] = _interleave(y)


def _interleave(rows):
    n, d = rows.shape
    return jnp.swapaxes(rows.reshape(SUBLANES, n // SUBLANES, d), 0, 1)


def _deinterleave(planes):
    p, s, d = planes.shape
    return jnp.swapaxes(planes, 0, 1).reshape(s * p, d)


def _postmix(x, oa, ob, g_pre, g_post, w_g, w_pa, w_pb, w_out):
    n, d = x.shape
    tm = min(ROW_TILE, n)
    row = lambda i: (i, 0)
    fixed = lambda i: (0, 0)
    channel_major = oa.ndim == 3
    if channel_major:
        tiles_per_seq = oa.shape[2] // tm
        o_spec = pl.BlockSpec((1, WIDTH, tm), lambda i: (i // tiles_per_seq, 0, i % tiles_per_seq))
    else:
        o_spec = pl.BlockSpec((tm, WIDTH), row)
    return pl.pallas_call(
        functools.partial(_postmix_kernel, channel_major=channel_major),
        grid=(n // tm,),
        in_specs=[pl.BlockSpec((tm, d), row),
                  o_spec,
                  o_spec,
                  pl.BlockSpec((1, d), fixed),
                  pl.BlockSpec((1, d), fixed),
                  _resident(w_g.shape, fixed),
                  _resident(w_pa.shape, fixed),
                  _resident(w_pb.shape, fixed),
                  _resident(w_out.shape, fixed)],
        out_specs=pl.BlockSpec((tm // SUBLANES, SUBLANES, d), lambda i: (i, 0, 0)),
        out_shape=jax.ShapeDtypeStruct((n // SUBLANES, SUBLANES, d), F32),
        compiler_params=_params("parallel"),
        name="postmix",
    )(x, oa, ob, g_pre, g_post, w_g, w_pa, w_pb, w_out)


def _ffn_kernel(x_ref, st_ref, gpre_ref, gpost_ref, wup_ref, cw_ref, cb_ref, wdn_ref,
                y_ref, nst_ref, hist_ref, ext_ref, h_ref, f_ref, *, nseg):
    @pl.when(pl.program_id(1) == 0)
    def _():
        hist_ref[...] = st_ref[...]

    planes, _, d = x_ref.shape
    tm = planes * SUBLANES
    d_ff = wdn_ref.shape[0]
    h_ref[...] = _rmsnorm(x_ref[...].reshape(tm, d), gpre_ref[...]).astype(BF16)
    n_chunks = d_ff // FFN_COLS
    first_sublane = lax.broadcasted_iota(jnp.int32, (SUBLANES, FFN_COLS), 0) == 0

    def parts(c):
        for part in range(2):
            yield (slice(part * d_ff + c * FFN_COLS, part * d_ff + (c + 1) * FFN_COLS),
                   slice(part * FFN_COLS, (part + 1) * FFN_COLS))

    def up(c):
        ext = ext_ref.at[c % 2]
        for cols, dst in parts(c):
            u = _dot(h_ref[...], wup_ref[:, cols]).reshape(planes, SUBLANES, FFN_COLS)
            ext[CONV_W - 1:, :, dst] = u
            for k in range(CONV_W - 1):
                last = u[planes - (CONV_W - 1) + k]
                if nseg == 1:
                    ext[k, :, dst] = jnp.where(first_sublane, hist_ref[0, k:k + 1, cols], pltpu.roll(last, 1, 0))
                    hist_ref[0, k:k + 1, cols] = last[SUBLANES - 1:, :]
                else:
                    ext[k, :, dst] = hist_ref[:, k, cols]
                    hist_ref[:, k, cols] = last

    def down(c):
        ext = ext_ref.at[c % 2]
        halves = []
        for cols, dst in parts(c):
            y = cb_ref[:, cols]
            for tap in range(CONV_W):
                y = y + ext[tap:tap + planes, :, dst] * cw_ref[tap:tap + 1, cols]
            halves.append(y)
        act = (jax.nn.gelu(halves[0]) * halves[1]).reshape(tm, FFN_COLS)
        f_ref[:, c * FFN_COLS:(c + 1) * FFN_COLS] = act.astype(BF16)

    up(0)
    for c in range(n_chunks):
        if c + 1 < n_chunks:
            up(c + 1)
        down(c)
    f = _dot(f_ref[...], wdn_ref[...])
    y = x_ref[...].reshape(tm, d) + _rmsnorm(f, gpost_ref[...])
    y_ref[...] = _deinterleave(y.reshape(planes, SUBLANES, d))
    nst_ref[...] = hist_ref[...]


def _ffn(x, bsz, s, state, g_pre, g_post, w_up, conv_w, conv_b, w_down):
    d = x.shape[-1]
    up = w_up.shape[1]
    tm = min(ROW_TILE, bsz * s)
    planes = tm // SUBLANES
    if s >= tm:
        nseg, tiles = 1, s // tm
    else:
        nseg, tiles = tm // s, 1
        assert nseg == SUBLANES
    outer = bsz // nseg
    row = lambda o, t: (o * tiles + t, 0)
    fixed = lambda o, t: (0, 0)
    st_spec = pl.BlockSpec((nseg, CONV_W - 1, up), lambda o, t: (o, 0, 0))
    y, new_state = pl.pallas_call(
        functools.partial(_ffn_kernel, nseg=nseg),
        grid=(outer, tiles),
        in_specs=[pl.BlockSpec((planes, SUBLANES, d), lambda o, t: (o * tiles + t, 0, 0)),
                  st_spec,
                  pl.BlockSpec((1, d), fixed),
                  pl.BlockSpec((1, d), fixed),
                  _resident(w_up.shape, fixed),
                  pl.BlockSpec(conv_w.shape, fixed),
                  pl.BlockSpec((1, up), fixed),
                  _resident(w_down.shape, fixed)],
        out_specs=[pl.BlockSpec((tm, d), row), st_spec],
        out_shape=[jax.ShapeDtypeStruct((bsz * s, d), F32),
                   jax.ShapeDtypeStruct(state.shape, F32)],
        scratch_shapes=[pltpu.VMEM((nseg, CONV_W - 1, up), F32),
                        pltpu.VMEM((2, CONV_W - 1 + planes, SUBLANES, 2 * FFN_COLS), F32),
                        pltpu.VMEM((tm, d), BF16),
                        pltpu.VMEM((tm, w_down.shape[0]), BF16)],
        compiler_params=_params("arbitrary", "arbitrary"),
        name="conv_ffn",
    )(x, state, g_pre, g_post, w_up, conv_w, conv_b, w_down)
    return y.reshape(bsz, s, d), new_state


def _layer(x, caches, conv_state, bias, w):
    bsz, s, d = x.shape
    n = bsz * s
    x2 = x.reshape(n, d)
    keep = min(BAND, s)
    (qa, qb, ka16, kb16, vat16, vbt16, kat, vat, kbt, vbt, lft) = _inproj(
        x2, w["g_pre_mix"], w["w_qkv"], w["w_ft"], w["b_f"], seq_len=s, band_keep=keep)
    seq = lambda a: a.reshape(bsz, s, a.shape[-1])
    if caches is None:
        ccol, _ = _cumsum(lft)
        oa = _band_attn_prompt(seq(qa), seq(ka16), vat16, bias[1])
        ob = _fox_attn_prompt(seq(qb), seq(kb16), vbt16, ccol)
    else:
        ckat, cvat, ckbt, cvbt, clft = caches
        ccol, crow = _cumsum(clft, lft)
        oa = _band_attn_sample(seq(qa), ckat, cvat, seq(ka16), vat16, bias[0]).reshape(n, WIDTH)
        ob = _fox_attn_sample(seq(qb), ckbt, cvbt, seq(kb16), vbt16, ccol, crow).reshape(n, WIDTH)
    x1 = _postmix(x2, oa, ob, w["g_pre_mix"], w["g_post_mix"], w["w_g"], w["w_pa"], w["w_pb"], w["w_out"])
    y, new_conv = _ffn(x1, bsz, s, conv_state, w["g_pre_ffn"], w["g_post_ffn"],
                       w["w_up"], w["conv_w"], w["conv_b"], w["w_down"])
    heads = lambda a: a.reshape(bsz, HEADS, HEAD_DIM, a.shape[-1]).transpose(0, 3, 1, 2)
    return y, (heads(kat), heads(vat), heads(kbt), heads(vbt), lft.transpose(0, 2, 1), new_conv)


def _channel_major(cache):
    bsz, past = cache.shape[:2]
    return cache.transpose(0, 2, 3, 1).reshape(bsz, WIDTH, past)


def kernel(x_prompt, x_sample, cache_k_a, cache_v_a, cache_k_b, cache_v_b, cache_logf_b, state_conv_ffn,
           g_pre_mix, g_post_mix, g_pre_ffn, g_post_ffn, w_in, b_f, rel_table, w_proj_a, w_proj_b, w_out,
           w_up, conv_w, conv_b, w_down):
    depth = w_in.shape[0]
    up = w_up.shape[-1]
    x_p, x_s = x_prompt, x_sample
    p_states, s_states = [], []
    for l in range(depth):
        w = {
            "g_pre_mix": g_pre_mix[l][None], "g_post_mix": g_post_mix[l][None],
            "g_pre_ffn": g_pre_ffn[l][None], "g_post_ffn": g_post_ffn[l][None],
            "w_qkv": w_in[l][:, :6 * WIDTH].astype(BF16),
            "w_ft": w_in[l][:, 6 * WIDTH:6 * WIDTH + HEADS].T.astype(BF16),
            "w_g": w_in[l][:, 6 * WIDTH + HEADS:].astype(BF16),
            "b_f": b_f[l][:, None],
            "w_pa": w_proj_a[l].astype(BF16), "w_pb": w_proj_b[l].astype(BF16),
            "w_out": w_out[l].astype(BF16), "w_up": w_up[l].astype(BF16),
            "conv_w": conv_w[l], "conv_b": conv_b[l][None], "w_down": w_down[l].astype(BF16),
        }
        bias = _band_bias(rel_table[l])
        zero_state = jnp.zeros((x_p.shape[0], CONV_W - 1, up), F32)
        x_p, (ka, va, kb, vb, lf, cv) = _layer(x_p, None, zero_state, bias, w)
        p_states.append((ka, va, kb, vb, lf, cv))
        caches = (_channel_major(cache_k_a[l]), _channel_major(cache_v_a[l]), _channel_major(cache_k_b[l]),
                  _channel_major(cache_v_b[l]), cache_logf_b[l].transpose(0, 2, 1))
        x_s, st = _layer(x_s, caches, state_conv_ffn[l], bias, w)
        s_states.append(st)
    stack = lambda states: [jnp.stack(s) for s in zip(*states)]
    return (x_p, x_s, *stack(p_states), *stack(s_states))
```

```python
import functools

import jax
import jax.numpy as jnp
from jax import lax
from jax.experimental import pallas as pl
from jax.experimental.pallas import tpu as pltpu

F32, BF16 = jnp.float32, jnp.bfloat16

HEADS = 8
HEAD_DIM = 64
WIDTH = HEADS * HEAD_DIM
CHUNK = 64
PAST_CHUNKS = 8
BAND = PAST_CHUNKS * CHUNK
REL_CLIP = 128
REL_SIZE = 2 * REL_CLIP + 1
CONV_W = 3
EPS = 1e-6
NEG = -1e30
LOG2E = 1.4426950408889634

SUBLANES = 8
ATT_BLK = 256
BAND_KEYS = 3 * ATT_BLK
ROW_TILE = 512
FFN_COLS = 256
CUMSUM_BLK = 256
VMEM_LIMIT = 56 * 1024 * 1024


def _params(*sem):
    return pltpu.CompilerParams(dimension_semantics=sem, vmem_limit_bytes=VMEM_LIMIT)


def _resident(shape, index_map):
    return pl.BlockSpec(shape, index_map, pipeline_mode=pl.Buffered(1))


def _rmsnorm(x, g):
    return x * lax.rsqrt(jnp.mean(x * x, axis=-1, keepdims=True) + EPS) * g


def _dot(a, b):
    return jnp.dot(a, b, preferred_element_type=F32)


def _dot_nt(a, b):
    return lax.dot_general(a, b, (((1,), (1,)), ((), ())), preferred_element_type=F32)


def _split3(x):
    hi = x.astype(BF16)
    r = x - hi.astype(F32)
    mid = r.astype(BF16)
    lo = (r - mid.astype(F32)).astype(BF16)
    return hi, mid, lo


def _inproj_kernel(x_ref, g_ref, w_ref, wft_ref, bf_ref,
                   qa_ref, qb_ref, ka_ref, kb_ref, vat16_ref, vbt16_ref,
                   kat_ref, vat_ref, kbt_ref, vbt_ref, lft_ref):
    h = _rmsnorm(x_ref[...], g_ref[...]).astype(BF16)
    scale = HEAD_DIM ** -0.5

    def proj(c):
        return _dot(h, w_ref[:, c * WIDTH:(c + 1) * WIDTH])

    def put(ref, zt):
        per_seq = zt.shape[1] // ref.shape[0]
        for sq in range(ref.shape[0]):
            ref[sq] = zt[:, sq * per_seq:(sq + 1) * per_seq]

    qa_ref[...] = (proj(0) * (scale * LOG2E)).astype(BF16)
    qb_ref[...] = (proj(3) * (scale * LOG2E)).astype(BF16)
    for c, row_ref, t_ref in ((1, ka_ref, kat_ref), (4, kb_ref, kbt_ref)):
        z = proj(c)
        row_ref[...] = z.astype(BF16)
        put(t_ref, z.T)
    for c, t16_ref, t_ref in ((2, vat16_ref, vat_ref), (5, vbt16_ref, vbt_ref)):
        zt = proj(c).T
        put(t_ref, zt)
        put(t16_ref, zt.astype(BF16))
    put(lft_ref, jax.nn.log_sigmoid(_dot_nt(wft_ref[...], h) + bf_ref[...]))


def _inproj(x, g, w_qkv, w_ft, b_f, seq_len, band_keep):
    n, d = x.shape
    bsz = n // seq_len
    tm = min(ROW_TILE, n)
    tiles_per_seq = max(seq_len // tm, 1)
    seqs_per_tile = max(tm // seq_len, 1)
    cols = tm // seqs_per_tile
    assert band_keep == cols
    row = lambda i: (i, 0)
    fixed = lambda i: (0, 0)
    along = lambda i: (i // tiles_per_seq, 0, i % tiles_per_seq)
    kept = lambda i: (i // tiles_per_seq, 0, 0)
    rows16 = (jax.ShapeDtypeStruct((n, WIDTH), BF16), pl.BlockSpec((tm, WIDTH), row))

    def chan(channels, length, dtype, index_map):
        return (jax.ShapeDtypeStruct((bsz, channels, length), dtype),
                pl.BlockSpec((seqs_per_tile, channels, cols), index_map))

    outs = [rows16] * 4
    outs += [chan(WIDTH, seq_len, BF16, along)] * 2
    outs += [chan(WIDTH, band_keep, F32, kept)] * 2
    outs += [chan(WIDTH, seq_len, F32, along)] * 2
    outs += [chan(HEADS, seq_len, F32, along)]
    return pl.pallas_call(
        _inproj_kernel,
        grid=(n // tm,),
        in_specs=[pl.BlockSpec((tm, d), row),
                  pl.BlockSpec((1, d), fixed),
                  _resident(w_qkv.shape, fixed),
                  pl.BlockSpec(w_ft.shape, fixed),
                  pl.BlockSpec((HEADS, 1), fixed)],
        out_specs=[spec for _, spec in outs],
        out_shape=[shape for shape, _ in outs],
        compiler_params=_params("arbitrary"),
        name="inproj",
    )(x, g, w_qkv, w_ft, b_f)


def _cumsum_kernel(*refs, seg_lens, col_from):
    seg_refs = refs[:len(seg_lens)]
    ccol_ref, crow_ref = refs[len(seg_lens):]
    carry = jnp.zeros((crow_ref.shape[0], 1), F32)
    off = 0
    for ref, n in zip(seg_refs, seg_lens):
        for o in range(0, n, CUMSUM_BLK):
            b = min(CUMSUM_BLK, n - o)
            r = lax.broadcasted_iota(jnp.int32, (b, b), 0)
            c = lax.broadcasted_iota(jnp.int32, (b, b), 1)
            upper = jnp.where(r <= c, 1.0, 0.0).astype(BF16)
            sums = carry
            for p in _split3(ref[:, o:o + b]):
                sums = sums + _dot(p, upper)
            crow_ref[:, off + o:off + o + b] = sums
            carry = sums[:, b - 1:b]
            if off + o >= col_from:
                by_position = sums.T
                for bb in range(ccol_ref.shape[0]):
                    ccol_ref[bb, off + o - col_from:off + o - col_from + b, :] = (
                        by_position[:, bb * HEADS:(bb + 1) * HEADS])
        off += n


def _cumsum(*segs, col_from=0):
    bsz = segs[0].shape[0]
    seg_lens = tuple(s.shape[2] for s in segs)
    total = sum(seg_lens)
    assert col_from % CUMSUM_BLK == 0 or col_from in (0, seg_lens[0])
    ccol, crow = pl.pallas_call(
        functools.partial(_cumsum_kernel, seg_lens=seg_lens, col_from=col_from),
        out_shape=[jax.ShapeDtypeStruct((bsz, total - col_from, HEADS), F32),
                   jax.ShapeDtypeStruct((bsz * HEADS, total), F32)],
        compiler_params=pltpu.CompilerParams(vmem_limit_bytes=VMEM_LIMIT),
        name="cumsum_logf",
    )(*[s.reshape(bsz * HEADS, s.shape[2]) for s in segs])
    return ccol, crow.reshape(bsz, HEADS, total)


REL_PAD = 384
DIST_SPAN = 1024


def _band_bias_kernel(tbl_ref, qk_ref, kq_ref):
    j = lax.broadcasted_iota(jnp.int32, (REL_PAD, DIST_SPAN), 1)
    r = lax.broadcasted_iota(jnp.int32, (REL_PAD, DIST_SPAN), 0)
    parts = _split3(tbl_ref[...])

    def by_offset(dist):
        onehot = jnp.where(jnp.clip(dist, -REL_CLIP, REL_CLIP) + REL_CLIP == r, 1.0, 0.0).astype(BF16)
        return sum(_dot(p, onehot) for p in parts) * LOG2E

    def skewed(e_row, rows, cols, shift):
        wide = jnp.broadcast_to(e_row, (rows, DIST_SPAN))
        return pltpu.roll(wide, shift % DIST_SPAN, 1, stride=1, stride_axis=0)[:, :cols]

    e_qk = by_offset((BAND_KEYS - 1) - j)
    e_kq = by_offset(j - (ATT_BLK - 1))
    qc = lax.broadcasted_iota(jnp.int32, (ATT_BLK, BAND_KEYS), 0) // CHUNK
    kc = lax.broadcasted_iota(jnp.int32, (ATT_BLK, BAND_KEYS), 1) // CHUNK
    vis_qk = (kc >= qc) & (kc <= qc + PAST_CHUNKS)
    kc = lax.broadcasted_iota(jnp.int32, (BAND_KEYS, ATT_BLK), 0) // CHUNK
    qc = lax.broadcasted_iota(jnp.int32, (BAND_KEYS, ATT_BLK), 1) // CHUNK
    vis_kq = (kc >= qc) & (kc <= qc + PAST_CHUNKS)
    for h in range(HEADS):
        qk_ref[h] = jnp.where(vis_qk, skewed(e_qk[h:h + 1, :], ATT_BLK, BAND_KEYS, -(ATT_BLK - 1)), NEG)
        kq_ref[h] = jnp.where(vis_kq, skewed(e_kq[h:h + 1, :], BAND_KEYS, ATT_BLK, -(BAND_KEYS - 1)), NEG)


def _band_bias(rel_table):
    tbl = jnp.pad(rel_table, ((0, 0), (0, REL_PAD - REL_SIZE)))
    return pl.pallas_call(
        _band_bias_kernel,
        out_shape=[jax.ShapeDtypeStruct((HEADS, ATT_BLK, BAND_KEYS), F32),
                   jax.ShapeDtypeStruct((HEADS, BAND_KEYS, ATT_BLK), F32)],
        compiler_params=pltpu.CompilerParams(vmem_limit_bytes=VMEM_LIMIT),
        name="band_bias",
    )(tbl)


BAND_BLOCKS = BAND_KEYS // ATT_BLK


def _band_attn_prompt_kernel(q_ref, k_ref, vt_ref, bias_ref, o_ref, s_ref):
    i = pl.program_id(1)
    ones = jnp.ones((SUM_ROWS, ATT_BLK), BF16)

    def run(clipped):
        blocks = []
        for g in range(BAND_BLOCKS):
            j = i - (BAND_BLOCKS - 1) + g
            start = pl.multiple_of(jnp.maximum(j, 0) * ATT_BLK, ATT_BLK)
            masks = clipped and g < BAND_BLOCKS - 1
            blocks.append((start, jnp.where(j >= 0, 0.0, NEG) if masks else None))

        def scores(h):
            sl = slice(h * HEAD_DIM, (h + 1) * HEAD_DIM)
            q = q_ref[0, :, sl]
            for g, (start, penalty) in enumerate(blocks):
                s = _dot_nt(k_ref[0, pl.ds(start, ATT_BLK), sl], q) + bias_ref[h, g * ATT_BLK:(g + 1) * ATT_BLK, :]
                s_ref[h % 2, g] = s if penalty is None else s + penalty

        def finish(h):
            sl = slice(h * HEAD_DIM, (h + 1) * HEAD_DIM)
            s = [s_ref[h % 2, g] for g in range(BAND_BLOCKS)]
            m = functools.reduce(jnp.maximum, [jnp.max(x, axis=0, keepdims=True) for x in s])
            acc = 0.0
            for x, (start, _) in zip(s, blocks):
                vt = jnp.concatenate([vt_ref[0, sl, pl.ds(start, ATT_BLK)], ones], axis=0)
                acc = acc + _dot(vt, jnp.exp2(x - m).astype(BF16))
            o_ref[0, sl, :] = (acc[:HEAD_DIM] / acc[HEAD_DIM:HEAD_DIM + 1]).astype(BF16)

        scores(0)
        for h in range(HEADS):
            if h + 1 < HEADS:
                scores(h + 1)
            finish(h)

    pl.when(i >= BAND_BLOCKS - 1)(lambda: run(clipped=False))
    pl.when(i < BAND_BLOCKS - 1)(lambda: run(clipped=True))


def _band_attn_prompt(q, k, vt, bias_kq):
    bsz, s, _ = q.shape
    qblk = pl.BlockSpec((1, ATT_BLK, WIDTH), lambda b, i: (b, i, 0))
    return pl.pallas_call(
        _band_attn_prompt_kernel,
        grid=(bsz, s // ATT_BLK),
        in_specs=[qblk,
                  pl.BlockSpec((1, s, WIDTH), lambda b, i: (b, 0, 0)),
                  pl.BlockSpec((1, WIDTH, s), lambda b, i: (b, 0, 0)),
                  _resident(bias_kq.shape, lambda b, i: (0, 0, 0))],
        out_specs=pl.BlockSpec((1, WIDTH, ATT_BLK), lambda b, i: (b, 0, i)),
        out_shape=jax.ShapeDtypeStruct((bsz, WIDTH, s), BF16),
        scratch_shapes=[pltpu.VMEM((2, BAND_BLOCKS, ATT_BLK, ATT_BLK), F32)],
        compiler_params=_params("parallel", "parallel"),
        name="band_attn_prompt",
    )(q, k, vt, bias_kq)


def _decay_lanes(c, key_side):
    terms = jnp.concatenate(_split3(c), axis=1)
    row = lax.broadcasted_iota(jnp.int32, (3 * HEADS, WIDTH), 0)
    lane = lax.broadcasted_iota(jnp.int32, (3 * HEADS, WIDTH), 1)
    first_one, first_term, sign = (3, 0, -1.0) if key_side else (0, 3, 1.0)
    place = jnp.where(lane == (row % HEADS) * HEAD_DIM + first_term + row // HEADS, sign, 0.0).astype(BF16)
    in_group = lax.broadcasted_iota(jnp.int32, (1, WIDTH), 1) % HEAD_DIM
    ones = jnp.where((in_group >= first_one) & (in_group < first_one + 3), 1.0, 0.0)
    return (ones + _dot(terms, place)).astype(BF16)


PAD_DIM = 2 * HEAD_DIM
SUM_ROWS = 16


def _fox_attn_prompt_kernel(q_ref, k_ref, vt_ref, ccol_ref, o_ref,
                            kp_ref, qp_ref, s_ref, m_ref, acc_ref):
    i = pl.program_id(1)
    qstart = pl.multiple_of(i * ATT_BLK, ATT_BLK)

    @pl.when(i == 0)
    def _():
        extra = _decay_lanes(ccol_ref[0] * LOG2E, key_side=True)
        for h in range(HEADS):
            sl = slice(h * HEAD_DIM, (h + 1) * HEAD_DIM)
            kp_ref[:, h * PAD_DIM:(h + 1) * PAD_DIM] = jnp.concatenate([k_ref[0, :, sl], extra[:, sl]], axis=1)

    extra = _decay_lanes(ccol_ref[0, pl.ds(qstart, ATT_BLK), :] * LOG2E, key_side=False)
    for h in range(HEADS):
        sl = slice(h * HEAD_DIM, (h + 1) * HEAD_DIM)
        qp_ref[:, h * PAD_DIM:(h + 1) * PAD_DIM] = jnp.concatenate([q_ref[0, :, sl], extra[:, sl]], axis=1)

    key = lax.broadcasted_iota(jnp.int32, (ATT_BLK, ATT_BLK), 0)
    qry = lax.broadcasted_iota(jnp.int32, (ATT_BLK, ATT_BLK), 1)
    causal = key <= qry
    ones = jnp.ones((SUM_ROWS, ATT_BLK), BF16)
    m_ref[...] = jnp.full(m_ref.shape, NEG, F32)
    acc_ref[...] = jnp.zeros(acc_ref.shape, F32)

    def scores(j, h):
        start = pl.multiple_of(j * ATT_BLK, ATT_BLK)
        pad = slice(h * PAD_DIM, (h + 1) * PAD_DIM)
        return _dot_nt(kp_ref[pl.ds(start, ATT_BLK), pad], qp_ref[:, pad])

    def update(j, h, s, masked):
        start = pl.multiple_of(j * ATT_BLK, ATT_BLK)
        if masked:
            s = jnp.where(causal, s, NEG)
        m_old = m_ref[h]
        m_new = jnp.maximum(m_old, jnp.max(s, axis=0, keepdims=True))
        alpha = jnp.exp2(m_old - m_new)
        p = jnp.exp2(s - m_new)
        vt = vt_ref[0, h * HEAD_DIM:(h + 1) * HEAD_DIM, pl.ds(start, ATT_BLK)]
        acc_ref[h] = alpha * acc_ref[h] + _dot(jnp.concatenate([vt, ones], axis=0), p.astype(BF16))
        m_ref[h] = m_new

    def step(j, cur, masked=False, prefetch=True):
        if prefetch:
            for h in range(HEADS):
                s_ref[1 - cur, h] = scores(j + 1, h)
        for h in range(HEADS):
            update(j, h, s_ref[cur, h], masked)

    for h in range(HEADS):
        s_ref[0, h] = scores(0, h)

    def pair(t, carry):
        step(2 * t, 0)
        step(2 * t + 1, 1)
        return carry

    lax.fori_loop(0, i // 2, pair, 0)

    @pl.when(i % 2 == 1)
    def _():
        step(i - 1, 0)
        step(i, 1, masked=True, prefetch=False)

    @pl.when(i % 2 == 0)
    def _():
        step(i, 0, masked=True, prefetch=False)

    for h in range(HEADS):
        sl = slice(h * HEAD_DIM, (h + 1) * HEAD_DIM)
        out = acc_ref[h, :HEAD_DIM, :] / acc_ref[h, HEAD_DIM:HEAD_DIM + 1, :]
        o_ref[0, sl, :] = out.astype(BF16)


def _fox_attn_prompt(q, k, vt, ccol):
    bsz, s, _ = q.shape
    qblk = pl.BlockSpec((1, ATT_BLK, WIDTH), lambda b, i: (b, i, 0))
    return pl.pallas_call(
        _fox_attn_prompt_kernel,
        grid=(bsz, s // ATT_BLK),
        in_specs=[qblk,
                  pl.BlockSpec((1, s, WIDTH), lambda b, i: (b, 0, 0)),
                  pl.BlockSpec((1, WIDTH, s), lambda b, i: (b, 0, 0)),
                  pl.BlockSpec((1, s, HEADS), lambda b, i: (b, 0, 0))],
        out_specs=pl.BlockSpec((1, WIDTH, ATT_BLK), lambda b, i: (b, 0, i)),
        out_shape=jax.ShapeDtypeStruct((bsz, WIDTH, s), BF16),
        scratch_shapes=[pltpu.VMEM((s, HEADS * PAD_DIM), BF16),
                        pltpu.VMEM((ATT_BLK, HEADS * PAD_DIM), BF16),
                        pltpu.VMEM((2, HEADS, ATT_BLK, ATT_BLK), F32),
                        pltpu.VMEM((HEADS, 1, ATT_BLK), F32),
                        pltpu.VMEM((HEADS, HEAD_DIM + SUM_ROWS, ATT_BLK), F32)],
        compiler_params=_params("parallel", "arbitrary"),
        name="fox_attn_prompt",
    )(q, k, vt, ccol)


def _pipelined_heads(scores, finish):
    ahead = scores(0)
    for h in range(HEADS):
        current = ahead
        if h + 1 < HEADS:
            ahead = scores(h + 1)
        finish(h, current)


def _sample_attention(s_past, s_new, vct, vnt):
    m = jnp.maximum(jnp.max(s_past, axis=-1, keepdims=True), jnp.max(s_new, axis=-1, keepdims=True))
    p_past = jnp.exp2(s_past - m)
    p_new = jnp.exp2(s_new - m)
    l = jnp.sum(p_past, axis=-1, keepdims=True) + jnp.sum(p_new, axis=-1, keepdims=True)
    acc = _dot_nt(p_past.astype(BF16), vct) + _dot_nt(p_new.astype(BF16), vnt)
    return (acc / l).astype(BF16)


def _band_attn_sample_kernel(q_ref, kct_ref, vct_ref, kn_ref, vnt_ref, bias_ref, o_ref):
    t = q_ref.shape[1]
    past = kct_ref.shape[2]

    def scores(h):
        sl = slice(h * HEAD_DIM, (h + 1) * HEAD_DIM)
        q = q_ref[0, :, sl]
        s_past = _dot(q, kct_ref[0, sl, :].astype(BF16)) + bias_ref[h, 0:t, BAND - past:BAND]
        s_new = _dot_nt(q, kn_ref[0, :, sl]) + bias_ref[h, 0:t, BAND:BAND + t]
        return s_past, s_new

    def finish(h, s):
        sl = slice(h * HEAD_DIM, (h + 1) * HEAD_DIM)
        o_ref[0, :, sl] = _sample_attention(*s, vct_ref[0, sl, :].astype(BF16), vnt_ref[0, sl, :])

    _pipelined_heads(scores, finish)


def _band_attn_sample(q, k_cache_t, v_cache_t, k_new, v_new_t, bias):
    bsz, t, _ = q.shape
    past = k_cache_t.shape[2]
    new = pl.BlockSpec((1, t, WIDTH), lambda b: (b, 0, 0))
    new_t = pl.BlockSpec((1, WIDTH, t), lambda b: (b, 0, 0))
    old_t = pl.BlockSpec((1, WIDTH, past), lambda b: (b, 0, 0))
    return pl.pallas_call(
        _band_attn_sample_kernel,
        grid=(bsz,),
        in_specs=[new, old_t, old_t, new, new_t, _resident(bias.shape, lambda b: (0, 0, 0))],
        out_specs=new,
        out_shape=jax.ShapeDtypeStruct(q.shape, BF16),
        compiler_params=_params("parallel"),
        name="band_attn_sample",
    )(q, k_cache_t, v_cache_t, k_new, v_new_t, bias)


def _fox_attn_sample_kernel(q_ref, kct_ref, vct_ref, kn_ref, vnt_ref, ccol_ref, crow_ref, o_ref):
    t = q_ref.shape[1]
    past = kct_ref.shape[2]
    row = lax.broadcasted_iota(jnp.int32, (t, t), 0)
    col = lax.broadcasted_iota(jnp.int32, (t, t), 1)
    causal = col <= row

    def scores(h):
        sl = slice(h * HEAD_DIM, (h + 1) * HEAD_DIM)
        q = q_ref[0, :, sl]
        cq = ccol_ref[0, :, h:h + 1]
        s_past = _dot(q, kct_ref[0, sl, :].astype(BF16)) + (cq - crow_ref[0, h:h + 1, 0:past]) * LOG2E
        s_new = _dot_nt(q, kn_ref[0, :, sl]) + (cq - crow_ref[0, h:h + 1, past:past + t]) * LOG2E
        return s_past, jnp.where(causal, s_new, NEG)

    def finish(h, s):
        sl = slice(h * HEAD_DIM, (h + 1) * HEAD_DIM)
        o_ref[0, :, sl] = _sample_attention(*s, vct_ref[0, sl, :].astype(BF16), vnt_ref[0, sl, :])

    _pipelined_heads(scores, finish)


def _fox_attn_sample(q, k_cache_t, v_cache_t, k_new, v_new_t, ccol, crow):
    bsz, t, _ = q.shape
    past = k_cache_t.shape[2]
    assert past % t == 0
    new = pl.BlockSpec((1, t, WIDTH), lambda b: (b, 0, 0))
    new_t = pl.BlockSpec((1, WIDTH, t), lambda b: (b, 0, 0))
    old_t = pl.BlockSpec((1, WIDTH, past), lambda b: (b, 0, 0))
    return pl.pallas_call(
        _fox_attn_sample_kernel,
        grid=(bsz,),
        in_specs=[new, old_t, old_t, new, new_t,
                  pl.BlockSpec((1, t, HEADS), lambda b: (b, 0, 0)),
                  pl.BlockSpec((1, HEADS, past + t), lambda b: (b, 0, 0))],
        out_specs=new,
        out_shape=jax.ShapeDtypeStruct(q.shape, BF16),
        compiler_params=_params("parallel"),
        name="fox_attn_sample",
    )(q, k_cache_t, v_cache_t, k_new, v_new_t, ccol, crow)


def _postmix_kernel(x_ref, oa_ref, ob_ref, gpre_ref, gpost_ref, wg_ref, wpa_ref, wpb_ref, wout_ref, y_ref,
                    *, channel_major):
    x = x_ref[...]
    d = x.shape[-1]
    h = _rmsnorm(x, gpre_ref[...]).astype(BF16)
    gate_a = jax.nn.sigmoid(_dot(h, wg_ref[:, :d]))
    gate_b = jax.nn.sigmoid(_dot(h, wg_ref[:, d:]))
    if channel_major:
        proj = lambda o_ref, w_ref: lax.dot_general(o_ref[0], w_ref[...], (((0,), (0,)), ((), ())),
                                                    preferred_element_type=F32)
    else:
        proj = lambda o_ref, w_ref: _dot(o_ref[...], w_ref[...])
    merged = gate_a * proj(oa_ref, wpa_ref) + gate_b * proj(ob_ref, wpb_ref)
    y = x + _rmsnorm(_dot(merged.astype(BF16), wout_ref[...]), gpost_ref[...])
    y_ref[...] = _interleave(y)


def _interleave(rows):
    n, d = rows.shape
    return jnp.swapaxes(rows.reshape(SUBLANES, n // SUBLANES, d), 0, 1)


def _deinterleave(planes):
    p, s, d = planes.shape
    return jnp.swapaxes(planes, 0, 1).reshape(s * p, d)


def _postmix(x, oa, ob, g_pre, g_post, w_g, w_pa, w_pb, w_out):
    n, d = x.shape
    tm = min(ROW_TILE, n)
    row = lambda i: (i, 0)
    fixed = lambda i: (0, 0)
    channel_major = oa.ndim == 3
    if channel_major:
        tiles_per_seq = oa.shape[2] // tm
        o_spec = pl.BlockSpec((1, WIDTH, tm), lambda i: (i // tiles_per_seq, 0, i % tiles_per_seq))
    else:
        o_spec = pl.BlockSpec((tm, WIDTH), row)
    return pl.pallas_call(
        functools.partial(_postmix_kernel, channel_major=channel_major),
        grid=(n // tm,),
        in_specs=[pl.BlockSpec((tm, d), row),
                  o_spec,
                  o_spec,
                  pl.BlockSpec((1, d), fixed),
                  pl.BlockSpec((1, d), fixed),
                  _resident(w_g.shape, fixed),
                  _resident(w_pa.shape, fixed),
                  _resident(w_pb.shape, fixed),
                  _resident(w_out.shape, fixed)],
        out_specs=pl.BlockSpec((tm // SUBLANES, SUBLANES, d), lambda i: (i, 0, 0)),
        out_shape=jax.ShapeDtypeStruct((n // SUBLANES, SUBLANES, d), F32),
        compiler_params=_params("parallel"),
        name="postmix",
    )(x, oa, ob, g_pre, g_post, w_g, w_pa, w_pb, w_out)


def _ffn_kernel(x_ref, st_ref, gpre_ref, gpost_ref, wup_ref, cw_ref, cb_ref, wdn_ref,
                y_ref, nst_ref, hist_ref, ext_ref, h_ref, f_ref, *, nseg):
    @pl.when(pl.program_id(1) == 0)
    def _():
        hist_ref[...] = st_ref[...]

    planes, _, d = x_ref.shape
    tm = planes * SUBLANES
    d_ff = wdn_ref.shape[0]
    h_ref[...] = _rmsnorm(x_ref[...].reshape(tm, d), gpre_ref[...]).astype(BF16)
    n_chunks = d_ff // FFN_COLS
    first_sublane = lax.broadcasted_iota(jnp.int32, (SUBLANES, FFN_COLS), 0) == 0

    def parts(c):
        for part in range(2):
            yield (slice(part * d_ff + c * FFN_COLS, part * d_ff + (c + 1) * FFN_COLS),
                   slice(part * FFN_COLS, (part + 1) * FFN_COLS))

    def up(c):
        ext = ext_ref.at[c % 2]
        for cols, dst in parts(c):
            u = _dot(h_ref[...], wup_ref[:, cols]).reshape(planes, SUBLANES, FFN_COLS)
            ext[CONV_W - 1:, :, dst] = u
            for k in range(CONV_W - 1):
                last = u[planes - (CONV_W - 1) + k]
                if nseg == 1:
                    ext[k, :, dst] = jnp.where(first_sublane, hist_ref[0, k:k + 1, cols], pltpu.roll(last, 1, 0))
                    hist_ref[0, k:k + 1, cols] = last[SUBLANES - 1:, :]
                else:
                    ext[k, :, dst] = hist_ref[:, k, cols]
                    hist_ref[:, k, cols] = last

    def down(c):
        ext = ext_ref.at[c % 2]
        halves = []
        for cols, dst in parts(c):
            y = cb_ref[:, cols]
            for tap in range(CONV_W):
                y = y + ext[tap:tap + planes, :, dst] * cw_ref[tap:tap + 1, cols]
            halves.append(y)
        act = (jax.nn.gelu(halves[0]) * halves[1]).reshape(tm, FFN_COLS)
        f_ref[:, c * FFN_COLS:(c + 1) * FFN_COLS] = act.astype(BF16)

    up(0)
    for c in range(n_chunks):
        if c + 1 < n_chunks:
            up(c + 1)
        down(c)
    f = _dot(f_ref[...], wdn_ref[...])
    y = x_ref[...].reshape(tm, d) + _rmsnorm(f, gpost_ref[...])
    y_ref[...] = _deinterleave(y.reshape(planes, SUBLANES, d))
    nst_ref[...] = hist_ref[...]


def _ffn(x, bsz, s, state, g_pre, g_post, w_up, conv_w, conv_b, w_down):
    d = x.shape[-1]
    up = w_up.shape[1]
    tm = min(ROW_TILE, bsz * s)
    planes = tm // SUBLANES
    if s >= tm:
        nseg, tiles = 1, s // tm
    else:
        nseg, tiles = tm // s, 1
        assert nseg == SUBLANES
    outer = bsz // nseg
    row = lambda o, t: (o * tiles + t, 0)
    fixed = lambda o, t: (0, 0)
    st_spec = pl.BlockSpec((nseg, CONV_W - 1, up), lambda o, t: (o, 0, 0))
    y, new_state = pl.pallas_call(
        functools.partial(_ffn_kernel, nseg=nseg),
        grid=(outer, tiles),
        in_specs=[pl.BlockSpec((planes, SUBLANES, d), lambda o, t: (o * tiles + t, 0, 0)),
                  st_spec,
                  pl.BlockSpec((1, d), fixed),
                  pl.BlockSpec((1, d), fixed),
                  _resident(w_up.shape, fixed),
                  pl.BlockSpec(conv_w.shape, fixed),
                  pl.BlockSpec((1, up), fixed),
                  _resident(w_down.shape, fixed)],
        out_specs=[pl.BlockSpec((tm, d), row), st_spec],
        out_shape=[jax.ShapeDtypeStruct((bsz * s, d), F32),
                   jax.ShapeDtypeStruct(state.shape, F32)],
        scratch_shapes=[pltpu.VMEM((nseg, CONV_W - 1, up), F32),
                        pltpu.VMEM((2, CONV_W - 1 + planes, SUBLANES, 2 * FFN_COLS), F32),
                        pltpu.VMEM((tm, d), BF16),
                        pltpu.VMEM((tm, w_down.shape[0]), BF16)],
        compiler_params=_params("arbitrary", "arbitrary"),
        name="conv_ffn",
    )(x, state, g_pre, g_post, w_up, conv_w, conv_b, w_down)
    return y.reshape(bsz, s, d), new_state


def _layer(x, caches, conv_state, bias, w):
    bsz, s, d = x.shape
    n = bsz * s
    x2 = x.reshape(n, d)
    keep = min(BAND, s)
    (qa, qb, ka16, kb16, vat16, vbt16, kat, vat, kbt, vbt, lft) = _inproj(
        x2, w["g_pre_mix"], w["w_qkv"], w["w_ft"], w["b_f"], seq_len=s, band_keep=keep)
    seq = lambda a: a.reshape(bsz, s, a.shape[-1])
    if caches is None:
        ccol, _ = _cumsum(lft)
        oa = _band_attn_prompt(seq(qa), seq(ka16), vat16, bias[1])
        ob = _fox_attn_prompt(seq(qb), seq(kb16), vbt16, ccol)
    else:
        ckat, cvat, ckbt, cvbt, clft = caches
        ccol, crow = _cumsum(clft, lft, col_from=clft.shape[2])
        oa = _band_attn_sample(seq(qa), ckat, cvat, seq(ka16), vat16, bias[0]).reshape(n, WIDTH)
        ob = _fox_attn_sample(seq(qb), ckbt, cvbt, seq(kb16), vbt16, ccol, crow).reshape(n, WIDTH)
    x1 = _postmix(x2, oa, ob, w["g_pre_mix"], w["g_post_mix"], w["w_g"], w["w_pa"], w["w_pb"], w["w_out"])
    y, new_conv = _ffn(x1, bsz, s, conv_state, w["g_pre_ffn"], w["g_post_ffn"],
                       w["w_up"], w["conv_w"], w["conv_b"], w["w_down"])
    heads = lambda a: a.reshape(bsz, HEADS, HEAD_DIM, a.shape[-1]).transpose(0, 3, 1, 2)
    return y, (heads(kat), heads(vat), heads(kbt), heads(vbt), lft.transpose(0, 2, 1), new_conv)


def _channel_major(cache):
    bsz, past = cache.shape[:2]
    return cache.transpose(0, 2, 3, 1).reshape(bsz, WIDTH, past)


def kernel(x_prompt, x_sample, cache_k_a, cache_v_a, cache_k_b, cache_v_b, cache_logf_b, state_conv_ffn,
           g_pre_mix, g_post_mix, g_pre_ffn, g_post_ffn, w_in, b_f, rel_table, w_proj_a, w_proj_b, w_out,
           w_up, conv_w, conv_b, w_down):
    depth = w_in.shape[0]
    up = w_up.shape[-1]
    x_p, x_s = x_prompt, x_sample
    p_states, s_states = [], []
    for l in range(depth):
        w = {
            "g_pre_mix": g_pre_mix[l][None], "g_post_mix": g_post_mix[l][None],
            "g_pre_ffn": g_pre_ffn[l][None], "g_post_ffn": g_post_ffn[l][None],
            "w_qkv": w_in[l][:, :6 * WIDTH].astype(BF16),
            "w_ft": w_in[l][:, 6 * WIDTH:6 * WIDTH + HEADS].T.astype(BF16),
            "w_g": w_in[l][:, 6 * WIDTH + HEADS:].astype(BF16),
            "b_f": b_f[l][:, None],
            "w_pa": w_proj_a[l].astype(BF16), "w_pb": w_proj_b[l].astype(BF16),
            "w_out": w_out[l].astype(BF16), "w_up": w_up[l].astype(BF16),
            "conv_w": conv_w[l], "conv_b": conv_b[l][None], "w_down": w_down[l].astype(BF16),
        }
        bias = _band_bias(rel_table[l])
        zero_state = jnp.zeros((x_p.shape[0], CONV_W - 1, up), F32)
        x_p, (ka, va, kb, vb, lf, cv) = _layer(x_p, None, zero_state, bias, w)
        p_states.append((ka, va, kb, vb, lf, cv))
        caches = (_channel_major(cache_k_a[l]), _channel_major(cache_v_a[l]), _channel_major(cache_k_b[l]),
                  _channel_major(cache_v_b[l]), cache_logf_b[l].transpose(0, 2, 1))
        x_s, st = _layer(x_s, caches, state_conv_ffn[l], bias, w)
        s_states.append(st)
    stack = lambda states: [jnp.stack(s) for s in zip(*states)]
    return (x_p, x_s, *stack(p_states), *stack(s_states))
```

```python
import functools

import jax
import jax.numpy as jnp
from jax import lax
from jax.experimental import pallas as pl
from jax.experimental.pallas import tpu as pltpu

F32, BF16 = jnp.float32, jnp.bfloat16

HEADS = 8
HEAD_DIM = 64
WIDTH = HEADS * HEAD_DIM
CHUNK = 64
PAST_CHUNKS = 8
BAND = PAST_CHUNKS * CHUNK
REL_CLIP = 128
REL_SIZE = 2 * REL_CLIP + 1
CONV_W = 3
EPS = 1e-6
NEG = -1e30
LOG2E = 1.4426950408889634

SUBLANES = 8
ATT_BLK = 256
BAND_KEYS = 3 * ATT_BLK
ROW_TILE = 512
FFN_COLS = 256
CUMSUM_BLK = 256
VMEM_LIMIT = 56 * 1024 * 1024


def _params(*sem):
    return pltpu.CompilerParams(dimension_semantics=sem, vmem_limit_bytes=VMEM_LIMIT)


def _resident(shape, index_map):
    return pl.BlockSpec(shape, index_map, pipeline_mode=pl.Buffered(1))


def _rmsnorm(x, g):
    return x * lax.rsqrt(jnp.mean(x * x, axis=-1, keepdims=True) + EPS) * g


def _dot(a, b):
    return jnp.dot(a, b, preferred_element_type=F32)


def _dot_nt(a, b):
    return lax.dot_general(a, b, (((1,), (1,)), ((), ())), preferred_element_type=F32)


def _split3(x):
    hi = x.astype(BF16)
    r = x - hi.astype(F32)
    mid = r.astype(BF16)
    lo = (r - mid.astype(F32)).astype(BF16)
    return hi, mid, lo


def _inproj_kernel(x_ref, g_ref, w_ref, wft_ref, bf_ref,
                   qa_ref, qb_ref, ka_ref, kb_ref, vat16_ref, vbt16_ref,
                   kat_ref, vat_ref, kbt_ref, vbt_ref, lft_ref):
    h = _rmsnorm(x_ref[...], g_ref[...]).astype(BF16)
    scale = HEAD_DIM ** -0.5

    def proj(c):
        return _dot_nt(h, w_ref[c * WIDTH:(c + 1) * WIDTH, :].astype(BF16))

    def put(ref, zt):
        per_seq = zt.shape[1] // ref.shape[0]
        for sq in range(ref.shape[0]):
            ref[sq] = zt[:, sq * per_seq:(sq + 1) * per_seq]

    qa_ref[...] = (proj(0) * (scale * LOG2E)).astype(BF16)
    qb_ref[...] = (proj(3) * (scale * LOG2E)).astype(BF16)
    for c, row_ref, t_ref in ((1, ka_ref, kat_ref), (4, kb_ref, kbt_ref)):
        z = proj(c)
        row_ref[...] = z.astype(BF16)
        put(t_ref, z.T)
    for c, t16_ref, t_ref in ((2, vat16_ref, vat_ref), (5, vbt16_ref, vbt_ref)):
        zt = proj(c).T
        put(t_ref, zt)
        put(t16_ref, zt.astype(BF16))
    put(lft_ref, jax.nn.log_sigmoid(_dot_nt(wft_ref[...].astype(BF16), h) + bf_ref[...]))


def _inproj(x, g, w_in_t, b_f, seq_len, band_keep):
    n, d = x.shape
    bsz = n // seq_len
    tm = min(ROW_TILE, n)
    tiles_per_seq = max(seq_len // tm, 1)
    seqs_per_tile = max(tm // seq_len, 1)
    cols = tm // seqs_per_tile
    assert band_keep == cols
    row = lambda i: (i, 0)
    fixed = lambda i: (0, 0)
    along = lambda i: (i // tiles_per_seq, 0, i % tiles_per_seq)
    kept = lambda i: (i // tiles_per_seq, 0, 0)
    rows16 = (jax.ShapeDtypeStruct((n, WIDTH), BF16), pl.BlockSpec((tm, WIDTH), row))

    def chan(channels, length, dtype, index_map):
        return (jax.ShapeDtypeStruct((bsz, channels, length), dtype),
                pl.BlockSpec((seqs_per_tile, channels, cols), index_map))

    outs = [rows16] * 4
    outs += [chan(WIDTH, seq_len, BF16, along)] * 2
    outs += [chan(WIDTH, band_keep, F32, kept)] * 2
    outs += [chan(WIDTH, seq_len, F32, along)] * 2
    outs += [chan(HEADS, seq_len, F32, along)]
    return pl.pallas_call(
        _inproj_kernel,
        grid=(n // tm,),
        in_specs=[pl.BlockSpec((tm, d), row),
                  pl.BlockSpec((1, d), fixed),
                  _resident((6 * WIDTH, d), fixed),
                  pl.BlockSpec((HEADS, d), lambda i: (6 * WIDTH // HEADS, 0)),
                  pl.BlockSpec((HEADS, 1), fixed)],
        out_specs=[spec for _, spec in outs],
        out_shape=[shape for shape, _ in outs],
        compiler_params=_params("arbitrary"),
        name="inproj",
    )(x, g, w_in_t, w_in_t, b_f)


def _cumsum_kernel(*refs, seg_lens, col_from):
    seg_refs = refs[:len(seg_lens)]
    ccol_ref, crow_ref = refs[len(seg_lens):]
    carry = jnp.zeros((crow_ref.shape[0], 1), F32)
    off = 0
    for ref, n in zip(seg_refs, seg_lens):
        for o in range(0, n, CUMSUM_BLK):
            b = min(CUMSUM_BLK, n - o)
            r = lax.broadcasted_iota(jnp.int32, (b, b), 0)
            c = lax.broadcasted_iota(jnp.int32, (b, b), 1)
            upper = jnp.where(r <= c, 1.0, 0.0).astype(BF16)
            sums = carry
            for p in _split3(ref[:, o:o + b]):
                sums = sums + _dot(p, upper)
            crow_ref[:, off + o:off + o + b] = sums
            carry = sums[:, b - 1:b]
            if off + o >= col_from:
                by_position = sums.T
                for bb in range(ccol_ref.shape[0]):
                    ccol_ref[bb, off + o - col_from:off + o - col_from + b, :] = (
                        by_position[:, bb * HEADS:(bb + 1) * HEADS])
        off += n


def _cumsum(*segs, col_from=0):
    bsz = segs[0].shape[0]
    seg_lens = tuple(s.shape[2] for s in segs)
    total = sum(seg_lens)
    assert col_from % CUMSUM_BLK == 0 or col_from in (0, seg_lens[0])
    ccol, crow = pl.pallas_call(
        functools.partial(_cumsum_kernel, seg_lens=seg_lens, col_from=col_from),
        out_shape=[jax.ShapeDtypeStruct((bsz, total - col_from, HEADS), F32),
                   jax.ShapeDtypeStruct((bsz * HEADS, total), F32)],
        compiler_params=pltpu.CompilerParams(vmem_limit_bytes=VMEM_LIMIT),
        name="cumsum_logf",
    )(*[s.reshape(bsz * HEADS, s.shape[2]) for s in segs])
    return ccol, crow.reshape(bsz, HEADS, total)


REL_PAD = 384
DIST_SPAN = 1024


def _band_bias_kernel(tbl_ref, qk_ref, kq_ref):
    j = lax.broadcasted_iota(jnp.int32, (REL_PAD, DIST_SPAN), 1)
    r = lax.broadcasted_iota(jnp.int32, (REL_PAD, DIST_SPAN), 0)
    parts = _split3(tbl_ref[...])

    def by_offset(dist):
        onehot = jnp.where(jnp.clip(dist, -REL_CLIP, REL_CLIP) + REL_CLIP == r, 1.0, 0.0).astype(BF16)
        return sum(_dot(p, onehot) for p in parts) * LOG2E

    def skewed(e_row, rows, cols, shift):
        wide = jnp.broadcast_to(e_row, (rows, DIST_SPAN))
        return pltpu.roll(wide, shift % DIST_SPAN, 1, stride=1, stride_axis=0)[:, :cols]

    e_qk = by_offset((BAND_KEYS - 1) - j)
    e_kq = by_offset(j - (ATT_BLK - 1))
    qc = lax.broadcasted_iota(jnp.int32, (ATT_BLK, BAND_KEYS), 0) // CHUNK
    kc = lax.broadcasted_iota(jnp.int32, (ATT_BLK, BAND_KEYS), 1) // CHUNK
    vis_qk = (kc >= qc) & (kc <= qc + PAST_CHUNKS)
    kc = lax.broadcasted_iota(jnp.int32, (BAND_KEYS, ATT_BLK), 0) // CHUNK
    qc = lax.broadcasted_iota(jnp.int32, (BAND_KEYS, ATT_BLK), 1) // CHUNK
    vis_kq = (kc >= qc) & (kc <= qc + PAST_CHUNKS)
    for h in range(HEADS):
        qk_ref[h] = jnp.where(vis_qk, skewed(e_qk[h:h + 1, :], ATT_BLK, BAND_KEYS, -(ATT_BLK - 1)), NEG)
        kq_ref[h] = jnp.where(vis_kq, skewed(e_kq[h:h + 1, :], BAND_KEYS, ATT_BLK, -(BAND_KEYS - 1)), NEG)


def _band_bias(rel_table):
    tbl = jnp.pad(rel_table, ((0, 0), (0, REL_PAD - REL_SIZE)))
    return pl.pallas_call(
        _band_bias_kernel,
        out_shape=[jax.ShapeDtypeStruct((HEADS, ATT_BLK, BAND_KEYS), F32),
                   jax.ShapeDtypeStruct((HEADS, BAND_KEYS, ATT_BLK), F32)],
        compiler_params=pltpu.CompilerParams(vmem_limit_bytes=VMEM_LIMIT),
        name="band_bias",
    )(tbl)


BAND_BLOCKS = BAND_KEYS // ATT_BLK


def _band_attn_prompt_kernel(q_ref, k_ref, vt_ref, bias_ref, o_ref, s_ref):
    i = pl.program_id(1)
    ones = jnp.ones((SUM_ROWS, ATT_BLK), BF16)

    def run(clipped):
        blocks = []
        for g in range(BAND_BLOCKS):
            j = i - (BAND_BLOCKS - 1) + g
            start = pl.multiple_of(jnp.maximum(j, 0) * ATT_BLK, ATT_BLK)
            masks = clipped and g < BAND_BLOCKS - 1
            blocks.append((start, jnp.where(j >= 0, 0.0, NEG) if masks else None))

        def scores(h):
            sl = slice(h * HEAD_DIM, (h + 1) * HEAD_DIM)
            q = q_ref[0, :, sl]
            for g, (start, penalty) in enumerate(blocks):
                s = _dot_nt(k_ref[0, pl.ds(start, ATT_BLK), sl], q) + bias_ref[h, g * ATT_BLK:(g + 1) * ATT_BLK, :]
                s_ref[h % 2, g] = s if penalty is None else s + penalty

        def finish(h):
            sl = slice(h * HEAD_DIM, (h + 1) * HEAD_DIM)
            s = [s_ref[h % 2, g] for g in range(BAND_BLOCKS)]
            m = functools.reduce(jnp.maximum, [jnp.max(x, axis=0, keepdims=True) for x in s])
            acc = 0.0
            for x, (start, _) in zip(s, blocks):
                vt = jnp.concatenate([vt_ref[0, sl, pl.ds(start, ATT_BLK)], ones], axis=0)
                acc = acc + _dot(vt, jnp.exp2(x - m).astype(BF16))
            o_ref[0, sl, :] = (acc[:HEAD_DIM] / acc[HEAD_DIM:HEAD_DIM + 1]).astype(BF16)

        scores(0)
        for h in range(HEADS):
            if h + 1 < HEADS:
                scores(h + 1)
            finish(h)

    pl.when(i >= BAND_BLOCKS - 1)(lambda: run(clipped=False))
    pl.when(i < BAND_BLOCKS - 1)(lambda: run(clipped=True))


def _band_attn_prompt(q, k, vt, bias_kq):
    bsz, s, _ = q.shape
    qblk = pl.BlockSpec((1, ATT_BLK, WIDTH), lambda b, i: (b, i, 0))
    return pl.pallas_call(
        _band_attn_prompt_kernel,
        grid=(bsz, s // ATT_BLK),
        in_specs=[qblk,
                  pl.BlockSpec((1, s, WIDTH), lambda b, i: (b, 0, 0)),
                  pl.BlockSpec((1, WIDTH, s), lambda b, i: (b, 0, 0)),
                  _resident(bias_kq.shape, lambda b, i: (0, 0, 0))],
        out_specs=pl.BlockSpec((1, WIDTH, ATT_BLK), lambda b, i: (b, 0, i)),
        out_shape=jax.ShapeDtypeStruct((bsz, WIDTH, s), BF16),
        scratch_shapes=[pltpu.VMEM((2, BAND_BLOCKS, ATT_BLK, ATT_BLK), F32)],
        compiler_params=_params("parallel", "parallel"),
        name="band_attn_prompt",
    )(q, k, vt, bias_kq)


def _decay_lanes(c, key_side):
    terms = jnp.concatenate(_split3(c), axis=1)
    row = lax.broadcasted_iota(jnp.int32, (3 * HEADS, WIDTH), 0)
    lane = lax.broadcasted_iota(jnp.int32, (3 * HEADS, WIDTH), 1)
    first_one, first_term, sign = (3, 0, -1.0) if key_side else (0, 3, 1.0)
    place = jnp.where(lane == (row % HEADS) * HEAD_DIM + first_term + row // HEADS, sign, 0.0).astype(BF16)
    in_group = lax.broadcasted_iota(jnp.int32, (1, WIDTH), 1) % HEAD_DIM
    ones = jnp.where((in_group >= first_one) & (in_group < first_one + 3), 1.0, 0.0)
    return (ones + _dot(terms, place)).astype(BF16)


PAD_DIM = 2 * HEAD_DIM
SUM_ROWS = 16


def _fox_attn_prompt_kernel(q_ref, k_ref, vt_ref, ccol_ref, o_ref,
                            kp_ref, qp_ref, s_ref, m_ref, acc_ref):
    i = pl.program_id(1)
    qstart = pl.multiple_of(i * ATT_BLK, ATT_BLK)

    @pl.when(i == 0)
    def _():
        extra = _decay_lanes(ccol_ref[0] * LOG2E, key_side=True)
        for h in range(HEADS):
            sl = slice(h * HEAD_DIM, (h + 1) * HEAD_DIM)
            kp_ref[:, h * PAD_DIM:(h + 1) * PAD_DIM] = jnp.concatenate([k_ref[0, :, sl], extra[:, sl]], axis=1)

    extra = _decay_lanes(ccol_ref[0, pl.ds(qstart, ATT_BLK), :] * LOG2E, key_side=False)
    for h in range(HEADS):
        sl = slice(h * HEAD_DIM, (h + 1) * HEAD_DIM)
        qp_ref[:, h * PAD_DIM:(h + 1) * PAD_DIM] = jnp.concatenate([q_ref[0, :, sl], extra[:, sl]], axis=1)

    key = lax.broadcasted_iota(jnp.int32, (ATT_BLK, ATT_BLK), 0)
    qry = lax.broadcasted_iota(jnp.int32, (ATT_BLK, ATT_BLK), 1)
    causal = key <= qry
    ones = jnp.ones((SUM_ROWS, ATT_BLK), BF16)
    m_ref[...] = jnp.full(m_ref.shape, NEG, F32)
    acc_ref[...] = jnp.zeros(acc_ref.shape, F32)

    def scores(j, h):
        start = pl.multiple_of(j * ATT_BLK, ATT_BLK)
        pad = slice(h * PAD_DIM, (h + 1) * PAD_DIM)
        return _dot_nt(kp_ref[pl.ds(start, ATT_BLK), pad], qp_ref[:, pad])

    def update(j, h, s, masked):
        start = pl.multiple_of(j * ATT_BLK, ATT_BLK)
        if masked:
            s = jnp.where(causal, s, NEG)
        m_old = m_ref[h]
        m_new = jnp.maximum(m_old, jnp.max(s, axis=0, keepdims=True))
        alpha = jnp.exp2(m_old - m_new)
        p = jnp.exp2(s - m_new)
        vt = vt_ref[0, h * HEAD_DIM:(h + 1) * HEAD_DIM, pl.ds(start, ATT_BLK)]
        acc_ref[h] = alpha * acc_ref[h] + _dot(jnp.concatenate([vt, ones], axis=0), p.astype(BF16))
        m_ref[h] = m_new

    def step(j, cur, masked=False, prefetch=True):
        if prefetch:
            for h in range(HEADS):
                s_ref[1 - cur, h] = scores(j + 1, h)
        for h in range(HEADS):
            update(j, h, s_ref[cur, h], masked)

    for h in range(HEADS):
        s_ref[0, h] = scores(0, h)

    def pair(t, carry):
        step(2 * t, 0)
        step(2 * t + 1, 1)
        return carry

    lax.fori_loop(0, i // 2, pair, 0)

    @pl.when(i % 2 == 1)
    def _():
        step(i - 1, 0)
        step(i, 1, masked=True, prefetch=False)

    @pl.when(i % 2 == 0)
    def _():
        step(i, 0, masked=True, prefetch=False)

    for h in range(HEADS):
        sl = slice(h * HEAD_DIM, (h + 1) * HEAD_DIM)
        out = acc_ref[h, :HEAD_DIM, :] / acc_ref[h, HEAD_DIM:HEAD_DIM + 1, :]
        o_ref[0, sl, :] = out.astype(BF16)


def _fox_attn_prompt(q, k, vt, ccol):
    bsz, s, _ = q.shape
    qblk = pl.BlockSpec((1, ATT_BLK, WIDTH), lambda b, i: (b, i, 0))
    return pl.pallas_call(
        _fox_attn_prompt_kernel,
        grid=(bsz, s // ATT_BLK),
        in_specs=[qblk,
                  pl.BlockSpec((1, s, WIDTH), lambda b, i: (b, 0, 0)),
                  pl.BlockSpec((1, WIDTH, s), lambda b, i: (b, 0, 0)),
                  pl.BlockSpec((1, s, HEADS), lambda b, i: (b, 0, 0))],
        out_specs=pl.BlockSpec((1, WIDTH, ATT_BLK), lambda b, i: (b, 0, i)),
        out_shape=jax.ShapeDtypeStruct((bsz, WIDTH, s), BF16),
        scratch_shapes=[pltpu.VMEM((s, HEADS * PAD_DIM), BF16),
                        pltpu.VMEM((ATT_BLK, HEADS * PAD_DIM), BF16),
                        pltpu.VMEM((2, HEADS, ATT_BLK, ATT_BLK), F32),
                        pltpu.VMEM((HEADS, 1, ATT_BLK), F32),
                        pltpu.VMEM((HEADS, HEAD_DIM + SUM_ROWS, ATT_BLK), F32)],
        compiler_params=_params("parallel", "arbitrary"),
        name="fox_attn_prompt",
    )(q, k, vt, ccol)


def _pipelined_heads(scores, finish):
    ahead = scores(0)
    for h in range(HEADS):
        current = ahead
        if h + 1 < HEADS:
            ahead = scores(h + 1)
        finish(h, current)


def _sample_attention(s_past, s_new, vct, vnt):
    m = jnp.maximum(jnp.max(s_past, axis=-1, keepdims=True), jnp.max(s_new, axis=-1, keepdims=True))
    p_past = jnp.exp2(s_past - m)
    p_new = jnp.exp2(s_new - m)
    l = jnp.sum(p_past, axis=-1, keepdims=True) + jnp.sum(p_new, axis=-1, keepdims=True)
    acc = _dot_nt(p_past.astype(BF16), vct) + _dot_nt(p_new.astype(BF16), vnt)
    return (acc / l).astype(BF16)


def _band_attn_sample_kernel(q_ref, kct_ref, vct_ref, kn_ref, vnt_ref, bias_ref, o_ref):
    t = q_ref.shape[1]
    past = kct_ref.shape[2]

    def scores(h):
        sl = slice(h * HEAD_DIM, (h + 1) * HEAD_DIM)
        q = q_ref[0, :, sl]
        s_past = _dot(q, kct_ref[0, sl, :].astype(BF16)) + bias_ref[h, 0:t, BAND - past:BAND]
        s_new = _dot_nt(q, kn_ref[0, :, sl]) + bias_ref[h, 0:t, BAND:BAND + t]
        return s_past, s_new

    def finish(h, s):
        sl = slice(h * HEAD_DIM, (h + 1) * HEAD_DIM)
        o_ref[0, :, sl] = _sample_attention(*s, vct_ref[0, sl, :].astype(BF16), vnt_ref[0, sl, :])

    _pipelined_heads(scores, finish)


def _band_attn_sample(q, k_cache_t, v_cache_t, k_new, v_new_t, bias):
    bsz, t, _ = q.shape
    past = k_cache_t.shape[2]
    new = pl.BlockSpec((1, t, WIDTH), lambda b: (b, 0, 0))
    new_t = pl.BlockSpec((1, WIDTH, t), lambda b: (b, 0, 0))
    old_t = pl.BlockSpec((1, WIDTH, past), lambda b: (b, 0, 0))
    return pl.pallas_call(
        _band_attn_sample_kernel,
        grid=(bsz,),
        in_specs=[new, old_t, old_t, new, new_t, _resident(bias.shape, lambda b: (0, 0, 0))],
        out_specs=new,
        out_shape=jax.ShapeDtypeStruct(q.shape, BF16),
        compiler_params=_params("parallel"),
        name="band_attn_sample",
    )(q, k_cache_t, v_cache_t, k_new, v_new_t, bias)


def _fox_attn_sample_kernel(q_ref, kct_ref, vct_ref, kn_ref, vnt_ref, ccol_ref, crow_ref, o_ref):
    t = q_ref.shape[1]
    past = kct_ref.shape[2]
    row = lax.broadcasted_iota(jnp.int32, (t, t), 0)
    col = lax.broadcasted_iota(jnp.int32, (t, t), 1)
    causal = col <= row

    def scores(h):
        sl = slice(h * HEAD_DIM, (h + 1) * HEAD_DIM)
        q = q_ref[0, :, sl]
        cq = ccol_ref[0, :, h:h + 1]
        s_past = _dot(q, kct_ref[0, sl, :].astype(BF16)) + (cq - crow_ref[0, h:h + 1, 0:past]) * LOG2E
        s_new = _dot_nt(q, kn_ref[0, :, sl]) + (cq - crow_ref[0, h:h + 1, past:past + t]) * LOG2E
        return s_past, jnp.where(causal, s_new, NEG)

    def finish(h, s):
        sl = slice(h * HEAD_DIM, (h + 1) * HEAD_DIM)
        o_ref[0, :, sl] = _sample_attention(*s, vct_ref[0, sl, :].astype(BF16), vnt_ref[0, sl, :])

    _pipelined_heads(scores, finish)


def _fox_attn_sample(q, k_cache_t, v_cache_t, k_new, v_new_t, ccol, crow):
    bsz, t, _ = q.shape
    past = k_cache_t.shape[2]
    assert past % t == 0
    new = pl.BlockSpec((1, t, WIDTH), lambda b: (b, 0, 0))
    new_t = pl.BlockSpec((1, WIDTH, t), lambda b: (b, 0, 0))
    old_t = pl.BlockSpec((1, WIDTH, past), lambda b: (b, 0, 0))
    return pl.pallas_call(
        _fox_attn_sample_kernel,
        grid=(bsz,),
        in_specs=[new, old_t, old_t, new, new_t,
                  pl.BlockSpec((1, t, HEADS), lambda b: (b, 0, 0)),
                  pl.BlockSpec((1, HEADS, past + t), lambda b: (b, 0, 0))],
        out_specs=new,
        out_shape=jax.ShapeDtypeStruct(q.shape, BF16),
        compiler_params=_params("parallel"),
        name="fox_attn_sample",
    )(q, k_cache_t, v_cache_t, k_new, v_new_t, ccol, crow)


def _postmix_kernel(x_ref, oa_ref, ob_ref, gpre_ref, gpost_ref, wt_ref, wpa_ref, wpb_ref, wout_ref, y_ref,
                    *, channel_major):
    x = x_ref[...]
    d = x.shape[-1]
    h = _rmsnorm(x, gpre_ref[...]).astype(BF16)
    gates = wt_ref.shape[0] - 2 * d
    gate_a = jax.nn.sigmoid(_dot_nt(h, wt_ref[gates:gates + d, :].astype(BF16)))
    gate_b = jax.nn.sigmoid(_dot_nt(h, wt_ref[gates + d:, :].astype(BF16)))
    if channel_major:
        proj = lambda o_ref, w_ref: lax.dot_general(o_ref[0], w_ref[...], (((0,), (0,)), ((), ())),
                                                    preferred_element_type=F32)
    else:
        proj = lambda o_ref, w_ref: _dot(o_ref[...], w_ref[...])
    merged = gate_a * proj(oa_ref, wpa_ref) + gate_b * proj(ob_ref, wpb_ref)
    y = x + _rmsnorm(_dot(merged.astype(BF16), wout_ref[...]), gpost_ref[...])
    y_ref[...] = _interleave(y)


def _interleave(rows):
    n, d = rows.shape
    return jnp.swapaxes(rows.reshape(SUBLANES, n // SUBLANES, d), 0, 1)


def _deinterleave(planes):
    p, s, d = planes.shape
    return jnp.swapaxes(planes, 0, 1).reshape(s * p, d)


def _postmix(x, oa, ob, g_pre, g_post, w_in_t, w_pa, w_pb, w_out):
    n, d = x.shape
    tm = min(ROW_TILE, n)
    row = lambda i: (i, 0)
    fixed = lambda i: (0, 0)
    channel_major = oa.ndim == 3
    if channel_major:
        tiles_per_seq = oa.shape[2] // tm
        o_spec = pl.BlockSpec((1, WIDTH, tm), lambda i: (i // tiles_per_seq, 0, i % tiles_per_seq))
    else:
        o_spec = pl.BlockSpec((tm, WIDTH), row)
    return pl.pallas_call(
        functools.partial(_postmix_kernel, channel_major=channel_major),
        grid=(n // tm,),
        in_specs=[pl.BlockSpec((tm, d), row),
                  o_spec,
                  o_spec,
                  pl.BlockSpec((1, d), fixed),
                  pl.BlockSpec((1, d), fixed),
                  _resident(w_in_t.shape, fixed),
                  _resident(w_pa.shape, fixed),
                  _resident(w_pb.shape, fixed),
                  _resident(w_out.shape, fixed)],
        out_specs=pl.BlockSpec((tm // SUBLANES, SUBLANES, d), lambda i: (i, 0, 0)),
        out_shape=jax.ShapeDtypeStruct((n // SUBLANES, SUBLANES, d), F32),
        compiler_params=_params("parallel"),
        name="postmix",
    )(x, oa, ob, g_pre, g_post, w_in_t, w_pa, w_pb, w_out)


def _ffn_kernel(x_ref, st_ref, gpre_ref, gpost_ref, wup_ref, cw_ref, cb_ref, wdn_ref,
                y_ref, nst_ref, hist_ref, ext_ref, h_ref, f_ref, *, nseg):
    @pl.when(pl.program_id(1) == 0)
    def _():
        hist_ref[...] = st_ref[...]

    planes, _, d = x_ref.shape
    tm = planes * SUBLANES
    d_ff = wdn_ref.shape[0]
    h_ref[...] = _rmsnorm(x_ref[...].reshape(tm, d), gpre_ref[...]).astype(BF16)
    n_chunks = d_ff // FFN_COLS
    first_sublane = lax.broadcasted_iota(jnp.int32, (SUBLANES, FFN_COLS), 0) == 0

    def parts(c):
        for part in range(2):
            yield (slice(part * d_ff + c * FFN_COLS, part * d_ff + (c + 1) * FFN_COLS),
                   slice(part * FFN_COLS, (part + 1) * FFN_COLS))

    def up(c):
        ext = ext_ref.at[c % 2]
        for cols, dst in parts(c):
            u = _dot(h_ref[...], wup_ref[:, cols]).reshape(planes, SUBLANES, FFN_COLS)
            ext[CONV_W - 1:, :, dst] = u
            for k in range(CONV_W - 1):
                last = u[planes - (CONV_W - 1) + k]
                if nseg == 1:
                    ext[k, :, dst] = jnp.where(first_sublane, hist_ref[0, k:k + 1, cols], pltpu.roll(last, 1, 0))
                    hist_ref[0, k:k + 1, cols] = last[SUBLANES - 1:, :]
                else:
                    ext[k, :, dst] = hist_ref[:, k, cols]
                    hist_ref[:, k, cols] = last

    def down(c):
        ext = ext_ref.at[c % 2]
        halves = []
        for cols, dst in parts(c):
            y = cb_ref[:, cols]
            for tap in range(CONV_W):
                y = y + ext[tap:tap + planes, :, dst] * cw_ref[tap:tap + 1, cols]
            halves.append(y)
        act = (jax.nn.gelu(halves[0]) * halves[1]).reshape(tm, FFN_COLS)
        f_ref[:, c * FFN_COLS:(c + 1) * FFN_COLS] = act.astype(BF16)

    up(0)
    for c in range(n_chunks):
        if c + 1 < n_chunks:
            up(c + 1)
        down(c)
    f = _dot(f_ref[...], wdn_ref[...])
    y = x_ref[...].reshape(tm, d) + _rmsnorm(f, gpost_ref[...])
    y_ref[...] = _deinterleave(y.reshape(planes, SUBLANES, d))
    nst_ref[...] = hist_ref[...]


def _ffn(x, bsz, s, state, g_pre, g_post, w_up, conv_w, conv_b, w_down):
    d = x.shape[-1]
    up = w_up.shape[1]
    tm = min(ROW_TILE, bsz * s)
    planes = tm // SUBLANES
    if s >= tm:
        nseg, tiles = 1, s // tm
    else:
        nseg, tiles = tm // s, 1
        assert nseg == SUBLANES
    outer = bsz // nseg
    row = lambda o, t: (o * tiles + t, 0)
    fixed = lambda o, t: (0, 0)
    st_spec = pl.BlockSpec((nseg, CONV_W - 1, up), lambda o, t: (o, 0, 0))
    y, new_state = pl.pallas_call(
        functools.partial(_ffn_kernel, nseg=nseg),
        grid=(outer, tiles),
        in_specs=[pl.BlockSpec((planes, SUBLANES, d), lambda o, t: (o * tiles + t, 0, 0)),
                  st_spec,
                  pl.BlockSpec((1, d), fixed),
                  pl.BlockSpec((1, d), fixed),
                  _resident(w_up.shape, fixed),
                  pl.BlockSpec(conv_w.shape, fixed),
                  pl.BlockSpec((1, up), fixed),
                  _resident(w_down.shape, fixed)],
        out_specs=[pl.BlockSpec((tm, d), row), st_spec],
        out_shape=[jax.ShapeDtypeStruct((bsz * s, d), F32),
                   jax.ShapeDtypeStruct(state.shape, F32)],
        scratch_shapes=[pltpu.VMEM((nseg, CONV_W - 1, up), F32),
                        pltpu.VMEM((2, CONV_W - 1 + planes, SUBLANES, 2 * FFN_COLS), F32),
                        pltpu.VMEM((tm, d), BF16),
                        pltpu.VMEM((tm, w_down.shape[0]), BF16)],
        compiler_params=_params("arbitrary", "arbitrary"),
        name="conv_ffn",
    )(x, state, g_pre, g_post, w_up, conv_w, conv_b, w_down)
    return y.reshape(bsz, s, d), new_state


def _layer(x, caches, conv_state, bias, w):
    bsz, s, d = x.shape
    n = bsz * s
    x2 = x.reshape(n, d)
    keep = min(BAND, s)
    (qa, qb, ka16, kb16, vat16, vbt16, kat, vat, kbt, vbt, lft) = _inproj(
        x2, w["g_pre_mix"], w["w_in_t"], w["b_f"], seq_len=s, band_keep=keep)
    seq = lambda a: a.reshape(bsz, s, a.shape[-1])
    if caches is None:
        ccol, _ = _cumsum(lft)
        oa = _band_attn_prompt(seq(qa), seq(ka16), vat16, bias[1])
        ob = _fox_attn_prompt(seq(qb), seq(kb16), vbt16, ccol)
    else:
        ckat, cvat, ckbt, cvbt, clft = caches
        ccol, crow = _cumsum(clft, lft, col_from=clft.shape[2])
        oa = _band_attn_sample(seq(qa), ckat, cvat, seq(ka16), vat16, bias[0]).reshape(n, WIDTH)
        ob = _fox_attn_sample(seq(qb), ckbt, cvbt, seq(kb16), vbt16, ccol, crow).reshape(n, WIDTH)
    x1 = _postmix(x2, oa, ob, w["g_pre_mix"], w["g_post_mix"], w["w_in_t"], w["w_pa"], w["w_pb"], w["w_out"])
    y, new_conv = _ffn(x1, bsz, s, conv_state, w["g_pre_ffn"], w["g_post_ffn"],
                       w["w_up"], w["conv_w"], w["conv_b"], w["w_down"])
    heads = lambda a: a.reshape(bsz, HEADS, HEAD_DIM, a.shape[-1]).transpose(0, 3, 1, 2)
    return y, (heads(kat), heads(vat), heads(kbt), heads(vbt), lft.transpose(0, 2, 1), new_conv)


def _channel_major(cache):
    bsz, past = cache.shape[:2]
    return cache.transpose(0, 2, 3, 1).reshape(bsz, WIDTH, past)


def kernel(x_prompt, x_sample, cache_k_a, cache_v_a, cache_k_b, cache_v_b, cache_logf_b, state_conv_ffn,
           g_pre_mix, g_post_mix, g_pre_ffn, g_post_ffn, w_in, b_f, rel_table, w_proj_a, w_proj_b, w_out,
           w_up, conv_w, conv_b, w_down):
    depth = w_in.shape[0]
    up = w_up.shape[-1]
    x_p, x_s = x_prompt, x_sample
    p_states, s_states = [], []
    for l in range(depth):
        w = {
            "g_pre_mix": g_pre_mix[l][None], "g_post_mix": g_post_mix[l][None],
            "g_pre_ffn": g_pre_ffn[l][None], "g_post_ffn": g_post_ffn[l][None],
            "w_in_t": w_in[l].T,
            "b_f": b_f[l][:, None],
            "w_pa": w_proj_a[l].astype(BF16), "w_pb": w_proj_b[l].astype(BF16),
            "w_out": w_out[l].astype(BF16), "w_up": w_up[l].astype(BF16),
            "conv_w": conv_w[l], "conv_b": conv_b[l][None], "w_down": w_down[l].astype(BF16),
        }
        bias = _band_bias(rel_table[l])
        zero_state = jnp.zeros((x_p.shape[0], CONV_W - 1, up), F32)
        x_p, (ka, va, kb, vb, lf, cv) = _layer(x_p, None, zero_state, bias, w)
        p_states.append((ka, va, kb, vb, lf, cv))
        caches = (_channel_major(cache_k_a[l]), _channel_major(cache_v_a[l]), _channel_major(cache_k_b[l]),
                  _channel_major(cache_v_b[l]), cache_logf_b[l].transpose(0, 2, 1))
        x_s, st = _layer(x_s, caches, state_conv_ffn[l], bias, w)
        s_states.append(st)
    stack = lambda states: [jnp.stack(s) for s in zip(*states)]
    return (x_p, x_s, *stack(p_states), *stack(s_states))
```

```python
import functools

import jax
import jax.numpy as jnp
from jax import lax
from jax.experimental import pallas as pl
from jax.experimental.pallas import tpu as pltpu

F32, BF16 = jnp.float32, jnp.bfloat16

HEADS = 8
HEAD_DIM = 64
WIDTH = HEADS * HEAD_DIM
CHUNK = 64
PAST_CHUNKS = 8
BAND = PAST_CHUNKS * CHUNK
REL_CLIP = 128
REL_SIZE = 2 * REL_CLIP + 1
CONV_W = 3
EPS = 1e-6
NEG = -1e30
LOG2E = 1.4426950408889634

SUBLANES = 8
ATT_BLK = 256
BAND_KEYS = 3 * ATT_BLK
BAND_BLOCKS = BAND_KEYS // ATT_BLK
PAD_DIM = 2 * HEAD_DIM
SUM_ROWS = 16
HEADS_AHEAD = 2
ROW_TILE = 512
FFN_COLS = 256
CUMSUM_BLK = 256
VMEM_LIMIT = 56 * 1024 * 1024


def _params(*sem):
    return pltpu.CompilerParams(dimension_semantics=sem, vmem_limit_bytes=VMEM_LIMIT)


def _resident(shape, index_map):
    return pl.BlockSpec(shape, index_map, pipeline_mode=pl.Buffered(1))


def _rmsnorm(x, g):
    return x * lax.rsqrt(jnp.mean(x * x, axis=-1, keepdims=True) + EPS) * g


def _dot(a, b):
    return jnp.dot(a, b, preferred_element_type=F32)


def _dot_nt(a, b):
    return lax.dot_general(a, b, (((1,), (1,)), ((), ())), preferred_element_type=F32)


def _split3(x):
    hi = x.astype(BF16)
    r = x - hi.astype(F32)
    mid = r.astype(BF16)
    lo = (r - mid.astype(F32)).astype(BF16)
    return hi, mid, lo


def _inproj_kernel(x_ref, g_ref, w_ref, wft_ref, bf_ref,
                   qa_ref, qb_ref, ka_ref, kb_ref, vat16_ref, vbt16_ref,
                   kat_ref, vat_ref, kbt_ref, vbt_ref, lft_ref):
    h = _rmsnorm(x_ref[...], g_ref[...]).astype(BF16)
    scale = HEAD_DIM ** -0.5

    def proj(c):
        return _dot_nt(h, w_ref[c * WIDTH:(c + 1) * WIDTH, :].astype(BF16))

    def put(ref, zt):
        per_seq = zt.shape[1] // ref.shape[0]
        for sq in range(ref.shape[0]):
            ref[sq] = zt[:, sq * per_seq:(sq + 1) * per_seq]

    qa_ref[...] = (proj(0) * (scale * LOG2E)).astype(BF16)
    qb_ref[...] = (proj(3) * (scale * LOG2E)).astype(BF16)
    for c, row_ref, t_ref in ((1, ka_ref, kat_ref), (4, kb_ref, kbt_ref)):
        z = proj(c)
        row_ref[...] = z.astype(BF16)
        put(t_ref, z.T)
    for c, t16_ref, t_ref in ((2, vat16_ref, vat_ref), (5, vbt16_ref, vbt_ref)):
        zt = proj(c).T
        put(t_ref, zt)
        put(t16_ref, zt.astype(BF16))
    put(lft_ref, jax.nn.log_sigmoid(_dot_nt(wft_ref[...].astype(BF16), h) + bf_ref[...]))


def _inproj(x, g, w_in_t, b_f, seq_len, band_keep):
    n, d = x.shape
    bsz = n // seq_len
    tm = min(ROW_TILE, n)
    tiles_per_seq = max(seq_len // tm, 1)
    seqs_per_tile = max(tm // seq_len, 1)
    cols = tm // seqs_per_tile
    assert band_keep == cols
    row = lambda i: (i, 0)
    fixed = lambda i: (0, 0)
    along = lambda i: (i // tiles_per_seq, 0, i % tiles_per_seq)
    kept = lambda i: (i // tiles_per_seq, 0, 0)
    rows16 = (jax.ShapeDtypeStruct((n, WIDTH), BF16), pl.BlockSpec((tm, WIDTH), row))

    def chan(channels, length, dtype, index_map):
        return (jax.ShapeDtypeStruct((bsz, channels, length), dtype),
                pl.BlockSpec((seqs_per_tile, channels, cols), index_map))

    outs = [rows16] * 4
    outs += [chan(WIDTH, seq_len, BF16, along)] * 2
    outs += [chan(WIDTH, band_keep, F32, kept)] * 2
    outs += [chan(WIDTH, seq_len, F32, along)] * 2
    outs += [chan(HEADS, seq_len, F32, along)]
    return pl.pallas_call(
        _inproj_kernel,
        grid=(n // tm,),
        in_specs=[pl.BlockSpec((tm, d), row),
                  pl.BlockSpec((1, d), fixed),
                  _resident((6 * WIDTH, d), fixed),
                  pl.BlockSpec((HEADS, d), lambda i: (6 * WIDTH // HEADS, 0)),
                  pl.BlockSpec((HEADS, 1), fixed)],
        out_specs=[spec for _, spec in outs],
        out_shape=[shape for shape, _ in outs],
        compiler_params=_params("arbitrary"),
        name="inproj",
    )(x, g, w_in_t, w_in_t, b_f)


def _cumsum_kernel(*refs, seg_lens, col_from):
    seg_refs = refs[:len(seg_lens)]
    ccol_ref, crow_ref = refs[len(seg_lens):]
    carry = jnp.zeros((crow_ref.shape[0], 1), F32)
    off = 0
    for ref, n in zip(seg_refs, seg_lens):
        for o in range(0, n, CUMSUM_BLK):
            b = min(CUMSUM_BLK, n - o)
            r = lax.broadcasted_iota(jnp.int32, (b, b), 0)
            c = lax.broadcasted_iota(jnp.int32, (b, b), 1)
            upper = jnp.where(r <= c, 1.0, 0.0).astype(BF16)
            sums = carry
            for p in _split3(ref[:, o:o + b]):
                sums = sums + _dot(p, upper)
            crow_ref[:, off + o:off + o + b] = sums
            carry = sums[:, b - 1:b]
            if off + o >= col_from:
                by_position = sums.T
                for bb in range(ccol_ref.shape[0]):
                    ccol_ref[bb, off + o - col_from:off + o - col_from + b, :] = (
                        by_position[:, bb * HEADS:(bb + 1) * HEADS])
        off += n


def _cumsum(*segs, col_from=0):
    bsz = segs[0].shape[0]
    seg_lens = tuple(s.shape[2] for s in segs)
    total = sum(seg_lens)
    assert col_from % CUMSUM_BLK == 0 or col_from in (0, seg_lens[0])
    ccol, crow = pl.pallas_call(
        functools.partial(_cumsum_kernel, seg_lens=seg_lens, col_from=col_from),
        out_shape=[jax.ShapeDtypeStruct((bsz, total - col_from, HEADS), F32),
                   jax.ShapeDtypeStruct((bsz * HEADS, total), F32)],
        compiler_params=pltpu.CompilerParams(vmem_limit_bytes=VMEM_LIMIT),
        name="cumsum_logf",
    )(*[s.reshape(bsz * HEADS, s.shape[2]) for s in segs])
    return ccol, crow.reshape(bsz, HEADS, total)


REL_PAD = 384
DIST_SPAN = 1024


def _band_bias_kernel(tbl_ref, qk_ref, kq_ref):
    j = lax.broadcasted_iota(jnp.int32, (REL_PAD, DIST_SPAN), 1)
    r = lax.broadcasted_iota(jnp.int32, (REL_PAD, DIST_SPAN), 0)
    parts = _split3(tbl_ref[...])

    def by_offset(dist):
        onehot = jnp.where(jnp.clip(dist, -REL_CLIP, REL_CLIP) + REL_CLIP == r, 1.0, 0.0).astype(BF16)
        return sum(_dot(p, onehot) for p in parts) * LOG2E

    def skewed(e_row, rows, cols, shift):
        wide = jnp.broadcast_to(e_row, (rows, DIST_SPAN))
        return pltpu.roll(wide, shift % DIST_SPAN, 1, stride=1, stride_axis=0)[:, :cols]

    e_qk = by_offset((BAND_KEYS - 1) - j)
    e_kq = by_offset(j - (ATT_BLK - 1))
    qc = lax.broadcasted_iota(jnp.int32, (ATT_BLK, BAND_KEYS), 0) // CHUNK
    kc = lax.broadcasted_iota(jnp.int32, (ATT_BLK, BAND_KEYS), 1) // CHUNK
    vis_qk = (kc >= qc) & (kc <= qc + PAST_CHUNKS)
    kc = lax.broadcasted_iota(jnp.int32, (BAND_KEYS, ATT_BLK), 0) // CHUNK
    qc = lax.broadcasted_iota(jnp.int32, (BAND_KEYS, ATT_BLK), 1) // CHUNK
    vis_kq = (kc >= qc) & (kc <= qc + PAST_CHUNKS)
    for h in range(HEADS):
        qk_ref[h] = jnp.where(vis_qk, skewed(e_qk[h:h + 1, :], ATT_BLK, BAND_KEYS, -(ATT_BLK - 1)), NEG)
        kq_ref[h] = jnp.where(vis_kq, skewed(e_kq[h:h + 1, :], BAND_KEYS, ATT_BLK, -(BAND_KEYS - 1)), NEG)


def _band_bias(rel_table):
    tbl = jnp.pad(rel_table, ((0, 0), (0, REL_PAD - REL_SIZE)))
    return pl.pallas_call(
        _band_bias_kernel,
        out_shape=[jax.ShapeDtypeStruct((HEADS, ATT_BLK, BAND_KEYS), F32),
                   jax.ShapeDtypeStruct((HEADS, BAND_KEYS, ATT_BLK), F32)],
        compiler_params=pltpu.CompilerParams(vmem_limit_bytes=VMEM_LIMIT),
        name="band_bias",
    )(tbl)


def _band_attn_prompt_kernel(q_ref, k_ref, vt_ref, bias_ref, o_ref, s_ref):
    i = pl.program_id(1)
    ones = jnp.ones((SUM_ROWS, ATT_BLK), BF16)

    def run(clipped):
        blocks = []
        for g in range(BAND_BLOCKS):
            j = i - (BAND_BLOCKS - 1) + g
            start = pl.multiple_of(jnp.maximum(j, 0) * ATT_BLK, ATT_BLK)
            masks = clipped and g < BAND_BLOCKS - 1
            blocks.append((start, jnp.where(j >= 0, 0.0, NEG) if masks else None))

        def scores(h):
            sl = slice(h * HEAD_DIM, (h + 1) * HEAD_DIM)
            q = q_ref[0, :, sl]
            for g, (start, penalty) in enumerate(blocks):
                s = _dot_nt(k_ref[0, pl.ds(start, ATT_BLK), sl], q) + bias_ref[h, g * ATT_BLK:(g + 1) * ATT_BLK, :]
                s_ref[h % (HEADS_AHEAD + 1), g] = s if penalty is None else s + penalty

        def finish(h):
            sl = slice(h * HEAD_DIM, (h + 1) * HEAD_DIM)
            s = [s_ref[h % (HEADS_AHEAD + 1), g] for g in range(BAND_BLOCKS)]
            m = functools.reduce(jnp.maximum, [jnp.max(x, axis=0, keepdims=True) for x in s])
            acc = 0.0
            for x, (start, _) in zip(s, blocks):
                vt = jnp.concatenate([vt_ref[0, sl, pl.ds(start, ATT_BLK)], ones], axis=0)
                acc = acc + _dot(vt, jnp.exp2(x - m).astype(BF16))
            o_ref[0, sl, :] = (acc[:HEAD_DIM] / acc[HEAD_DIM:HEAD_DIM + 1]).astype(BF16)

        for h in range(HEADS + HEADS_AHEAD):
            if h < HEADS:
                scores(h)
            if h >= HEADS_AHEAD:
                finish(h - HEADS_AHEAD)

    pl.when(i >= BAND_BLOCKS - 1)(lambda: run(clipped=False))
    pl.when(i < BAND_BLOCKS - 1)(lambda: run(clipped=True))


def _band_attn_prompt(q, k, vt, bias_kq):
    bsz, s, _ = q.shape
    qblk = pl.BlockSpec((1, ATT_BLK, WIDTH), lambda b, i: (b, i, 0))
    return pl.pallas_call(
        _band_attn_prompt_kernel,
        grid=(bsz, s // ATT_BLK),
        in_specs=[qblk,
                  pl.BlockSpec((1, s, WIDTH), lambda b, i: (b, 0, 0)),
                  pl.BlockSpec((1, WIDTH, s), lambda b, i: (b, 0, 0)),
                  _resident(bias_kq.shape, lambda b, i: (0, 0, 0))],
        out_specs=pl.BlockSpec((1, WIDTH, ATT_BLK), lambda b, i: (b, 0, i)),
        out_shape=jax.ShapeDtypeStruct((bsz, WIDTH, s), BF16),
        scratch_shapes=[pltpu.VMEM((HEADS_AHEAD + 1, BAND_BLOCKS, ATT_BLK, ATT_BLK), F32)],
        compiler_params=_params("parallel", "parallel"),
        name="band_attn_prompt",
    )(q, k, vt, bias_kq)


def _decay_lanes(c, key_side):
    terms = jnp.concatenate(_split3(c), axis=1)
    row = lax.broadcasted_iota(jnp.int32, (3 * HEADS, WIDTH), 0)
    lane = lax.broadcasted_iota(jnp.int32, (3 * HEADS, WIDTH), 1)
    first_one, first_term, sign = (3, 0, -1.0) if key_side else (0, 3, 1.0)
    place = jnp.where(lane == (row % HEADS) * HEAD_DIM + first_term + row // HEADS, sign, 0.0).astype(BF16)
    in_group = lax.broadcasted_iota(jnp.int32, (1, WIDTH), 1) % HEAD_DIM
    ones = jnp.where((in_group >= first_one) & (in_group < first_one + 3), 1.0, 0.0)
    return (ones + _dot(terms, place)).astype(BF16)


def _fox_attn_prompt_kernel(q_ref, k_ref, vt_ref, ccol_ref, o_ref,
                            kp_ref, qp_ref, s_ref, m_ref, acc_ref):
    i = pl.program_id(1)
    qstart = pl.multiple_of(i * ATT_BLK, ATT_BLK)

    @pl.when(i == 0)
    def _():
        extra = _decay_lanes(ccol_ref[0] * LOG2E, key_side=True)
        for h in range(HEADS):
            sl = slice(h * HEAD_DIM, (h + 1) * HEAD_DIM)
            kp_ref[:, h * PAD_DIM:(h + 1) * PAD_DIM] = jnp.concatenate([k_ref[0, :, sl], extra[:, sl]], axis=1)

    extra = _decay_lanes(ccol_ref[0, pl.ds(qstart, ATT_BLK), :] * LOG2E, key_side=False)
    for h in range(HEADS):
        sl = slice(h * HEAD_DIM, (h + 1) * HEAD_DIM)
        qp_ref[:, h * PAD_DIM:(h + 1) * PAD_DIM] = jnp.concatenate([q_ref[0, :, sl], extra[:, sl]], axis=1)

    key = lax.broadcasted_iota(jnp.int32, (ATT_BLK, ATT_BLK), 0)
    qry = lax.broadcasted_iota(jnp.int32, (ATT_BLK, ATT_BLK), 1)
    causal = key <= qry
    ones = jnp.ones((SUM_ROWS, ATT_BLK), BF16)
    m_ref[...] = jnp.full(m_ref.shape, NEG, F32)
    acc_ref[...] = jnp.zeros(acc_ref.shape, F32)

    def scores(j, h):
        start = pl.multiple_of(j * ATT_BLK, ATT_BLK)
        pad = slice(h * PAD_DIM, (h + 1) * PAD_DIM)
        return _dot_nt(kp_ref[pl.ds(start, ATT_BLK), pad], qp_ref[:, pad])

    def update(j, h, s, masked):
        start = pl.multiple_of(j * ATT_BLK, ATT_BLK)
        if masked:
            s = jnp.where(causal, s, NEG)
        m_old = m_ref[h]
        m_new = jnp.maximum(m_old, jnp.max(s, axis=0, keepdims=True))
        alpha = jnp.exp2(m_old - m_new)
        p = jnp.exp2(s - m_new)
        vt = vt_ref[0, h * HEAD_DIM:(h + 1) * HEAD_DIM, pl.ds(start, ATT_BLK)]
        acc_ref[h] = alpha * acc_ref[h] + _dot(jnp.concatenate([vt, ones], axis=0), p.astype(BF16))
        m_ref[h] = m_new

    def step(j, cur, masked=False, prefetch=True):
        for h in range(HEADS + HEADS_AHEAD):
            if prefetch and h < HEADS:
                s_ref[1 - cur, h] = scores(j + 1, h)
            if h >= HEADS_AHEAD:
                update(j, h - HEADS_AHEAD, s_ref[cur, h - HEADS_AHEAD], masked)

    for h in range(HEADS):
        s_ref[0, h] = scores(0, h)

    def pair(t, carry):
        step(2 * t, 0)
        step(2 * t + 1, 1)
        return carry

    lax.fori_loop(0, i // 2, pair, 0)

    @pl.when(i % 2 == 1)
    def _():
        step(i - 1, 0)
        step(i, 1, masked=True, prefetch=False)

    @pl.when(i % 2 == 0)
    def _():
        step(i, 0, masked=True, prefetch=False)

    for h in range(HEADS):
        sl = slice(h * HEAD_DIM, (h + 1) * HEAD_DIM)
        out = acc_ref[h, :HEAD_DIM, :] / acc_ref[h, HEAD_DIM:HEAD_DIM + 1, :]
        o_ref[0, sl, :] = out.astype(BF16)


def _fox_attn_prompt(q, k, vt, ccol):
    bsz, s, _ = q.shape
    qblk = pl.BlockSpec((1, ATT_BLK, WIDTH), lambda b, i: (b, i, 0))
    return pl.pallas_call(
        _fox_attn_prompt_kernel,
        grid=(bsz, s // ATT_BLK),
        in_specs=[qblk,
                  pl.BlockSpec((1, s, WIDTH), lambda b, i: (b, 0, 0)),
                  pl.BlockSpec((1, WIDTH, s), lambda b, i: (b, 0, 0)),
                  pl.BlockSpec((1, s, HEADS), lambda b, i: (b, 0, 0))],
        out_specs=pl.BlockSpec((1, WIDTH, ATT_BLK), lambda b, i: (b, 0, i)),
        out_shape=jax.ShapeDtypeStruct((bsz, WIDTH, s), BF16),
        scratch_shapes=[pltpu.VMEM((s, HEADS * PAD_DIM), BF16),
                        pltpu.VMEM((ATT_BLK, HEADS * PAD_DIM), BF16),
                        pltpu.VMEM((2, HEADS, ATT_BLK, ATT_BLK), F32),
                        pltpu.VMEM((HEADS, 1, ATT_BLK), F32),
                        pltpu.VMEM((HEADS, HEAD_DIM + SUM_ROWS, ATT_BLK), F32)],
        compiler_params=_params("parallel", "arbitrary"),
        name="fox_attn_prompt",
    )(q, k, vt, ccol)


def _pipelined_heads(scores, finish):
    pending = []
    for h in range(HEADS + HEADS_AHEAD):
        if h < HEADS:
            pending.append(scores(h))
        if h >= HEADS_AHEAD:
            finish(h - HEADS_AHEAD, pending.pop(0))


def _sample_attention(s_past, s_new, vct, vnt):
    m = jnp.maximum(jnp.max(s_past, axis=-1, keepdims=True), jnp.max(s_new, axis=-1, keepdims=True))
    p_past = jnp.exp2(s_past - m)
    p_new = jnp.exp2(s_new - m)
    l = jnp.sum(p_past, axis=-1, keepdims=True) + jnp.sum(p_new, axis=-1, keepdims=True)
    acc = _dot_nt(p_past.astype(BF16), vct) + _dot_nt(p_new.astype(BF16), vnt)
    return (acc / l).astype(BF16)


def _band_attn_sample_kernel(q_ref, kct_ref, vct_ref, kn_ref, vnt_ref, bias_ref, o_ref):
    t = q_ref.shape[1]
    past = kct_ref.shape[2]

    def scores(h):
        sl = slice(h * HEAD_DIM, (h + 1) * HEAD_DIM)
        q = q_ref[0, :, sl]
        s_past = _dot(q, kct_ref[0, sl, :].astype(BF16)) + bias_ref[h, 0:t, BAND - past:BAND]
        s_new = _dot_nt(q, kn_ref[0, :, sl]) + bias_ref[h, 0:t, BAND:BAND + t]
        return s_past, s_new

    def finish(h, s):
        sl = slice(h * HEAD_DIM, (h + 1) * HEAD_DIM)
        o_ref[0, :, sl] = _sample_attention(*s, vct_ref[0, sl, :].astype(BF16), vnt_ref[0, sl, :])

    _pipelined_heads(scores, finish)


def _band_attn_sample(q, k_cache_t, v_cache_t, k_new, v_new_t, bias):
    bsz, t, _ = q.shape
    past = k_cache_t.shape[2]
    new = pl.BlockSpec((1, t, WIDTH), lambda b: (b, 0, 0))
    new_t = pl.BlockSpec((1, WIDTH, t), lambda b: (b, 0, 0))
    old_t = pl.BlockSpec((1, WIDTH, past), lambda b: (b, 0, 0))
    return pl.pallas_call(
        _band_attn_sample_kernel,
        grid=(bsz,),
        in_specs=[new, old_t, old_t, new, new_t, _resident(bias.shape, lambda b: (0, 0, 0))],
        out_specs=new,
        out_shape=jax.ShapeDtypeStruct(q.shape, BF16),
        compiler_params=_params("parallel"),
        name="band_attn_sample",
    )(q, k_cache_t, v_cache_t, k_new, v_new_t, bias)


def _fox_attn_sample_kernel(q_ref, kct_ref, vct_ref, kn_ref, vnt_ref, ccol_ref, crow_ref, o_ref):
    t = q_ref.shape[1]
    past = kct_ref.shape[2]
    row = lax.broadcasted_iota(jnp.int32, (t, t), 0)
    col = lax.broadcasted_iota(jnp.int32, (t, t), 1)
    causal = col <= row

    def scores(h):
        sl = slice(h * HEAD_DIM, (h + 1) * HEAD_DIM)
        q = q_ref[0, :, sl]
        cq = ccol_ref[0, :, h:h + 1]
        s_past = _dot(q, kct_ref[0, sl, :].astype(BF16)) + (cq - crow_ref[0, h:h + 1, 0:past]) * LOG2E
        s_new = _dot_nt(q, kn_ref[0, :, sl]) + (cq - crow_ref[0, h:h + 1, past:past + t]) * LOG2E
        return s_past, jnp.where(causal, s_new, NEG)

    def finish(h, s):
        sl = slice(h * HEAD_DIM, (h + 1) * HEAD_DIM)
        o_ref[0, :, sl] = _sample_attention(*s, vct_ref[0, sl, :].astype(BF16), vnt_ref[0, sl, :])

    _pipelined_heads(scores, finish)


def _fox_attn_sample(q, k_cache_t, v_cache_t, k_new, v_new_t, ccol, crow):
    bsz, t, _ = q.shape
    past = k_cache_t.shape[2]
    assert past % t == 0
    new = pl.BlockSpec((1, t, WIDTH), lambda b: (b, 0, 0))
    new_t = pl.BlockSpec((1, WIDTH, t), lambda b: (b, 0, 0))
    old_t = pl.BlockSpec((1, WIDTH, past), lambda b: (b, 0, 0))
    return pl.pallas_call(
        _fox_attn_sample_kernel,
        grid=(bsz,),
        in_specs=[new, old_t, old_t, new, new_t,
                  pl.BlockSpec((1, t, HEADS), lambda b: (b, 0, 0)),
                  pl.BlockSpec((1, HEADS, past + t), lambda b: (b, 0, 0))],
        out_specs=new,
        out_shape=jax.ShapeDtypeStruct(q.shape, BF16),
        compiler_params=_params("parallel"),
        name="fox_attn_sample",
    )(q, k_cache_t, v_cache_t, k_new, v_new_t, ccol, crow)


def _postmix_kernel(x_ref, oa_ref, ob_ref, gpre_ref, gpost_ref, wt_ref, wpa_ref, wpb_ref, wout_ref, y_ref,
                    *, channel_major):
    x = x_ref[...]
    d = x.shape[-1]
    h = _rmsnorm(x, gpre_ref[...]).astype(BF16)
    gates = wt_ref.shape[0] - 2 * d
    gate_a = jax.nn.sigmoid(_dot_nt(h, wt_ref[gates:gates + d, :].astype(BF16)))
    gate_b = jax.nn.sigmoid(_dot_nt(h, wt_ref[gates + d:, :].astype(BF16)))
    if channel_major:
        proj = lambda o_ref, w_ref: lax.dot_general(o_ref[0], w_ref[...], (((0,), (0,)), ((), ())),
                                                    preferred_element_type=F32)
    else:
        proj = lambda o_ref, w_ref: _dot(o_ref[...], w_ref[...])
    merged = gate_a * proj(oa_ref, wpa_ref) + gate_b * proj(ob_ref, wpb_ref)
    y = x + _rmsnorm(_dot(merged.astype(BF16), wout_ref[...]), gpost_ref[...])
    y_ref[...] = _interleave(y)


def _interleave(rows):
    n, d = rows.shape
    return jnp.swapaxes(rows.reshape(SUBLANES, n // SUBLANES, d), 0, 1)


def _deinterleave(planes):
    p, s, d = planes.shape
    return jnp.swapaxes(planes, 0, 1).reshape(s * p, d)


def _postmix(x, oa, ob, g_pre, g_post, w_in_t, w_pa, w_pb, w_out):
    n, d = x.shape
    tm = min(ROW_TILE, n)
    row = lambda i: (i, 0)
    fixed = lambda i: (0, 0)
    channel_major = oa.ndim == 3
    if channel_major:
        tiles_per_seq = oa.shape[2] // tm
        o_spec = pl.BlockSpec((1, WIDTH, tm), lambda i: (i // tiles_per_seq, 0, i % tiles_per_seq))
    else:
        o_spec = pl.BlockSpec((tm, WIDTH), row)
    return pl.pallas_call(
        functools.partial(_postmix_kernel, channel_major=channel_major),
        grid=(n // tm,),
        in_specs=[pl.BlockSpec((tm, d), row),
                  o_spec,
                  o_spec,
                  pl.BlockSpec((1, d), fixed),
                  pl.BlockSpec((1, d), fixed),
                  _resident(w_in_t.shape, fixed),
                  _resident(w_pa.shape, fixed),
                  _resident(w_pb.shape, fixed),
                  _resident(w_out.shape, fixed)],
        out_specs=pl.BlockSpec((tm // SUBLANES, SUBLANES, d), lambda i: (i, 0, 0)),
        out_shape=jax.ShapeDtypeStruct((n // SUBLANES, SUBLANES, d), F32),
        compiler_params=_params("parallel"),
        name="postmix",
    )(x, oa, ob, g_pre, g_post, w_in_t, w_pa, w_pb, w_out)


def _ffn_kernel(x_ref, st_ref, gpre_ref, gpost_ref, wup_ref, cw_ref, cb_ref, wdn_ref,
                y_ref, nst_ref, hist_ref, ext_ref, h_ref, f_ref, *, nseg):
    @pl.when(pl.program_id(1) == 0)
    def _():
        hist_ref[...] = st_ref[...]

    planes, _, d = x_ref.shape
    tm = planes * SUBLANES
    d_ff = wdn_ref.shape[0]
    h_ref[...] = _rmsnorm(x_ref[...].reshape(tm, d), gpre_ref[...]).astype(BF16)
    n_chunks = d_ff // FFN_COLS
    first_sublane = lax.broadcasted_iota(jnp.int32, (SUBLANES, FFN_COLS), 0) == 0

    def parts(c):
        for part in range(2):
            yield (slice(part * d_ff + c * FFN_COLS, part * d_ff + (c + 1) * FFN_COLS),
                   slice(part * FFN_COLS, (part + 1) * FFN_COLS))

    def up(c):
        ext = ext_ref.at[c % 2]
        for cols, dst in parts(c):
            u = _dot(h_ref[...], wup_ref[:, cols]).reshape(planes, SUBLANES, FFN_COLS)
            ext[CONV_W - 1:, :, dst] = u
            for k in range(CONV_W - 1):
                last = u[planes - (CONV_W - 1) + k]
                if nseg == 1:
                    ext[k, :, dst] = jnp.where(first_sublane, hist_ref[0, k:k + 1, cols], pltpu.roll(last, 1, 0))
                    hist_ref[0, k:k + 1, cols] = last[SUBLANES - 1:, :]
                else:
                    ext[k, :, dst] = hist_ref[:, k, cols]
                    hist_ref[:, k, cols] = last

    def down(c):
        ext = ext_ref.at[c % 2]
        halves = []
        for cols, dst in parts(c):
            y = cb_ref[:, cols]
            for tap in range(CONV_W):
                y = y + ext[tap:tap + planes, :, dst] * cw_ref[tap:tap + 1, cols]
            halves.append(y)
        act = (jax.nn.gelu(halves[0]) * halves[1]).reshape(tm, FFN_COLS)
        f_ref[:, c * FFN_COLS:(c + 1) * FFN_COLS] = act.astype(BF16)

    up(0)
    for c in range(n_chunks):
        if c + 1 < n_chunks:
            up(c + 1)
        down(c)
    f = _dot(f_ref[...], wdn_ref[...])
    y = x_ref[...].reshape(tm, d) + _rmsnorm(f, gpost_ref[...])
    y_ref[...] = _deinterleave(y.reshape(planes, SUBLANES, d))
    nst_ref[...] = hist_ref[...]


def _ffn(x, bsz, s, state, g_pre, g_post, w_up, conv_w, conv_b, w_down):
    d = x.shape[-1]
    up = w_up.shape[1]
    tm = min(ROW_TILE, bsz * s)
    planes = tm // SUBLANES
    if s >= tm:
        nseg, tiles = 1, s // tm
    else:
        nseg, tiles = tm // s, 1
        assert nseg == SUBLANES
    outer = bsz // nseg
    row = lambda o, t: (o * tiles + t, 0)
    fixed = lambda o, t: (0, 0)
    st_spec = pl.BlockSpec((nseg, CONV_W - 1, up), lambda o, t: (o, 0, 0))
    y, new_state = pl.pallas_call(
        functools.partial(_ffn_kernel, nseg=nseg),
        grid=(outer, tiles),
        in_specs=[pl.BlockSpec((planes, SUBLANES, d), lambda o, t: (o * tiles + t, 0, 0)),
                  st_spec,
                  pl.BlockSpec((1, d), fixed),
                  pl.BlockSpec((1, d), fixed),
                  _resident(w_up.shape, fixed),
                  pl.BlockSpec(conv_w.shape, fixed),
                  pl.BlockSpec((1, up), fixed),
                  _resident(w_down.shape, fixed)],
        out_specs=[pl.BlockSpec((tm, d), row), st_spec],
        out_shape=[jax.ShapeDtypeStruct((bsz * s, d), F32),
                   jax.ShapeDtypeStruct(state.shape, F32)],
        scratch_shapes=[pltpu.VMEM((nseg, CONV_W - 1, up), F32),
                        pltpu.VMEM((2, CONV_W - 1 + planes, SUBLANES, 2 * FFN_COLS), F32),
                        pltpu.VMEM((tm, d), BF16),
                        pltpu.VMEM((tm, w_down.shape[0]), BF16)],
        compiler_params=_params("arbitrary", "arbitrary"),
        name="conv_ffn",
    )(x, state, g_pre, g_post, w_up, conv_w, conv_b, w_down)
    return y.reshape(bsz, s, d), new_state


def _layer(x, caches, conv_state, bias, w):
    bsz, s, d = x.shape
    n = bsz * s
    x2 = x.reshape(n, d)
    keep = min(BAND, s)
    (qa, qb, ka16, kb16, vat16, vbt16, kat, vat, kbt, vbt, lft) = _inproj(
        x2, w["g_pre_mix"], w["w_in_t"], w["b_f"], seq_len=s, band_keep=keep)
    seq = lambda a: a.reshape(bsz, s, a.shape[-1])
    if caches is None:
        ccol, _ = _cumsum(lft)
        oa = _band_attn_prompt(seq(qa), seq(ka16), vat16, bias[1])
        ob = _fox_attn_prompt(seq(qb), seq(kb16), vbt16, ccol)
    else:
        ckat, cvat, ckbt, cvbt, clft = caches
        ccol, crow = _cumsum(clft, lft, col_from=clft.shape[2])
        oa = _band_attn_sample(seq(qa), ckat, cvat, seq(ka16), vat16, bias[0]).reshape(n, WIDTH)
        ob = _fox_attn_sample(seq(qb), ckbt, cvbt, seq(kb16), vbt16, ccol, crow).reshape(n, WIDTH)
    x1 = _postmix(x2, oa, ob, w["g_pre_mix"], w["g_post_mix"], w["w_in_t"], w["w_pa"], w["w_pb"], w["w_out"])
    y, new_conv = _ffn(x1, bsz, s, conv_state, w["g_pre_ffn"], w["g_post_ffn"],
                       w["w_up"], w["conv_w"], w["conv_b"], w["w_down"])
    heads = lambda a: a.reshape(bsz, HEADS, HEAD_DIM, a.shape[-1]).transpose(0, 3, 1, 2)
    return y, (heads(kat), heads(vat), heads(kbt), heads(vbt), lft.transpose(0, 2, 1), new_conv)


def _channel_major(cache):
    bsz, past = cache.shape[:2]
    return cache.transpose(0, 2, 3, 1).reshape(bsz, WIDTH, past)


def kernel(x_prompt, x_sample, cache_k_a, cache_v_a, cache_k_b, cache_v_b, cache_logf_b, state_conv_ffn,
           g_pre_mix, g_post_mix, g_pre_ffn, g_post_ffn, w_in, b_f, rel_table, w_proj_a, w_proj_b, w_out,
           w_up, conv_w, conv_b, w_down):
    depth = w_in.shape[0]
    up = w_up.shape[-1]
    x_p, x_s = x_prompt, x_sample
    p_states, s_states = [], []
    for l in range(depth):
        w = {
            "g_pre_mix": g_pre_mix[l][None], "g_post_mix": g_post_mix[l][None],
            "g_pre_ffn": g_pre_ffn[l][None], "g_post_ffn": g_post_ffn[l][None],
            "w_in_t": w_in[l].T,
            "b_f": b_f[l][:, None],
            "w_pa": w_proj_a[l].astype(BF16), "w_pb": w_proj_b[l].astype(BF16),
            "w_out": w_out[l].astype(BF16), "w_up": w_up[l].astype(BF16),
            "conv_w": conv_w[l], "conv_b": conv_b[l][None], "w_down": w_down[l].astype(BF16),
        }
        bias = _band_bias(rel_table[l])
        zero_state = jnp.zeros((x_p.shape[0], CONV_W - 1, up), F32)
        x_p, (ka, va, kb, vb, lf, cv) = _layer(x_p, None, zero_state, bias, w)
        p_states.append((ka, va, kb, vb, lf, cv))
        caches = (_channel_major(cache_k_a[l]), _channel_major(cache_v_a[l]), _channel_major(cache_k_b[l]),
                  _channel_major(cache_v_b[l]), cache_logf_b[l].transpose(0, 2, 1))
        x_s, st = _layer(x_s, caches, state_conv_ffn[l], bias, w)
        s_states.append(st)
    stack = lambda states: [jnp.stack(s) for s in zip(*states)]
    return (x_p, x_s, *stack(p_states), *stack(s_states))
```

```python
import functools

import jax
import jax.numpy as jnp
from jax import lax
from jax.experimental import pallas as pl
from jax.experimental.pallas import tpu as pltpu

F32, BF16 = jnp.float32, jnp.bfloat16

HEADS = 8
HEAD_DIM = 64
WIDTH = HEADS * HEAD_DIM
CHUNK = 64
PAST_CHUNKS = 8
BAND = PAST_CHUNKS * CHUNK
REL_CLIP = 128
REL_SIZE = 2 * REL_CLIP + 1
CONV_W = 3
EPS = 1e-6
NEG = -1e30
LOG2E = 1.4426950408889634

SUBLANES = 8
ATT_BLK = 256
BAND_KEYS = 3 * ATT_BLK
BAND_BLOCKS = BAND_KEYS // ATT_BLK
PAD_DIM = 2 * HEAD_DIM
SUM_ROWS = 16
HEADS_AHEAD = 2
ROW_TILE = 512
FFN_COLS = 256
FFN_TILES_PER_STEP = 2
CUMSUM_BLK = 256
VMEM_LIMIT = 56 * 1024 * 1024


def _params(*sem):
    return pltpu.CompilerParams(dimension_semantics=sem, vmem_limit_bytes=VMEM_LIMIT)


def _resident(shape, index_map):
    return pl.BlockSpec(shape, index_map, pipeline_mode=pl.Buffered(1))


def _rmsnorm(x, g):
    return x * lax.rsqrt(jnp.mean(x * x, axis=-1, keepdims=True) + EPS) * g


def _dot(a, b):
    return jnp.dot(a, b, preferred_element_type=F32)


def _dot_nt(a, b):
    return lax.dot_general(a, b, (((1,), (1,)), ((), ())), preferred_element_type=F32)


def _split3(x):
    hi = x.astype(BF16)
    r = x - hi.astype(F32)
    mid = r.astype(BF16)
    lo = (r - mid.astype(F32)).astype(BF16)
    return hi, mid, lo


def _inproj_kernel(x_ref, g_ref, w_ref, wft_ref, bf_ref,
                   qa_ref, qb_ref, ka_ref, kb_ref, vat16_ref, vbt16_ref,
                   kat_ref, vat_ref, kbt_ref, vbt_ref, lft_ref):
    h = _rmsnorm(x_ref[...], g_ref[...]).astype(BF16)
    scale = HEAD_DIM ** -0.5

    def proj(c):
        return _dot_nt(h, w_ref[c * WIDTH:(c + 1) * WIDTH, :].astype(BF16))

    def put(ref, zt):
        per_seq = zt.shape[1] // ref.shape[0]
        for sq in range(ref.shape[0]):
            ref[sq] = zt[:, sq * per_seq:(sq + 1) * per_seq]

    qa_ref[...] = (proj(0) * (scale * LOG2E)).astype(BF16)
    qb_ref[...] = (proj(3) * (scale * LOG2E)).astype(BF16)
    for c, row_ref, t_ref in ((1, ka_ref, kat_ref), (4, kb_ref, kbt_ref)):
        z = proj(c)
        row_ref[...] = z.astype(BF16)
        put(t_ref, z.T)
    for c, t16_ref, t_ref in ((2, vat16_ref, vat_ref), (5, vbt16_ref, vbt_ref)):
        zt = proj(c).T
        put(t_ref, zt)
        put(t16_ref, zt.astype(BF16))
    put(lft_ref, jax.nn.log_sigmoid(_dot_nt(wft_ref[...].astype(BF16), h) + bf_ref[...]))


def _inproj(x, g, w_in_t, b_f, seq_len, band_keep):
    n, d = x.shape
    bsz = n // seq_len
    tm = min(ROW_TILE, n)
    tiles_per_seq = max(seq_len // tm, 1)
    seqs_per_tile = max(tm // seq_len, 1)
    cols = tm // seqs_per_tile
    assert band_keep == cols
    row = lambda i: (i, 0)
    fixed = lambda i: (0, 0)
    along = lambda i: (i // tiles_per_seq, 0, i % tiles_per_seq)
    kept = lambda i: (i // tiles_per_seq, 0, 0)
    rows16 = (jax.ShapeDtypeStruct((n, WIDTH), BF16), pl.BlockSpec((tm, WIDTH), row))

    def chan(channels, length, dtype, index_map):
        return (jax.ShapeDtypeStruct((bsz, channels, length), dtype),
                pl.BlockSpec((seqs_per_tile, channels, cols), index_map))

    outs = [rows16] * 4
    outs += [chan(WIDTH, seq_len, BF16, along)] * 2
    outs += [chan(WIDTH, band_keep, F32, kept)] * 2
    outs += [chan(WIDTH, seq_len, F32, along)] * 2
    outs += [chan(HEADS, seq_len, F32, along)]
    return pl.pallas_call(
        _inproj_kernel,
        grid=(n // tm,),
        in_specs=[pl.BlockSpec((tm, d), row),
                  pl.BlockSpec((1, d), fixed),
                  _resident((6 * WIDTH, d), fixed),
                  pl.BlockSpec((HEADS, d), lambda i: (6 * WIDTH // HEADS, 0)),
                  pl.BlockSpec((HEADS, 1), fixed)],
        out_specs=[spec for _, spec in outs],
        out_shape=[shape for shape, _ in outs],
        compiler_params=_params("arbitrary"),
        name="inproj",
    )(x, g, w_in_t, w_in_t, b_f)


def _cumsum_kernel(*refs, seg_lens, col_from):
    seg_refs = refs[:len(seg_lens)]
    ccol_ref, crow_ref = refs[len(seg_lens):]
    carry = jnp.zeros((crow_ref.shape[0], 1), F32)
    off = 0
    for ref, n in zip(seg_refs, seg_lens):
        for o in range(0, n, CUMSUM_BLK):
            b = min(CUMSUM_BLK, n - o)
            r = lax.broadcasted_iota(jnp.int32, (b, b), 0)
            c = lax.broadcasted_iota(jnp.int32, (b, b), 1)
            upper = jnp.where(r <= c, 1.0, 0.0).astype(BF16)
            sums = carry
            for p in _split3(ref[:, o:o + b]):
                sums = sums + _dot(p, upper)
            crow_ref[:, off + o:off + o + b] = sums
            carry = sums[:, b - 1:b]
            if off + o >= col_from:
                by_position = sums.T
                for bb in range(ccol_ref.shape[0]):
                    ccol_ref[bb, off + o - col_from:off + o - col_from + b, :] = (
                        by_position[:, bb * HEADS:(bb + 1) * HEADS])
        off += n


def _cumsum(*segs, col_from=0):
    bsz = segs[0].shape[0]
    seg_lens = tuple(s.shape[2] for s in segs)
    total = sum(seg_lens)
    assert col_from % CUMSUM_BLK == 0 or col_from in (0, seg_lens[0])
    ccol, crow = pl.pallas_call(
        functools.partial(_cumsum_kernel, seg_lens=seg_lens, col_from=col_from),
        out_shape=[jax.ShapeDtypeStruct((bsz, total - col_from, HEADS), F32),
                   jax.ShapeDtypeStruct((bsz * HEADS, total), F32)],
        compiler_params=pltpu.CompilerParams(vmem_limit_bytes=VMEM_LIMIT),
        name="cumsum_logf",
    )(*[s.reshape(bsz * HEADS, s.shape[2]) for s in segs])
    return ccol, crow.reshape(bsz, HEADS, total)


REL_PAD = 384
DIST_SPAN = 1024


def _band_bias_kernel(tbl_ref, qk_ref, kq_ref):
    j = lax.broadcasted_iota(jnp.int32, (REL_PAD, DIST_SPAN), 1)
    r = lax.broadcasted_iota(jnp.int32, (REL_PAD, DIST_SPAN), 0)
    parts = _split3(tbl_ref[...])

    def by_offset(dist):
        onehot = jnp.where(jnp.clip(dist, -REL_CLIP, REL_CLIP) + REL_CLIP == r, 1.0, 0.0).astype(BF16)
        return sum(_dot(p, onehot) for p in parts) * LOG2E

    def skewed(e_row, rows, cols, shift):
        wide = jnp.broadcast_to(e_row, (rows, DIST_SPAN))
        return pltpu.roll(wide, shift % DIST_SPAN, 1, stride=1, stride_axis=0)[:, :cols]

    e_qk = by_offset((BAND_KEYS - 1) - j)
    e_kq = by_offset(j - (ATT_BLK - 1))
    qc = lax.broadcasted_iota(jnp.int32, (ATT_BLK, BAND_KEYS), 0) // CHUNK
    kc = lax.broadcasted_iota(jnp.int32, (ATT_BLK, BAND_KEYS), 1) // CHUNK
    vis_qk = (kc >= qc) & (kc <= qc + PAST_CHUNKS)
    kc = lax.broadcasted_iota(jnp.int32, (BAND_KEYS, ATT_BLK), 0) // CHUNK
    qc = lax.broadcasted_iota(jnp.int32, (BAND_KEYS, ATT_BLK), 1) // CHUNK
    vis_kq = (kc >= qc) & (kc <= qc + PAST_CHUNKS)
    for h in range(HEADS):
        qk_ref[h] = jnp.where(vis_qk, skewed(e_qk[h:h + 1, :], ATT_BLK, BAND_KEYS, -(ATT_BLK - 1)), NEG)
        kq_ref[h] = jnp.where(vis_kq, skewed(e_kq[h:h + 1, :], BAND_KEYS, ATT_BLK, -(BAND_KEYS - 1)), NEG)


def _band_bias(rel_table):
    tbl = jnp.pad(rel_table, ((0, 0), (0, REL_PAD - REL_SIZE)))
    return pl.pallas_call(
        _band_bias_kernel,
        out_shape=[jax.ShapeDtypeStruct((HEADS, ATT_BLK, BAND_KEYS), F32),
                   jax.ShapeDtypeStruct((HEADS, BAND_KEYS, ATT_BLK), F32)],
        compiler_params=pltpu.CompilerParams(vmem_limit_bytes=VMEM_LIMIT),
        name="band_bias",
    )(tbl)


def _band_attn_prompt_kernel(q_ref, k_ref, vt_ref, bias_ref, o_ref, s_ref):
    i = pl.program_id(1)
    ones = jnp.ones((SUM_ROWS, ATT_BLK), BF16)

    def run(clipped):
        blocks = []
        for g in range(BAND_BLOCKS):
            j = i - (BAND_BLOCKS - 1) + g
            start = pl.multiple_of(jnp.maximum(j, 0) * ATT_BLK, ATT_BLK)
            masks = clipped and g < BAND_BLOCKS - 1
            blocks.append((start, jnp.where(j >= 0, 0.0, NEG) if masks else None))

        def scores(h):
            sl = slice(h * HEAD_DIM, (h + 1) * HEAD_DIM)
            q = q_ref[0, :, sl]
            for g, (start, penalty) in enumerate(blocks):
                s = _dot_nt(k_ref[0, pl.ds(start, ATT_BLK), sl], q) + bias_ref[h, g * ATT_BLK:(g + 1) * ATT_BLK, :]
                s_ref[h % (HEADS_AHEAD + 1), g] = s if penalty is None else s + penalty

        def finish(h):
            sl = slice(h * HEAD_DIM, (h + 1) * HEAD_DIM)
            s = [s_ref[h % (HEADS_AHEAD + 1), g] for g in range(BAND_BLOCKS)]
            m = functools.reduce(jnp.maximum, [jnp.max(x, axis=0, keepdims=True) for x in s])
            acc = 0.0
            for x, (start, _) in zip(s, blocks):
                vt = jnp.concatenate([vt_ref[0, sl, pl.ds(start, ATT_BLK)], ones], axis=0)
                acc = acc + _dot(vt, jnp.exp2(x - m).astype(BF16))
            o_ref[0, sl, :] = (acc[:HEAD_DIM] / acc[HEAD_DIM:HEAD_DIM + 1]).astype(BF16)

        for h in range(HEADS + HEADS_AHEAD):
            if h < HEADS:
                scores(h)
            if h >= HEADS_AHEAD:
                finish(h - HEADS_AHEAD)

    pl.when(i >= BAND_BLOCKS - 1)(lambda: run(clipped=False))
    pl.when(i < BAND_BLOCKS - 1)(lambda: run(clipped=True))


def _band_attn_prompt(q, k, vt, bias_kq):
    bsz, s, _ = q.shape
    qblk = pl.BlockSpec((1, ATT_BLK, WIDTH), lambda b, i: (b, i, 0))
    return pl.pallas_call(
        _band_attn_prompt_kernel,
        grid=(bsz, s // ATT_BLK),
        in_specs=[qblk,
                  pl.BlockSpec((1, s, WIDTH), lambda b, i: (b, 0, 0)),
                  pl.BlockSpec((1, WIDTH, s), lambda b, i: (b, 0, 0)),
                  _resident(bias_kq.shape, lambda b, i: (0, 0, 0))],
        out_specs=pl.BlockSpec((1, WIDTH, ATT_BLK), lambda b, i: (b, 0, i)),
        out_shape=jax.ShapeDtypeStruct((bsz, WIDTH, s), BF16),
        scratch_shapes=[pltpu.VMEM((HEADS_AHEAD + 1, BAND_BLOCKS, ATT_BLK, ATT_BLK), F32)],
        compiler_params=_params("parallel", "parallel"),
        name="band_attn_prompt",
    )(q, k, vt, bias_kq)


def _decay_lanes(c, key_side):
    terms = jnp.concatenate(_split3(c), axis=1)
    row = lax.broadcasted_iota(jnp.int32, (3 * HEADS, WIDTH), 0)
    lane = lax.broadcasted_iota(jnp.int32, (3 * HEADS, WIDTH), 1)
    first_one, first_term, sign = (3, 0, -1.0) if key_side else (0, 3, 1.0)
    place = jnp.where(lane == (row % HEADS) * HEAD_DIM + first_term + row // HEADS, sign, 0.0).astype(BF16)
    in_group = lax.broadcasted_iota(jnp.int32, (1, WIDTH), 1) % HEAD_DIM
    ones = jnp.where((in_group >= first_one) & (in_group < first_one + 3), 1.0, 0.0)
    return (ones + _dot(terms, place)).astype(BF16)


def _fox_attn_prompt_kernel(q_ref, k_ref, vt_ref, ccol_ref, o_ref,
                            kp_ref, qp_ref, s_ref, m_ref, acc_ref):
    i = pl.program_id(1)
    qstart = pl.multiple_of(i * ATT_BLK, ATT_BLK)

    @pl.when(i == 0)
    def _():
        extra = _decay_lanes(ccol_ref[0] * LOG2E, key_side=True)
        for h in range(HEADS):
            sl = slice(h * HEAD_DIM, (h + 1) * HEAD_DIM)
            kp_ref[:, h * PAD_DIM:(h + 1) * PAD_DIM] = jnp.concatenate([k_ref[0, :, sl], extra[:, sl]], axis=1)

    extra = _decay_lanes(ccol_ref[0, pl.ds(qstart, ATT_BLK), :] * LOG2E, key_side=False)
    for h in range(HEADS):
        sl = slice(h * HEAD_DIM, (h + 1) * HEAD_DIM)
        qp_ref[:, h * PAD_DIM:(h + 1) * PAD_DIM] = jnp.concatenate([q_ref[0, :, sl], extra[:, sl]], axis=1)

    key = lax.broadcasted_iota(jnp.int32, (ATT_BLK, ATT_BLK), 0)
    qry = lax.broadcasted_iota(jnp.int32, (ATT_BLK, ATT_BLK), 1)
    causal = key <= qry
    ones = jnp.ones((SUM_ROWS, ATT_BLK), BF16)
    m_ref[...] = jnp.full(m_ref.shape, NEG, F32)
    acc_ref[...] = jnp.zeros(acc_ref.shape, F32)

    def scores(j, h):
        start = pl.multiple_of(j * ATT_BLK, ATT_BLK)
        pad = slice(h * PAD_DIM, (h + 1) * PAD_DIM)
        return _dot_nt(kp_ref[pl.ds(start, ATT_BLK), pad], qp_ref[:, pad])

    def update(j, h, s, masked):
        start = pl.multiple_of(j * ATT_BLK, ATT_BLK)
        if masked:
            s = jnp.where(causal, s, NEG)
        m_old = m_ref[h]
        m_new = jnp.maximum(m_old, jnp.max(s, axis=0, keepdims=True))
        alpha = jnp.exp2(m_old - m_new)
        p = jnp.exp2(s - m_new)
        vt = vt_ref[0, h * HEAD_DIM:(h + 1) * HEAD_DIM, pl.ds(start, ATT_BLK)]
        acc_ref[h] = alpha * acc_ref[h] + _dot(jnp.concatenate([vt, ones], axis=0), p.astype(BF16))
        m_ref[h] = m_new

    def step(j, cur, masked=False, prefetch=True):
        for h in range(HEADS + HEADS_AHEAD):
            if prefetch and h < HEADS:
                s_ref[1 - cur, h] = scores(j + 1, h)
            if h >= HEADS_AHEAD:
                update(j, h - HEADS_AHEAD, s_ref[cur, h - HEADS_AHEAD], masked)

    for h in range(HEADS):
        s_ref[0, h] = scores(0, h)

    def pair(t, carry):
        step(2 * t, 0)
        step(2 * t + 1, 1)
        return carry

    lax.fori_loop(0, i // 2, pair, 0)

    @pl.when(i % 2 == 1)
    def _():
        step(i - 1, 0)
        step(i, 1, masked=True, prefetch=False)

    @pl.when(i % 2 == 0)
    def _():
        step(i, 0, masked=True, prefetch=False)

    for h in range(HEADS):
        sl = slice(h * HEAD_DIM, (h + 1) * HEAD_DIM)
        out = acc_ref[h, :HEAD_DIM, :] / acc_ref[h, HEAD_DIM:HEAD_DIM + 1, :]
        o_ref[0, sl, :] = out.astype(BF16)


def _fox_attn_prompt(q, k, vt, ccol):
    bsz, s, _ = q.shape
    qblk = pl.BlockSpec((1, ATT_BLK, WIDTH), lambda b, i: (b, i, 0))
    return pl.pallas_call(
        _fox_attn_prompt_kernel,
        grid=(bsz, s // ATT_BLK),
        in_specs=[qblk,
                  pl.BlockSpec((1, s, WIDTH), lambda b, i: (b, 0, 0)),
                  pl.BlockSpec((1, WIDTH, s), lambda b, i: (b, 0, 0)),
                  pl.BlockSpec((1, s, HEADS), lambda b, i: (b, 0, 0))],
        out_specs=pl.BlockSpec((1, WIDTH, ATT_BLK), lambda b, i: (b, 0, i)),
        out_shape=jax.ShapeDtypeStruct((bsz, WIDTH, s), BF16),
        scratch_shapes=[pltpu.VMEM((s, HEADS * PAD_DIM), BF16),
                        pltpu.VMEM((ATT_BLK, HEADS * PAD_DIM), BF16),
                        pltpu.VMEM((2, HEADS, ATT_BLK, ATT_BLK), F32),
                        pltpu.VMEM((HEADS, 1, ATT_BLK), F32),
                        pltpu.VMEM((HEADS, HEAD_DIM + SUM_ROWS, ATT_BLK), F32)],
        compiler_params=_params("parallel", "arbitrary"),
        name="fox_attn_prompt",
    )(q, k, vt, ccol)


def _pipelined_heads(scores, finish):
    pending = []
    for h in range(HEADS + HEADS_AHEAD):
        if h < HEADS:
            pending.append(scores(h))
        if h >= HEADS_AHEAD:
            finish(h - HEADS_AHEAD, pending.pop(0))


def _sample_attention(s_past, s_new, vct, vnt):
    m = jnp.maximum(jnp.max(s_past, axis=-1, keepdims=True), jnp.max(s_new, axis=-1, keepdims=True))
    p_past = jnp.exp2(s_past - m)
    p_new = jnp.exp2(s_new - m)
    l = jnp.sum(p_past, axis=-1, keepdims=True) + jnp.sum(p_new, axis=-1, keepdims=True)
    acc = _dot_nt(p_past.astype(BF16), vct) + _dot_nt(p_new.astype(BF16), vnt)
    return (acc / l).astype(BF16)


def _band_attn_sample_kernel(q_ref, kct_ref, vct_ref, kn_ref, vnt_ref, bias_ref, o_ref):
    t = q_ref.shape[1]
    past = kct_ref.shape[2]

    def scores(h):
        sl = slice(h * HEAD_DIM, (h + 1) * HEAD_DIM)
        q = q_ref[0, :, sl]
        s_past = _dot(q, kct_ref[0, sl, :].astype(BF16)) + bias_ref[h, 0:t, BAND - past:BAND]
        s_new = _dot_nt(q, kn_ref[0, :, sl]) + bias_ref[h, 0:t, BAND:BAND + t]
        return s_past, s_new

    def finish(h, s):
        sl = slice(h * HEAD_DIM, (h + 1) * HEAD_DIM)
        o_ref[0, :, sl] = _sample_attention(*s, vct_ref[0, sl, :].astype(BF16), vnt_ref[0, sl, :])

    _pipelined_heads(scores, finish)


def _band_attn_sample(q, k_cache_t, v_cache_t, k_new, v_new_t, bias):
    bsz, t, _ = q.shape
    past = k_cache_t.shape[2]
    new = pl.BlockSpec((1, t, WIDTH), lambda b: (b, 0, 0))
    new_t = pl.BlockSpec((1, WIDTH, t), lambda b: (b, 0, 0))
    old_t = pl.BlockSpec((1, WIDTH, past), lambda b: (b, 0, 0))
    return pl.pallas_call(
        _band_attn_sample_kernel,
        grid=(bsz,),
        in_specs=[new, old_t, old_t, new, new_t, _resident(bias.shape, lambda b: (0, 0, 0))],
        out_specs=new,
        out_shape=jax.ShapeDtypeStruct(q.shape, BF16),
        compiler_params=_params("parallel"),
        name="band_attn_sample",
    )(q, k_cache_t, v_cache_t, k_new, v_new_t, bias)


def _fox_attn_sample_kernel(q_ref, kct_ref, vct_ref, kn_ref, vnt_ref, ccol_ref, crow_ref, o_ref):
    t = q_ref.shape[1]
    past = kct_ref.shape[2]
    row = lax.broadcasted_iota(jnp.int32, (t, t), 0)
    col = lax.broadcasted_iota(jnp.int32, (t, t), 1)
    causal = col <= row

    def scores(h):
        sl = slice(h * HEAD_DIM, (h + 1) * HEAD_DIM)
        q = q_ref[0, :, sl]
        cq = ccol_ref[0, :, h:h + 1]
        s_past = _dot(q, kct_ref[0, sl, :].astype(BF16)) + (cq - crow_ref[0, h:h + 1, 0:past]) * LOG2E
        s_new = _dot_nt(q, kn_ref[0, :, sl]) + (cq - crow_ref[0, h:h + 1, past:past + t]) * LOG2E
        return s_past, jnp.where(causal, s_new, NEG)

    def finish(h, s):
        sl = slice(h * HEAD_DIM, (h + 1) * HEAD_DIM)
        o_ref[0, :, sl] = _sample_attention(*s, vct_ref[0, sl, :].astype(BF16), vnt_ref[0, sl, :])

    _pipelined_heads(scores, finish)


def _fox_attn_sample(q, k_cache_t, v_cache_t, k_new, v_new_t, ccol, crow):
    bsz, t, _ = q.shape
    past = k_cache_t.shape[2]
    assert past % t == 0
    new = pl.BlockSpec((1, t, WIDTH), lambda b: (b, 0, 0))
    new_t = pl.BlockSpec((1, WIDTH, t), lambda b: (b, 0, 0))
    old_t = pl.BlockSpec((1, WIDTH, past), lambda b: (b, 0, 0))
    return pl.pallas_call(
        _fox_attn_sample_kernel,
        grid=(bsz,),
        in_specs=[new, old_t, old_t, new, new_t,
                  pl.BlockSpec((1, t, HEADS), lambda b: (b, 0, 0)),
                  pl.BlockSpec((1, HEADS, past + t), lambda b: (b, 0, 0))],
        out_specs=new,
        out_shape=jax.ShapeDtypeStruct(q.shape, BF16),
        compiler_params=_params("parallel"),
        name="fox_attn_sample",
    )(q, k_cache_t, v_cache_t, k_new, v_new_t, ccol, crow)


def _postmix_kernel(x_ref, oa_ref, ob_ref, gpre_ref, gpost_ref, wt_ref, wpa_ref, wpb_ref, wout_ref, y_ref,
                    *, channel_major):
    tm, d = x_ref.shape
    planes = tm // SUBLANES
    gates = wt_ref.shape[0] - 2 * d
    halves = [slice(k * tm // 2, (k + 1) * tm // 2) for k in range(2)]

    def mix(rows):
        h = _rmsnorm(x_ref[rows, :], gpre_ref[...]).astype(BF16)
        gate_a = jax.nn.sigmoid(_dot_nt(h, wt_ref[gates:gates + d, :].astype(BF16)))
        gate_b = jax.nn.sigmoid(_dot_nt(h, wt_ref[gates + d:, :].astype(BF16)))
        if channel_major:
            proj = lambda o_ref, w_ref: lax.dot_general(o_ref[0, :, rows], w_ref[...], (((0,), (0,)), ((), ())),
                                                        preferred_element_type=F32)
        else:
            proj = lambda o_ref, w_ref: _dot(o_ref[rows, :], w_ref[...])
        return (gate_a * proj(oa_ref, wpa_ref) + gate_b * proj(ob_ref, wpb_ref)).astype(BF16)

    merged = [mix(rows) for rows in halves]
    outs = [_dot(m, wout_ref[...]) for m in merged]
    for k, rows in enumerate(halves):
        y = x_ref[rows, :] + _rmsnorm(outs[k], gpost_ref[...])
        subs = slice(k * SUBLANES // 2, (k + 1) * SUBLANES // 2)
        y_ref[:, subs, :] = jnp.swapaxes(y.reshape(SUBLANES // 2, planes, d), 0, 1)


def _interleave(rows):
    n, d = rows.shape
    return jnp.swapaxes(rows.reshape(SUBLANES, n // SUBLANES, d), 0, 1)


def _deinterleave(planes):
    p, s, d = planes.shape
    return jnp.swapaxes(planes, 0, 1).reshape(s * p, d)


def _postmix(x, oa, ob, g_pre, g_post, w_in_t, w_pa, w_pb, w_out):
    n, d = x.shape
    tm = min(ROW_TILE, n)
    row = lambda i: (i, 0)
    fixed = lambda i: (0, 0)
    channel_major = oa.ndim == 3
    if channel_major:
        tiles_per_seq = oa.shape[2] // tm
        o_spec = pl.BlockSpec((1, WIDTH, tm), lambda i: (i // tiles_per_seq, 0, i % tiles_per_seq))
    else:
        o_spec = pl.BlockSpec((tm, WIDTH), row)
    return pl.pallas_call(
        functools.partial(_postmix_kernel, channel_major=channel_major),
        grid=(n // tm,),
        in_specs=[pl.BlockSpec((tm, d), row),
                  o_spec,
                  o_spec,
                  pl.BlockSpec((1, d), fixed),
                  pl.BlockSpec((1, d), fixed),
                  _resident(w_in_t.shape, fixed),
                  _resident(w_pa.shape, fixed),
                  _resident(w_pb.shape, fixed),
                  _resident(w_out.shape, fixed)],
        out_specs=pl.BlockSpec((tm // SUBLANES, SUBLANES, d), lambda i: (i, 0, 0)),
        out_shape=jax.ShapeDtypeStruct((n // SUBLANES, SUBLANES, d), F32),
        compiler_params=_params("parallel"),
        name="postmix",
    )(x, oa, ob, g_pre, g_post, w_in_t, w_pa, w_pb, w_out)


def _ffn_kernel(x_ref, st_ref, gpre_ref, gpost_ref, wup_ref, cw_ref, cb_ref, wdn_ref,
                y_ref, nst_ref, hist_ref, ext_ref, h_ref, f_ref, *, nseg):
    @pl.when(pl.program_id(1) == 0)
    def _():
        hist_ref[...] = st_ref[...]

    n_tiles = h_ref.shape[0]
    planes = x_ref.shape[0] // n_tiles
    d = x_ref.shape[-1]
    tm = planes * SUBLANES
    d_ff = wdn_ref.shape[0]
    n_chunks = d_ff // FFN_COLS
    first_sublane = lax.broadcasted_iota(jnp.int32, (SUBLANES, FFN_COLS), 0) == 0

    def tile_x(t):
        return x_ref[t * planes:(t + 1) * planes].reshape(tm, d)

    def parts(c):
        for part in range(2):
            yield (slice(part * d_ff + c * FFN_COLS, part * d_ff + (c + 1) * FFN_COLS),
                   slice(part * FFN_COLS, (part + 1) * FFN_COLS))

    def up(t, c):
        ext = ext_ref.at[t, c % 2]
        seqs = slice(t * nseg, (t + 1) * nseg)
        for cols, dst in parts(c):
            u = _dot(h_ref[t], wup_ref[:, cols]).reshape(planes, SUBLANES, FFN_COLS)
            ext[CONV_W - 1:, :, dst] = u
            for k in range(CONV_W - 1):
                last = u[planes - (CONV_W - 1) + k]
                if nseg == 1:
                    ext[k, :, dst] = jnp.where(first_sublane, hist_ref[0, k:k + 1, cols], pltpu.roll(last, 1, 0))
                    hist_ref[0, k:k + 1, cols] = last[SUBLANES - 1:, :]
                else:
                    ext[k, :, dst] = hist_ref[seqs, k, cols]
                    hist_ref[seqs, k, cols] = last

    def activate(t, c):
        ext = ext_ref.at[t, c % 2]
        halves = []
        for cols, dst in parts(c):
            y = cb_ref[:, cols]
            for tap in range(CONV_W):
                y = y + ext[tap:tap + planes, :, dst] * cw_ref[tap:tap + 1, cols]
            halves.append(y)
        act = (jax.nn.gelu(halves[0]) * halves[1]).reshape(tm, FFN_COLS)
        f_ref[t, :, c * FFN_COLS:(c + 1) * FFN_COLS] = act.astype(BF16)

    for t in range(n_tiles):
        h_ref[t] = _rmsnorm(tile_x(t), gpre_ref[...]).astype(BF16)
    down = []
    for t in range(n_tiles):
        up(t, 0)
        for c in range(n_chunks):
            if c + 1 < n_chunks:
                up(t, c + 1)
            activate(t, c)
        down.append(_dot(f_ref[t], wdn_ref[...]))
    for t in range(n_tiles):
        y = tile_x(t) + _rmsnorm(down[t], gpost_ref[...])
        y_ref[t * tm:(t + 1) * tm, :] = _deinterleave(y.reshape(planes, SUBLANES, d))
    nst_ref[...] = hist_ref[...]


def _ffn(x, bsz, s, state, g_pre, g_post, w_up, conv_w, conv_b, w_down):
    d = x.shape[-1]
    up = w_up.shape[1]
    tm = min(ROW_TILE, bsz * s)
    planes = tm // SUBLANES
    per_step = FFN_TILES_PER_STEP
    if s >= tm:
        nseg, steps, seqs = 1, s // (tm * per_step), 1
    else:
        nseg, steps, seqs = tm // s, 1, per_step * (tm // s)
        assert nseg == SUBLANES
    outer = bsz // seqs
    row = lambda o, t: (o * steps + t, 0)
    fixed = lambda o, t: (0, 0)
    st_spec = pl.BlockSpec((seqs, CONV_W - 1, up), lambda o, t: (o, 0, 0))
    y, new_state = pl.pallas_call(
        functools.partial(_ffn_kernel, nseg=nseg),
        grid=(outer, steps),
        in_specs=[pl.BlockSpec((per_step * planes, SUBLANES, d), lambda o, t: (o * steps + t, 0, 0)),
                  st_spec,
                  pl.BlockSpec((1, d), fixed),
                  pl.BlockSpec((1, d), fixed),
                  _resident(w_up.shape, fixed),
                  pl.BlockSpec(conv_w.shape, fixed),
                  pl.BlockSpec((1, up), fixed),
                  _resident(w_down.shape, fixed)],
        out_specs=[pl.BlockSpec((per_step * tm, d), row), st_spec],
        out_shape=[jax.ShapeDtypeStruct((bsz * s, d), F32),
                   jax.ShapeDtypeStruct(state.shape, F32)],
        scratch_shapes=[pltpu.VMEM((seqs, CONV_W - 1, up), F32),
                        pltpu.VMEM((per_step, 2, CONV_W - 1 + planes, SUBLANES, 2 * FFN_COLS), F32),
                        pltpu.VMEM((per_step, tm, d), BF16),
                        pltpu.VMEM((per_step, tm, w_down.shape[0]), BF16)],
        compiler_params=_params("arbitrary", "arbitrary"),
        name="conv_ffn",
    )(x, state, g_pre, g_post, w_up, conv_w, conv_b, w_down)
    return y.reshape(bsz, s, d), new_state


def _layer(x, caches, conv_state, bias, w):
    bsz, s, d = x.shape
    n = bsz * s
    x2 = x.reshape(n, d)
    keep = min(BAND, s)
    (qa, qb, ka16, kb16, vat16, vbt16, kat, vat, kbt, vbt, lft) = _inproj(
        x2, w["g_pre_mix"], w["w_in_t"], w["b_f"], seq_len=s, band_keep=keep)
    seq = lambda a: a.reshape(bsz, s, a.shape[-1])
    if caches is None:
        ccol, _ = _cumsum(lft)
        oa = _band_attn_prompt(seq(qa), seq(ka16), vat16, bias[1])
        ob = _fox_attn_prompt(seq(qb), seq(kb16), vbt16, ccol)
    else:
        ckat, cvat, ckbt, cvbt, clft = caches
        ccol, crow = _cumsum(clft, lft, col_from=clft.shape[2])
        oa = _band_attn_sample(seq(qa), ckat, cvat, seq(ka16), vat16, bias[0]).reshape(n, WIDTH)
        ob = _fox_attn_sample(seq(qb), ckbt, cvbt, seq(kb16), vbt16, ccol, crow).reshape(n, WIDTH)
    x1 = _postmix(x2, oa, ob, w["g_pre_mix"], w["g_post_mix"], w["w_in_t"], w["w_pa"], w["w_pb"], w["w_out"])
    y, new_conv = _ffn(x1, bsz, s, conv_state, w["g_pre_ffn"], w["g_post_ffn"],
                       w["w_up"], w["conv_w"], w["conv_b"], w["w_down"])
    heads = lambda a: a.reshape(bsz, HEADS, HEAD_DIM, a.shape[-1]).transpose(0, 3, 1, 2)
    return y, (heads(kat), heads(vat), heads(kbt), heads(vbt), lft.transpose(0, 2, 1), new_conv)


def _channel_major(cache):
    bsz, past = cache.shape[:2]
    return cache.transpose(0, 2, 3, 1).reshape(bsz, WIDTH, past)


def kernel(x_prompt, x_sample, cache_k_a, cache_v_a, cache_k_b, cache_v_b, cache_logf_b, state_conv_ffn,
           g_pre_mix, g_post_mix, g_pre_ffn, g_post_ffn, w_in, b_f, rel_table, w_proj_a, w_proj_b, w_out,
           w_up, conv_w, conv_b, w_down):
    depth = w_in.shape[0]
    up = w_up.shape[-1]
    x_p, x_s = x_prompt, x_sample
    p_states, s_states = [], []
    for l in range(depth):
        w = {
            "g_pre_mix": g_pre_mix[l][None], "g_post_mix": g_post_mix[l][None],
            "g_pre_ffn": g_pre_ffn[l][None], "g_post_ffn": g_post_ffn[l][None],
            "w_in_t": w_in[l].T,
            "b_f": b_f[l][:, None],
            "w_pa": w_proj_a[l].astype(BF16), "w_pb": w_proj_b[l].astype(BF16),
            "w_out": w_out[l].astype(BF16), "w_up": w_up[l].astype(BF16),
            "conv_w": conv_w[l], "conv_b": conv_b[l][None], "w_down": w_down[l].astype(BF16),
        }
        bias = _band_bias(rel_table[l])
        zero_state = jnp.zeros((x_p.shape[0], CONV_W - 1, up), F32)
        x_p, (ka, va, kb, vb, lf, cv) = _layer(x_p, None, zero_state, bias, w)
        p_states.append((ka, va, kb, vb, lf, cv))
        caches = (_channel_major(cache_k_a[l]), _channel_major(cache_v_a[l]), _channel_major(cache_k_b[l]),
                  _channel_major(cache_v_b[l]), cache_logf_b[l].transpose(0, 2, 1))
        x_s, st = _layer(x_s, caches, state_conv_ffn[l], bias, w)
        s_states.append(st)
    stack = lambda states: [jnp.stack(s) for s in zip(*states)]
    return (x_p, x_s, *stack(p_states), *stack(s_states))
```

```python
import functools

import jax
import jax.numpy as jnp
from jax import lax
from jax.experimental import pallas as pl
from jax.experimental.pallas import tpu as pltpu

F32, BF16 = jnp.float32, jnp.bfloat16

HEADS = 8
HEAD_DIM = 64
WIDTH = HEADS * HEAD_DIM
CHUNK = 64
PAST_CHUNKS = 8
BAND = PAST_CHUNKS * CHUNK
REL_CLIP = 128
REL_SIZE = 2 * REL_CLIP + 1
CONV_W = 3
EPS = 1e-6
NEG = -1e30
LOG2E = 1.4426950408889634

SUBLANES = 8
ATT_BLK = 256
BAND_KEYS = 3 * ATT_BLK
BAND_BLOCKS = BAND_KEYS // ATT_BLK
PAD_DIM = 2 * HEAD_DIM
SUM_ROWS = 16
HEADS_AHEAD = 2
ROW_TILE = 512
FFN_COLS = 256
CUMSUM_BLK = 256
VMEM_LIMIT = 56 * 1024 * 1024


def _params(*sem):
    return pltpu.CompilerParams(dimension_semantics=sem, vmem_limit_bytes=VMEM_LIMIT)


def _resident(shape, index_map):
    return pl.BlockSpec(shape, index_map, pipeline_mode=pl.Buffered(1))


def _rmsnorm(x, g):
    return x * lax.rsqrt(jnp.mean(x * x, axis=-1, keepdims=True) + EPS) * g


def _dot(a, b):
    return jnp.dot(a, b, preferred_element_type=F32)


def _dot_nt(a, b):
    return lax.dot_general(a, b, (((1,), (1,)), ((), ())), preferred_element_type=F32)


def _split3(x):
    hi = x.astype(BF16)
    r = x - hi.astype(F32)
    mid = r.astype(BF16)
    lo = (r - mid.astype(F32)).astype(BF16)
    return hi, mid, lo


def _inproj_kernel(x_ref, g_ref, w_ref, wft_ref, bf_ref,
                   qa_ref, qb_ref, ka_ref, kb_ref, vat16_ref, vbt16_ref,
                   kat_ref, vat_ref, kbt_ref, vbt_ref, lft_ref):
    h = _rmsnorm(x_ref[...], g_ref[...]).astype(BF16)
    scale = HEAD_DIM ** -0.5

    def proj(c):
        return _dot_nt(h, w_ref[c * WIDTH:(c + 1) * WIDTH, :].astype(BF16))

    def put(ref, zt):
        per_seq = zt.shape[1] // ref.shape[0]
        for sq in range(ref.shape[0]):
            ref[sq] = zt[:, sq * per_seq:(sq + 1) * per_seq]

    qa_ref[...] = (proj(0) * (scale * LOG2E)).astype(BF16)
    qb_ref[...] = (proj(3) * (scale * LOG2E)).astype(BF16)
    for c, row_ref, t_ref in ((1, ka_ref, kat_ref), (4, kb_ref, kbt_ref)):
        z = proj(c)
        row_ref[...] = z.astype(BF16)
        put(t_ref, z.T)
    for c, t16_ref, t_ref in ((2, vat16_ref, vat_ref), (5, vbt16_ref, vbt_ref)):
        zt = proj(c).T
        put(t_ref, zt)
        put(t16_ref, zt.astype(BF16))
    put(lft_ref, jax.nn.log_sigmoid(_dot_nt(wft_ref[...].astype(BF16), h) + bf_ref[...]))


def _inproj(x, g, w_in_t, b_f, seq_len, band_keep):
    n, d = x.shape
    bsz = n // seq_len
    tm = min(ROW_TILE, n)
    tiles_per_seq = max(seq_len // tm, 1)
    seqs_per_tile = max(tm // seq_len, 1)
    cols = tm // seqs_per_tile
    assert band_keep == cols
    row = lambda i: (i, 0)
    fixed = lambda i: (0, 0)
    along = lambda i: (i // tiles_per_seq, 0, i % tiles_per_seq)
    kept = lambda i: (i // tiles_per_seq, 0, 0)
    rows16 = (jax.ShapeDtypeStruct((n, WIDTH), BF16), pl.BlockSpec((tm, WIDTH), row))

    def chan(channels, length, dtype, index_map):
        return (jax.ShapeDtypeStruct((bsz, channels, length), dtype),
                pl.BlockSpec((seqs_per_tile, channels, cols), index_map))

    outs = [rows16] * 4
    outs += [chan(WIDTH, seq_len, BF16, along)] * 2
    outs += [chan(WIDTH, band_keep, F32, kept)] * 2
    outs += [chan(WIDTH, seq_len, F32, along)] * 2
    outs += [chan(HEADS, seq_len, F32, along)]
    return pl.pallas_call(
        _inproj_kernel,
        grid=(n // tm,),
        in_specs=[pl.BlockSpec((tm, d), row),
                  pl.BlockSpec((1, d), fixed),
                  _resident((6 * WIDTH, d), fixed),
                  pl.BlockSpec((HEADS, d), lambda i: (6 * WIDTH // HEADS, 0)),
                  pl.BlockSpec((HEADS, 1), fixed)],
        out_specs=[spec for _, spec in outs],
        out_shape=[shape for shape, _ in outs],
        compiler_params=_params("arbitrary"),
        name="inproj",
    )(x, g, w_in_t, w_in_t, b_f)


def _cumsum_kernel(*refs, seg_lens, col_from):
    seg_refs = refs[:len(seg_lens)]
    ccol_ref, crow_ref = refs[len(seg_lens):]
    carry = jnp.zeros((crow_ref.shape[0], 1), F32)
    off = 0
    for ref, n in zip(seg_refs, seg_lens):
        for o in range(0, n, CUMSUM_BLK):
            b = min(CUMSUM_BLK, n - o)
            r = lax.broadcasted_iota(jnp.int32, (b, b), 0)
            c = lax.broadcasted_iota(jnp.int32, (b, b), 1)
            upper = jnp.where(r <= c, 1.0, 0.0).astype(BF16)
            sums = carry
            for p in _split3(ref[:, o:o + b]):
                sums = sums + _dot(p, upper)
            crow_ref[:, off + o:off + o + b] = sums
            carry = sums[:, b - 1:b]
            if off + o >= col_from:
                by_position = sums.T
                for bb in range(ccol_ref.shape[0]):
                    ccol_ref[bb, off + o - col_from:off + o - col_from + b, :] = (
                        by_position[:, bb * HEADS:(bb + 1) * HEADS])
        off += n


def _cumsum(*segs, col_from=0):
    bsz = segs[0].shape[0]
    seg_lens = tuple(s.shape[2] for s in segs)
    total = sum(seg_lens)
    assert col_from % CUMSUM_BLK == 0 or col_from in (0, seg_lens[0])
    ccol, crow = pl.pallas_call(
        functools.partial(_cumsum_kernel, seg_lens=seg_lens, col_from=col_from),
        out_shape=[jax.ShapeDtypeStruct((bsz, total - col_from, HEADS), F32),
                   jax.ShapeDtypeStruct((bsz * HEADS, total), F32)],
        compiler_params=pltpu.CompilerParams(vmem_limit_bytes=VMEM_LIMIT),
        name="cumsum_logf",
    )(*[s.reshape(bsz * HEADS, s.shape[2]) for s in segs])
    return ccol, crow.reshape(bsz, HEADS, total)


REL_PAD = 384
DIST_SPAN = 1024


def _band_bias_kernel(tbl_ref, qk_ref, kq_ref):
    j = lax.broadcasted_iota(jnp.int32, (REL_PAD, DIST_SPAN), 1)
    r = lax.broadcasted_iota(jnp.int32, (REL_PAD, DIST_SPAN), 0)
    parts = _split3(tbl_ref[...])

    def by_offset(dist):
        onehot = jnp.where(jnp.clip(dist, -REL_CLIP, REL_CLIP) + REL_CLIP == r, 1.0, 0.0).astype(BF16)
        return sum(_dot(p, onehot) for p in parts) * LOG2E

    def skewed(e_row, rows, cols, shift):
        wide = jnp.broadcast_to(e_row, (rows, DIST_SPAN))
        return pltpu.roll(wide, shift % DIST_SPAN, 1, stride=1, stride_axis=0)[:, :cols]

    e_qk = by_offset((BAND_KEYS - 1) - j)
    e_kq = by_offset(j - (ATT_BLK - 1))
    qc = lax.broadcasted_iota(jnp.int32, (ATT_BLK, BAND_KEYS), 0) // CHUNK
    kc = lax.broadcasted_iota(jnp.int32, (ATT_BLK, BAND_KEYS), 1) // CHUNK
    vis_qk = (kc >= qc) & (kc <= qc + PAST_CHUNKS)
    kc = lax.broadcasted_iota(jnp.int32, (BAND_KEYS, ATT_BLK), 0) // CHUNK
    qc = lax.broadcasted_iota(jnp.int32, (BAND_KEYS, ATT_BLK), 1) // CHUNK
    vis_kq = (kc >= qc) & (kc <= qc + PAST_CHUNKS)
    for h in range(HEADS):
        qk_ref[h] = jnp.where(vis_qk, skewed(e_qk[h:h + 1, :], ATT_BLK, BAND_KEYS, -(ATT_BLK - 1)), NEG)
        kq_ref[h] = jnp.where(vis_kq, skewed(e_kq[h:h + 1, :], BAND_KEYS, ATT_BLK, -(BAND_KEYS - 1)), NEG)


def _band_bias(rel_table):
    tbl = jnp.pad(rel_table, ((0, 0), (0, REL_PAD - REL_SIZE)))
    return pl.pallas_call(
        _band_bias_kernel,
        out_shape=[jax.ShapeDtypeStruct((HEADS, ATT_BLK, BAND_KEYS), F32),
                   jax.ShapeDtypeStruct((HEADS, BAND_KEYS, ATT_BLK), F32)],
        compiler_params=pltpu.CompilerParams(vmem_limit_bytes=VMEM_LIMIT),
        name="band_bias",
    )(tbl)


def _band_attn_prompt_kernel(q_ref, k_ref, vt_ref, bias_ref, o_ref, s_ref):
    i = pl.program_id(1)
    ones = jnp.ones((SUM_ROWS, ATT_BLK), BF16)

    def run(clipped):
        blocks = []
        for g in range(BAND_BLOCKS):
            j = i - (BAND_BLOCKS - 1) + g
            start = pl.multiple_of(jnp.maximum(j, 0) * ATT_BLK, ATT_BLK)
            masks = clipped and g < BAND_BLOCKS - 1
            blocks.append((start, jnp.where(j >= 0, 0.0, NEG) if masks else None))

        def scores(h):
            sl = slice(h * HEAD_DIM, (h + 1) * HEAD_DIM)
            q = q_ref[0, :, sl]
            for g, (start, penalty) in enumerate(blocks):
                s = _dot_nt(k_ref[0, pl.ds(start, ATT_BLK), sl], q) + bias_ref[h, g * ATT_BLK:(g + 1) * ATT_BLK, :]
                s_ref[h % (HEADS_AHEAD + 1), g] = s if penalty is None else s + penalty

        def finish(h):
            sl = slice(h * HEAD_DIM, (h + 1) * HEAD_DIM)
            s = [s_ref[h % (HEADS_AHEAD + 1), g] for g in range(BAND_BLOCKS)]
            m = functools.reduce(jnp.maximum, [jnp.max(x, axis=0, keepdims=True) for x in s])
            acc = 0.0
            for x, (start, _) in zip(s, blocks):
                vt = jnp.concatenate([vt_ref[0, sl, pl.ds(start, ATT_BLK)], ones], axis=0)
                acc = acc + _dot(vt, jnp.exp2(x - m).astype(BF16))
            o_ref[0, sl, :] = (acc[:HEAD_DIM] / acc[HEAD_DIM:HEAD_DIM + 1]).astype(BF16)

        for h in range(HEADS + HEADS_AHEAD):
            if h < HEADS:
                scores(h)
            if h >= HEADS_AHEAD:
                finish(h - HEADS_AHEAD)

    pl.when(i >= BAND_BLOCKS - 1)(lambda: run(clipped=False))
    pl.when(i < BAND_BLOCKS - 1)(lambda: run(clipped=True))


def _band_attn_prompt(q, k, vt, bias_kq):
    bsz, s, _ = q.shape
    qblk = pl.BlockSpec((1, ATT_BLK, WIDTH), lambda b, i: (b, i, 0))
    return pl.pallas_call(
        _band_attn_prompt_kernel,
        grid=(bsz, s // ATT_BLK),
        in_specs=[qblk,
                  pl.BlockSpec((1, s, WIDTH), lambda b, i: (b, 0, 0)),
                  pl.BlockSpec((1, WIDTH, s), lambda b, i: (b, 0, 0)),
                  _resident(bias_kq.shape, lambda b, i: (0, 0, 0))],
        out_specs=pl.BlockSpec((1, WIDTH, ATT_BLK), lambda b, i: (b, 0, i)),
        out_shape=jax.ShapeDtypeStruct((bsz, WIDTH, s), BF16),
        scratch_shapes=[pltpu.VMEM((HEADS_AHEAD + 1, BAND_BLOCKS, ATT_BLK, ATT_BLK), F32)],
        compiler_params=_params("parallel", "parallel"),
        name="band_attn_prompt",
    )(q, k, vt, bias_kq)


def _decay_lanes(c, key_side):
    terms = jnp.concatenate(_split3(c), axis=1)
    row = lax.broadcasted_iota(jnp.int32, (3 * HEADS, WIDTH), 0)
    lane = lax.broadcasted_iota(jnp.int32, (3 * HEADS, WIDTH), 1)
    first_one, first_term, sign = (3, 0, -1.0) if key_side else (0, 3, 1.0)
    place = jnp.where(lane == (row % HEADS) * HEAD_DIM + first_term + row // HEADS, sign, 0.0).astype(BF16)
    in_group = lax.broadcasted_iota(jnp.int32, (1, WIDTH), 1) % HEAD_DIM
    ones = jnp.where((in_group >= first_one) & (in_group < first_one + 3), 1.0, 0.0)
    return (ones + _dot(terms, place)).astype(BF16)


def _fox_attn_prompt_kernel(q_ref, k_ref, vt_ref, ccol_ref, o_ref,
                            kp_ref, qp_ref, s_ref, m_ref, acc_ref):
    i = pl.program_id(1)
    qstart = pl.multiple_of(i * ATT_BLK, ATT_BLK)

    @pl.when(i == 0)
    def _():
        extra = _decay_lanes(ccol_ref[0] * LOG2E, key_side=True)
        for h in range(HEADS):
            sl = slice(h * HEAD_DIM, (h + 1) * HEAD_DIM)
            kp_ref[:, h * PAD_DIM:(h + 1) * PAD_DIM] = jnp.concatenate([k_ref[0, :, sl], extra[:, sl]], axis=1)

    extra = _decay_lanes(ccol_ref[0, pl.ds(qstart, ATT_BLK), :] * LOG2E, key_side=False)
    for h in range(HEADS):
        sl = slice(h * HEAD_DIM, (h + 1) * HEAD_DIM)
        qp_ref[:, h * PAD_DIM:(h + 1) * PAD_DIM] = jnp.concatenate([q_ref[0, :, sl], extra[:, sl]], axis=1)

    key = lax.broadcasted_iota(jnp.int32, (ATT_BLK, ATT_BLK), 0)
    qry = lax.broadcasted_iota(jnp.int32, (ATT_BLK, ATT_BLK), 1)
    causal = key <= qry
    ones = jnp.ones((SUM_ROWS, ATT_BLK), BF16)
    m_ref[...] = jnp.full(m_ref.shape, NEG, F32)
    acc_ref[...] = jnp.zeros(acc_ref.shape, F32)

    def scores(j, h):
        start = pl.multiple_of(j * ATT_BLK, ATT_BLK)
        pad = slice(h * PAD_DIM, (h + 1) * PAD_DIM)
        return _dot_nt(kp_ref[pl.ds(start, ATT_BLK), pad], qp_ref[:, pad])

    def update(j, h, s, masked):
        start = pl.multiple_of(j * ATT_BLK, ATT_BLK)
        if masked:
            s = jnp.where(causal, s, NEG)
        m_old = m_ref[h]
        m_new = jnp.maximum(m_old, jnp.max(s, axis=0, keepdims=True))
        alpha = jnp.exp2(m_old - m_new)
        p = jnp.exp2(s - m_new)
        vt = vt_ref[0, h * HEAD_DIM:(h + 1) * HEAD_DIM, pl.ds(start, ATT_BLK)]
        acc_ref[h] = alpha * acc_ref[h] + _dot(jnp.concatenate([vt, ones], axis=0), p.astype(BF16))
        m_ref[h] = m_new

    def step(j, cur, masked=False, prefetch=True):
        for h in range(HEADS + HEADS_AHEAD):
            if prefetch and h < HEADS:
                s_ref[1 - cur, h] = scores(j + 1, h)
            if h >= HEADS_AHEAD:
                update(j, h - HEADS_AHEAD, s_ref[cur, h - HEADS_AHEAD], masked)

    for h in range(HEADS):
        s_ref[0, h] = scores(0, h)

    def pair(t, carry):
        step(2 * t, 0)
        step(2 * t + 1, 1)
        return carry

    lax.fori_loop(0, i // 2, pair, 0)

    @pl.when(i % 2 == 1)
    def _():
        step(i - 1, 0)
        step(i, 1, masked=True, prefetch=False)

    @pl.when(i % 2 == 0)
    def _():
        step(i, 0, masked=True, prefetch=False)

    for h in range(HEADS):
        sl = slice(h * HEAD_DIM, (h + 1) * HEAD_DIM)
        out = acc_ref[h, :HEAD_DIM, :] / acc_ref[h, HEAD_DIM:HEAD_DIM + 1, :]
        o_ref[0, sl, :] = out.astype(BF16)


def _fox_attn_prompt(q, k, vt, ccol):
    bsz, s, _ = q.shape
    qblk = pl.BlockSpec((1, ATT_BLK, WIDTH), lambda b, i: (b, i, 0))
    return pl.pallas_call(
        _fox_attn_prompt_kernel,
        grid=(bsz, s // ATT_BLK),
        in_specs=[qblk,
                  pl.BlockSpec((1, s, WIDTH), lambda b, i: (b, 0, 0)),
                  pl.BlockSpec((1, WIDTH, s), lambda b, i: (b, 0, 0)),
                  pl.BlockSpec((1, s, HEADS), lambda b, i: (b, 0, 0))],
        out_specs=pl.BlockSpec((1, WIDTH, ATT_BLK), lambda b, i: (b, 0, i)),
        out_shape=jax.ShapeDtypeStruct((bsz, WIDTH, s), BF16),
        scratch_shapes=[pltpu.VMEM((s, HEADS * PAD_DIM), BF16),
                        pltpu.VMEM((ATT_BLK, HEADS * PAD_DIM), BF16),
                        pltpu.VMEM((2, HEADS, ATT_BLK, ATT_BLK), F32),
                        pltpu.VMEM((HEADS, 1, ATT_BLK), F32),
                        pltpu.VMEM((HEADS, HEAD_DIM + SUM_ROWS, ATT_BLK), F32)],
        compiler_params=_params("parallel", "arbitrary"),
        name="fox_attn_prompt",
    )(q, k, vt, ccol)


def _pipelined_heads(scores, finish):
    pending = []
    for h in range(HEADS + HEADS_AHEAD):
        if h < HEADS:
            pending.append(scores(h))
        if h >= HEADS_AHEAD:
            finish(h - HEADS_AHEAD, pending.pop(0))


def _sample_attention(s_past, s_new, vct, vnt):
    m = jnp.maximum(jnp.max(s_past, axis=-1, keepdims=True), jnp.max(s_new, axis=-1, keepdims=True))
    p_past = jnp.exp2(s_past - m)
    p_new = jnp.exp2(s_new - m)
    l = jnp.sum(p_past, axis=-1, keepdims=True) + jnp.sum(p_new, axis=-1, keepdims=True)
    acc = _dot_nt(p_past.astype(BF16), vct) + _dot_nt(p_new.astype(BF16), vnt)
    return (acc / l).astype(BF16)


def _band_attn_sample_kernel(q_ref, kct_ref, vct_ref, kn_ref, vnt_ref, bias_ref, o_ref):
    t = q_ref.shape[1]
    past = kct_ref.shape[2]

    def scores(h):
        sl = slice(h * HEAD_DIM, (h + 1) * HEAD_DIM)
        q = q_ref[0, :, sl]
        s_past = _dot(q, kct_ref[0, sl, :].astype(BF16)) + bias_ref[h, 0:t, BAND - past:BAND]
        s_new = _dot_nt(q, kn_ref[0, :, sl]) + bias_ref[h, 0:t, BAND:BAND + t]
        return s_past, s_new

    def finish(h, s):
        sl = slice(h * HEAD_DIM, (h + 1) * HEAD_DIM)
        o_ref[0, :, sl] = _sample_attention(*s, vct_ref[0, sl, :].astype(BF16), vnt_ref[0, sl, :])

    _pipelined_heads(scores, finish)


def _band_attn_sample(q, k_cache_t, v_cache_t, k_new, v_new_t, bias):
    bsz, t, _ = q.shape
    past = k_cache_t.shape[2]
    new = pl.BlockSpec((1, t, WIDTH), lambda b: (b, 0, 0))
    new_t = pl.BlockSpec((1, WIDTH, t), lambda b: (b, 0, 0))
    old_t = pl.BlockSpec((1, WIDTH, past), lambda b: (b, 0, 0))
    return pl.pallas_call(
        _band_attn_sample_kernel,
        grid=(bsz,),
        in_specs=[new, old_t, old_t, new, new_t, _resident(bias.shape, lambda b: (0, 0, 0))],
        out_specs=new,
        out_shape=jax.ShapeDtypeStruct(q.shape, BF16),
        compiler_params=_params("parallel"),
        name="band_attn_sample",
    )(q, k_cache_t, v_cache_t, k_new, v_new_t, bias)


def _fox_attn_sample_kernel(q_ref, kct_ref, vct_ref, kn_ref, vnt_ref, ccol_ref, crow_ref, o_ref):
    t = q_ref.shape[1]
    past = kct_ref.shape[2]
    row = lax.broadcasted_iota(jnp.int32, (t, t), 0)
    col = lax.broadcasted_iota(jnp.int32, (t, t), 1)
    causal = col <= row

    def scores(h):
        sl = slice(h * HEAD_DIM, (h + 1) * HEAD_DIM)
        q = q_ref[0, :, sl]
        cq = ccol_ref[0, :, h:h + 1]
        s_past = _dot(q, kct_ref[0, sl, :].astype(BF16)) + (cq - crow_ref[0, h:h + 1, 0:past]) * LOG2E
        s_new = _dot_nt(q, kn_ref[0, :, sl]) + (cq - crow_ref[0, h:h + 1, past:past + t]) * LOG2E
        return s_past, jnp.where(causal, s_new, NEG)

    def finish(h, s):
        sl = slice(h * HEAD_DIM, (h + 1) * HEAD_DIM)
        o_ref[0, :, sl] = _sample_attention(*s, vct_ref[0, sl, :].astype(BF16), vnt_ref[0, sl, :])

    _pipelined_heads(scores, finish)


def _fox_attn_sample(q, k_cache_t, v_cache_t, k_new, v_new_t, ccol, crow):
    bsz, t, _ = q.shape
    past = k_cache_t.shape[2]
    assert past % t == 0
    new = pl.BlockSpec((1, t, WIDTH), lambda b: (b, 0, 0))
    new_t = pl.BlockSpec((1, WIDTH, t), lambda b: (b, 0, 0))
    old_t = pl.BlockSpec((1, WIDTH, past), lambda b: (b, 0, 0))
    return pl.pallas_call(
        _fox_attn_sample_kernel,
        grid=(bsz,),
        in_specs=[new, old_t, old_t, new, new_t,
                  pl.BlockSpec((1, t, HEADS), lambda b: (b, 0, 0)),
                  pl.BlockSpec((1, HEADS, past + t), lambda b: (b, 0, 0))],
        out_specs=new,
        out_shape=jax.ShapeDtypeStruct(q.shape, BF16),
        compiler_params=_params("parallel"),
        name="fox_attn_sample",
    )(q, k_cache_t, v_cache_t, k_new, v_new_t, ccol, crow)


def _postmix_kernel(x_ref, oa_ref, ob_ref, gpre_ref, gpost_ref, wt_ref, wpa_ref, wpb_ref, wout_ref, y_ref,
                    *, channel_major):
    tm, d = x_ref.shape
    planes = tm // SUBLANES
    gates = wt_ref.shape[0] - 2 * d
    halves = [slice(k * tm // 2, (k + 1) * tm // 2) for k in range(2)]

    def mix(rows):
        h = _rmsnorm(x_ref[rows, :], gpre_ref[...]).astype(BF16)
        gate_a = jax.nn.sigmoid(_dot_nt(h, wt_ref[gates:gates + d, :].astype(BF16)))
        gate_b = jax.nn.sigmoid(_dot_nt(h, wt_ref[gates + d:, :].astype(BF16)))
        if channel_major:
            proj = lambda o_ref, w_ref: lax.dot_general(o_ref[0, :, rows], w_ref[...], (((0,), (0,)), ((), ())),
                                                        preferred_element_type=F32)
        else:
            proj = lambda o_ref, w_ref: _dot(o_ref[rows, :], w_ref[...])
        return (gate_a * proj(oa_ref, wpa_ref) + gate_b * proj(ob_ref, wpb_ref)).astype(BF16)

    merged = [mix(rows) for rows in halves]
    outs = [_dot(m, wout_ref[...]) for m in merged]
    for k, rows in enumerate(halves):
        y = x_ref[rows, :] + _rmsnorm(outs[k], gpost_ref[...])
        subs = slice(k * SUBLANES // 2, (k + 1) * SUBLANES // 2)
        y_ref[:, subs, :] = jnp.swapaxes(y.reshape(SUBLANES // 2, planes, d), 0, 1)


def _deinterleave(planes):
    p, s, d = planes.shape
    return jnp.swapaxes(planes, 0, 1).reshape(s * p, d)


def _postmix(x, oa, ob, g_pre, g_post, w_in_t, w_pa, w_pb, w_out):
    n, d = x.shape
    tm = min(ROW_TILE, n)
    row = lambda i: (i, 0)
    fixed = lambda i: (0, 0)
    channel_major = oa.ndim == 3
    if channel_major:
        tiles_per_seq = oa.shape[2] // tm
        o_spec = pl.BlockSpec((1, WIDTH, tm), lambda i: (i // tiles_per_seq, 0, i % tiles_per_seq))
    else:
        o_spec = pl.BlockSpec((tm, WIDTH), row)
    return pl.pallas_call(
        functools.partial(_postmix_kernel, channel_major=channel_major),
        grid=(n // tm,),
        in_specs=[pl.BlockSpec((tm, d), row),
                  o_spec,
                  o_spec,
                  pl.BlockSpec((1, d), fixed),
                  pl.BlockSpec((1, d), fixed),
                  _resident(w_in_t.shape, fixed),
                  _resident(w_pa.shape, fixed),
                  _resident(w_pb.shape, fixed),
                  _resident(w_out.shape, fixed)],
        out_specs=pl.BlockSpec((tm // SUBLANES, SUBLANES, d), lambda i: (i, 0, 0)),
        out_shape=jax.ShapeDtypeStruct((n // SUBLANES, SUBLANES, d), F32),
        compiler_params=_params("parallel"),
        name="postmix",
    )(x, oa, ob, g_pre, g_post, w_in_t, w_pa, w_pb, w_out)


def _ffn_kernel(x_ref, st_ref, gpre_ref, gpost_ref, wup_ref, cw_ref, cb_ref, wdn_ref,
                y_ref, nst_ref, hist_ref, ext_ref, h_ref, f_ref, *, nseg):
    @pl.when(pl.program_id(1) == 0)
    def _():
        hist_ref[...] = st_ref[...]

    planes, _, d = x_ref.shape
    tm = planes * SUBLANES
    d_ff = wdn_ref.shape[0]
    h_ref[...] = _rmsnorm(x_ref[...].reshape(tm, d), gpre_ref[...]).astype(BF16)
    n_chunks = d_ff // FFN_COLS
    first_sublane = lax.broadcasted_iota(jnp.int32, (SUBLANES, FFN_COLS), 0) == 0

    def parts(c):
        for part in range(2):
            yield (slice(part * d_ff + c * FFN_COLS, part * d_ff + (c + 1) * FFN_COLS),
                   slice(part * FFN_COLS, (part + 1) * FFN_COLS))

    def up(c):
        ext = ext_ref.at[c % 2]
        for cols, dst in parts(c):
            u = _dot(h_ref[...], wup_ref[:, cols]).reshape(planes, SUBLANES, FFN_COLS)
            ext[CONV_W - 1:, :, dst] = u
            for k in range(CONV_W - 1):
                last = u[planes - (CONV_W - 1) + k]
                if nseg == 1:
                    ext[k, :, dst] = jnp.where(first_sublane, hist_ref[0, k:k + 1, cols], pltpu.roll(last, 1, 0))
                    hist_ref[0, k:k + 1, cols] = last[SUBLANES - 1:, :]
                else:
                    ext[k, :, dst] = hist_ref[:, k, cols]
                    hist_ref[:, k, cols] = last

    def activate(c):
        ext = ext_ref.at[c % 2]
        halves = []
        for cols, dst in parts(c):
            y = cb_ref[:, cols]
            for tap in range(CONV_W):
                y = y + ext[tap:tap + planes, :, dst] * cw_ref[tap:tap + 1, cols]
            halves.append(y)
        act = (jax.nn.gelu(halves[0]) * halves[1]).reshape(tm, FFN_COLS)
        f_ref[:, c * FFN_COLS:(c + 1) * FFN_COLS] = act.astype(BF16)

    up(0)
    for c in range(n_chunks):
        if c + 1 < n_chunks:
            up(c + 1)
        activate(c)
    f = _dot(f_ref[...], wdn_ref[...])
    y = x_ref[...].reshape(tm, d) + _rmsnorm(f, gpost_ref[...])
    y_ref[...] = _deinterleave(y.reshape(planes, SUBLANES, d))
    nst_ref[...] = hist_ref[...]


def _ffn(x, bsz, s, state, g_pre, g_post, w_up, conv_w, conv_b, w_down):
    d = x.shape[-1]
    up = w_up.shape[1]
    tm = min(ROW_TILE, bsz * s)
    planes = tm // SUBLANES
    if s >= tm:
        nseg, tiles = 1, s // tm
    else:
        nseg, tiles = tm // s, 1
        assert nseg == SUBLANES
    outer = bsz // nseg
    row = lambda o, t: (o * tiles + t, 0)
    fixed = lambda o, t: (0, 0)
    st_spec = pl.BlockSpec((nseg, CONV_W - 1, up), lambda o, t: (o, 0, 0))
    y, new_state = pl.pallas_call(
        functools.partial(_ffn_kernel, nseg=nseg),
        grid=(outer, tiles),
        in_specs=[pl.BlockSpec((planes, SUBLANES, d), lambda o, t: (o * tiles + t, 0, 0)),
                  st_spec,
                  pl.BlockSpec((1, d), fixed),
                  pl.BlockSpec((1, d), fixed),
                  _resident(w_up.shape, fixed),
                  pl.BlockSpec(conv_w.shape, fixed),
                  pl.BlockSpec((1, up), fixed),
                  _resident(w_down.shape, fixed)],
        out_specs=[pl.BlockSpec((tm, d), row), st_spec],
        out_shape=[jax.ShapeDtypeStruct((bsz * s, d), F32),
                   jax.ShapeDtypeStruct(state.shape, F32)],
        scratch_shapes=[pltpu.VMEM((nseg, CONV_W - 1, up), F32),
                        pltpu.VMEM((2, CONV_W - 1 + planes, SUBLANES, 2 * FFN_COLS), F32),
                        pltpu.VMEM((tm, d), BF16),
                        pltpu.VMEM((tm, w_down.shape[0]), BF16)],
        compiler_params=_params("arbitrary", "arbitrary"),
        name="conv_ffn",
    )(x, state, g_pre, g_post, w_up, conv_w, conv_b, w_down)
    return y.reshape(bsz, s, d), new_state


def _layer(x, caches, conv_state, bias, w):
    bsz, s, d = x.shape
    n = bsz * s
    x2 = x.reshape(n, d)
    keep = min(BAND, s)
    (qa, qb, ka16, kb16, vat16, vbt16, kat, vat, kbt, vbt, lft) = _inproj(
        x2, w["g_pre_mix"], w["w_in_t"], w["b_f"], seq_len=s, band_keep=keep)
    seq = lambda a: a.reshape(bsz, s, a.shape[-1])
    if caches is None:
        ccol, _ = _cumsum(lft)
        oa = _band_attn_prompt(seq(qa), seq(ka16), vat16, bias[1])
        ob = _fox_attn_prompt(seq(qb), seq(kb16), vbt16, ccol)
    else:
        ckat, cvat, ckbt, cvbt, clft = caches
        ccol, crow = _cumsum(clft, lft, col_from=clft.shape[2])
        oa = _band_attn_sample(seq(qa), ckat, cvat, seq(ka16), vat16, bias[0]).reshape(n, WIDTH)
        ob = _fox_attn_sample(seq(qb), ckbt, cvbt, seq(kb16), vbt16, ccol, crow).reshape(n, WIDTH)
    x1 = _postmix(x2, oa, ob, w["g_pre_mix"], w["g_post_mix"], w["w_in_t"], w["w_pa"], w["w_pb"], w["w_out"])
    y, new_conv = _ffn(x1, bsz, s, conv_state, w["g_pre_ffn"], w["g_post_ffn"],
                       w["w_up"], w["conv_w"], w["conv_b"], w["w_down"])
    heads = lambda a: a.reshape(bsz, HEADS, HEAD_DIM, a.shape[-1]).transpose(0, 3, 1, 2)
    return y, (heads(kat), heads(vat), heads(kbt), heads(vbt), lft.transpose(0, 2, 1), new_conv)


def _channel_major(cache):
    bsz, past = cache.shape[:2]
    return cache.transpose(0, 2, 3, 1).reshape(bsz, WIDTH, past)


def kernel(x_prompt, x_sample, cache_k_a, cache_v_a, cache_k_b, cache_v_b, cache_logf_b, state_conv_ffn,
           g_pre_mix, g_post_mix, g_pre_ffn, g_post_ffn, w_in, b_f, rel_table, w_proj_a, w_proj_b, w_out,
           w_up, conv_w, conv_b, w_down):
    depth = w_in.shape[0]
    up = w_up.shape[-1]
    x_p, x_s = x_prompt, x_sample
    p_states, s_states = [], []
    for l in range(depth):
        w = {
            "g_pre_mix": g_pre_mix[l][None], "g_post_mix": g_post_mix[l][None],
            "g_pre_ffn": g_pre_ffn[l][None], "g_post_ffn": g_post_ffn[l][None],
            "w_in_t": w_in[l].T,
            "b_f": b_f[l][:, None],
            "w_pa": w_proj_a[l].astype(BF16), "w_pb": w_proj_b[l].astype(BF16),
            "w_out": w_out[l].astype(BF16), "w_up": w_up[l].astype(BF16),
            "conv_w": conv_w[l], "conv_b": conv_b[l][None], "w_down": w_down[l].astype(BF16),
        }
        bias = _band_bias(rel_table[l])
        zero_state = jnp.zeros((x_p.shape[0], CONV_W - 1, up), F32)
        x_p, (ka, va, kb, vb, lf, cv) = _layer(x_p, None, zero_state, bias, w)
        p_states.append((ka, va, kb, vb, lf, cv))
        caches = (_channel_major(cache_k_a[l]), _channel_major(cache_v_a[l]), _channel_major(cache_k_b[l]),
                  _channel_major(cache_v_b[l]), cache_logf_b[l].transpose(0, 2, 1))
        x_s, st = _layer(x_s, caches, state_conv_ffn[l], bias, w)
        s_states.append(st)
    stack = lambda states: [jnp.stack(s) for s in zip(*states)]
    return (x_p, x_s, *stack(p_states), *stack(s_states))
```

```python
import functools

import jax
import jax.numpy as jnp
from jax import lax
from jax.experimental import pallas as pl
from jax.experimental.pallas import tpu as pltpu

F32, BF16 = jnp.float32, jnp.bfloat16

HEADS = 8
HEAD_DIM = 64
WIDTH = HEADS * HEAD_DIM
CHUNK = 64
PAST_CHUNKS = 8
BAND = PAST_CHUNKS * CHUNK
REL_CLIP = 128
REL_SIZE = 2 * REL_CLIP + 1
CONV_W = 3
EPS = 1e-6
NEG = -1e30
LOG2E = 1.4426950408889634

SUBLANES = 8
ATT_BLK = 256
BAND_KEYS = 3 * ATT_BLK
BAND_BLOCKS = BAND_KEYS // ATT_BLK
PAD_DIM = 2 * HEAD_DIM
SUM_ROWS = 16
HEADS_AHEAD = 2
ROW_TILE = 512
MIX_TILE = 1024
FFN_COLS = 256
CUMSUM_BLK = 256
VMEM_LIMIT = 56 * 1024 * 1024


def _params(*sem):
    return pltpu.CompilerParams(dimension_semantics=sem, vmem_limit_bytes=VMEM_LIMIT)


def _resident(shape, index_map):
    return pl.BlockSpec(shape, index_map, pipeline_mode=pl.Buffered(1))


def _mix_tile(seq_len):
    return MIX_TILE if seq_len >= MIX_TILE else SUBLANES * seq_len


def _rmsnorm(x, g):
    return x * lax.rsqrt(jnp.mean(x * x, axis=-1, keepdims=True) + EPS) * g


def _dot(a, b):
    return jnp.dot(a, b, preferred_element_type=F32)


def _dot_nt(a, b):
    return lax.dot_general(a, b, (((1,), (1,)), ((), ())), preferred_element_type=F32)


def _split3(x):
    hi = x.astype(BF16)
    r = x - hi.astype(F32)
    mid = r.astype(BF16)
    lo = (r - mid.astype(F32)).astype(BF16)
    return hi, mid, lo


def _inproj_kernel(x_ref, g_ref, w_ref, wft_ref, bf_ref,
                   qa_ref, qb_ref, ka_ref, kb_ref, vat16_ref, vbt16_ref,
                   kat_ref, vat_ref, kbt_ref, vbt_ref, lft_ref):
    h = _rmsnorm(x_ref[...], g_ref[...]).astype(BF16)
    scale = HEAD_DIM ** -0.5

    def proj(c):
        return _dot_nt(h, w_ref[c * WIDTH:(c + 1) * WIDTH, :].astype(BF16))

    def put(ref, zt):
        per_seq = zt.shape[1] // ref.shape[0]
        for sq in range(ref.shape[0]):
            ref[sq] = zt[:, sq * per_seq:(sq + 1) * per_seq]

    qa_ref[...] = (proj(0) * (scale * LOG2E)).astype(BF16)
    qb_ref[...] = (proj(3) * (scale * LOG2E)).astype(BF16)
    for c, row_ref, t_ref in ((1, ka_ref, kat_ref), (4, kb_ref, kbt_ref)):
        z = proj(c)
        row_ref[...] = z.astype(BF16)
        put(t_ref, z.T)
    for c, t16_ref, t_ref in ((2, vat16_ref, vat_ref), (5, vbt16_ref, vbt_ref)):
        zt = proj(c).T
        put(t_ref, zt)
        put(t16_ref, zt.astype(BF16))
    put(lft_ref, jax.nn.log_sigmoid(_dot_nt(wft_ref[...].astype(BF16), h) + bf_ref[...]))


def _inproj(x, g, w_in_t, b_f, seq_len, band_keep):
    n, d = x.shape
    bsz = n // seq_len
    tm = min(ROW_TILE, n)
    tiles_per_seq = max(seq_len // tm, 1)
    seqs_per_tile = max(tm // seq_len, 1)
    cols = tm // seqs_per_tile
    assert band_keep == cols
    row = lambda i: (i, 0)
    fixed = lambda i: (0, 0)
    along = lambda i: (i // tiles_per_seq, 0, i % tiles_per_seq)
    kept = lambda i: (i // tiles_per_seq, 0, 0)
    rows16 = (jax.ShapeDtypeStruct((n, WIDTH), BF16), pl.BlockSpec((tm, WIDTH), row))

    def chan(channels, length, dtype, index_map):
        return (jax.ShapeDtypeStruct((bsz, channels, length), dtype),
                pl.BlockSpec((seqs_per_tile, channels, cols), index_map))

    outs = [rows16] * 4
    outs += [chan(WIDTH, seq_len, BF16, along)] * 2
    outs += [chan(WIDTH, band_keep, F32, kept)] * 2
    outs += [chan(WIDTH, seq_len, F32, along)] * 2
    outs += [chan(HEADS, seq_len, F32, along)]
    return pl.pallas_call(
        _inproj_kernel,
        grid=(n // tm,),
        in_specs=[pl.BlockSpec((tm, d), row),
                  pl.BlockSpec((1, d), fixed),
                  _resident((6 * WIDTH, d), fixed),
                  pl.BlockSpec((HEADS, d), lambda i: (6 * WIDTH // HEADS, 0)),
                  pl.BlockSpec((HEADS, 1), fixed)],
        out_specs=[spec for _, spec in outs],
        out_shape=[shape for shape, _ in outs],
        compiler_params=_params("arbitrary"),
        name="inproj",
    )(x, g, w_in_t, w_in_t, b_f)


def _cumsum_kernel(*refs, seg_lens, col_from):
    seg_refs = refs[:len(seg_lens)]
    ccol_ref, crow_ref = refs[len(seg_lens):]
    carry = jnp.zeros((crow_ref.shape[0], 1), F32)
    off = 0
    for ref, n in zip(seg_refs, seg_lens):
        for o in range(0, n, CUMSUM_BLK):
            b = min(CUMSUM_BLK, n - o)
            r = lax.broadcasted_iota(jnp.int32, (b, b), 0)
            c = lax.broadcasted_iota(jnp.int32, (b, b), 1)
            upper = jnp.where(r <= c, 1.0, 0.0).astype(BF16)
            sums = carry
            for p in _split3(ref[:, o:o + b]):
                sums = sums + _dot(p, upper)
            crow_ref[:, off + o:off + o + b] = sums
            carry = sums[:, b - 1:b]
            if off + o >= col_from:
                by_position = sums.T
                for bb in range(ccol_ref.shape[0]):
                    ccol_ref[bb, off + o - col_from:off + o - col_from + b, :] = (
                        by_position[:, bb * HEADS:(bb + 1) * HEADS])
        off += n


def _cumsum(*segs, col_from=0):
    bsz = segs[0].shape[0]
    seg_lens = tuple(s.shape[2] for s in segs)
    total = sum(seg_lens)
    assert col_from % CUMSUM_BLK == 0 or col_from in (0, seg_lens[0])
    ccol, crow = pl.pallas_call(
        functools.partial(_cumsum_kernel, seg_lens=seg_lens, col_from=col_from),
        out_shape=[jax.ShapeDtypeStruct((bsz, total - col_from, HEADS), F32),
                   jax.ShapeDtypeStruct((bsz * HEADS, total), F32)],
        compiler_params=pltpu.CompilerParams(vmem_limit_bytes=VMEM_LIMIT),
        name="cumsum_logf",
    )(*[s.reshape(bsz * HEADS, s.shape[2]) for s in segs])
    return ccol, crow.reshape(bsz, HEADS, total)


REL_PAD = 384
DIST_SPAN = 1024


def _band_bias_kernel(tbl_ref, qk_ref, kq_ref):
    j = lax.broadcasted_iota(jnp.int32, (REL_PAD, DIST_SPAN), 1)
    r = lax.broadcasted_iota(jnp.int32, (REL_PAD, DIST_SPAN), 0)
    parts = _split3(tbl_ref[...])

    def by_offset(dist):
        onehot = jnp.where(jnp.clip(dist, -REL_CLIP, REL_CLIP) + REL_CLIP == r, 1.0, 0.0).astype(BF16)
        return sum(_dot(p, onehot) for p in parts) * LOG2E

    def skewed(e_row, rows, cols, shift):
        wide = jnp.broadcast_to(e_row, (rows, DIST_SPAN))
        return pltpu.roll(wide, shift % DIST_SPAN, 1, stride=1, stride_axis=0)[:, :cols]

    e_qk = by_offset((BAND_KEYS - 1) - j)
    e_kq = by_offset(j - (ATT_BLK - 1))
    qc = lax.broadcasted_iota(jnp.int32, (ATT_BLK, BAND_KEYS), 0) // CHUNK
    kc = lax.broadcasted_iota(jnp.int32, (ATT_BLK, BAND_KEYS), 1) // CHUNK
    vis_qk = (kc >= qc) & (kc <= qc + PAST_CHUNKS)
    kc = lax.broadcasted_iota(jnp.int32, (BAND_KEYS, ATT_BLK), 0) // CHUNK
    qc = lax.broadcasted_iota(jnp.int32, (BAND_KEYS, ATT_BLK), 1) // CHUNK
    vis_kq = (kc >= qc) & (kc <= qc + PAST_CHUNKS)
    for h in range(HEADS):
        qk_ref[h] = jnp.where(vis_qk, skewed(e_qk[h:h + 1, :], ATT_BLK, BAND_KEYS, -(ATT_BLK - 1)), NEG)
        kq_ref[h] = jnp.where(vis_kq, skewed(e_kq[h:h + 1, :], BAND_KEYS, ATT_BLK, -(BAND_KEYS - 1)), NEG)


def _band_bias(rel_table):
    tbl = jnp.pad(rel_table, ((0, 0), (0, REL_PAD - REL_SIZE)))
    return pl.pallas_call(
        _band_bias_kernel,
        out_shape=[jax.ShapeDtypeStruct((HEADS, ATT_BLK, BAND_KEYS), F32),
                   jax.ShapeDtypeStruct((HEADS, BAND_KEYS, ATT_BLK), F32)],
        compiler_params=pltpu.CompilerParams(vmem_limit_bytes=VMEM_LIMIT),
        name="band_bias",
    )(tbl)


def _band_attn_prompt_kernel(q_ref, k_ref, vt_ref, bias_ref, o_ref, s_ref):
    i = pl.program_id(1)
    ones = jnp.ones((SUM_ROWS, ATT_BLK), BF16)

    def run(clipped):
        blocks = []
        for g in range(BAND_BLOCKS):
            j = i - (BAND_BLOCKS - 1) + g
            start = pl.multiple_of(jnp.maximum(j, 0) * ATT_BLK, ATT_BLK)
            masks = clipped and g < BAND_BLOCKS - 1
            blocks.append((start, jnp.where(j >= 0, 0.0, NEG) if masks else None))

        def scores(h):
            sl = slice(h * HEAD_DIM, (h + 1) * HEAD_DIM)
            q = q_ref[0, :, sl]
            for g, (start, penalty) in enumerate(blocks):
                s = _dot_nt(k_ref[0, pl.ds(start, ATT_BLK), sl], q) + bias_ref[h, g * ATT_BLK:(g + 1) * ATT_BLK, :]
                s_ref[h % (HEADS_AHEAD + 1), g] = s if penalty is None else s + penalty

        def finish(h):
            sl = slice(h * HEAD_DIM, (h + 1) * HEAD_DIM)
            s = [s_ref[h % (HEADS_AHEAD + 1), g] for g in range(BAND_BLOCKS)]
            m = functools.reduce(jnp.maximum, [jnp.max(x, axis=0, keepdims=True) for x in s])
            acc = 0.0
            for x, (start, _) in zip(s, blocks):
                vt = jnp.concatenate([vt_ref[0, sl, pl.ds(start, ATT_BLK)], ones], axis=0)
                acc = acc + _dot(vt, jnp.exp2(x - m).astype(BF16))
            o_ref[0, sl, :] = (acc[:HEAD_DIM] / acc[HEAD_DIM:HEAD_DIM + 1]).astype(BF16)

        for h in range(HEADS + HEADS_AHEAD):
            if h < HEADS:
                scores(h)
            if h >= HEADS_AHEAD:
                finish(h - HEADS_AHEAD)

    pl.when(i >= BAND_BLOCKS - 1)(lambda: run(clipped=False))
    pl.when(i < BAND_BLOCKS - 1)(lambda: run(clipped=True))


def _band_attn_prompt(q, k, vt, bias_kq):
    bsz, s, _ = q.shape
    qblk = pl.BlockSpec((1, ATT_BLK, WIDTH), lambda b, i: (b, i, 0))
    return pl.pallas_call(
        _band_attn_prompt_kernel,
        grid=(bsz, s // ATT_BLK),
        in_specs=[qblk,
                  pl.BlockSpec((1, s, WIDTH), lambda b, i: (b, 0, 0)),
                  pl.BlockSpec((1, WIDTH, s), lambda b, i: (b, 0, 0)),
                  _resident(bias_kq.shape, lambda b, i: (0, 0, 0))],
        out_specs=pl.BlockSpec((1, WIDTH, ATT_BLK), lambda b, i: (b, 0, i)),
        out_shape=jax.ShapeDtypeStruct((bsz, WIDTH, s), BF16),
        scratch_shapes=[pltpu.VMEM((HEADS_AHEAD + 1, BAND_BLOCKS, ATT_BLK, ATT_BLK), F32)],
        compiler_params=_params("parallel", "parallel"),
        name="band_attn_prompt",
    )(q, k, vt, bias_kq)


def _decay_lanes(c, key_side):
    terms = jnp.concatenate(_split3(c), axis=1)
    row = lax.broadcasted_iota(jnp.int32, (3 * HEADS, WIDTH), 0)
    lane = lax.broadcasted_iota(jnp.int32, (3 * HEADS, WIDTH), 1)
    first_one, first_term, sign = (3, 0, -1.0) if key_side else (0, 3, 1.0)
    place = jnp.where(lane == (row % HEADS) * HEAD_DIM + first_term + row // HEADS, sign, 0.0).astype(BF16)
    in_group = lax.broadcasted_iota(jnp.int32, (1, WIDTH), 1) % HEAD_DIM
    ones = jnp.where((in_group >= first_one) & (in_group < first_one + 3), 1.0, 0.0)
    return (ones + _dot(terms, place)).astype(BF16)


def _fox_attn_prompt_kernel(q_ref, k_ref, vt_ref, ccol_ref, o_ref,
                            kp_ref, qp_ref, s_ref, m_ref, acc_ref):
    i = pl.program_id(1)
    qstart = pl.multiple_of(i * ATT_BLK, ATT_BLK)

    @pl.when(i == 0)
    def _():
        extra = _decay_lanes(ccol_ref[0] * LOG2E, key_side=True)
        for h in range(HEADS):
            sl = slice(h * HEAD_DIM, (h + 1) * HEAD_DIM)
            kp_ref[:, h * PAD_DIM:(h + 1) * PAD_DIM] = jnp.concatenate([k_ref[0, :, sl], extra[:, sl]], axis=1)

    extra = _decay_lanes(ccol_ref[0, pl.ds(qstart, ATT_BLK), :] * LOG2E, key_side=False)
    for h in range(HEADS):
        sl = slice(h * HEAD_DIM, (h + 1) * HEAD_DIM)
        qp_ref[:, h * PAD_DIM:(h + 1) * PAD_DIM] = jnp.concatenate([q_ref[0, :, sl], extra[:, sl]], axis=1)

    key = lax.broadcasted_iota(jnp.int32, (ATT_BLK, ATT_BLK), 0)
    qry = lax.broadcasted_iota(jnp.int32, (ATT_BLK, ATT_BLK), 1)
    causal = key <= qry
    ones = jnp.ones((SUM_ROWS, ATT_BLK), BF16)
    m_ref[...] = jnp.full(m_ref.shape, NEG, F32)
    acc_ref[...] = jnp.zeros(acc_ref.shape, F32)

    def scores(j, h):
        start = pl.multiple_of(j * ATT_BLK, ATT_BLK)
        pad = slice(h * PAD_DIM, (h + 1) * PAD_DIM)
        return _dot_nt(kp_ref[pl.ds(start, ATT_BLK), pad], qp_ref[:, pad])

    def update(j, h, s, masked):
        start = pl.multiple_of(j * ATT_BLK, ATT_BLK)
        if masked:
            s = jnp.where(causal, s, NEG)
        m_old = m_ref[h]
        m_new = jnp.maximum(m_old, jnp.max(s, axis=0, keepdims=True))
        alpha = jnp.exp2(m_old - m_new)
        p = jnp.exp2(s - m_new)
        vt = vt_ref[0, h * HEAD_DIM:(h + 1) * HEAD_DIM, pl.ds(start, ATT_BLK)]
        acc_ref[h] = alpha * acc_ref[h] + _dot(jnp.concatenate([vt, ones], axis=0), p.astype(BF16))
        m_ref[h] = m_new

    def step(j, cur, masked=False, prefetch=True):
        for h in range(HEADS + HEADS_AHEAD):
            if prefetch and h < HEADS:
                s_ref[1 - cur, h] = scores(j + 1, h)
            if h >= HEADS_AHEAD:
                update(j, h - HEADS_AHEAD, s_ref[cur, h - HEADS_AHEAD], masked)

    for h in range(HEADS):
        s_ref[0, h] = scores(0, h)

    def pair(t, carry):
        step(2 * t, 0)
        step(2 * t + 1, 1)
        return carry

    lax.fori_loop(0, i // 2, pair, 0)

    @pl.when(i % 2 == 1)
    def _():
        step(i - 1, 0)
        step(i, 1, masked=True, prefetch=False)

    @pl.when(i % 2 == 0)
    def _():
        step(i, 0, masked=True, prefetch=False)

    for h in range(HEADS):
        sl = slice(h * HEAD_DIM, (h + 1) * HEAD_DIM)
        out = acc_ref[h, :HEAD_DIM, :] / acc_ref[h, HEAD_DIM:HEAD_DIM + 1, :]
        o_ref[0, sl, :] = out.astype(BF16)


def _fox_attn_prompt(q, k, vt, ccol):
    bsz, s, _ = q.shape
    qblk = pl.BlockSpec((1, ATT_BLK, WIDTH), lambda b, i: (b, i, 0))
    return pl.pallas_call(
        _fox_attn_prompt_kernel,
        grid=(bsz, s // ATT_BLK),
        in_specs=[qblk,
                  pl.BlockSpec((1, s, WIDTH), lambda b, i: (b, 0, 0)),
                  pl.BlockSpec((1, WIDTH, s), lambda b, i: (b, 0, 0)),
                  pl.BlockSpec((1, s, HEADS), lambda b, i: (b, 0, 0))],
        out_specs=pl.BlockSpec((1, WIDTH, ATT_BLK), lambda b, i: (b, 0, i)),
        out_shape=jax.ShapeDtypeStruct((bsz, WIDTH, s), BF16),
        scratch_shapes=[pltpu.VMEM((s, HEADS * PAD_DIM), BF16),
                        pltpu.VMEM((ATT_BLK, HEADS * PAD_DIM), BF16),
                        pltpu.VMEM((2, HEADS, ATT_BLK, ATT_BLK), F32),
                        pltpu.VMEM((HEADS, 1, ATT_BLK), F32),
                        pltpu.VMEM((HEADS, HEAD_DIM + SUM_ROWS, ATT_BLK), F32)],
        compiler_params=_params("parallel", "arbitrary"),
        name="fox_attn_prompt",
    )(q, k, vt, ccol)


def _pipelined_heads(scores, finish):
    pending = []
    for h in range(HEADS + HEADS_AHEAD):
        if h < HEADS:
            pending.append(scores(h))
        if h >= HEADS_AHEAD:
            finish(h - HEADS_AHEAD, pending.pop(0))


def _sample_attention(s_past, s_new, vct, vnt):
    m = jnp.maximum(jnp.max(s_past, axis=-1, keepdims=True), jnp.max(s_new, axis=-1, keepdims=True))
    p_past = jnp.exp2(s_past - m)
    p_new = jnp.exp2(s_new - m)
    l = jnp.sum(p_past, axis=-1, keepdims=True) + jnp.sum(p_new, axis=-1, keepdims=True)
    acc = _dot_nt(p_past.astype(BF16), vct) + _dot_nt(p_new.astype(BF16), vnt)
    return (acc / l).astype(BF16)


def _band_attn_sample_kernel(q_ref, kct_ref, vct_ref, kn_ref, vnt_ref, bias_ref, o_ref):
    t = q_ref.shape[1]
    past = kct_ref.shape[2]

    def scores(h):
        sl = slice(h * HEAD_DIM, (h + 1) * HEAD_DIM)
        q = q_ref[0, :, sl]
        s_past = _dot(q, kct_ref[0, sl, :].astype(BF16)) + bias_ref[h, 0:t, BAND - past:BAND]
        s_new = _dot_nt(q, kn_ref[0, :, sl]) + bias_ref[h, 0:t, BAND:BAND + t]
        return s_past, s_new

    def finish(h, s):
        sl = slice(h * HEAD_DIM, (h + 1) * HEAD_DIM)
        o_ref[0, :, sl] = _sample_attention(*s, vct_ref[0, sl, :].astype(BF16), vnt_ref[0, sl, :])

    _pipelined_heads(scores, finish)


def _band_attn_sample(q, k_cache_t, v_cache_t, k_new, v_new_t, bias):
    bsz, t, _ = q.shape
    past = k_cache_t.shape[2]
    new = pl.BlockSpec((1, t, WIDTH), lambda b: (b, 0, 0))
    new_t = pl.BlockSpec((1, WIDTH, t), lambda b: (b, 0, 0))
    old_t = pl.BlockSpec((1, WIDTH, past), lambda b: (b, 0, 0))
    return pl.pallas_call(
        _band_attn_sample_kernel,
        grid=(bsz,),
        in_specs=[new, old_t, old_t, new, new_t, _resident(bias.shape, lambda b: (0, 0, 0))],
        out_specs=new,
        out_shape=jax.ShapeDtypeStruct(q.shape, BF16),
        compiler_params=_params("parallel"),
        name="band_attn_sample",
    )(q, k_cache_t, v_cache_t, k_new, v_new_t, bias)


def _fox_attn_sample_kernel(q_ref, kct_ref, vct_ref, kn_ref, vnt_ref, ccol_ref, crow_ref, o_ref):
    t = q_ref.shape[1]
    past = kct_ref.shape[2]
    row = lax.broadcasted_iota(jnp.int32, (t, t), 0)
    col = lax.broadcasted_iota(jnp.int32, (t, t), 1)
    causal = col <= row

    def scores(h):
        sl = slice(h * HEAD_DIM, (h + 1) * HEAD_DIM)
        q = q_ref[0, :, sl]
        cq = ccol_ref[0, :, h:h + 1]
        s_past = _dot(q, kct_ref[0, sl, :].astype(BF16)) + (cq - crow_ref[0, h:h + 1, 0:past]) * LOG2E
        s_new = _dot_nt(q, kn_ref[0, :, sl]) + (cq - crow_ref[0, h:h + 1, past:past + t]) * LOG2E
        return s_past, jnp.where(causal, s_new, NEG)

    def finish(h, s):
        sl = slice(h * HEAD_DIM, (h + 1) * HEAD_DIM)
        o_ref[0, :, sl] = _sample_attention(*s, vct_ref[0, sl, :].astype(BF16), vnt_ref[0, sl, :])

    _pipelined_heads(scores, finish)


def _fox_attn_sample(q, k_cache_t, v_cache_t, k_new, v_new_t, ccol, crow):
    bsz, t, _ = q.shape
    past = k_cache_t.shape[2]
    assert past % t == 0
    new = pl.BlockSpec((1, t, WIDTH), lambda b: (b, 0, 0))
    new_t = pl.BlockSpec((1, WIDTH, t), lambda b: (b, 0, 0))
    old_t = pl.BlockSpec((1, WIDTH, past), lambda b: (b, 0, 0))
    return pl.pallas_call(
        _fox_attn_sample_kernel,
        grid=(bsz,),
        in_specs=[new, old_t, old_t, new, new_t,
                  pl.BlockSpec((1, t, HEADS), lambda b: (b, 0, 0)),
                  pl.BlockSpec((1, HEADS, past + t), lambda b: (b, 0, 0))],
        out_specs=new,
        out_shape=jax.ShapeDtypeStruct(q.shape, BF16),
        compiler_params=_params("parallel"),
        name="fox_attn_sample",
    )(q, k_cache_t, v_cache_t, k_new, v_new_t, ccol, crow)


def _postmix_kernel(x_ref, oa_ref, ob_ref, gpre_ref, gpost_ref, wt_ref, wpa_ref, wpb_ref, wout_ref, y_ref,
                    *, channel_major):
    tm, d = x_ref.shape
    planes = tm // SUBLANES
    gates = wt_ref.shape[0] - 2 * d
    halves = [slice(k * tm // 2, (k + 1) * tm // 2) for k in range(2)]

    def mix(rows):
        h = _rmsnorm(x_ref[rows, :], gpre_ref[...]).astype(BF16)
        gate_a = jax.nn.sigmoid(_dot_nt(h, wt_ref[gates:gates + d, :].astype(BF16)))
        gate_b = jax.nn.sigmoid(_dot_nt(h, wt_ref[gates + d:, :].astype(BF16)))
        if channel_major:
            proj = lambda o_ref, w_ref: lax.dot_general(o_ref[0, :, rows], w_ref[...], (((0,), (0,)), ((), ())),
                                                        preferred_element_type=F32)
        else:
            proj = lambda o_ref, w_ref: _dot(o_ref[rows, :], w_ref[...])
        return (gate_a * proj(oa_ref, wpa_ref) + gate_b * proj(ob_ref, wpb_ref)).astype(BF16)

    merged = [mix(rows) for rows in halves]
    outs = [_dot(m, wout_ref[...]) for m in merged]
    for k, rows in enumerate(halves):
        y = x_ref[rows, :] + _rmsnorm(outs[k], gpost_ref[...])
        subs = slice(k * SUBLANES // 2, (k + 1) * SUBLANES // 2)
        y_ref[:, subs, :] = jnp.swapaxes(y.reshape(SUBLANES // 2, planes, d), 0, 1)


def _deinterleave(planes):
    p, s, d = planes.shape
    return jnp.swapaxes(planes, 0, 1).reshape(s * p, d)


def _postmix(x, oa, ob, g_pre, g_post, w_in_t, w_pa, w_pb, w_out, seq_len):
    n, d = x.shape
    tm = _mix_tile(seq_len)
    row = lambda i: (i, 0)
    fixed = lambda i: (0, 0)
    channel_major = oa.ndim == 3
    if channel_major:
        tiles_per_seq = oa.shape[2] // tm
        o_spec = pl.BlockSpec((1, WIDTH, tm), lambda i: (i // tiles_per_seq, 0, i % tiles_per_seq))
    else:
        o_spec = pl.BlockSpec((tm, WIDTH), row)
    return pl.pallas_call(
        functools.partial(_postmix_kernel, channel_major=channel_major),
        grid=(n // tm,),
        in_specs=[pl.BlockSpec((tm, d), row),
                  o_spec,
                  o_spec,
                  pl.BlockSpec((1, d), fixed),
                  pl.BlockSpec((1, d), fixed),
                  _resident(w_in_t.shape, fixed),
                  _resident(w_pa.shape, fixed),
                  _resident(w_pb.shape, fixed),
                  _resident(w_out.shape, fixed)],
        out_specs=pl.BlockSpec((tm // SUBLANES, SUBLANES, d), lambda i: (i, 0, 0)),
        out_shape=jax.ShapeDtypeStruct((n // SUBLANES, SUBLANES, d), F32),
        compiler_params=_params("parallel"),
        name="postmix",
    )(x, oa, ob, g_pre, g_post, w_in_t, w_pa, w_pb, w_out)


def _ffn_kernel(x_ref, st_ref, gpre_ref, gpost_ref, wup_ref, cw_ref, cb_ref, wdn_ref,
                y_ref, nst_ref, hist_ref, ext_ref, h_ref, f_ref, *, nseg):
    @pl.when(pl.program_id(1) == 0)
    def _():
        hist_ref[...] = st_ref[...]

    planes, _, d = x_ref.shape
    tm = planes * SUBLANES
    d_ff = wdn_ref.shape[0]
    h_ref[...] = _rmsnorm(x_ref[...].reshape(tm, d), gpre_ref[...]).astype(BF16)
    n_chunks = d_ff // FFN_COLS
    first_sublane = lax.broadcasted_iota(jnp.int32, (SUBLANES, FFN_COLS), 0) == 0

    def parts(c):
        for part in range(2):
            yield (slice(part * d_ff + c * FFN_COLS, part * d_ff + (c + 1) * FFN_COLS),
                   slice(part * FFN_COLS, (part + 1) * FFN_COLS))

    def up(c):
        ext = ext_ref.at[c % 2]
        for cols, dst in parts(c):
            u = _dot(h_ref[...], wup_ref[:, cols]).reshape(planes, SUBLANES, FFN_COLS)
            ext[CONV_W - 1:, :, dst] = u
            for k in range(CONV_W - 1):
                last = u[planes - (CONV_W - 1) + k]
                if nseg == 1:
                    ext[k, :, dst] = jnp.where(first_sublane, hist_ref[0, k:k + 1, cols], pltpu.roll(last, 1, 0))
                    hist_ref[0, k:k + 1, cols] = last[SUBLANES - 1:, :]
                else:
                    ext[k, :, dst] = hist_ref[:, k, cols]
                    hist_ref[:, k, cols] = last

    def activate(c):
        ext = ext_ref.at[c % 2]
        halves = []
        for cols, dst in parts(c):
            y = cb_ref[:, cols]
            for tap in range(CONV_W):
                y = y + ext[tap:tap + planes, :, dst] * cw_ref[tap:tap + 1, cols]
            halves.append(y)
        act = (jax.nn.gelu(halves[0]) * halves[1]).reshape(tm, FFN_COLS)
        f_ref[:, c * FFN_COLS:(c + 1) * FFN_COLS] = act.astype(BF16)

    up(0)
    for c in range(n_chunks):
        if c + 1 < n_chunks:
            up(c + 1)
        activate(c)
    f = _dot(f_ref[...], wdn_ref[...])
    y = x_ref[...].reshape(tm, d) + _rmsnorm(f, gpost_ref[...])
    y_ref[...] = _deinterleave(y.reshape(planes, SUBLANES, d))
    nst_ref[...] = hist_ref[...]


def _ffn(x, bsz, s, state, g_pre, g_post, w_up, conv_w, conv_b, w_down):
    d = x.shape[-1]
    up = w_up.shape[1]
    tm = _mix_tile(s)
    planes = tm // SUBLANES
    if s >= tm:
        nseg, tiles = 1, s // tm
    else:
        nseg, tiles = tm // s, 1
        assert nseg == SUBLANES
    outer = bsz // nseg
    row = lambda o, t: (o * tiles + t, 0)
    fixed = lambda o, t: (0, 0)
    st_spec = pl.BlockSpec((nseg, CONV_W - 1, up), lambda o, t: (o, 0, 0))
    y, new_state = pl.pallas_call(
        functools.partial(_ffn_kernel, nseg=nseg),
        grid=(outer, tiles),
        in_specs=[pl.BlockSpec((planes, SUBLANES, d), lambda o, t: (o * tiles + t, 0, 0)),
                  st_spec,
                  pl.BlockSpec((1, d), fixed),
                  pl.BlockSpec((1, d), fixed),
                  _resident(w_up.shape, fixed),
                  pl.BlockSpec(conv_w.shape, fixed),
                  pl.BlockSpec((1, up), fixed),
                  _resident(w_down.shape, fixed)],
        out_specs=[pl.BlockSpec((tm, d), row), st_spec],
        out_shape=[jax.ShapeDtypeStruct((bsz * s, d), F32),
                   jax.ShapeDtypeStruct(state.shape, F32)],
        scratch_shapes=[pltpu.VMEM((nseg, CONV_W - 1, up), F32),
                        pltpu.VMEM((2, CONV_W - 1 + planes, SUBLANES, 2 * FFN_COLS), F32),
                        pltpu.VMEM((tm, d), BF16),
                        pltpu.VMEM((tm, w_down.shape[0]), BF16)],
        compiler_params=_params("arbitrary", "arbitrary"),
        name="conv_ffn",
    )(x, state, g_pre, g_post, w_up, conv_w, conv_b, w_down)
    return y.reshape(bsz, s, d), new_state


def _layer(x, caches, conv_state, bias, w):
    bsz, s, d = x.shape
    n = bsz * s
    x2 = x.reshape(n, d)
    keep = min(BAND, s)
    (qa, qb, ka16, kb16, vat16, vbt16, kat, vat, kbt, vbt, lft) = _inproj(
        x2, w["g_pre_mix"], w["w_in_t"], w["b_f"], seq_len=s, band_keep=keep)
    seq = lambda a: a.reshape(bsz, s, a.shape[-1])
    if caches is None:
        ccol, _ = _cumsum(lft)
        oa = _band_attn_prompt(seq(qa), seq(ka16), vat16, bias[1])
        ob = _fox_attn_prompt(seq(qb), seq(kb16), vbt16, ccol)
    else:
        ckat, cvat, ckbt, cvbt, clft = caches
        ccol, crow = _cumsum(clft, lft, col_from=clft.shape[2])
        oa = _band_attn_sample(seq(qa), ckat, cvat, seq(ka16), vat16, bias[0]).reshape(n, WIDTH)
        ob = _fox_attn_sample(seq(qb), ckbt, cvbt, seq(kb16), vbt16, ccol, crow).reshape(n, WIDTH)
    x1 = _postmix(x2, oa, ob, w["g_pre_mix"], w["g_post_mix"], w["w_in_t"], w["w_pa"], w["w_pb"], w["w_out"],
                  seq_len=s)
    y, new_conv = _ffn(x1, bsz, s, conv_state, w["g_pre_ffn"], w["g_post_ffn"],
                       w["w_up"], w["conv_w"], w["conv_b"], w["w_down"])
    heads = lambda a: a.reshape(bsz, HEADS, HEAD_DIM, a.shape[-1]).transpose(0, 3, 1, 2)
    return y, (heads(kat), heads(vat), heads(kbt), heads(vbt), lft.transpose(0, 2, 1), new_conv)


def _channel_major(cache):
    bsz, past = cache.shape[:2]
    return cache.transpose(0, 2, 3, 1).reshape(bsz, WIDTH, past)


def kernel(x_prompt, x_sample, cache_k_a, cache_v_a, cache_k_b, cache_v_b, cache_logf_b, state_conv_ffn,
           g_pre_mix, g_post_mix, g_pre_ffn, g_post_ffn, w_in, b_f, rel_table, w_proj_a, w_proj_b, w_out,
           w_up, conv_w, conv_b, w_down):
    depth = w_in.shape[0]
    up = w_up.shape[-1]
    x_p, x_s = x_prompt, x_sample
    p_states, s_states = [], []
    for l in range(depth):
        w = {
            "g_pre_mix": g_pre_mix[l][None], "g_post_mix": g_post_mix[l][None],
            "g_pre_ffn": g_pre_ffn[l][None], "g_post_ffn": g_post_ffn[l][None],
            "w_in_t": w_in[l].T,
            "b_f": b_f[l][:, None],
            "w_pa": w_proj_a[l].astype(BF16), "w_pb": w_proj_b[l].astype(BF16),
            "w_out": w_out[l].astype(BF16), "w_up": w_up[l].astype(BF16),
            "conv_w": conv_w[l], "conv_b": conv_b[l][None], "w_down": w_down[l].astype(BF16),
        }
        bias = _band_bias(rel_table[l])
        zero_state = jnp.zeros((x_p.shape[0], CONV_W - 1, up), F32)
        x_p, (ka, va, kb, vb, lf, cv) = _layer(x_p, None, zero_state, bias, w)
        p_states.append((ka, va, kb, vb, lf, cv))
        caches = (_channel_major(cache_k_a[l]), _channel_major(cache_v_a[l]), _channel_major(cache_k_b[l]),
                  _channel_major(cache_v_b[l]), cache_logf_b[l].transpose(0, 2, 1))
        x_s, st = _layer(x_s, caches, state_conv_ffn[l], bias, w)
        s_states.append(st)
    stack = lambda states: [jnp.stack(s) for s in zip(*states)]
    return (x_p, x_s, *stack(p_states), *stack(s_states))
```

```python
import functools

import jax
import jax.numpy as jnp
from jax import lax
from jax.experimental import pallas as pl
from jax.experimental.pallas import tpu as pltpu

F32, BF16 = jnp.float32, jnp.bfloat16

HEADS = 8
HEAD_DIM = 64
WIDTH = HEADS * HEAD_DIM
CHUNK = 64
PAST_CHUNKS = 8
BAND = PAST_CHUNKS * CHUNK
REL_CLIP = 128
REL_SIZE = 2 * REL_CLIP + 1
CONV_W = 3
EPS = 1e-6
NEG = -1e30
LOG2E = 1.4426950408889634

SUBLANES = 8
ATT_BLK = 256
BAND_KEYS = 3 * ATT_BLK
BAND_BLOCKS = BAND_KEYS // ATT_BLK
PAD_DIM = 2 * HEAD_DIM
SUM_ROWS = 16
HEADS_AHEAD = 2
ROW_TILE = 512
MIX_TILE = 1024
FFN_COLS = 256
CUMSUM_BLK = 256
VMEM_LIMIT = 56 * 1024 * 1024


def _params(*sem):
    return pltpu.CompilerParams(dimension_semantics=sem, vmem_limit_bytes=VMEM_LIMIT)


def _resident(shape, index_map):
    return pl.BlockSpec(shape, index_map, pipeline_mode=pl.Buffered(1))


def _mix_tile(seq_len):
    return MIX_TILE if seq_len >= MIX_TILE else SUBLANES * seq_len


def _rmsnorm(x, g):
    return x * lax.rsqrt(jnp.mean(x * x, axis=-1, keepdims=True) + EPS) * g


def _dot(a, b):
    return jnp.dot(a, b, preferred_element_type=F32)


def _dot_nt(a, b):
    return lax.dot_general(a, b, (((1,), (1,)), ((), ())), preferred_element_type=F32)


def _split3(x):
    hi = x.astype(BF16)
    r = x - hi.astype(F32)
    mid = r.astype(BF16)
    lo = (r - mid.astype(F32)).astype(BF16)
    return hi, mid, lo


def _inproj_kernel(x_ref, g_ref, w_ref, wft_ref, bf_ref,
                   qa_ref, qb_ref, ka_ref, kb_ref, vat16_ref, vbt16_ref,
                   kat_ref, vat_ref, kbt_ref, vbt_ref, lft_ref):
    h = _rmsnorm(x_ref[...], g_ref[...]).astype(BF16)
    scale = HEAD_DIM ** -0.5

    def proj(c):
        return _dot_nt(h, w_ref[c * WIDTH:(c + 1) * WIDTH, :].astype(BF16))

    def put(ref, zt):
        per_seq = zt.shape[1] // ref.shape[0]
        for sq in range(ref.shape[0]):
            ref[sq] = zt[:, sq * per_seq:(sq + 1) * per_seq]

    qa_ref[...] = (proj(0) * (scale * LOG2E)).astype(BF16)
    qb_ref[...] = (proj(3) * (scale * LOG2E)).astype(BF16)
    for c, row_ref, t_ref in ((1, ka_ref, kat_ref), (4, kb_ref, kbt_ref)):
        z = proj(c)
        row_ref[...] = z.astype(BF16)
        put(t_ref, z.T)
    for c, t16_ref, t_ref in ((2, vat16_ref, vat_ref), (5, vbt16_ref, vbt_ref)):
        zt = proj(c).T
        put(t_ref, zt)
        put(t16_ref, zt.astype(BF16))
    put(lft_ref, jax.nn.log_sigmoid(_dot_nt(wft_ref[...].astype(BF16), h) + bf_ref[...]))


def _inproj(x, g, w_in_t, b_f, seq_len, band_keep):
    n, d = x.shape
    bsz = n // seq_len
    tm = min(ROW_TILE, n)
    tiles_per_seq = max(seq_len // tm, 1)
    seqs_per_tile = max(tm // seq_len, 1)
    cols = tm // seqs_per_tile
    assert band_keep == cols
    row = lambda i: (i, 0)
    fixed = lambda i: (0, 0)
    along = lambda i: (i // tiles_per_seq, 0, i % tiles_per_seq)
    kept = lambda i: (i // tiles_per_seq, 0, 0)
    rows16 = (jax.ShapeDtypeStruct((n, WIDTH), BF16), pl.BlockSpec((tm, WIDTH), row))

    def chan(channels, length, dtype, index_map):
        return (jax.ShapeDtypeStruct((bsz, channels, length), dtype),
                pl.BlockSpec((seqs_per_tile, channels, cols), index_map))

    outs = [rows16] * 4
    outs += [chan(WIDTH, seq_len, BF16, along)] * 2
    outs += [chan(WIDTH, band_keep, F32, kept)] * 2
    outs += [chan(WIDTH, seq_len, F32, along)] * 2
    outs += [chan(HEADS, seq_len, F32, along)]
    return pl.pallas_call(
        _inproj_kernel,
        grid=(n // tm,),
        in_specs=[pl.BlockSpec((tm, d), row),
                  pl.BlockSpec((1, d), fixed),
                  _resident((6 * WIDTH, d), fixed),
                  pl.BlockSpec((HEADS, d), lambda i: (6 * WIDTH // HEADS, 0)),
                  pl.BlockSpec((HEADS, 1), fixed)],
        out_specs=[spec for _, spec in outs],
        out_shape=[shape for shape, _ in outs],
        compiler_params=_params("arbitrary"),
        name="inproj",
    )(x, g, w_in_t, w_in_t, b_f)


def _cumsum_kernel(*refs, seg_lens, col_from):
    seg_refs = refs[:len(seg_lens)]
    ccol_ref, crow_ref = refs[len(seg_lens):]
    carry = jnp.zeros((crow_ref.shape[0], 1), F32)
    off = 0
    for ref, n in zip(seg_refs, seg_lens):
        for o in range(0, n, CUMSUM_BLK):
            b = min(CUMSUM_BLK, n - o)
            r = lax.broadcasted_iota(jnp.int32, (b, b), 0)
            c = lax.broadcasted_iota(jnp.int32, (b, b), 1)
            upper = jnp.where(r <= c, 1.0, 0.0).astype(BF16)
            sums = carry
            for p in _split3(ref[:, o:o + b]):
                sums = sums + _dot(p, upper)
            crow_ref[:, off + o:off + o + b] = sums
            carry = sums[:, b - 1:b]
            if off + o >= col_from:
                by_position = sums.T
                for bb in range(ccol_ref.shape[0]):
                    ccol_ref[bb, off + o - col_from:off + o - col_from + b, :] = (
                        by_position[:, bb * HEADS:(bb + 1) * HEADS])
        off += n


def _cumsum(*segs, col_from=0):
    bsz = segs[0].shape[0]
    seg_lens = tuple(s.shape[2] for s in segs)
    total = sum(seg_lens)
    assert col_from % CUMSUM_BLK == 0 or col_from in (0, seg_lens[0])
    ccol, crow = pl.pallas_call(
        functools.partial(_cumsum_kernel, seg_lens=seg_lens, col_from=col_from),
        out_shape=[jax.ShapeDtypeStruct((bsz, total - col_from, HEADS), F32),
                   jax.ShapeDtypeStruct((bsz * HEADS, total), F32)],
        compiler_params=pltpu.CompilerParams(vmem_limit_bytes=VMEM_LIMIT),
        name="cumsum_logf",
    )(*[s.reshape(bsz * HEADS, s.shape[2]) for s in segs])
    return ccol, crow.reshape(bsz, HEADS, total)


REL_PAD = 384
DIST_SPAN = 1024


def _band_bias_kernel(tbl_ref, qk_ref, kq_ref):
    j = lax.broadcasted_iota(jnp.int32, (REL_PAD, DIST_SPAN), 1)
    r = lax.broadcasted_iota(jnp.int32, (REL_PAD, DIST_SPAN), 0)
    parts = _split3(tbl_ref[...])

    def by_offset(dist):
        onehot = jnp.where(jnp.clip(dist, -REL_CLIP, REL_CLIP) + REL_CLIP == r, 1.0, 0.0).astype(BF16)
        return sum(_dot(p, onehot) for p in parts) * LOG2E

    def skewed(e_row, rows, cols, shift):
        wide = jnp.broadcast_to(e_row, (rows, DIST_SPAN))
        return pltpu.roll(wide, shift % DIST_SPAN, 1, stride=1, stride_axis=0)[:, :cols]

    e_qk = by_offset((BAND_KEYS - 1) - j)
    e_kq = by_offset(j - (ATT_BLK - 1))
    qc = lax.broadcasted_iota(jnp.int32, (ATT_BLK, BAND_KEYS), 0) // CHUNK
    kc = lax.broadcasted_iota(jnp.int32, (ATT_BLK, BAND_KEYS), 1) // CHUNK
    vis_qk = (kc >= qc) & (kc <= qc + PAST_CHUNKS)
    kc = lax.broadcasted_iota(jnp.int32, (BAND_KEYS, ATT_BLK), 0) // CHUNK
    qc = lax.broadcasted_iota(jnp.int32, (BAND_KEYS, ATT_BLK), 1) // CHUNK
    vis_kq = (kc >= qc) & (kc <= qc + PAST_CHUNKS)
    for h in range(HEADS):
        qk_ref[h] = jnp.where(vis_qk, skewed(e_qk[h:h + 1, :], ATT_BLK, BAND_KEYS, -(ATT_BLK - 1)), NEG)
        kq_ref[h] = jnp.where(vis_kq, skewed(e_kq[h:h + 1, :], BAND_KEYS, ATT_BLK, -(BAND_KEYS - 1)), NEG)


def _band_bias(rel_table):
    tbl = jnp.pad(rel_table, ((0, 0), (0, REL_PAD - REL_SIZE)))
    return pl.pallas_call(
        _band_bias_kernel,
        out_shape=[jax.ShapeDtypeStruct((HEADS, ATT_BLK, BAND_KEYS), F32),
                   jax.ShapeDtypeStruct((HEADS, BAND_KEYS, ATT_BLK), F32)],
        compiler_params=pltpu.CompilerParams(vmem_limit_bytes=VMEM_LIMIT),
        name="band_bias",
    )(tbl)


def _band_attn_prompt_kernel(q_ref, k_ref, vt_ref, bias_ref, o_ref, s_ref):
    i = pl.program_id(1)
    ones = jnp.ones((SUM_ROWS, ATT_BLK), BF16)

    def run(clipped):
        blocks = []
        for g in range(BAND_BLOCKS):
            j = i - (BAND_BLOCKS - 1) + g
            start = pl.multiple_of(jnp.maximum(j, 0) * ATT_BLK, ATT_BLK)
            masks = clipped and g < BAND_BLOCKS - 1
            blocks.append((start, jnp.where(j >= 0, 0.0, NEG) if masks else None))

        def scores(h):
            sl = slice(h * HEAD_DIM, (h + 1) * HEAD_DIM)
            q = q_ref[0, :, sl]
            for g, (start, penalty) in enumerate(blocks):
                s = _dot_nt(k_ref[0, pl.ds(start, ATT_BLK), sl], q) + bias_ref[h, g * ATT_BLK:(g + 1) * ATT_BLK, :]
                s_ref[h % (HEADS_AHEAD + 1), g] = s if penalty is None else s + penalty

        def finish(h):
            sl = slice(h * HEAD_DIM, (h + 1) * HEAD_DIM)
            s = [s_ref[h % (HEADS_AHEAD + 1), g] for g in range(BAND_BLOCKS)]
            m = functools.reduce(jnp.maximum, [jnp.max(x, axis=0, keepdims=True) for x in s])
            acc = 0.0
            for x, (start, _) in zip(s, blocks):
                vt = jnp.concatenate([vt_ref[0, sl, pl.ds(start, ATT_BLK)], ones], axis=0)
                acc = acc + _dot(vt, jnp.exp2(x - m).astype(BF16))
            o_ref[0, sl, :] = (acc[:HEAD_DIM] / acc[HEAD_DIM:HEAD_DIM + 1]).astype(BF16)

        for h in range(HEADS + HEADS_AHEAD):
            if h < HEADS:
                scores(h)
            if h >= HEADS_AHEAD:
                finish(h - HEADS_AHEAD)

    pl.when(i >= BAND_BLOCKS - 1)(lambda: run(clipped=False))
    pl.when(i < BAND_BLOCKS - 1)(lambda: run(clipped=True))


def _band_attn_prompt(q, k, vt, bias_kq):
    bsz, s, _ = q.shape
    qblk = pl.BlockSpec((1, ATT_BLK, WIDTH), lambda b, i: (b, i, 0))
    return pl.pallas_call(
        _band_attn_prompt_kernel,
        grid=(bsz, s // ATT_BLK),
        in_specs=[qblk,
                  pl.BlockSpec((1, s, WIDTH), lambda b, i: (b, 0, 0)),
                  pl.BlockSpec((1, WIDTH, s), lambda b, i: (b, 0, 0)),
                  _resident(bias_kq.shape, lambda b, i: (0, 0, 0))],
        out_specs=pl.BlockSpec((1, WIDTH, ATT_BLK), lambda b, i: (b, 0, i)),
        out_shape=jax.ShapeDtypeStruct((bsz, WIDTH, s), BF16),
        scratch_shapes=[pltpu.VMEM((HEADS_AHEAD + 1, BAND_BLOCKS, ATT_BLK, ATT_BLK), F32)],
        compiler_params=_params("parallel", "parallel"),
        name="band_attn_prompt",
    )(q, k, vt, bias_kq)


def _decay_lanes(c, key_side):
    terms = jnp.concatenate(_split3(c), axis=1)
    row = lax.broadcasted_iota(jnp.int32, (3 * HEADS, WIDTH), 0)
    lane = lax.broadcasted_iota(jnp.int32, (3 * HEADS, WIDTH), 1)
    first_one, first_term, sign = (3, 0, -1.0) if key_side else (0, 3, 1.0)
    place = jnp.where(lane == (row % HEADS) * HEAD_DIM + first_term + row // HEADS, sign, 0.0).astype(BF16)
    in_group = lax.broadcasted_iota(jnp.int32, (1, WIDTH), 1) % HEAD_DIM
    ones = jnp.where((in_group >= first_one) & (in_group < first_one + 3), 1.0, 0.0)
    return (ones + _dot(terms, place)).astype(BF16)


def _fox_attn_prompt_kernel(q_ref, k_ref, vt_ref, ccol_ref, o_ref,
                            kp_ref, qp_ref, s_ref, m_ref, acc_ref):
    i = pl.program_id(1)
    qstart = pl.multiple_of(i * ATT_BLK, ATT_BLK)

    @pl.when(i == 0)
    def _():
        extra = _decay_lanes(ccol_ref[0] * LOG2E, key_side=True)
        for h in range(HEADS):
            sl = slice(h * HEAD_DIM, (h + 1) * HEAD_DIM)
            kp_ref[:, h * PAD_DIM:(h + 1) * PAD_DIM] = jnp.concatenate([k_ref[0, :, sl], extra[:, sl]], axis=1)

    extra = _decay_lanes(ccol_ref[0, pl.ds(qstart, ATT_BLK), :] * LOG2E, key_side=False)
    for h in range(HEADS):
        sl = slice(h * HEAD_DIM, (h + 1) * HEAD_DIM)
        qp_ref[:, h * PAD_DIM:(h + 1) * PAD_DIM] = jnp.concatenate([q_ref[0, :, sl], extra[:, sl]], axis=1)

    key = lax.broadcasted_iota(jnp.int32, (ATT_BLK, ATT_BLK), 0)
    qry = lax.broadcasted_iota(jnp.int32, (ATT_BLK, ATT_BLK), 1)
    causal = key <= qry
    ones = jnp.ones((SUM_ROWS, ATT_BLK), BF16)
    m_ref[...] = jnp.full(m_ref.shape, NEG, F32)
    acc_ref[...] = jnp.zeros(acc_ref.shape, F32)

    def scores(j, h):
        start = pl.multiple_of(j * ATT_BLK, ATT_BLK)
        pad = slice(h * PAD_DIM, (h + 1) * PAD_DIM)
        return _dot_nt(kp_ref[pl.ds(start, ATT_BLK), pad], qp_ref[:, pad])

    def update(j, h, s, masked):
        start = pl.multiple_of(j * ATT_BLK, ATT_BLK)
        if masked:
            s = jnp.where(causal, s, NEG)
        m_old = m_ref[h]
        m_new = jnp.maximum(m_old, jnp.max(s, axis=0, keepdims=True))
        alpha = jnp.exp2(m_old - m_new)
        p = jnp.exp2(s - m_new)
        vt = vt_ref[0, h * HEAD_DIM:(h + 1) * HEAD_DIM, pl.ds(start, ATT_BLK)]
        acc_ref[h] = alpha * acc_ref[h] + _dot(jnp.concatenate([vt, ones], axis=0), p.astype(BF16))
        m_ref[h] = m_new

    def step(j, cur, masked=False, prefetch=True):
        for h in range(HEADS + HEADS_AHEAD):
            if prefetch and h < HEADS:
                s_ref[1 - cur, h] = scores(j + 1, h)
            if h >= HEADS_AHEAD:
                update(j, h - HEADS_AHEAD, s_ref[cur, h - HEADS_AHEAD], masked)

    for h in range(HEADS):
        s_ref[0, h] = scores(0, h)

    def pair(t, carry):
        step(2 * t, 0)
        step(2 * t + 1, 1)
        return carry

    lax.fori_loop(0, i // 2, pair, 0)

    @pl.when(i % 2 == 1)
    def _():
        step(i - 1, 0)
        step(i, 1, masked=True, prefetch=False)

    @pl.when(i % 2 == 0)
    def _():
        step(i, 0, masked=True, prefetch=False)

    for h in range(HEADS):
        sl = slice(h * HEAD_DIM, (h + 1) * HEAD_DIM)
        out = acc_ref[h, :HEAD_DIM, :] / acc_ref[h, HEAD_DIM:HEAD_DIM + 1, :]
        o_ref[0, sl, :] = out.astype(BF16)


def _fox_attn_prompt(q, k, vt, ccol):
    bsz, s, _ = q.shape
    qblk = pl.BlockSpec((1, ATT_BLK, WIDTH), lambda b, i: (b, i, 0))
    return pl.pallas_call(
        _fox_attn_prompt_kernel,
        grid=(bsz, s // ATT_BLK),
        in_specs=[qblk,
                  pl.BlockSpec((1, s, WIDTH), lambda b, i: (b, 0, 0)),
                  pl.BlockSpec((1, WIDTH, s), lambda b, i: (b, 0, 0)),
                  pl.BlockSpec((1, s, HEADS), lambda b, i: (b, 0, 0))],
        out_specs=pl.BlockSpec((1, WIDTH, ATT_BLK), lambda b, i: (b, 0, i)),
        out_shape=jax.ShapeDtypeStruct((bsz, WIDTH, s), BF16),
        scratch_shapes=[pltpu.VMEM((s, HEADS * PAD_DIM), BF16),
                        pltpu.VMEM((ATT_BLK, HEADS * PAD_DIM), BF16),
                        pltpu.VMEM((2, HEADS, ATT_BLK, ATT_BLK), F32),
                        pltpu.VMEM((HEADS, 1, ATT_BLK), F32),
                        pltpu.VMEM((HEADS, HEAD_DIM + SUM_ROWS, ATT_BLK), F32)],
        compiler_params=_params("parallel", "arbitrary"),
        name="fox_attn_prompt",
    )(q, k, vt, ccol)


def _sample_attention(s_past, s_new, vct, vnt):
    m = jnp.maximum(jnp.max(s_past, axis=-1, keepdims=True), jnp.max(s_new, axis=-1, keepdims=True))
    p_past = jnp.exp2(s_past - m)
    p_new = jnp.exp2(s_new - m)
    l = jnp.sum(p_past, axis=-1, keepdims=True) + jnp.sum(p_new, axis=-1, keepdims=True)
    acc = _dot_nt(p_past.astype(BF16), vct) + _dot_nt(p_new.astype(BF16), vnt)
    return (acc / l).astype(BF16)


def _sample_attn_kernel(qa_ref, kcat_ref, vcat_ref, kna_ref, vnat_ref, bias_ref,
                        qb_ref, kcbt_ref, vcbt_ref, knb_ref, vnbt_ref, ccol_ref, crow_ref, oa_ref, ob_ref):
    t = qa_ref.shape[1]
    past_a = kcat_ref.shape[2]
    past_b = kcbt_ref.shape[2]
    row = lax.broadcasted_iota(jnp.int32, (t, t), 0)
    col = lax.broadcasted_iota(jnp.int32, (t, t), 1)
    causal = col <= row

    def scores(idx):
        h = idx // 2
        sl = slice(h * HEAD_DIM, (h + 1) * HEAD_DIM)
        if idx % 2 == 0:
            q = qa_ref[0, :, sl]
            s_past = _dot(q, kcat_ref[0, sl, :].astype(BF16)) + bias_ref[h, 0:t, BAND - past_a:BAND]
            s_new = _dot_nt(q, kna_ref[0, :, sl]) + bias_ref[h, 0:t, BAND:BAND + t]
            return s_past, s_new
        q = qb_ref[0, :, sl]
        cq = ccol_ref[0, :, h:h + 1]
        s_past = _dot(q, kcbt_ref[0, sl, :].astype(BF16)) + (cq - crow_ref[0, h:h + 1, 0:past_b]) * LOG2E
        s_new = _dot_nt(q, knb_ref[0, :, sl]) + (cq - crow_ref[0, h:h + 1, past_b:past_b + t]) * LOG2E
        return s_past, jnp.where(causal, s_new, NEG)

    def finish(idx, s):
        h = idx // 2
        sl = slice(h * HEAD_DIM, (h + 1) * HEAD_DIM)
        if idx % 2 == 0:
            oa_ref[0, :, sl] = _sample_attention(*s, vcat_ref[0, sl, :].astype(BF16), vnat_ref[0, sl, :])
        else:
            ob_ref[0, :, sl] = _sample_attention(*s, vcbt_ref[0, sl, :].astype(BF16), vnbt_ref[0, sl, :])

    pending = []
    for idx in range(2 * HEADS + HEADS_AHEAD):
        if idx < 2 * HEADS:
            pending.append(scores(idx))
        if idx >= HEADS_AHEAD:
            finish(idx - HEADS_AHEAD, pending.pop(0))


def _sample_attn(qa, kcat, vcat, kna, vnat, bias, qb, kcbt, vcbt, knb, vnbt, ccol, crow):
    bsz, t, _ = qa.shape
    new = pl.BlockSpec((1, t, WIDTH), lambda b: (b, 0, 0))
    new_t = pl.BlockSpec((1, WIDTH, t), lambda b: (b, 0, 0))
    old = lambda a: pl.BlockSpec((1, WIDTH, a.shape[2]), lambda b: (b, 0, 0))
    return pl.pallas_call(
        _sample_attn_kernel,
        grid=(bsz,),
        in_specs=[new, old(kcat), old(vcat), new, new_t, _resident(bias.shape, lambda b: (0, 0, 0)),
                  new, old(kcbt), old(vcbt), new, new_t,
                  pl.BlockSpec((1, t, HEADS), lambda b: (b, 0, 0)),
                  pl.BlockSpec((1, HEADS, crow.shape[2]), lambda b: (b, 0, 0))],
        out_specs=[new, new],
        out_shape=[jax.ShapeDtypeStruct(qa.shape, BF16)] * 2,
        compiler_params=_params("parallel"),
        name="sample_attn",
    )(qa, kcat, vcat, kna, vnat, bias, qb, kcbt, vcbt, knb, vnbt, ccol, crow)


def _postmix_kernel(x_ref, oa_ref, ob_ref, gpre_ref, gpost_ref, wt_ref, wpa_ref, wpb_ref, wout_ref, y_ref,
                    *, channel_major):
    tm, d = x_ref.shape
    planes = tm // SUBLANES
    gates = wt_ref.shape[0] - 2 * d
    halves = [slice(k * tm // 2, (k + 1) * tm // 2) for k in range(2)]

    def mix(rows):
        h = _rmsnorm(x_ref[rows, :], gpre_ref[...]).astype(BF16)
        gate_a = jax.nn.sigmoid(_dot_nt(h, wt_ref[gates:gates + d, :].astype(BF16)))
        gate_b = jax.nn.sigmoid(_dot_nt(h, wt_ref[gates + d:, :].astype(BF16)))
        if channel_major:
            proj = lambda o_ref, w_ref: lax.dot_general(o_ref[0, :, rows], w_ref[...], (((0,), (0,)), ((), ())),
                                                        preferred_element_type=F32)
        else:
            proj = lambda o_ref, w_ref: _dot(o_ref[rows, :], w_ref[...])
        return (gate_a * proj(oa_ref, wpa_ref) + gate_b * proj(ob_ref, wpb_ref)).astype(BF16)

    merged = [mix(rows) for rows in halves]
    outs = [_dot(m, wout_ref[...]) for m in merged]
    for k, rows in enumerate(halves):
        y = x_ref[rows, :] + _rmsnorm(outs[k], gpost_ref[...])
        subs = slice(k * SUBLANES // 2, (k + 1) * SUBLANES // 2)
        y_ref[:, subs, :] = jnp.swapaxes(y.reshape(SUBLANES // 2, planes, d), 0, 1)


def _deinterleave(planes):
    p, s, d = planes.shape
    return jnp.swapaxes(planes, 0, 1).reshape(s * p, d)


def _postmix(x, oa, ob, g_pre, g_post, w_in_t, w_pa, w_pb, w_out, seq_len):
    n, d = x.shape
    tm = _mix_tile(seq_len)
    row = lambda i: (i, 0)
    fixed = lambda i: (0, 0)
    channel_major = oa.ndim == 3
    if channel_major:
        tiles_per_seq = oa.shape[2] // tm
        o_spec = pl.BlockSpec((1, WIDTH, tm), lambda i: (i // tiles_per_seq, 0, i % tiles_per_seq))
    else:
        o_spec = pl.BlockSpec((tm, WIDTH), row)
    return pl.pallas_call(
        functools.partial(_postmix_kernel, channel_major=channel_major),
        grid=(n // tm,),
        in_specs=[pl.BlockSpec((tm, d), row),
                  o_spec,
                  o_spec,
                  pl.BlockSpec((1, d), fixed),
                  pl.BlockSpec((1, d), fixed),
                  _resident(w_in_t.shape, fixed),
                  _resident(w_pa.shape, fixed),
                  _resident(w_pb.shape, fixed),
                  _resident(w_out.shape, fixed)],
        out_specs=pl.BlockSpec((tm // SUBLANES, SUBLANES, d), lambda i: (i, 0, 0)),
        out_shape=jax.ShapeDtypeStruct((n // SUBLANES, SUBLANES, d), F32),
        compiler_params=_params("parallel"),
        name="postmix",
    )(x, oa, ob, g_pre, g_post, w_in_t, w_pa, w_pb, w_out)


def _ffn_kernel(x_ref, st_ref, gpre_ref, gpost_ref, wup_ref, cw_ref, cb_ref, wdn_ref,
                y_ref, nst_ref, hist_ref, ext_ref, h_ref, f_ref, *, nseg):
    @pl.when(pl.program_id(1) == 0)
    def _():
        hist_ref[...] = st_ref[...]

    planes, _, d = x_ref.shape
    tm = planes * SUBLANES
    d_ff = wdn_ref.shape[0]
    h_ref[...] = _rmsnorm(x_ref[...].reshape(tm, d), gpre_ref[...]).astype(BF16)
    n_chunks = d_ff // FFN_COLS
    first_sublane = lax.broadcasted_iota(jnp.int32, (SUBLANES, FFN_COLS), 0) == 0

    def parts(c):
        for part in range(2):
            yield (slice(part * d_ff + c * FFN_COLS, part * d_ff + (c + 1) * FFN_COLS),
                   slice(part * FFN_COLS, (part + 1) * FFN_COLS))

    def up(c):
        ext = ext_ref.at[c % 2]
        for cols, dst in parts(c):
            u = _dot(h_ref[...], wup_ref[:, cols]).reshape(planes, SUBLANES, FFN_COLS)
            ext[CONV_W - 1:, :, dst] = u
            for k in range(CONV_W - 1):
                last = u[planes - (CONV_W - 1) + k]
                if nseg == 1:
                    ext[k, :, dst] = jnp.where(first_sublane, hist_ref[0, k:k + 1, cols], pltpu.roll(last, 1, 0))
                    hist_ref[0, k:k + 1, cols] = last[SUBLANES - 1:, :]
                else:
                    ext[k, :, dst] = hist_ref[:, k, cols]
                    hist_ref[:, k, cols] = last

    def activate(c):
        ext = ext_ref.at[c % 2]
        halves = []
        for cols, dst in parts(c):
            y = cb_ref[:, cols]
            for tap in range(CONV_W):
                y = y + ext[tap:tap + planes, :, dst] * cw_ref[tap:tap + 1, cols]
            halves.append(y)
        act = (jax.nn.gelu(halves[0]) * halves[1]).reshape(tm, FFN_COLS)
        f_ref[:, c * FFN_COLS:(c + 1) * FFN_COLS] = act.astype(BF16)

    up(0)
    for c in range(n_chunks):
        if c + 1 < n_chunks:
            up(c + 1)
        activate(c)
    f = _dot(f_ref[...], wdn_ref[...])
    y = x_ref[...].reshape(tm, d) + _rmsnorm(f, gpost_ref[...])
    y_ref[...] = _deinterleave(y.reshape(planes, SUBLANES, d))
    nst_ref[...] = hist_ref[...]


def _ffn(x, bsz, s, state, g_pre, g_post, w_up, conv_w, conv_b, w_down):
    d = x.shape[-1]
    up = w_up.shape[1]
    tm = _mix_tile(s)
    planes = tm // SUBLANES
    if s >= tm:
        nseg, tiles = 1, s // tm
    else:
        nseg, tiles = tm // s, 1
        assert nseg == SUBLANES
    outer = bsz // nseg
    row = lambda o, t: (o * tiles + t, 0)
    fixed = lambda o, t: (0, 0)
    st_spec = pl.BlockSpec((nseg, CONV_W - 1, up), lambda o, t: (o, 0, 0))
    y, new_state = pl.pallas_call(
        functools.partial(_ffn_kernel, nseg=nseg),
        grid=(outer, tiles),
        in_specs=[pl.BlockSpec((planes, SUBLANES, d), lambda o, t: (o * tiles + t, 0, 0)),
                  st_spec,
                  pl.BlockSpec((1, d), fixed),
                  pl.BlockSpec((1, d), fixed),
                  _resident(w_up.shape, fixed),
                  pl.BlockSpec(conv_w.shape, fixed),
                  pl.BlockSpec((1, up), fixed),
                  _resident(w_down.shape, fixed)],
        out_specs=[pl.BlockSpec((tm, d), row), st_spec],
        out_shape=[jax.ShapeDtypeStruct((bsz * s, d), F32),
                   jax.ShapeDtypeStruct(state.shape, F32)],
        scratch_shapes=[pltpu.VMEM((nseg, CONV_W - 1, up), F32),
                        pltpu.VMEM((2, CONV_W - 1 + planes, SUBLANES, 2 * FFN_COLS), F32),
                        pltpu.VMEM((tm, d), BF16),
                        pltpu.VMEM((tm, w_down.shape[0]), BF16)],
        compiler_params=_params("arbitrary", "arbitrary"),
        name="conv_ffn",
    )(x, state, g_pre, g_post, w_up, conv_w, conv_b, w_down)
    return y.reshape(bsz, s, d), new_state


def _layer(x, caches, conv_state, bias, w):
    bsz, s, d = x.shape
    n = bsz * s
    x2 = x.reshape(n, d)
    keep = min(BAND, s)
    (qa, qb, ka16, kb16, vat16, vbt16, kat, vat, kbt, vbt, lft) = _inproj(
        x2, w["g_pre_mix"], w["w_in_t"], w["b_f"], seq_len=s, band_keep=keep)
    seq = lambda a: a.reshape(bsz, s, a.shape[-1])
    if caches is None:
        ccol, _ = _cumsum(lft)
        oa = _band_attn_prompt(seq(qa), seq(ka16), vat16, bias[1])
        ob = _fox_attn_prompt(seq(qb), seq(kb16), vbt16, ccol)
    else:
        ckat, cvat, ckbt, cvbt, clft = caches
        ccol, crow = _cumsum(clft, lft, col_from=clft.shape[2])
        oa, ob = _sample_attn(seq(qa), ckat, cvat, seq(ka16), vat16, bias[0],
                              seq(qb), ckbt, cvbt, seq(kb16), vbt16, ccol, crow)
        oa, ob = oa.reshape(n, WIDTH), ob.reshape(n, WIDTH)
    x1 = _postmix(x2, oa, ob, w["g_pre_mix"], w["g_post_mix"], w["w_in_t"], w["w_pa"], w["w_pb"], w["w_out"],
                  seq_len=s)
    y, new_conv = _ffn(x1, bsz, s, conv_state, w["g_pre_ffn"], w["g_post_ffn"],
                       w["w_up"], w["conv_w"], w["conv_b"], w["w_down"])
    heads = lambda a: a.reshape(bsz, HEADS, HEAD_DIM, a.shape[-1]).transpose(0, 3, 1, 2)
    return y, (heads(kat), heads(vat), heads(kbt), heads(vbt), lft.transpose(0, 2, 1), new_conv)


def _channel_major(cache):
    bsz, past = cache.shape[:2]
    return cache.transpose(0, 2, 3, 1).reshape(bsz, WIDTH, past)


def kernel(x_prompt, x_sample, cache_k_a, cache_v_a, cache_k_b, cache_v_b, cache_logf_b, state_conv_ffn,
           g_pre_mix, g_post_mix, g_pre_ffn, g_post_ffn, w_in, b_f, rel_table, w_proj_a, w_proj_b, w_out,
           w_up, conv_w, conv_b, w_down):
    depth = w_in.shape[0]
    up = w_up.shape[-1]
    x_p, x_s = x_prompt, x_sample
    p_states, s_states = [], []
    for l in range(depth):
        w = {
            "g_pre_mix": g_pre_mix[l][None], "g_post_mix": g_post_mix[l][None],
            "g_pre_ffn": g_pre_ffn[l][None], "g_post_ffn": g_post_ffn[l][None],
            "w_in_t": w_in[l].T,
            "b_f": b_f[l][:, None],
            "w_pa": w_proj_a[l].astype(BF16), "w_pb": w_proj_b[l].astype(BF16),
            "w_out": w_out[l].astype(BF16), "w_up": w_up[l].astype(BF16),
            "conv_w": conv_w[l], "conv_b": conv_b[l][None], "w_down": w_down[l].astype(BF16),
        }
        bias = _band_bias(rel_table[l])
        zero_state = jnp.zeros((x_p.shape[0], CONV_W - 1, up), F32)
        x_p, (ka, va, kb, vb, lf, cv) = _layer(x_p, None, zero_state, bias, w)
        p_states.append((ka, va, kb, vb, lf, cv))
        caches = (_channel_major(cache_k_a[l]), _channel_major(cache_v_a[l]), _channel_major(cache_k_b[l]),
                  _channel_major(cache_v_b[l]), cache_logf_b[l].transpose(0, 2, 1))
        x_s, st = _layer(x_s, caches, state_conv_ffn[l], bias, w)
        s_states.append(st)
    stack = lambda states: [jnp.stack(s) for s in zip(*states)]
    return (x_p, x_s, *stack(p_states), *stack(s_states))
```

```python
import functools

import jax
import jax.numpy as jnp
from jax import lax
from jax.experimental import pallas as pl
from jax.experimental.pallas import tpu as pltpu

F32, BF16 = jnp.float32, jnp.bfloat16

HEADS = 8
HEAD_DIM = 64
WIDTH = HEADS * HEAD_DIM
CHUNK = 64
PAST_CHUNKS = 8
BAND = PAST_CHUNKS * CHUNK
REL_CLIP = 128
REL_SIZE = 2 * REL_CLIP + 1
CONV_W = 3
EPS = 1e-6
NEG = -1e30
LOG2E = 1.4426950408889634

SUBLANES = 8
ATT_BLK = 256
BAND_KEYS = 3 * ATT_BLK
BAND_BLOCKS = BAND_KEYS // ATT_BLK
PAD_DIM = 2 * HEAD_DIM
SUM_ROWS = 16
HEADS_AHEAD = 2
SAMPLE_AHEAD = 3
ROW_TILE = 512
MIX_TILE = 1024
FFN_COLS = 256
CUMSUM_BLK = 256
VMEM_LIMIT = 56 * 1024 * 1024


def _params(*sem):
    return pltpu.CompilerParams(dimension_semantics=sem, vmem_limit_bytes=VMEM_LIMIT)


def _resident(shape, index_map):
    return pl.BlockSpec(shape, index_map, pipeline_mode=pl.Buffered(1))


def _mix_tile(seq_len):
    return MIX_TILE if seq_len >= MIX_TILE else SUBLANES * seq_len


def _rmsnorm(x, g):
    return x * lax.rsqrt(jnp.mean(x * x, axis=-1, keepdims=True) + EPS) * g


def _dot(a, b):
    return jnp.dot(a, b, preferred_element_type=F32)


def _dot_nt(a, b):
    return lax.dot_general(a, b, (((1,), (1,)), ((), ())), preferred_element_type=F32)


def _split3(x):
    hi = x.astype(BF16)
    r = x - hi.astype(F32)
    mid = r.astype(BF16)
    lo = (r - mid.astype(F32)).astype(BF16)
    return hi, mid, lo


def _inproj_kernel(x_ref, g_ref, w_ref, wft_ref, bf_ref,
                   qa_ref, qb_ref, ka_ref, kb_ref, vat16_ref, vbt16_ref,
                   kat_ref, vat_ref, kbt_ref, vbt_ref, lft_ref):
    h = _rmsnorm(x_ref[...], g_ref[...]).astype(BF16)
    scale = HEAD_DIM ** -0.5

    def proj(c):
        return _dot_nt(h, w_ref[c * WIDTH:(c + 1) * WIDTH, :].astype(BF16))

    def put(ref, zt):
        per_seq = zt.shape[1] // ref.shape[0]
        for sq in range(ref.shape[0]):
            ref[sq] = zt[:, sq * per_seq:(sq + 1) * per_seq]

    qa_ref[...] = (proj(0) * (scale * LOG2E)).astype(BF16)
    qb_ref[...] = (proj(3) * (scale * LOG2E)).astype(BF16)
    for c, row_ref, t_ref in ((1, ka_ref, kat_ref), (4, kb_ref, kbt_ref)):
        z = proj(c)
        row_ref[...] = z.astype(BF16)
        put(t_ref, z.T)
    for c, t16_ref, t_ref in ((2, vat16_ref, vat_ref), (5, vbt16_ref, vbt_ref)):
        zt = proj(c).T
        put(t_ref, zt)
        put(t16_ref, zt.astype(BF16))
    put(lft_ref, jax.nn.log_sigmoid(_dot_nt(wft_ref[...].astype(BF16), h) + bf_ref[...]))


def _inproj(x, g, w_in_t, b_f, seq_len, band_keep):
    n, d = x.shape
    bsz = n // seq_len
    tm = min(ROW_TILE, n)
    tiles_per_seq = max(seq_len // tm, 1)
    seqs_per_tile = max(tm // seq_len, 1)
    cols = tm // seqs_per_tile
    assert band_keep == cols
    row = lambda i: (i, 0)
    fixed = lambda i: (0, 0)
    along = lambda i: (i // tiles_per_seq, 0, i % tiles_per_seq)
    kept = lambda i: (i // tiles_per_seq, 0, 0)
    rows16 = (jax.ShapeDtypeStruct((n, WIDTH), BF16), pl.BlockSpec((tm, WIDTH), row))

    def chan(channels, length, dtype, index_map):
        return (jax.ShapeDtypeStruct((bsz, channels, length), dtype),
                pl.BlockSpec((seqs_per_tile, channels, cols), index_map))

    outs = [rows16] * 4
    outs += [chan(WIDTH, seq_len, BF16, along)] * 2
    outs += [chan(WIDTH, band_keep, F32, kept)] * 2
    outs += [chan(WIDTH, seq_len, F32, along)] * 2
    outs += [chan(HEADS, seq_len, F32, along)]
    return pl.pallas_call(
        _inproj_kernel,
        grid=(n // tm,),
        in_specs=[pl.BlockSpec((tm, d), row),
                  pl.BlockSpec((1, d), fixed),
                  _resident((6 * WIDTH, d), fixed),
                  pl.BlockSpec((HEADS, d), lambda i: (6 * WIDTH // HEADS, 0)),
                  pl.BlockSpec((HEADS, 1), fixed)],
        out_specs=[spec for _, spec in outs],
        out_shape=[shape for shape, _ in outs],
        compiler_params=_params("arbitrary"),
        name="inproj",
    )(x, g, w_in_t, w_in_t, b_f)


def _cumsum_kernel(*refs, seg_lens, col_from):
    seg_refs = refs[:len(seg_lens)]
    ccol_ref, crow_ref = refs[len(seg_lens):]
    carry = jnp.zeros((crow_ref.shape[0], 1), F32)
    off = 0
    for ref, n in zip(seg_refs, seg_lens):
        for o in range(0, n, CUMSUM_BLK):
            b = min(CUMSUM_BLK, n - o)
            r = lax.broadcasted_iota(jnp.int32, (b, b), 0)
            c = lax.broadcasted_iota(jnp.int32, (b, b), 1)
            upper = jnp.where(r <= c, 1.0, 0.0).astype(BF16)
            sums = carry
            for p in _split3(ref[:, o:o + b]):
                sums = sums + _dot(p, upper)
            crow_ref[:, off + o:off + o + b] = sums
            carry = sums[:, b - 1:b]
            if off + o >= col_from:
                by_position = sums.T
                for bb in range(ccol_ref.shape[0]):
                    ccol_ref[bb, off + o - col_from:off + o - col_from + b, :] = (
                        by_position[:, bb * HEADS:(bb + 1) * HEADS])
        off += n


def _cumsum(*segs, col_from=0):
    bsz = segs[0].shape[0]
    seg_lens = tuple(s.shape[2] for s in segs)
    total = sum(seg_lens)
    assert col_from % CUMSUM_BLK == 0 or col_from in (0, seg_lens[0])
    ccol, crow = pl.pallas_call(
        functools.partial(_cumsum_kernel, seg_lens=seg_lens, col_from=col_from),
        out_shape=[jax.ShapeDtypeStruct((bsz, total - col_from, HEADS), F32),
                   jax.ShapeDtypeStruct((bsz * HEADS, total), F32)],
        compiler_params=pltpu.CompilerParams(vmem_limit_bytes=VMEM_LIMIT),
        name="cumsum_logf",
    )(*[s.reshape(bsz * HEADS, s.shape[2]) for s in segs])
    return ccol, crow.reshape(bsz, HEADS, total)


REL_PAD = 384
DIST_SPAN = 1024


def _band_bias_kernel(tbl_ref, qk_ref, kq_ref):
    j = lax.broadcasted_iota(jnp.int32, (REL_PAD, DIST_SPAN), 1)
    r = lax.broadcasted_iota(jnp.int32, (REL_PAD, DIST_SPAN), 0)
    parts = _split3(tbl_ref[...])

    def by_offset(dist):
        onehot = jnp.where(jnp.clip(dist, -REL_CLIP, REL_CLIP) + REL_CLIP == r, 1.0, 0.0).astype(BF16)
        return sum(_dot(p, onehot) for p in parts) * LOG2E

    def skewed(e_row, rows, cols, shift):
        wide = jnp.broadcast_to(e_row, (rows, DIST_SPAN))
        return pltpu.roll(wide, shift % DIST_SPAN, 1, stride=1, stride_axis=0)[:, :cols]

    e_qk = by_offset((BAND_KEYS - 1) - j)
    e_kq = by_offset(j - (ATT_BLK - 1))
    qc = lax.broadcasted_iota(jnp.int32, (ATT_BLK, BAND_KEYS), 0) // CHUNK
    kc = lax.broadcasted_iota(jnp.int32, (ATT_BLK, BAND_KEYS), 1) // CHUNK
    vis_qk = (kc >= qc) & (kc <= qc + PAST_CHUNKS)
    kc = lax.broadcasted_iota(jnp.int32, (BAND_KEYS, ATT_BLK), 0) // CHUNK
    qc = lax.broadcasted_iota(jnp.int32, (BAND_KEYS, ATT_BLK), 1) // CHUNK
    vis_kq = (kc >= qc) & (kc <= qc + PAST_CHUNKS)
    for h in range(HEADS):
        qk_ref[h] = jnp.where(vis_qk, skewed(e_qk[h:h + 1, :], ATT_BLK, BAND_KEYS, -(ATT_BLK - 1)), NEG)
        kq_ref[h] = jnp.where(vis_kq, skewed(e_kq[h:h + 1, :], BAND_KEYS, ATT_BLK, -(BAND_KEYS - 1)), NEG)


def _band_bias(rel_table):
    tbl = jnp.pad(rel_table, ((0, 0), (0, REL_PAD - REL_SIZE)))
    return pl.pallas_call(
        _band_bias_kernel,
        out_shape=[jax.ShapeDtypeStruct((HEADS, ATT_BLK, BAND_KEYS), F32),
                   jax.ShapeDtypeStruct((HEADS, BAND_KEYS, ATT_BLK), F32)],
        compiler_params=pltpu.CompilerParams(vmem_limit_bytes=VMEM_LIMIT),
        name="band_bias",
    )(tbl)


def _band_attn_prompt_kernel(q_ref, k_ref, vt_ref, bias_ref, o_ref, s_ref):
    i = pl.program_id(1)
    ones = jnp.ones((SUM_ROWS, ATT_BLK), BF16)

    def run(clipped):
        blocks = []
        for g in range(BAND_BLOCKS):
            j = i - (BAND_BLOCKS - 1) + g
            start = pl.multiple_of(jnp.maximum(j, 0) * ATT_BLK, ATT_BLK)
            masks = clipped and g < BAND_BLOCKS - 1
            blocks.append((start, jnp.where(j >= 0, 0.0, NEG) if masks else None))

        def scores(h):
            sl = slice(h * HEAD_DIM, (h + 1) * HEAD_DIM)
            q = q_ref[0, :, sl]
            for g, (start, penalty) in enumerate(blocks):
                s = _dot_nt(k_ref[0, pl.ds(start, ATT_BLK), sl], q) + bias_ref[h, g * ATT_BLK:(g + 1) * ATT_BLK, :]
                s_ref[h % (HEADS_AHEAD + 1), g] = s if penalty is None else s + penalty

        def finish(h):
            sl = slice(h * HEAD_DIM, (h + 1) * HEAD_DIM)
            s = [s_ref[h % (HEADS_AHEAD + 1), g] for g in range(BAND_BLOCKS)]
            m = functools.reduce(jnp.maximum, [jnp.max(x, axis=0, keepdims=True) for x in s])
            acc = 0.0
            for x, (start, _) in zip(s, blocks):
                vt = jnp.concatenate([vt_ref[0, sl, pl.ds(start, ATT_BLK)], ones], axis=0)
                acc = acc + _dot(vt, jnp.exp2(x - m).astype(BF16))
            o_ref[0, sl, :] = (acc[:HEAD_DIM] / acc[HEAD_DIM:HEAD_DIM + 1]).astype(BF16)

        for h in range(HEADS + HEADS_AHEAD):
            if h < HEADS:
                scores(h)
            if h >= HEADS_AHEAD:
                finish(h - HEADS_AHEAD)

    pl.when(i >= BAND_BLOCKS - 1)(lambda: run(clipped=False))
    pl.when(i < BAND_BLOCKS - 1)(lambda: run(clipped=True))


def _band_attn_prompt(q, k, vt, bias_kq):
    bsz, s, _ = q.shape
    qblk = pl.BlockSpec((1, ATT_BLK, WIDTH), lambda b, i: (b, i, 0))
    return pl.pallas_call(
        _band_attn_prompt_kernel,
        grid=(bsz, s // ATT_BLK),
        in_specs=[qblk,
                  pl.BlockSpec((1, s, WIDTH), lambda b, i: (b, 0, 0)),
                  pl.BlockSpec((1, WIDTH, s), lambda b, i: (b, 0, 0)),
                  _resident(bias_kq.shape, lambda b, i: (0, 0, 0))],
        out_specs=pl.BlockSpec((1, WIDTH, ATT_BLK), lambda b, i: (b, 0, i)),
        out_shape=jax.ShapeDtypeStruct((bsz, WIDTH, s), BF16),
        scratch_shapes=[pltpu.VMEM((HEADS_AHEAD + 1, BAND_BLOCKS, ATT_BLK, ATT_BLK), F32)],
        compiler_params=_params("parallel", "parallel"),
        name="band_attn_prompt",
    )(q, k, vt, bias_kq)


def _decay_lanes(c, key_side):
    terms = jnp.concatenate(_split3(c), axis=1)
    row = lax.broadcasted_iota(jnp.int32, (3 * HEADS, WIDTH), 0)
    lane = lax.broadcasted_iota(jnp.int32, (3 * HEADS, WIDTH), 1)
    first_one, first_term, sign = (3, 0, -1.0) if key_side else (0, 3, 1.0)
    place = jnp.where(lane == (row % HEADS) * HEAD_DIM + first_term + row // HEADS, sign, 0.0).astype(BF16)
    in_group = lax.broadcasted_iota(jnp.int32, (1, WIDTH), 1) % HEAD_DIM
    ones = jnp.where((in_group >= first_one) & (in_group < first_one + 3), 1.0, 0.0)
    return (ones + _dot(terms, place)).astype(BF16)


def _fox_attn_prompt_kernel(q_ref, k_ref, vt_ref, ccol_ref, o_ref,
                            kp_ref, qp_ref, s_ref, m_ref, acc_ref):
    i = pl.program_id(1)
    qstart = pl.multiple_of(i * ATT_BLK, ATT_BLK)

    @pl.when(i == 0)
    def _():
        extra = _decay_lanes(ccol_ref[0] * LOG2E, key_side=True)
        for h in range(HEADS):
            sl = slice(h * HEAD_DIM, (h + 1) * HEAD_DIM)
            kp_ref[:, h * PAD_DIM:(h + 1) * PAD_DIM] = jnp.concatenate([k_ref[0, :, sl], extra[:, sl]], axis=1)

    extra = _decay_lanes(ccol_ref[0, pl.ds(qstart, ATT_BLK), :] * LOG2E, key_side=False)
    for h in range(HEADS):
        sl = slice(h * HEAD_DIM, (h + 1) * HEAD_DIM)
        qp_ref[:, h * PAD_DIM:(h + 1) * PAD_DIM] = jnp.concatenate([q_ref[0, :, sl], extra[:, sl]], axis=1)

    key = lax.broadcasted_iota(jnp.int32, (ATT_BLK, ATT_BLK), 0)
    qry = lax.broadcasted_iota(jnp.int32, (ATT_BLK, ATT_BLK), 1)
    causal = key <= qry
    ones = jnp.ones((SUM_ROWS, ATT_BLK), BF16)
    m_ref[...] = jnp.full(m_ref.shape, NEG, F32)
    acc_ref[...] = jnp.zeros(acc_ref.shape, F32)

    def scores(j, h):
        start = pl.multiple_of(j * ATT_BLK, ATT_BLK)
        pad = slice(h * PAD_DIM, (h + 1) * PAD_DIM)
        return _dot_nt(kp_ref[pl.ds(start, ATT_BLK), pad], qp_ref[:, pad])

    def update(j, h, s, masked):
        start = pl.multiple_of(j * ATT_BLK, ATT_BLK)
        if masked:
            s = jnp.where(causal, s, NEG)
        m_old = m_ref[h]
        m_new = jnp.maximum(m_old, jnp.max(s, axis=0, keepdims=True))
        alpha = jnp.exp2(m_old - m_new)
        p = jnp.exp2(s - m_new)
        vt = vt_ref[0, h * HEAD_DIM:(h + 1) * HEAD_DIM, pl.ds(start, ATT_BLK)]
        acc_ref[h] = alpha * acc_ref[h] + _dot(jnp.concatenate([vt, ones], axis=0), p.astype(BF16))
        m_ref[h] = m_new

    def step(j, cur, masked=False, prefetch=True):
        for h in range(HEADS + HEADS_AHEAD):
            if prefetch and h < HEADS:
                s_ref[1 - cur, h] = scores(j + 1, h)
            if h >= HEADS_AHEAD:
                update(j, h - HEADS_AHEAD, s_ref[cur, h - HEADS_AHEAD], masked)

    for h in range(HEADS):
        s_ref[0, h] = scores(0, h)

    def pair(t, carry):
        step(2 * t, 0)
        step(2 * t + 1, 1)
        return carry

    lax.fori_loop(0, i // 2, pair, 0)

    @pl.when(i % 2 == 1)
    def _():
        step(i - 1, 0)
        step(i, 1, masked=True, prefetch=False)

    @pl.when(i % 2 == 0)
    def _():
        step(i, 0, masked=True, prefetch=False)

    for h in range(HEADS):
        sl = slice(h * HEAD_DIM, (h + 1) * HEAD_DIM)
        out = acc_ref[h, :HEAD_DIM, :] / acc_ref[h, HEAD_DIM:HEAD_DIM + 1, :]
        o_ref[0, sl, :] = out.astype(BF16)


def _fox_attn_prompt(q, k, vt, ccol):
    bsz, s, _ = q.shape
    qblk = pl.BlockSpec((1, ATT_BLK, WIDTH), lambda b, i: (b, i, 0))
    return pl.pallas_call(
        _fox_attn_prompt_kernel,
        grid=(bsz, s // ATT_BLK),
        in_specs=[qblk,
                  pl.BlockSpec((1, s, WIDTH), lambda b, i: (b, 0, 0)),
                  pl.BlockSpec((1, WIDTH, s), lambda b, i: (b, 0, 0)),
                  pl.BlockSpec((1, s, HEADS), lambda b, i: (b, 0, 0))],
        out_specs=pl.BlockSpec((1, WIDTH, ATT_BLK), lambda b, i: (b, 0, i)),
        out_shape=jax.ShapeDtypeStruct((bsz, WIDTH, s), BF16),
        scratch_shapes=[pltpu.VMEM((s, HEADS * PAD_DIM), BF16),
                        pltpu.VMEM((ATT_BLK, HEADS * PAD_DIM), BF16),
                        pltpu.VMEM((2, HEADS, ATT_BLK, ATT_BLK), F32),
                        pltpu.VMEM((HEADS, 1, ATT_BLK), F32),
                        pltpu.VMEM((HEADS, HEAD_DIM + SUM_ROWS, ATT_BLK), F32)],
        compiler_params=_params("parallel", "arbitrary"),
        name="fox_attn_prompt",
    )(q, k, vt, ccol)


def _sample_attention(s_past, s_new, vct, vnt):
    m = jnp.maximum(jnp.max(s_past, axis=-1, keepdims=True), jnp.max(s_new, axis=-1, keepdims=True))
    p_past = jnp.exp2(s_past - m)
    p_new = jnp.exp2(s_new - m)
    l = jnp.sum(p_past, axis=-1, keepdims=True) + jnp.sum(p_new, axis=-1, keepdims=True)
    acc = _dot_nt(p_past.astype(BF16), vct) + _dot_nt(p_new.astype(BF16), vnt)
    return (acc / l).astype(BF16)


def _sample_attn_kernel(qa_ref, kcat_ref, vcat_ref, kna_ref, vnat_ref, bias_ref,
                        qb_ref, kcbt_ref, vcbt_ref, knb_ref, vnbt_ref, ccol_ref, crow_ref, oa_ref, ob_ref):
    t = qa_ref.shape[1]
    past_a = kcat_ref.shape[2]
    past_b = kcbt_ref.shape[2]
    row = lax.broadcasted_iota(jnp.int32, (t, t), 0)
    col = lax.broadcasted_iota(jnp.int32, (t, t), 1)
    causal = col <= row

    def scores(idx):
        h = idx // 2
        sl = slice(h * HEAD_DIM, (h + 1) * HEAD_DIM)
        if idx % 2 == 0:
            q = qa_ref[0, :, sl]
            s_past = _dot(q, kcat_ref[0, sl, :].astype(BF16)) + bias_ref[h, 0:t, BAND - past_a:BAND]
            s_new = _dot_nt(q, kna_ref[0, :, sl]) + bias_ref[h, 0:t, BAND:BAND + t]
            return s_past, s_new
        q = qb_ref[0, :, sl]
        cq = ccol_ref[0, :, h:h + 1]
        s_past = _dot(q, kcbt_ref[0, sl, :].astype(BF16)) + (cq - crow_ref[0, h:h + 1, 0:past_b]) * LOG2E
        s_new = _dot_nt(q, knb_ref[0, :, sl]) + (cq - crow_ref[0, h:h + 1, past_b:past_b + t]) * LOG2E
        return s_past, jnp.where(causal, s_new, NEG)

    def finish(idx, s):
        h = idx // 2
        sl = slice(h * HEAD_DIM, (h + 1) * HEAD_DIM)
        if idx % 2 == 0:
            oa_ref[0, :, sl] = _sample_attention(*s, vcat_ref[0, sl, :].astype(BF16), vnat_ref[0, sl, :])
        else:
            ob_ref[0, :, sl] = _sample_attention(*s, vcbt_ref[0, sl, :].astype(BF16), vnbt_ref[0, sl, :])

    pending = []
    for idx in range(2 * HEADS + SAMPLE_AHEAD):
        if idx < 2 * HEADS:
            pending.append(scores(idx))
        if idx >= SAMPLE_AHEAD:
            finish(idx - SAMPLE_AHEAD, pending.pop(0))


def _sample_attn(qa, kcat, vcat, kna, vnat, bias, qb, kcbt, vcbt, knb, vnbt, ccol, crow):
    bsz, t, _ = qa.shape
    new = pl.BlockSpec((1, t, WIDTH), lambda b: (b, 0, 0))
    new_t = pl.BlockSpec((1, WIDTH, t), lambda b: (b, 0, 0))
    old = lambda a: pl.BlockSpec((1, WIDTH, a.shape[2]), lambda b: (b, 0, 0))
    return pl.pallas_call(
        _sample_attn_kernel,
        grid=(bsz,),
        in_specs=[new, old(kcat), old(vcat), new, new_t, _resident(bias.shape, lambda b: (0, 0, 0)),
                  new, old(kcbt), old(vcbt), new, new_t,
                  pl.BlockSpec((1, t, HEADS), lambda b: (b, 0, 0)),
                  pl.BlockSpec((1, HEADS, crow.shape[2]), lambda b: (b, 0, 0))],
        out_specs=[new, new],
        out_shape=[jax.ShapeDtypeStruct(qa.shape, BF16)] * 2,
        compiler_params=_params("parallel"),
        name="sample_attn",
    )(qa, kcat, vcat, kna, vnat, bias, qb, kcbt, vcbt, knb, vnbt, ccol, crow)


def _postmix_kernel(x_ref, oa_ref, ob_ref, gpre_ref, gpost_ref, wt_ref, wpa_ref, wpb_ref, wout_ref, y_ref,
                    *, channel_major):
    tm, d = x_ref.shape
    planes = tm // SUBLANES
    gates = wt_ref.shape[0] - 2 * d
    halves = [slice(k * tm // 2, (k + 1) * tm // 2) for k in range(2)]

    def mix(rows):
        h = _rmsnorm(x_ref[rows, :], gpre_ref[...]).astype(BF16)
        gate_a = jax.nn.sigmoid(_dot_nt(h, wt_ref[gates:gates + d, :].astype(BF16)))
        gate_b = jax.nn.sigmoid(_dot_nt(h, wt_ref[gates + d:, :].astype(BF16)))
        if channel_major:
            proj = lambda o_ref, w_ref: lax.dot_general(o_ref[0, :, rows], w_ref[...], (((0,), (0,)), ((), ())),
                                                        preferred_element_type=F32)
        else:
            proj = lambda o_ref, w_ref: _dot(o_ref[rows, :], w_ref[...])
        return (gate_a * proj(oa_ref, wpa_ref) + gate_b * proj(ob_ref, wpb_ref)).astype(BF16)

    merged = [mix(rows) for rows in halves]
    outs = [_dot(m, wout_ref[...]) for m in merged]
    for k, rows in enumerate(halves):
        y = x_ref[rows, :] + _rmsnorm(outs[k], gpost_ref[...])
        subs = slice(k * SUBLANES // 2, (k + 1) * SUBLANES // 2)
        y_ref[:, subs, :] = jnp.swapaxes(y.reshape(SUBLANES // 2, planes, d), 0, 1)


def _deinterleave(planes):
    p, s, d = planes.shape
    return jnp.swapaxes(planes, 0, 1).reshape(s * p, d)


def _postmix(x, oa, ob, g_pre, g_post, w_in_t, w_pa, w_pb, w_out, seq_len):
    n, d = x.shape
    tm = _mix_tile(seq_len)
    row = lambda i: (i, 0)
    fixed = lambda i: (0, 0)
    channel_major = oa.ndim == 3
    if channel_major:
        tiles_per_seq = oa.shape[2] // tm
        o_spec = pl.BlockSpec((1, WIDTH, tm), lambda i: (i // tiles_per_seq, 0, i % tiles_per_seq))
    else:
        o_spec = pl.BlockSpec((tm, WIDTH), row)
    return pl.pallas_call(
        functools.partial(_postmix_kernel, channel_major=channel_major),
        grid=(n // tm,),
        in_specs=[pl.BlockSpec((tm, d), row),
                  o_spec,
                  o_spec,
                  pl.BlockSpec((1, d), fixed),
                  pl.BlockSpec((1, d), fixed),
                  _resident(w_in_t.shape, fixed),
                  _resident(w_pa.shape, fixed),
                  _resident(w_pb.shape, fixed),
                  _resident(w_out.shape, fixed)],
        out_specs=pl.BlockSpec((tm // SUBLANES, SUBLANES, d), lambda i: (i, 0, 0)),
        out_shape=jax.ShapeDtypeStruct((n // SUBLANES, SUBLANES, d), F32),
        compiler_params=_params("parallel"),
        name="postmix",
    )(x, oa, ob, g_pre, g_post, w_in_t, w_pa, w_pb, w_out)


def _ffn_kernel(x_ref, st_ref, gpre_ref, gpost_ref, wup_ref, cw_ref, cb_ref, wdn_ref,
                y_ref, nst_ref, hist_ref, ext_ref, h_ref, f_ref, *, nseg):
    @pl.when(pl.program_id(1) == 0)
    def _():
        hist_ref[...] = st_ref[...]

    planes, _, d = x_ref.shape
    tm = planes * SUBLANES
    d_ff = wdn_ref.shape[0]
    h_ref[...] = _rmsnorm(x_ref[...].reshape(tm, d), gpre_ref[...]).astype(BF16)
    n_chunks = d_ff // FFN_COLS
    first_sublane = lax.broadcasted_iota(jnp.int32, (SUBLANES, FFN_COLS), 0) == 0

    def parts(c):
        for part in range(2):
            yield (slice(part * d_ff + c * FFN_COLS, part * d_ff + (c + 1) * FFN_COLS),
                   slice(part * FFN_COLS, (part + 1) * FFN_COLS))

    def up(c):
        ext = ext_ref.at[c % 2]
        for cols, dst in parts(c):
            u = _dot(h_ref[...], wup_ref[:, cols]).reshape(planes, SUBLANES, FFN_COLS)
            ext[CONV_W - 1:, :, dst] = u
            for k in range(CONV_W - 1):
                last = u[planes - (CONV_W - 1) + k]
                if nseg == 1:
                    ext[k, :, dst] = jnp.where(first_sublane, hist_ref[0, k:k + 1, cols], pltpu.roll(last, 1, 0))
                    hist_ref[0, k:k + 1, cols] = last[SUBLANES - 1:, :]
                else:
                    ext[k, :, dst] = hist_ref[:, k, cols]
                    hist_ref[:, k, cols] = last

    def activate(c):
        ext = ext_ref.at[c % 2]
        halves = []
        for cols, dst in parts(c):
            y = cb_ref[:, cols]
            for tap in range(CONV_W):
                y = y + ext[tap:tap + planes, :, dst] * cw_ref[tap:tap + 1, cols]
            halves.append(y)
        act = (jax.nn.gelu(halves[0]) * halves[1]).reshape(tm, FFN_COLS)
        f_ref[:, c * FFN_COLS:(c + 1) * FFN_COLS] = act.astype(BF16)

    up(0)
    for c in range(n_chunks):
        if c + 1 < n_chunks:
            up(c + 1)
        activate(c)
    f = _dot(f_ref[...], wdn_ref[...])
    y = x_ref[...].reshape(tm, d) + _rmsnorm(f, gpost_ref[...])
    y_ref[...] = _deinterleave(y.reshape(planes, SUBLANES, d))
    nst_ref[...] = hist_ref[...]


def _ffn(x, bsz, s, state, g_pre, g_post, w_up, conv_w, conv_b, w_down):
    d = x.shape[-1]
    up = w_up.shape[1]
    tm = _mix_tile(s)
    planes = tm // SUBLANES
    if s >= tm:
        nseg, tiles = 1, s // tm
    else:
        nseg, tiles = tm // s, 1
        assert nseg == SUBLANES
    outer = bsz // nseg
    row = lambda o, t: (o * tiles + t, 0)
    fixed = lambda o, t: (0, 0)
    st_spec = pl.BlockSpec((nseg, CONV_W - 1, up), lambda o, t: (o, 0, 0))
    y, new_state = pl.pallas_call(
        functools.partial(_ffn_kernel, nseg=nseg),
        grid=(outer, tiles),
        in_specs=[pl.BlockSpec((planes, SUBLANES, d), lambda o, t: (o * tiles + t, 0, 0)),
                  st_spec,
                  pl.BlockSpec((1, d), fixed),
                  pl.BlockSpec((1, d), fixed),
                  _resident(w_up.shape, fixed),
                  pl.BlockSpec(conv_w.shape, fixed),
                  pl.BlockSpec((1, up), fixed),
                  _resident(w_down.shape, fixed)],
        out_specs=[pl.BlockSpec((tm, d), row), st_spec],
        out_shape=[jax.ShapeDtypeStruct((bsz * s, d), F32),
                   jax.ShapeDtypeStruct(state.shape, F32)],
        scratch_shapes=[pltpu.VMEM((nseg, CONV_W - 1, up), F32),
                        pltpu.VMEM((2, CONV_W - 1 + planes, SUBLANES, 2 * FFN_COLS), F32),
                        pltpu.VMEM((tm, d), BF16),
                        pltpu.VMEM((tm, w_down.shape[0]), BF16)],
        compiler_params=_params("arbitrary", "arbitrary"),
        name="conv_ffn",
    )(x, state, g_pre, g_post, w_up, conv_w, conv_b, w_down)
    return y.reshape(bsz, s, d), new_state


def _layer(x, caches, conv_state, bias, w):
    bsz, s, d = x.shape
    n = bsz * s
    x2 = x.reshape(n, d)
    keep = min(BAND, s)
    (qa, qb, ka16, kb16, vat16, vbt16, kat, vat, kbt, vbt, lft) = _inproj(
        x2, w["g_pre_mix"], w["w_in_t"], w["b_f"], seq_len=s, band_keep=keep)
    seq = lambda a: a.reshape(bsz, s, a.shape[-1])
    if caches is None:
        ccol, _ = _cumsum(lft)
        oa = _band_attn_prompt(seq(qa), seq(ka16), vat16, bias[1])
        ob = _fox_attn_prompt(seq(qb), seq(kb16), vbt16, ccol)
    else:
        ckat, cvat, ckbt, cvbt, clft = caches
        ccol, crow = _cumsum(clft, lft, col_from=clft.shape[2])
        oa, ob = _sample_attn(seq(qa), ckat, cvat, seq(ka16), vat16, bias[0],
                              seq(qb), ckbt, cvbt, seq(kb16), vbt16, ccol, crow)
        oa, ob = oa.reshape(n, WIDTH), ob.reshape(n, WIDTH)
    x1 = _postmix(x2, oa, ob, w["g_pre_mix"], w["g_post_mix"], w["w_in_t"], w["w_pa"], w["w_pb"], w["w_out"],
                  seq_len=s)
    y, new_conv = _ffn(x1, bsz, s, conv_state, w["g_pre_ffn"], w["g_post_ffn"],
                       w["w_up"], w["conv_w"], w["conv_b"], w["w_down"])
    heads = lambda a: a.reshape(bsz, HEADS, HEAD_DIM, a.shape[-1]).transpose(0, 3, 1, 2)
    return y, (heads(kat), heads(vat), heads(kbt), heads(vbt), lft.transpose(0, 2, 1), new_conv)


def _channel_major(cache):
    bsz, past = cache.shape[:2]
    return cache.transpose(0, 2, 3, 1).reshape(bsz, WIDTH, past)


def kernel(x_prompt, x_sample, cache_k_a, cache_v_a, cache_k_b, cache_v_b, cache_logf_b, state_conv_ffn,
           g_pre_mix, g_post_mix, g_pre_ffn, g_post_ffn, w_in, b_f, rel_table, w_proj_a, w_proj_b, w_out,
           w_up, conv_w, conv_b, w_down):
    depth = w_in.shape[0]
    up = w_up.shape[-1]
    x_p, x_s = x_prompt, x_sample
    p_states, s_states = [], []
    for l in range(depth):
        w = {
            "g_pre_mix": g_pre_mix[l][None], "g_post_mix": g_post_mix[l][None],
            "g_pre_ffn": g_pre_ffn[l][None], "g_post_ffn": g_post_ffn[l][None],
            "w_in_t": w_in[l].T,
            "b_f": b_f[l][:, None],
            "w_pa": w_proj_a[l].astype(BF16), "w_pb": w_proj_b[l].astype(BF16),
            "w_out": w_out[l].astype(BF16), "w_up": w_up[l].astype(BF16),
            "conv_w": conv_w[l], "conv_b": conv_b[l][None], "w_down": w_down[l].astype(BF16),
        }
        bias = _band_bias(rel_table[l])
        zero_state = jnp.zeros((x_p.shape[0], CONV_W - 1, up), F32)
        x_p, (ka, va, kb, vb, lf, cv) = _layer(x_p, None, zero_state, bias, w)
        p_states.append((ka, va, kb, vb, lf, cv))
        caches = (_channel_major(cache_k_a[l]), _channel_major(cache_v_a[l]), _channel_major(cache_k_b[l]),
                  _channel_major(cache_v_b[l]), cache_logf_b[l].transpose(0, 2, 1))
        x_s, st = _layer(x_s, caches, state_conv_ffn[l], bias, w)
        s_states.append(st)
    stack = lambda states: [jnp.stack(s) for s in zip(*states)]
    return (x_p, x_s, *stack(p_states), *stack(s_states))
```

```python
import functools

import jax
import jax.numpy as jnp
from jax import lax
from jax.experimental import pallas as pl
from jax.experimental.pallas import tpu as pltpu

F32, BF16 = jnp.float32, jnp.bfloat16

HEADS = 8
HEAD_DIM = 64
WIDTH = HEADS * HEAD_DIM
CHUNK = 64
PAST_CHUNKS = 8
BAND = PAST_CHUNKS * CHUNK
REL_CLIP = 128
REL_SIZE = 2 * REL_CLIP + 1
CONV_W = 3
EPS = 1e-6
NEG = -1e30
LOG2E = 1.4426950408889634

SUBLANES = 8
ATT_BLK = 256
BAND_KEYS = 3 * ATT_BLK
BAND_BLOCKS = BAND_KEYS // ATT_BLK
PAD_DIM = 2 * HEAD_DIM
SUM_ROWS = 16
HEADS_AHEAD = 2
SAMPLE_AHEAD = 3
ROW_TILE = 512
MIX_TILE = 1024
FFN_COLS = 256
CUMSUM_BLK = 256
VMEM_LIMIT = 56 * 1024 * 1024


def _params(*sem):
    return pltpu.CompilerParams(dimension_semantics=sem, vmem_limit_bytes=VMEM_LIMIT)


def _resident(shape, index_map):
    return pl.BlockSpec(shape, index_map, pipeline_mode=pl.Buffered(1))


def _mix_tile(seq_len):
    return MIX_TILE if seq_len >= MIX_TILE else SUBLANES * seq_len


def _rmsnorm(x, g):
    return x * lax.rsqrt(jnp.mean(x * x, axis=-1, keepdims=True) + EPS) * g


def _dot(a, b):
    return jnp.dot(a, b, preferred_element_type=F32)


def _dot_nt(a, b):
    return lax.dot_general(a, b, (((1,), (1,)), ((), ())), preferred_element_type=F32)


def _split3(x):
    hi = x.astype(BF16)
    r = x - hi.astype(F32)
    mid = r.astype(BF16)
    lo = (r - mid.astype(F32)).astype(BF16)
    return hi, mid, lo


def _inproj_kernel(x_ref, g_ref, w_ref, wft_ref, bf_ref,
                   qa_ref, qb_ref, ka_ref, kb_ref, vat16_ref, vbt16_ref,
                   kat_ref, vat_ref, kbt_ref, vbt_ref, lft_ref):
    h = _rmsnorm(x_ref[...], g_ref[...]).astype(BF16)
    scale = HEAD_DIM ** -0.5

    def proj(c):
        return _dot_nt(h, w_ref[c * WIDTH:(c + 1) * WIDTH, :].astype(BF16))

    def put(ref, zt):
        per_seq = zt.shape[1] // ref.shape[0]
        for sq in range(ref.shape[0]):
            ref[sq] = zt[:, sq * per_seq:(sq + 1) * per_seq]

    qa_ref[...] = (proj(0) * (scale * LOG2E)).astype(BF16)
    qb_ref[...] = (proj(3) * (scale * LOG2E)).astype(BF16)
    for c, row_ref, t_ref in ((1, ka_ref, kat_ref), (4, kb_ref, kbt_ref)):
        z = proj(c)
        row_ref[...] = z.astype(BF16)
        put(t_ref, z.T)
    for c, t16_ref, t_ref in ((2, vat16_ref, vat_ref), (5, vbt16_ref, vbt_ref)):
        zt = proj(c).T
        put(t_ref, zt)
        put(t16_ref, zt.astype(BF16))
    put(lft_ref, jax.nn.log_sigmoid(_dot_nt(wft_ref[...].astype(BF16), h) + bf_ref[...]))


def _inproj(x, g, w_in_t, b_f, seq_len, band_keep):
    n, d = x.shape
    bsz = n // seq_len
    tm = min(ROW_TILE, n)
    tiles_per_seq = max(seq_len // tm, 1)
    seqs_per_tile = max(tm // seq_len, 1)
    cols = tm // seqs_per_tile
    assert band_keep == cols
    row = lambda i: (i, 0)
    fixed = lambda i: (0, 0)
    along = lambda i: (i // tiles_per_seq, 0, i % tiles_per_seq)
    kept = lambda i: (i // tiles_per_seq, 0, 0)
    rows16 = (jax.ShapeDtypeStruct((n, WIDTH), BF16), pl.BlockSpec((tm, WIDTH), row))

    def chan(channels, length, dtype, index_map):
        return (jax.ShapeDtypeStruct((bsz, channels, length), dtype),
                pl.BlockSpec((seqs_per_tile, channels, cols), index_map))

    outs = [rows16] * 4
    outs += [chan(WIDTH, seq_len, BF16, along)] * 2
    outs += [chan(WIDTH, band_keep, F32, kept)] * 2
    outs += [chan(WIDTH, seq_len, F32, along)] * 2
    outs += [chan(HEADS, seq_len, F32, along)]
    return pl.pallas_call(
        _inproj_kernel,
        grid=(n // tm,),
        in_specs=[pl.BlockSpec((tm, d), row),
                  pl.BlockSpec((1, d), fixed),
                  _resident((6 * WIDTH, d), fixed),
                  pl.BlockSpec((HEADS, d), lambda i: (6 * WIDTH // HEADS, 0)),
                  pl.BlockSpec((HEADS, 1), fixed)],
        out_specs=[spec for _, spec in outs],
        out_shape=[shape for shape, _ in outs],
        compiler_params=_params("arbitrary"),
        name="inproj",
    )(x, g, w_in_t, w_in_t, b_f)


def _cumsum_kernel(*refs, seg_lens, col_from):
    seg_refs = refs[:len(seg_lens)]
    ccol_ref, crow_ref = refs[len(seg_lens):]
    carry = jnp.zeros((crow_ref.shape[0], 1), F32)
    off = 0
    for ref, n in zip(seg_refs, seg_lens):
        for o in range(0, n, CUMSUM_BLK):
            b = min(CUMSUM_BLK, n - o)
            r = lax.broadcasted_iota(jnp.int32, (b, b), 0)
            c = lax.broadcasted_iota(jnp.int32, (b, b), 1)
            upper = jnp.where(r <= c, 1.0, 0.0).astype(BF16)
            sums = carry
            for p in _split3(ref[:, o:o + b]):
                sums = sums + _dot(p, upper)
            crow_ref[:, off + o:off + o + b] = sums
            carry = sums[:, b - 1:b]
            if off + o >= col_from:
                by_position = sums.T
                for bb in range(ccol_ref.shape[0]):
                    ccol_ref[bb, off + o - col_from:off + o - col_from + b, :] = (
                        by_position[:, bb * HEADS:(bb + 1) * HEADS])
        off += n


def _cumsum(*segs, col_from=0):
    bsz = segs[0].shape[0]
    seg_lens = tuple(s.shape[2] for s in segs)
    total = sum(seg_lens)
    assert col_from % CUMSUM_BLK == 0 or col_from in (0, seg_lens[0])
    ccol, crow = pl.pallas_call(
        functools.partial(_cumsum_kernel, seg_lens=seg_lens, col_from=col_from),
        out_shape=[jax.ShapeDtypeStruct((bsz, total - col_from, HEADS), F32),
                   jax.ShapeDtypeStruct((bsz * HEADS, total), F32)],
        compiler_params=pltpu.CompilerParams(vmem_limit_bytes=VMEM_LIMIT),
        name="cumsum_logf",
    )(*[s.reshape(bsz * HEADS, s.shape[2]) for s in segs])
    return ccol, crow.reshape(bsz, HEADS, total)


REL_PAD = 384
DIST_SPAN = 1024


def _band_bias_kernel(tbl_ref, qk_ref, kq_ref):
    j = lax.broadcasted_iota(jnp.int32, (REL_PAD, DIST_SPAN), 1)
    r = lax.broadcasted_iota(jnp.int32, (REL_PAD, DIST_SPAN), 0)
    parts = _split3(tbl_ref[...])

    def by_offset(dist):
        onehot = jnp.where(jnp.clip(dist, -REL_CLIP, REL_CLIP) + REL_CLIP == r, 1.0, 0.0).astype(BF16)
        return sum(_dot(p, onehot) for p in parts) * LOG2E

    def skewed(e_row, rows, cols, shift):
        wide = jnp.broadcast_to(e_row, (rows, DIST_SPAN))
        return pltpu.roll(wide, shift % DIST_SPAN, 1, stride=1, stride_axis=0)[:, :cols]

    e_qk = by_offset((BAND_KEYS - 1) - j)
    e_kq = by_offset(j - (ATT_BLK - 1))
    qc = lax.broadcasted_iota(jnp.int32, (ATT_BLK, BAND_KEYS), 0) // CHUNK
    kc = lax.broadcasted_iota(jnp.int32, (ATT_BLK, BAND_KEYS), 1) // CHUNK
    vis_qk = (kc >= qc) & (kc <= qc + PAST_CHUNKS)
    kc = lax.broadcasted_iota(jnp.int32, (BAND_KEYS, ATT_BLK), 0) // CHUNK
    qc = lax.broadcasted_iota(jnp.int32, (BAND_KEYS, ATT_BLK), 1) // CHUNK
    vis_kq = (kc >= qc) & (kc <= qc + PAST_CHUNKS)
    for h in range(HEADS):
        qk_ref[h] = jnp.where(vis_qk, skewed(e_qk[h:h + 1, :], ATT_BLK, BAND_KEYS, -(ATT_BLK - 1)), NEG)
        kq_ref[h] = jnp.where(vis_kq, skewed(e_kq[h:h + 1, :], BAND_KEYS, ATT_BLK, -(BAND_KEYS - 1)), NEG)


def _band_bias(rel_table):
    tbl = jnp.pad(rel_table, ((0, 0), (0, REL_PAD - REL_SIZE)))
    return pl.pallas_call(
        _band_bias_kernel,
        out_shape=[jax.ShapeDtypeStruct((HEADS, ATT_BLK, BAND_KEYS), F32),
                   jax.ShapeDtypeStruct((HEADS, BAND_KEYS, ATT_BLK), F32)],
        compiler_params=pltpu.CompilerParams(vmem_limit_bytes=VMEM_LIMIT),
        name="band_bias",
    )(tbl)


def _band_attn_prompt_kernel(q_ref, k_ref, vt_ref, bias_ref, o_ref, s_ref):
    i = pl.program_id(1)
    ones = jnp.ones((SUM_ROWS, ATT_BLK), BF16)

    def run(clipped):
        blocks = []
        for g in range(BAND_BLOCKS):
            j = i - (BAND_BLOCKS - 1) + g
            start = pl.multiple_of(jnp.maximum(j, 0) * ATT_BLK, ATT_BLK)
            masks = clipped and g < BAND_BLOCKS - 1
            blocks.append((start, jnp.where(j >= 0, 0.0, NEG) if masks else None))

        def scores(h):
            sl = slice(h * HEAD_DIM, (h + 1) * HEAD_DIM)
            q = q_ref[0, :, sl]
            for g, (start, penalty) in enumerate(blocks):
                s = _dot_nt(k_ref[0, pl.ds(start, ATT_BLK), sl], q) + bias_ref[h, g * ATT_BLK:(g + 1) * ATT_BLK, :]
                s_ref[h % (HEADS_AHEAD + 1), g] = s if penalty is None else s + penalty

        def finish(h):
            sl = slice(h * HEAD_DIM, (h + 1) * HEAD_DIM)
            s = [s_ref[h % (HEADS_AHEAD + 1), g] for g in range(BAND_BLOCKS)]
            m = functools.reduce(jnp.maximum, [jnp.max(x, axis=0, keepdims=True) for x in s])
            acc = 0.0
            for x, (start, _) in zip(s, blocks):
                vt = jnp.concatenate([vt_ref[0, sl, pl.ds(start, ATT_BLK)], ones], axis=0)
                acc = acc + _dot(vt, jnp.exp2(x - m).astype(BF16))
            o_ref[0, sl, :] = (acc[:HEAD_DIM] / acc[HEAD_DIM:HEAD_DIM + 1]).astype(BF16)

        for h in range(HEADS + HEADS_AHEAD):
            if h < HEADS:
                scores(h)
            if h >= HEADS_AHEAD:
                finish(h - HEADS_AHEAD)

    pl.when(i >= BAND_BLOCKS - 1)(lambda: run(clipped=False))
    pl.when(i < BAND_BLOCKS - 1)(lambda: run(clipped=True))


def _band_attn_prompt(q, k, vt, bias_kq):
    bsz, s, _ = q.shape
    qblk = pl.BlockSpec((1, ATT_BLK, WIDTH), lambda b, i: (b, i, 0))
    return pl.pallas_call(
        _band_attn_prompt_kernel,
        grid=(bsz, s // ATT_BLK),
        in_specs=[qblk,
                  pl.BlockSpec((1, s, WIDTH), lambda b, i: (b, 0, 0)),
                  pl.BlockSpec((1, WIDTH, s), lambda b, i: (b, 0, 0)),
                  _resident(bias_kq.shape, lambda b, i: (0, 0, 0))],
        out_specs=pl.BlockSpec((1, WIDTH, ATT_BLK), lambda b, i: (b, 0, i)),
        out_shape=jax.ShapeDtypeStruct((bsz, WIDTH, s), BF16),
        scratch_shapes=[pltpu.VMEM((HEADS_AHEAD + 1, BAND_BLOCKS, ATT_BLK, ATT_BLK), F32)],
        compiler_params=_params("parallel", "parallel"),
        name="band_attn_prompt",
    )(q, k, vt, bias_kq)


def _decay_lanes(c, key_side):
    terms = jnp.concatenate(_split3(c), axis=1)
    row = lax.broadcasted_iota(jnp.int32, (3 * HEADS, WIDTH), 0)
    lane = lax.broadcasted_iota(jnp.int32, (3 * HEADS, WIDTH), 1)
    first_one, first_term, sign = (3, 0, -1.0) if key_side else (0, 3, 1.0)
    place = jnp.where(lane == (row % HEADS) * HEAD_DIM + first_term + row // HEADS, sign, 0.0).astype(BF16)
    in_group = lax.broadcasted_iota(jnp.int32, (1, WIDTH), 1) % HEAD_DIM
    ones = jnp.where((in_group >= first_one) & (in_group < first_one + 3), 1.0, 0.0)
    return (ones + _dot(terms, place)).astype(BF16)


def _fox_attn_prompt_kernel(q_ref, k_ref, vt_ref, ccol_ref, o_ref,
                            kp_ref, qp_ref, s_ref, m_ref, acc_ref):
    i = pl.program_id(1)
    qstart = pl.multiple_of(i * ATT_BLK, ATT_BLK)

    @pl.when(i == 0)
    def _():
        extra = _decay_lanes(ccol_ref[0] * LOG2E, key_side=True)
        for h in range(HEADS):
            sl = slice(h * HEAD_DIM, (h + 1) * HEAD_DIM)
            kp_ref[:, h * PAD_DIM:(h + 1) * PAD_DIM] = jnp.concatenate([k_ref[0, :, sl], extra[:, sl]], axis=1)

    extra = _decay_lanes(ccol_ref[0, pl.ds(qstart, ATT_BLK), :] * LOG2E, key_side=False)
    for h in range(HEADS):
        sl = slice(h * HEAD_DIM, (h + 1) * HEAD_DIM)
        qp_ref[:, h * PAD_DIM:(h + 1) * PAD_DIM] = jnp.concatenate([q_ref[0, :, sl], extra[:, sl]], axis=1)

    key = lax.broadcasted_iota(jnp.int32, (ATT_BLK, ATT_BLK), 0)
    qry = lax.broadcasted_iota(jnp.int32, (ATT_BLK, ATT_BLK), 1)
    causal = key <= qry
    ones = jnp.ones((SUM_ROWS, ATT_BLK), BF16)
    m_ref[...] = jnp.full(m_ref.shape, NEG, F32)
    acc_ref[...] = jnp.zeros(acc_ref.shape, F32)

    def scores(j, h):
        start = pl.multiple_of(j * ATT_BLK, ATT_BLK)
        pad = slice(h * PAD_DIM, (h + 1) * PAD_DIM)
        return _dot_nt(kp_ref[pl.ds(start, ATT_BLK), pad], qp_ref[:, pad])

    def update(j, h, s, masked):
        start = pl.multiple_of(j * ATT_BLK, ATT_BLK)
        if masked:
            s = jnp.where(causal, s, NEG)
        m_old = m_ref[h]
        m_new = jnp.maximum(m_old, jnp.max(s, axis=0, keepdims=True))
        alpha = jnp.exp2(m_old - m_new)
        p = jnp.exp2(s - m_new)
        vt = vt_ref[0, h * HEAD_DIM:(h + 1) * HEAD_DIM, pl.ds(start, ATT_BLK)]
        acc_ref[h] = alpha * acc_ref[h] + _dot(jnp.concatenate([vt, ones], axis=0), p.astype(BF16))
        m_ref[h] = m_new

    def step(j, cur, masked=False, prefetch=True):
        for h in range(HEADS + HEADS_AHEAD):
            if prefetch and h < HEADS:
                s_ref[1 - cur, h] = scores(j + 1, h)
            if h >= HEADS_AHEAD:
                update(j, h - HEADS_AHEAD, s_ref[cur, h - HEADS_AHEAD], masked)

    for h in range(HEADS):
        s_ref[0, h] = scores(0, h)

    def pair(t, carry):
        step(2 * t, 0)
        step(2 * t + 1, 1)
        return carry

    lax.fori_loop(0, i // 2, pair, 0)

    @pl.when(i % 2 == 1)
    def _():
        step(i - 1, 0)
        step(i, 1, masked=True, prefetch=False)

    @pl.when(i % 2 == 0)
    def _():
        step(i, 0, masked=True, prefetch=False)

    for h in range(HEADS):
        sl = slice(h * HEAD_DIM, (h + 1) * HEAD_DIM)
        out = acc_ref[h, :HEAD_DIM, :] / acc_ref[h, HEAD_DIM:HEAD_DIM + 1, :]
        o_ref[0, sl, :] = out.astype(BF16)


def _fox_attn_prompt(q, k, vt, ccol):
    bsz, s, _ = q.shape
    qblk = pl.BlockSpec((1, ATT_BLK, WIDTH), lambda b, i: (b, i, 0))
    return pl.pallas_call(
        _fox_attn_prompt_kernel,
        grid=(bsz, s // ATT_BLK),
        in_specs=[qblk,
                  pl.BlockSpec((1, s, WIDTH), lambda b, i: (b, 0, 0)),
                  pl.BlockSpec((1, WIDTH, s), lambda b, i: (b, 0, 0)),
                  pl.BlockSpec((1, s, HEADS), lambda b, i: (b, 0, 0))],
        out_specs=pl.BlockSpec((1, WIDTH, ATT_BLK), lambda b, i: (b, 0, i)),
        out_shape=jax.ShapeDtypeStruct((bsz, WIDTH, s), BF16),
        scratch_shapes=[pltpu.VMEM((s, HEADS * PAD_DIM), BF16),
                        pltpu.VMEM((ATT_BLK, HEADS * PAD_DIM), BF16),
                        pltpu.VMEM((2, HEADS, ATT_BLK, ATT_BLK), F32),
                        pltpu.VMEM((HEADS, 1, ATT_BLK), F32),
                        pltpu.VMEM((HEADS, HEAD_DIM + SUM_ROWS, ATT_BLK), F32)],
        compiler_params=_params("parallel", "arbitrary"),
        name="fox_attn_prompt",
    )(q, k, vt, ccol)


def _prompt_attn_kernel(qa_ref, ka_ref, vat_ref, bias_ref, qb_ref, kb_ref, vbt_ref, ccol_ref, oa_ref, ob_ref,
                        sa_ref, kp_ref, qp_ref, sb_ref, m_ref, acc_ref):
    i = pl.program_id(1)
    qstart = pl.multiple_of(i * ATT_BLK, ATT_BLK)
    ones = jnp.ones((SUM_ROWS, ATT_BLK), BF16)
    key = lax.broadcasted_iota(jnp.int32, (ATT_BLK, ATT_BLK), 0)
    qry = lax.broadcasted_iota(jnp.int32, (ATT_BLK, ATT_BLK), 1)
    causal = key <= qry

    @pl.when(i == 0)
    def _():
        extra = _decay_lanes(ccol_ref[0] * LOG2E, key_side=True)
        for h in range(HEADS):
            sl = slice(h * HEAD_DIM, (h + 1) * HEAD_DIM)
            kp_ref[:, h * PAD_DIM:(h + 1) * PAD_DIM] = jnp.concatenate([kb_ref[0, :, sl], extra[:, sl]], axis=1)

    def scores(j, h):
        start = pl.multiple_of(j * ATT_BLK, ATT_BLK)
        pad = slice(h * PAD_DIM, (h + 1) * PAD_DIM)
        return _dot_nt(kp_ref[pl.ds(start, ATT_BLK), pad], qp_ref[:, pad])

    def update(j, h, s, masked):
        start = pl.multiple_of(j * ATT_BLK, ATT_BLK)
        if masked:
            s = jnp.where(causal, s, NEG)
        m_old = m_ref[h]
        m_new = jnp.maximum(m_old, jnp.max(s, axis=0, keepdims=True))
        alpha = jnp.exp2(m_old - m_new)
        p = jnp.exp2(s - m_new)
        vt = vbt_ref[0, h * HEAD_DIM:(h + 1) * HEAD_DIM, pl.ds(start, ATT_BLK)]
        acc_ref[h] = alpha * acc_ref[h] + _dot(jnp.concatenate([vt, ones], axis=0), p.astype(BF16))
        m_ref[h] = m_new

    def fox_setup():
        extra = _decay_lanes(ccol_ref[0, pl.ds(qstart, ATT_BLK), :] * LOG2E, key_side=False)
        for h in range(HEADS):
            sl = slice(h * HEAD_DIM, (h + 1) * HEAD_DIM)
            qp_ref[:, h * PAD_DIM:(h + 1) * PAD_DIM] = jnp.concatenate([qb_ref[0, :, sl], extra[:, sl]], axis=1)
        m_ref[...] = jnp.full(m_ref.shape, NEG, F32)
        acc_ref[...] = jnp.zeros(acc_ref.shape, F32)
        for h in range(HEADS):
            sb_ref[0, h] = scores(0, h)

    def band(clipped):
        blocks = []
        for g in range(BAND_BLOCKS):
            j = i - (BAND_BLOCKS - 1) + g
            start = pl.multiple_of(jnp.maximum(j, 0) * ATT_BLK, ATT_BLK)
            masks = clipped and g < BAND_BLOCKS - 1
            blocks.append((start, jnp.where(j >= 0, 0.0, NEG) if masks else None))

        def band_scores(h):
            sl = slice(h * HEAD_DIM, (h + 1) * HEAD_DIM)
            q = qa_ref[0, :, sl]
            for g, (start, penalty) in enumerate(blocks):
                s = _dot_nt(ka_ref[0, pl.ds(start, ATT_BLK), sl], q) + bias_ref[h, g * ATT_BLK:(g + 1) * ATT_BLK, :]
                sa_ref[h % (HEADS_AHEAD + 1), g] = s if penalty is None else s + penalty

        def band_finish(h):
            sl = slice(h * HEAD_DIM, (h + 1) * HEAD_DIM)
            s = [sa_ref[h % (HEADS_AHEAD + 1), g] for g in range(BAND_BLOCKS)]
            m = functools.reduce(jnp.maximum, [jnp.max(x, axis=0, keepdims=True) for x in s])
            acc = 0.0
            for x, (start, _) in zip(s, blocks):
                vt = jnp.concatenate([vat_ref[0, sl, pl.ds(start, ATT_BLK)], ones], axis=0)
                acc = acc + _dot(vt, jnp.exp2(x - m).astype(BF16))
            oa_ref[0, sl, :] = (acc[:HEAD_DIM] / acc[HEAD_DIM:HEAD_DIM + 1]).astype(BF16)

        for h in range(HEADS + HEADS_AHEAD):
            if h < HEADS:
                band_scores(h)
            if h >= HEADS_AHEAD:
                band_finish(h - HEADS_AHEAD)

    def head_of_step(clipped):
        fox_setup()
        band(clipped)

    pl.when(i >= BAND_BLOCKS - 1)(lambda: head_of_step(clipped=False))
    pl.when(i < BAND_BLOCKS - 1)(lambda: head_of_step(clipped=True))

    def step(j, cur, masked=False, prefetch=True):
        for h in range(HEADS + HEADS_AHEAD):
            if prefetch and h < HEADS:
                sb_ref[1 - cur, h] = scores(j + 1, h)
            if h >= HEADS_AHEAD:
                update(j, h - HEADS_AHEAD, sb_ref[cur, h - HEADS_AHEAD], masked)

    def pair(t, carry):
        step(2 * t, 0)
        step(2 * t + 1, 1)
        return carry

    lax.fori_loop(0, i // 2, pair, 0)

    @pl.when(i % 2 == 1)
    def _():
        step(i - 1, 0)
        step(i, 1, masked=True, prefetch=False)

    @pl.when(i % 2 == 0)
    def _():
        step(i, 0, masked=True, prefetch=False)

    for h in range(HEADS):
        sl = slice(h * HEAD_DIM, (h + 1) * HEAD_DIM)
        out = acc_ref[h, :HEAD_DIM, :] / acc_ref[h, HEAD_DIM:HEAD_DIM + 1, :]
        ob_ref[0, sl, :] = out.astype(BF16)


def _prompt_attn(qa, ka, vat, bias_kq, qb, kb, vbt, ccol):
    bsz, s, _ = qa.shape
    qblk = pl.BlockSpec((1, ATT_BLK, WIDTH), lambda b, i: (b, i, 0))
    rows = pl.BlockSpec((1, s, WIDTH), lambda b, i: (b, 0, 0))
    chan = pl.BlockSpec((1, WIDTH, s), lambda b, i: (b, 0, 0))
    oblk = pl.BlockSpec((1, WIDTH, ATT_BLK), lambda b, i: (b, 0, i))
    return pl.pallas_call(
        _prompt_attn_kernel,
        grid=(bsz, s // ATT_BLK),
        in_specs=[qblk, rows, chan, _resident(bias_kq.shape, lambda b, i: (0, 0, 0)),
                  qblk, rows, chan, pl.BlockSpec((1, s, HEADS), lambda b, i: (b, 0, 0))],
        out_specs=[oblk, oblk],
        out_shape=[jax.ShapeDtypeStruct((bsz, WIDTH, s), BF16)] * 2,
        scratch_shapes=[pltpu.VMEM((HEADS_AHEAD + 1, BAND_BLOCKS, ATT_BLK, ATT_BLK), F32),
                        pltpu.VMEM((s, HEADS * PAD_DIM), BF16),
                        pltpu.VMEM((ATT_BLK, HEADS * PAD_DIM), BF16),
                        pltpu.VMEM((2, HEADS, ATT_BLK, ATT_BLK), F32),
                        pltpu.VMEM((HEADS, 1, ATT_BLK), F32),
                        pltpu.VMEM((HEADS, HEAD_DIM + SUM_ROWS, ATT_BLK), F32)],
        compiler_params=_params("parallel", "arbitrary"),
        name="prompt_attn",
    )(qa, ka, vat, bias_kq, qb, kb, vbt, ccol)


def _sample_attention(s_past, s_new, vct, vnt):
    m = jnp.maximum(jnp.max(s_past, axis=-1, keepdims=True), jnp.max(s_new, axis=-1, keepdims=True))
    p_past = jnp.exp2(s_past - m)
    p_new = jnp.exp2(s_new - m)
    l = jnp.sum(p_past, axis=-1, keepdims=True) + jnp.sum(p_new, axis=-1, keepdims=True)
    acc = _dot_nt(p_past.astype(BF16), vct) + _dot_nt(p_new.astype(BF16), vnt)
    return (acc / l).astype(BF16)


def _sample_attn_kernel(qa_ref, kcat_ref, vcat_ref, kna_ref, vnat_ref, bias_ref,
                        qb_ref, kcbt_ref, vcbt_ref, knb_ref, vnbt_ref, ccol_ref, crow_ref, oa_ref, ob_ref):
    t = qa_ref.shape[1]
    past_a = kcat_ref.shape[2]
    past_b = kcbt_ref.shape[2]
    row = lax.broadcasted_iota(jnp.int32, (t, t), 0)
    col = lax.broadcasted_iota(jnp.int32, (t, t), 1)
    causal = col <= row

    def scores(idx):
        h = idx // 2
        sl = slice(h * HEAD_DIM, (h + 1) * HEAD_DIM)
        if idx % 2 == 0:
            q = qa_ref[0, :, sl]
            s_past = _dot(q, kcat_ref[0, sl, :].astype(BF16)) + bias_ref[h, 0:t, BAND - past_a:BAND]
            s_new = _dot_nt(q, kna_ref[0, :, sl]) + bias_ref[h, 0:t, BAND:BAND + t]
            return s_past, s_new
        q = qb_ref[0, :, sl]
        cq = ccol_ref[0, :, h:h + 1]
        s_past = _dot(q, kcbt_ref[0, sl, :].astype(BF16)) + (cq - crow_ref[0, h:h + 1, 0:past_b]) * LOG2E
        s_new = _dot_nt(q, knb_ref[0, :, sl]) + (cq - crow_ref[0, h:h + 1, past_b:past_b + t]) * LOG2E
        return s_past, jnp.where(causal, s_new, NEG)

    def finish(idx, s):
        h = idx // 2
        sl = slice(h * HEAD_DIM, (h + 1) * HEAD_DIM)
        if idx % 2 == 0:
            oa_ref[0, :, sl] = _sample_attention(*s, vcat_ref[0, sl, :].astype(BF16), vnat_ref[0, sl, :])
        else:
            ob_ref[0, :, sl] = _sample_attention(*s, vcbt_ref[0, sl, :].astype(BF16), vnbt_ref[0, sl, :])

    pending = []
    for idx in range(2 * HEADS + SAMPLE_AHEAD):
        if idx < 2 * HEADS:
            pending.append(scores(idx))
        if idx >= SAMPLE_AHEAD:
            finish(idx - SAMPLE_AHEAD, pending.pop(0))


def _sample_attn(qa, kcat, vcat, kna, vnat, bias, qb, kcbt, vcbt, knb, vnbt, ccol, crow):
    bsz, t, _ = qa.shape
    new = pl.BlockSpec((1, t, WIDTH), lambda b: (b, 0, 0))
    new_t = pl.BlockSpec((1, WIDTH, t), lambda b: (b, 0, 0))
    old = lambda a: pl.BlockSpec((1, WIDTH, a.shape[2]), lambda b: (b, 0, 0))
    return pl.pallas_call(
        _sample_attn_kernel,
        grid=(bsz,),
        in_specs=[new, old(kcat), old(vcat), new, new_t, _resident(bias.shape, lambda b: (0, 0, 0)),
                  new, old(kcbt), old(vcbt), new, new_t,
                  pl.BlockSpec((1, t, HEADS), lambda b: (b, 0, 0)),
                  pl.BlockSpec((1, HEADS, crow.shape[2]), lambda b: (b, 0, 0))],
        out_specs=[new, new],
        out_shape=[jax.ShapeDtypeStruct(qa.shape, BF16)] * 2,
        compiler_params=_params("parallel"),
        name="sample_attn",
    )(qa, kcat, vcat, kna, vnat, bias, qb, kcbt, vcbt, knb, vnbt, ccol, crow)


def _postmix_kernel(x_ref, oa_ref, ob_ref, gpre_ref, gpost_ref, wt_ref, wpa_ref, wpb_ref, wout_ref, y_ref,
                    *, channel_major):
    tm, d = x_ref.shape
    planes = tm // SUBLANES
    gates = wt_ref.shape[0] - 2 * d
    halves = [slice(k * tm // 2, (k + 1) * tm // 2) for k in range(2)]

    def mix(rows):
        h = _rmsnorm(x_ref[rows, :], gpre_ref[...]).astype(BF16)
        gate_a = jax.nn.sigmoid(_dot_nt(h, wt_ref[gates:gates + d, :].astype(BF16)))
        gate_b = jax.nn.sigmoid(_dot_nt(h, wt_ref[gates + d:, :].astype(BF16)))
        if channel_major:
            proj = lambda o_ref, w_ref: lax.dot_general(o_ref[0, :, rows], w_ref[...], (((0,), (0,)), ((), ())),
                                                        preferred_element_type=F32)
        else:
            proj = lambda o_ref, w_ref: _dot(o_ref[rows, :], w_ref[...])
        return (gate_a * proj(oa_ref, wpa_ref) + gate_b * proj(ob_ref, wpb_ref)).astype(BF16)

    merged = [mix(rows) for rows in halves]
    outs = [_dot(m, wout_ref[...]) for m in merged]
    for k, rows in enumerate(halves):
        y = x_ref[rows, :] + _rmsnorm(outs[k], gpost_ref[...])
        subs = slice(k * SUBLANES // 2, (k + 1) * SUBLANES // 2)
        y_ref[:, subs, :] = jnp.swapaxes(y.reshape(SUBLANES // 2, planes, d), 0, 1)


def _deinterleave(planes):
    p, s, d = planes.shape
    return jnp.swapaxes(planes, 0, 1).reshape(s * p, d)


def _postmix(x, oa, ob, g_pre, g_post, w_in_t, w_pa, w_pb, w_out, seq_len):
    n, d = x.shape
    tm = _mix_tile(seq_len)
    row = lambda i: (i, 0)
    fixed = lambda i: (0, 0)
    channel_major = oa.ndim == 3
    if channel_major:
        tiles_per_seq = oa.shape[2] // tm
        o_spec = pl.BlockSpec((1, WIDTH, tm), lambda i: (i // tiles_per_seq, 0, i % tiles_per_seq))
    else:
        o_spec = pl.BlockSpec((tm, WIDTH), row)
    return pl.pallas_call(
        functools.partial(_postmix_kernel, channel_major=channel_major),
        grid=(n // tm,),
        in_specs=[pl.BlockSpec((tm, d), row),
                  o_spec,
                  o_spec,
                  pl.BlockSpec((1, d), fixed),
                  pl.BlockSpec((1, d), fixed),
                  _resident(w_in_t.shape, fixed),
                  _resident(w_pa.shape, fixed),
                  _resident(w_pb.shape, fixed),
                  _resident(w_out.shape, fixed)],
        out_specs=pl.BlockSpec((tm // SUBLANES, SUBLANES, d), lambda i: (i, 0, 0)),
        out_shape=jax.ShapeDtypeStruct((n // SUBLANES, SUBLANES, d), F32),
        compiler_params=_params("parallel"),
        name="postmix",
    )(x, oa, ob, g_pre, g_post, w_in_t, w_pa, w_pb, w_out)


def _ffn_kernel(x_ref, st_ref, gpre_ref, gpost_ref, wup_ref, cw_ref, cb_ref, wdn_ref,
                y_ref, nst_ref, hist_ref, ext_ref, h_ref, f_ref, *, nseg):
    @pl.when(pl.program_id(1) == 0)
    def _():
        hist_ref[...] = st_ref[...]

    planes, _, d = x_ref.shape
    tm = planes * SUBLANES
    d_ff = wdn_ref.shape[0]
    h_ref[...] = _rmsnorm(x_ref[...].reshape(tm, d), gpre_ref[...]).astype(BF16)
    n_chunks = d_ff // FFN_COLS
    first_sublane = lax.broadcasted_iota(jnp.int32, (SUBLANES, FFN_COLS), 0) == 0

    def parts(c):
        for part in range(2):
            yield (slice(part * d_ff + c * FFN_COLS, part * d_ff + (c + 1) * FFN_COLS),
                   slice(part * FFN_COLS, (part + 1) * FFN_COLS))

    def up(c):
        ext = ext_ref.at[c % 2]
        for cols, dst in parts(c):
            u = _dot(h_ref[...], wup_ref[:, cols]).reshape(planes, SUBLANES, FFN_COLS)
            ext[CONV_W - 1:, :, dst] = u
            for k in range(CONV_W - 1):
                last = u[planes - (CONV_W - 1) + k]
                if nseg == 1:
                    ext[k, :, dst] = jnp.where(first_sublane, hist_ref[0, k:k + 1, cols], pltpu.roll(last, 1, 0))
                    hist_ref[0, k:k + 1, cols] = last[SUBLANES - 1:, :]
                else:
                    ext[k, :, dst] = hist_ref[:, k, cols]
                    hist_ref[:, k, cols] = last

    def activate(c):
        ext = ext_ref.at[c % 2]
        halves = []
        for cols, dst in parts(c):
            y = cb_ref[:, cols]
            for tap in range(CONV_W):
                y = y + ext[tap:tap + planes, :, dst] * cw_ref[tap:tap + 1, cols]
            halves.append(y)
        act = (jax.nn.gelu(halves[0]) * halves[1]).reshape(tm, FFN_COLS)
        f_ref[:, c * FFN_COLS:(c + 1) * FFN_COLS] = act.astype(BF16)

    up(0)
    for c in range(n_chunks):
        if c + 1 < n_chunks:
            up(c + 1)
        activate(c)
    f = _dot(f_ref[...], wdn_ref[...])
    y = x_ref[...].reshape(tm, d) + _rmsnorm(f, gpost_ref[...])
    y_ref[...] = _deinterleave(y.reshape(planes, SUBLANES, d))
    nst_ref[...] = hist_ref[...]


def _ffn(x, bsz, s, state, g_pre, g_post, w_up, conv_w, conv_b, w_down):
    d = x.shape[-1]
    up = w_up.shape[1]
    tm = _mix_tile(s)
    planes = tm // SUBLANES
    if s >= tm:
        nseg, tiles = 1, s // tm
    else:
        nseg, tiles = tm // s, 1
        assert nseg == SUBLANES
    outer = bsz // nseg
    row = lambda o, t: (o * tiles + t, 0)
    fixed = lambda o, t: (0, 0)
    st_spec = pl.BlockSpec((nseg, CONV_W - 1, up), lambda o, t: (o, 0, 0))
    y, new_state = pl.pallas_call(
        functools.partial(_ffn_kernel, nseg=nseg),
        grid=(outer, tiles),
        in_specs=[pl.BlockSpec((planes, SUBLANES, d), lambda o, t: (o * tiles + t, 0, 0)),
                  st_spec,
                  pl.BlockSpec((1, d), fixed),
                  pl.BlockSpec((1, d), fixed),
                  _resident(w_up.shape, fixed),
                  pl.BlockSpec(conv_w.shape, fixed),
                  pl.BlockSpec((1, up), fixed),
                  _resident(w_down.shape, fixed)],
        out_specs=[pl.BlockSpec((tm, d), row), st_spec],
        out_shape=[jax.ShapeDtypeStruct((bsz * s, d), F32),
                   jax.ShapeDtypeStruct(state.shape, F32)],
        scratch_shapes=[pltpu.VMEM((nseg, CONV_W - 1, up), F32),
                        pltpu.VMEM((2, CONV_W - 1 + planes, SUBLANES, 2 * FFN_COLS), F32),
                        pltpu.VMEM((tm, d), BF16),
                        pltpu.VMEM((tm, w_down.shape[0]), BF16)],
        compiler_params=_params("arbitrary", "arbitrary"),
        name="conv_ffn",
    )(x, state, g_pre, g_post, w_up, conv_w, conv_b, w_down)
    return y.reshape(bsz, s, d), new_state


def _layer(x, caches, conv_state, bias, w):
    bsz, s, d = x.shape
    n = bsz * s
    x2 = x.reshape(n, d)
    keep = min(BAND, s)
    (qa, qb, ka16, kb16, vat16, vbt16, kat, vat, kbt, vbt, lft) = _inproj(
        x2, w["g_pre_mix"], w["w_in_t"], w["b_f"], seq_len=s, band_keep=keep)
    seq = lambda a: a.reshape(bsz, s, a.shape[-1])
    if caches is None:
        ccol, _ = _cumsum(lft)
        oa, ob = _prompt_attn(seq(qa), seq(ka16), vat16, bias[1], seq(qb), seq(kb16), vbt16, ccol)
    else:
        ckat, cvat, ckbt, cvbt, clft = caches
        ccol, crow = _cumsum(clft, lft, col_from=clft.shape[2])
        oa, ob = _sample_attn(seq(qa), ckat, cvat, seq(ka16), vat16, bias[0],
                              seq(qb), ckbt, cvbt, seq(kb16), vbt16, ccol, crow)
        oa, ob = oa.reshape(n, WIDTH), ob.reshape(n, WIDTH)
    x1 = _postmix(x2, oa, ob, w["g_pre_mix"], w["g_post_mix"], w["w_in_t"], w["w_pa"], w["w_pb"], w["w_out"],
                  seq_len=s)
    y, new_conv = _ffn(x1, bsz, s, conv_state, w["g_pre_ffn"], w["g_post_ffn"],
                       w["w_up"], w["conv_w"], w["conv_b"], w["w_down"])
    heads = lambda a: a.reshape(bsz, HEADS, HEAD_DIM, a.shape[-1]).transpose(0, 3, 1, 2)
    return y, (heads(kat), heads(vat), heads(kbt), heads(vbt), lft.transpose(0, 2, 1), new_conv)


def _channel_major(cache):
    bsz, past = cache.shape[:2]
    return cache.transpose(0, 2, 3, 1).reshape(bsz, WIDTH, past)


def kernel(x_prompt, x_sample, cache_k_a, cache_v_a, cache_k_b, cache_v_b, cache_logf_b, state_conv_ffn,
           g_pre_mix, g_post_mix, g_pre_ffn, g_post_ffn, w_in, b_f, rel_table, w_proj_a, w_proj_b, w_out,
           w_up, conv_w, conv_b, w_down):
    depth = w_in.shape[0]
    up = w_up.shape[-1]
    x_p, x_s = x_prompt, x_sample
    p_states, s_states = [], []
    for l in range(depth):
        w = {
            "g_pre_mix": g_pre_mix[l][None], "g_post_mix": g_post_mix[l][None],
            "g_pre_ffn": g_pre_ffn[l][None], "g_post_ffn": g_post_ffn[l][None],
            "w_in_t": w_in[l].T,
            "b_f": b_f[l][:, None],
            "w_pa": w_proj_a[l].astype(BF16), "w_pb": w_proj_b[l].astype(BF16),
            "w_out": w_out[l].astype(BF16), "w_up": w_up[l].astype(BF16),
            "conv_w": conv_w[l], "conv_b": conv_b[l][None], "w_down": w_down[l].astype(BF16),
        }
        bias = _band_bias(rel_table[l])
        zero_state = jnp.zeros((x_p.shape[0], CONV_W - 1, up), F32)
        x_p, (ka, va, kb, vb, lf, cv) = _layer(x_p, None, zero_state, bias, w)
        p_states.append((ka, va, kb, vb, lf, cv))
        caches = (_channel_major(cache_k_a[l]), _channel_major(cache_v_a[l]), _channel_major(cache_k_b[l]),
                  _channel_major(cache_v_b[l]), cache_logf_b[l].transpose(0, 2, 1))
        x_s, st = _layer(x_s, caches, state_conv_ffn[l], bias, w)
        s_states.append(st)
    stack = lambda states: [jnp.stack(s) for s in zip(*states)]
    return (x_p, x_s, *stack(p_states), *stack(s_states))
```

```python
import functools

import jax
import jax.numpy as jnp
from jax import lax
from jax.experimental import pallas as pl
from jax.experimental.pallas import tpu as pltpu

F32, BF16 = jnp.float32, jnp.bfloat16

HEADS = 8
HEAD_DIM = 64
WIDTH = HEADS * HEAD_DIM
CHUNK = 64
PAST_CHUNKS = 8
BAND = PAST_CHUNKS * CHUNK
REL_CLIP = 128
REL_SIZE = 2 * REL_CLIP + 1
CONV_W = 3
EPS = 1e-6
NEG = -1e30
LOG2E = 1.4426950408889634

SUBLANES = 8
ATT_BLK = 256
BAND_KEYS = 3 * ATT_BLK
BAND_BLOCKS = BAND_KEYS // ATT_BLK
PAD_DIM = 2 * HEAD_DIM
SUM_ROWS = 16
HEADS_AHEAD = 2
SAMPLE_AHEAD = 3
ROW_TILE = 512
MIX_TILE = 1024
FFN_COLS = 256
CUMSUM_BLK = 256
VMEM_LIMIT = 56 * 1024 * 1024


def _params(*sem):
    return pltpu.CompilerParams(dimension_semantics=sem, vmem_limit_bytes=VMEM_LIMIT)


def _resident(shape, index_map):
    return pl.BlockSpec(shape, index_map, pipeline_mode=pl.Buffered(1))


def _mix_tile(seq_len):
    return MIX_TILE if seq_len >= MIX_TILE else SUBLANES * seq_len


def _rmsnorm(x, g):
    return x * lax.rsqrt(jnp.mean(x * x, axis=-1, keepdims=True) + EPS) * g


def _dot(a, b):
    return jnp.dot(a, b, preferred_element_type=F32)


def _dot_nt(a, b):
    return lax.dot_general(a, b, (((1,), (1,)), ((), ())), preferred_element_type=F32)


def _split3(x):
    hi = x.astype(BF16)
    r = x - hi.astype(F32)
    mid = r.astype(BF16)
    lo = (r - mid.astype(F32)).astype(BF16)
    return hi, mid, lo


def _inproj_kernel(x_ref, g_ref, w_ref, wft_ref, bf_ref,
                   qa_ref, qb_ref, ka_ref, kb_ref, vat16_ref, vbt16_ref,
                   kat_ref, vat_ref, kbt_ref, vbt_ref, lft_ref):
    h = _rmsnorm(x_ref[...], g_ref[...]).astype(BF16)
    scale = HEAD_DIM ** -0.5

    def proj(c):
        return _dot_nt(h, w_ref[c * WIDTH:(c + 1) * WIDTH, :].astype(BF16))

    def put(ref, zt):
        per_seq = zt.shape[1] // ref.shape[0]
        for sq in range(ref.shape[0]):
            ref[sq] = zt[:, sq * per_seq:(sq + 1) * per_seq]

    qa_ref[...] = (proj(0) * (scale * LOG2E)).astype(BF16)
    qb_ref[...] = (proj(3) * (scale * LOG2E)).astype(BF16)
    for c, row_ref, t_ref in ((1, ka_ref, kat_ref), (4, kb_ref, kbt_ref)):
        z = proj(c)
        row_ref[...] = z.astype(BF16)
        put(t_ref, z.T)
    for c, t16_ref, t_ref in ((2, vat16_ref, vat_ref), (5, vbt16_ref, vbt_ref)):
        zt = proj(c).T
        put(t_ref, zt)
        put(t16_ref, zt.astype(BF16))
    put(lft_ref, jax.nn.log_sigmoid(_dot_nt(wft_ref[...].astype(BF16), h) + bf_ref[...]))


def _inproj(x, g, w_in_t, b_f, seq_len, band_keep):
    n, d = x.shape
    bsz = n // seq_len
    tm = min(ROW_TILE, n)
    tiles_per_seq = max(seq_len // tm, 1)
    seqs_per_tile = max(tm // seq_len, 1)
    cols = tm // seqs_per_tile
    assert band_keep == cols
    row = lambda i: (i, 0)
    fixed = lambda i: (0, 0)
    along = lambda i: (i // tiles_per_seq, 0, i % tiles_per_seq)
    kept = lambda i: (i // tiles_per_seq, 0, 0)
    rows16 = (jax.ShapeDtypeStruct((n, WIDTH), BF16), pl.BlockSpec((tm, WIDTH), row))

    def chan(channels, length, dtype, index_map):
        return (jax.ShapeDtypeStruct((bsz, channels, length), dtype),
                pl.BlockSpec((seqs_per_tile, channels, cols), index_map))

    outs = [rows16] * 4
    outs += [chan(WIDTH, seq_len, BF16, along)] * 2
    outs += [chan(WIDTH, band_keep, F32, kept)] * 2
    outs += [chan(WIDTH, seq_len, F32, along)] * 2
    outs += [chan(HEADS, seq_len, F32, along)]
    return pl.pallas_call(
        _inproj_kernel,
        grid=(n // tm,),
        in_specs=[pl.BlockSpec((tm, d), row),
                  pl.BlockSpec((1, d), fixed),
                  _resident((6 * WIDTH, d), fixed),
                  pl.BlockSpec((HEADS, d), lambda i: (6 * WIDTH // HEADS, 0)),
                  pl.BlockSpec((HEADS, 1), fixed)],
        out_specs=[spec for _, spec in outs],
        out_shape=[shape for shape, _ in outs],
        compiler_params=_params("arbitrary"),
        name="inproj",
    )(x, g, w_in_t, w_in_t, b_f)


def _cumsum_kernel(*refs, seg_lens, col_from):
    seg_refs = refs[:len(seg_lens)]
    ccol_ref, crow_ref = refs[len(seg_lens):]
    carry = jnp.zeros((crow_ref.shape[0], 1), F32)
    off = 0
    for ref, n in zip(seg_refs, seg_lens):
        for o in range(0, n, CUMSUM_BLK):
            b = min(CUMSUM_BLK, n - o)
            r = lax.broadcasted_iota(jnp.int32, (b, b), 0)
            c = lax.broadcasted_iota(jnp.int32, (b, b), 1)
            upper = jnp.where(r <= c, 1.0, 0.0).astype(BF16)
            sums = carry
            for p in _split3(ref[:, o:o + b]):
                sums = sums + _dot(p, upper)
            crow_ref[:, off + o:off + o + b] = sums
            carry = sums[:, b - 1:b]
            if off + o >= col_from:
                by_position = sums.T
                for bb in range(ccol_ref.shape[0]):
                    ccol_ref[bb, off + o - col_from:off + o - col_from + b, :] = (
                        by_position[:, bb * HEADS:(bb + 1) * HEADS])
        off += n


def _cumsum(*segs, col_from=0):
    bsz = segs[0].shape[0]
    seg_lens = tuple(s.shape[2] for s in segs)
    total = sum(seg_lens)
    assert col_from % CUMSUM_BLK == 0 or col_from in (0, seg_lens[0])
    ccol, crow = pl.pallas_call(
        functools.partial(_cumsum_kernel, seg_lens=seg_lens, col_from=col_from),
        out_shape=[jax.ShapeDtypeStruct((bsz, total - col_from, HEADS), F32),
                   jax.ShapeDtypeStruct((bsz * HEADS, total), F32)],
        compiler_params=pltpu.CompilerParams(vmem_limit_bytes=VMEM_LIMIT),
        name="cumsum_logf",
    )(*[s.reshape(bsz * HEADS, s.shape[2]) for s in segs])
    return ccol, crow.reshape(bsz, HEADS, total)


REL_PAD = 384
DIST_SPAN = 1024


def _band_bias_kernel(tbl_ref, qk_ref, kq_ref):
    j = lax.broadcasted_iota(jnp.int32, (REL_PAD, DIST_SPAN), 1)
    r = lax.broadcasted_iota(jnp.int32, (REL_PAD, DIST_SPAN), 0)
    parts = _split3(tbl_ref[...])

    def by_offset(dist):
        onehot = jnp.where(jnp.clip(dist, -REL_CLIP, REL_CLIP) + REL_CLIP == r, 1.0, 0.0).astype(BF16)
        return sum(_dot(p, onehot) for p in parts) * LOG2E

    def skewed(e_row, rows, cols, shift):
        wide = jnp.broadcast_to(e_row, (rows, DIST_SPAN))
        return pltpu.roll(wide, shift % DIST_SPAN, 1, stride=1, stride_axis=0)[:, :cols]

    e_qk = by_offset((BAND_KEYS - 1) - j)
    e_kq = by_offset(j - (ATT_BLK - 1))
    qc = lax.broadcasted_iota(jnp.int32, (ATT_BLK, BAND_KEYS), 0) // CHUNK
    kc = lax.broadcasted_iota(jnp.int32, (ATT_BLK, BAND_KEYS), 1) // CHUNK
    vis_qk = (kc >= qc) & (kc <= qc + PAST_CHUNKS)
    kc = lax.broadcasted_iota(jnp.int32, (BAND_KEYS, ATT_BLK), 0) // CHUNK
    qc = lax.broadcasted_iota(jnp.int32, (BAND_KEYS, ATT_BLK), 1) // CHUNK
    vis_kq = (kc >= qc) & (kc <= qc + PAST_CHUNKS)
    for h in range(HEADS):
        qk_ref[h] = jnp.where(vis_qk, skewed(e_qk[h:h + 1, :], ATT_BLK, BAND_KEYS, -(ATT_BLK - 1)), NEG)
        kq_ref[h] = jnp.where(vis_kq, skewed(e_kq[h:h + 1, :], BAND_KEYS, ATT_BLK, -(BAND_KEYS - 1)), NEG)


def _band_bias(rel_table):
    tbl = jnp.pad(rel_table, ((0, 0), (0, REL_PAD - REL_SIZE)))
    return pl.pallas_call(
        _band_bias_kernel,
        out_shape=[jax.ShapeDtypeStruct((HEADS, ATT_BLK, BAND_KEYS), F32),
                   jax.ShapeDtypeStruct((HEADS, BAND_KEYS, ATT_BLK), F32)],
        compiler_params=pltpu.CompilerParams(vmem_limit_bytes=VMEM_LIMIT),
        name="band_bias",
    )(tbl)


def _band_attn_prompt_kernel(q_ref, k_ref, vt_ref, bias_ref, o_ref, s_ref):
    i = pl.program_id(1)
    ones = jnp.ones((SUM_ROWS, ATT_BLK), BF16)

    def run(clipped):
        blocks = []
        for g in range(BAND_BLOCKS):
            j = i - (BAND_BLOCKS - 1) + g
            start = pl.multiple_of(jnp.maximum(j, 0) * ATT_BLK, ATT_BLK)
            masks = clipped and g < BAND_BLOCKS - 1
            blocks.append((start, jnp.where(j >= 0, 0.0, NEG) if masks else None))

        def scores(h):
            sl = slice(h * HEAD_DIM, (h + 1) * HEAD_DIM)
            q = q_ref[0, :, sl]
            for g, (start, penalty) in enumerate(blocks):
                s = _dot_nt(k_ref[0, pl.ds(start, ATT_BLK), sl], q) + bias_ref[h, g * ATT_BLK:(g + 1) * ATT_BLK, :]
                s_ref[h % (HEADS_AHEAD + 1), g] = s if penalty is None else s + penalty

        def finish(h):
            sl = slice(h * HEAD_DIM, (h + 1) * HEAD_DIM)
            s = [s_ref[h % (HEADS_AHEAD + 1), g] for g in range(BAND_BLOCKS)]
            m = functools.reduce(jnp.maximum, [jnp.max(x, axis=0, keepdims=True) for x in s])
            acc = 0.0
            for x, (start, _) in zip(s, blocks):
                vt = jnp.concatenate([vt_ref[0, sl, pl.ds(start, ATT_BLK)], ones], axis=0)
                acc = acc + _dot(vt, jnp.exp2(x - m).astype(BF16))
            o_ref[0, sl, :] = (acc[:HEAD_DIM] / acc[HEAD_DIM:HEAD_DIM + 1]).astype(BF16)

        for h in range(HEADS + HEADS_AHEAD):
            if h < HEADS:
                scores(h)
            if h >= HEADS_AHEAD:
                finish(h - HEADS_AHEAD)

    pl.when(i >= BAND_BLOCKS - 1)(lambda: run(clipped=False))
    pl.when(i < BAND_BLOCKS - 1)(lambda: run(clipped=True))


def _band_attn_prompt(q, k, vt, bias_kq):
    bsz, s, _ = q.shape
    qblk = pl.BlockSpec((1, ATT_BLK, WIDTH), lambda b, i: (b, i, 0))
    return pl.pallas_call(
        _band_attn_prompt_kernel,
        grid=(bsz, s // ATT_BLK),
        in_specs=[qblk,
                  pl.BlockSpec((1, s, WIDTH), lambda b, i: (b, 0, 0)),
                  pl.BlockSpec((1, WIDTH, s), lambda b, i: (b, 0, 0)),
                  _resident(bias_kq.shape, lambda b, i: (0, 0, 0))],
        out_specs=pl.BlockSpec((1, WIDTH, ATT_BLK), lambda b, i: (b, 0, i)),
        out_shape=jax.ShapeDtypeStruct((bsz, WIDTH, s), BF16),
        scratch_shapes=[pltpu.VMEM((HEADS_AHEAD + 1, BAND_BLOCKS, ATT_BLK, ATT_BLK), F32)],
        compiler_params=_params("parallel", "parallel"),
        name="band_attn_prompt",
    )(q, k, vt, bias_kq)


def _decay_lanes(c, key_side):
    terms = jnp.concatenate(_split3(c), axis=1)
    row = lax.broadcasted_iota(jnp.int32, (3 * HEADS, WIDTH), 0)
    lane = lax.broadcasted_iota(jnp.int32, (3 * HEADS, WIDTH), 1)
    first_one, first_term, sign = (3, 0, -1.0) if key_side else (0, 3, 1.0)
    place = jnp.where(lane == (row % HEADS) * HEAD_DIM + first_term + row // HEADS, sign, 0.0).astype(BF16)
    in_group = lax.broadcasted_iota(jnp.int32, (1, WIDTH), 1) % HEAD_DIM
    ones = jnp.where((in_group >= first_one) & (in_group < first_one + 3), 1.0, 0.0)
    return (ones + _dot(terms, place)).astype(BF16)


def _fox_attn_prompt_kernel(q_ref, k_ref, vt_ref, ccol_ref, o_ref,
                            kp_ref, qp_ref, s_ref, m_ref, acc_ref):
    i = pl.program_id(1)
    qstart = pl.multiple_of(i * ATT_BLK, ATT_BLK)

    @pl.when(i == 0)
    def _():
        extra = _decay_lanes(ccol_ref[0] * LOG2E, key_side=True)
        for h in range(HEADS):
            sl = slice(h * HEAD_DIM, (h + 1) * HEAD_DIM)
            kp_ref[:, h * PAD_DIM:(h + 1) * PAD_DIM] = jnp.concatenate([k_ref[0, :, sl], extra[:, sl]], axis=1)

    extra = _decay_lanes(ccol_ref[0, pl.ds(qstart, ATT_BLK), :] * LOG2E, key_side=False)
    for h in range(HEADS):
        sl = slice(h * HEAD_DIM, (h + 1) * HEAD_DIM)
        qp_ref[:, h * PAD_DIM:(h + 1) * PAD_DIM] = jnp.concatenate([q_ref[0, :, sl], extra[:, sl]], axis=1)

    key = lax.broadcasted_iota(jnp.int32, (ATT_BLK, ATT_BLK), 0)
    qry = lax.broadcasted_iota(jnp.int32, (ATT_BLK, ATT_BLK), 1)
    causal = key <= qry
    ones = jnp.ones((SUM_ROWS, ATT_BLK), BF16)
    m_ref[...] = jnp.full(m_ref.shape, NEG, F32)
    acc_ref[...] = jnp.zeros(acc_ref.shape, F32)

    def scores(j, h):
        start = pl.multiple_of(j * ATT_BLK, ATT_BLK)
        pad = slice(h * PAD_DIM, (h + 1) * PAD_DIM)
        return _dot_nt(kp_ref[pl.ds(start, ATT_BLK), pad], qp_ref[:, pad])

    def update(j, h, s, masked):
        start = pl.multiple_of(j * ATT_BLK, ATT_BLK)
        if masked:
            s = jnp.where(causal, s, NEG)
        m_old = m_ref[h]
        m_new = jnp.maximum(m_old, jnp.max(s, axis=0, keepdims=True))
        alpha = jnp.exp2(m_old - m_new)
        p = jnp.exp2(s - m_new)
        vt = vt_ref[0, h * HEAD_DIM:(h + 1) * HEAD_DIM, pl.ds(start, ATT_BLK)]
        acc_ref[h] = alpha * acc_ref[h] + _dot(jnp.concatenate([vt, ones], axis=0), p.astype(BF16))
        m_ref[h] = m_new

    def step(j, cur, masked=False, prefetch=True):
        for h in range(HEADS + HEADS_AHEAD):
            if prefetch and h < HEADS:
                s_ref[1 - cur, h] = scores(j + 1, h)
            if h >= HEADS_AHEAD:
                update(j, h - HEADS_AHEAD, s_ref[cur, h - HEADS_AHEAD], masked)

    for h in range(HEADS):
        s_ref[0, h] = scores(0, h)

    def pair(t, carry):
        step(2 * t, 0)
        step(2 * t + 1, 1)
        return carry

    lax.fori_loop(0, i // 2, pair, 0)

    @pl.when(i % 2 == 1)
    def _():
        step(i - 1, 0)
        step(i, 1, masked=True, prefetch=False)

    @pl.when(i % 2 == 0)
    def _():
        step(i, 0, masked=True, prefetch=False)

    for h in range(HEADS):
        sl = slice(h * HEAD_DIM, (h + 1) * HEAD_DIM)
        out = acc_ref[h, :HEAD_DIM, :] / acc_ref[h, HEAD_DIM:HEAD_DIM + 1, :]
        o_ref[0, sl, :] = out.astype(BF16)


def _fox_attn_prompt(q, k, vt, ccol):
    bsz, s, _ = q.shape
    qblk = pl.BlockSpec((1, ATT_BLK, WIDTH), lambda b, i: (b, i, 0))
    return pl.pallas_call(
        _fox_attn_prompt_kernel,
        grid=(bsz, s // ATT_BLK),
        in_specs=[qblk,
                  pl.BlockSpec((1, s, WIDTH), lambda b, i: (b, 0, 0)),
                  pl.BlockSpec((1, WIDTH, s), lambda b, i: (b, 0, 0)),
                  pl.BlockSpec((1, s, HEADS), lambda b, i: (b, 0, 0))],
        out_specs=pl.BlockSpec((1, WIDTH, ATT_BLK), lambda b, i: (b, 0, i)),
        out_shape=jax.ShapeDtypeStruct((bsz, WIDTH, s), BF16),
        scratch_shapes=[pltpu.VMEM((s, HEADS * PAD_DIM), BF16),
                        pltpu.VMEM((ATT_BLK, HEADS * PAD_DIM), BF16),
                        pltpu.VMEM((2, HEADS, ATT_BLK, ATT_BLK), F32),
                        pltpu.VMEM((HEADS, 1, ATT_BLK), F32),
                        pltpu.VMEM((HEADS, HEAD_DIM + SUM_ROWS, ATT_BLK), F32)],
        compiler_params=_params("parallel", "arbitrary"),
        name="fox_attn_prompt",
    )(q, k, vt, ccol)


def _prompt_attn_kernel(qa_ref, ka_ref, vat_ref, bias_ref, qb_ref, kb_ref, vbt_ref, ccol_ref, oa_ref, ob_ref,
                        sa_ref, kp_ref, qp_ref, sb_ref, m_ref, acc_ref):
    i = pl.program_id(1)
    qstart = pl.multiple_of(i * ATT_BLK, ATT_BLK)
    ones = jnp.ones((SUM_ROWS, ATT_BLK), BF16)
    key = lax.broadcasted_iota(jnp.int32, (ATT_BLK, ATT_BLK), 0)
    qry = lax.broadcasted_iota(jnp.int32, (ATT_BLK, ATT_BLK), 1)
    causal = key <= qry

    @pl.when(i == 0)
    def _():
        extra = _decay_lanes(ccol_ref[0] * LOG2E, key_side=True)
        for h in range(HEADS):
            sl = slice(h * HEAD_DIM, (h + 1) * HEAD_DIM)
            kp_ref[:, h * PAD_DIM:(h + 1) * PAD_DIM] = jnp.concatenate([kb_ref[0, :, sl], extra[:, sl]], axis=1)

    def scores(j, h):
        start = pl.multiple_of(j * ATT_BLK, ATT_BLK)
        pad = slice(h * PAD_DIM, (h + 1) * PAD_DIM)
        return _dot_nt(kp_ref[pl.ds(start, ATT_BLK), pad], qp_ref[:, pad])

    def update(j, h, s, masked):
        start = pl.multiple_of(j * ATT_BLK, ATT_BLK)
        if masked:
            s = jnp.where(causal, s, NEG)
        m_old = m_ref[h]
        m_new = jnp.maximum(m_old, jnp.max(s, axis=0, keepdims=True))
        alpha = jnp.exp2(m_old - m_new)
        p = jnp.exp2(s - m_new)
        vt = vbt_ref[0, h * HEAD_DIM:(h + 1) * HEAD_DIM, pl.ds(start, ATT_BLK)]
        acc_ref[h] = alpha * acc_ref[h] + _dot(jnp.concatenate([vt, ones], axis=0), p.astype(BF16))
        m_ref[h] = m_new

    def fox_setup():
        extra = _decay_lanes(ccol_ref[0, pl.ds(qstart, ATT_BLK), :] * LOG2E, key_side=False)
        for h in range(HEADS):
            sl = slice(h * HEAD_DIM, (h + 1) * HEAD_DIM)
            qp_ref[:, h * PAD_DIM:(h + 1) * PAD_DIM] = jnp.concatenate([qb_ref[0, :, sl], extra[:, sl]], axis=1)
        m_ref[...] = jnp.full(m_ref.shape, NEG, F32)
        acc_ref[...] = jnp.zeros(acc_ref.shape, F32)
        for h in range(HEADS):
            sb_ref[0, h] = scores(0, h)

    def band(clipped):
        blocks = []
        for g in range(BAND_BLOCKS):
            j = i - (BAND_BLOCKS - 1) + g
            start = pl.multiple_of(jnp.maximum(j, 0) * ATT_BLK, ATT_BLK)
            masks = clipped and g < BAND_BLOCKS - 1
            blocks.append((start, jnp.where(j >= 0, 0.0, NEG) if masks else None))

        def band_scores(h):
            sl = slice(h * HEAD_DIM, (h + 1) * HEAD_DIM)
            q = qa_ref[0, :, sl]
            for g, (start, penalty) in enumerate(blocks):
                s = _dot_nt(ka_ref[0, pl.ds(start, ATT_BLK), sl], q) + bias_ref[h, g * ATT_BLK:(g + 1) * ATT_BLK, :]
                sa_ref[h % (HEADS_AHEAD + 1), g] = s if penalty is None else s + penalty

        def band_finish(h):
            sl = slice(h * HEAD_DIM, (h + 1) * HEAD_DIM)
            s = [sa_ref[h % (HEADS_AHEAD + 1), g] for g in range(BAND_BLOCKS)]
            m = functools.reduce(jnp.maximum, [jnp.max(x, axis=0, keepdims=True) for x in s])
            acc = 0.0
            for x, (start, _) in zip(s, blocks):
                vt = jnp.concatenate([vat_ref[0, sl, pl.ds(start, ATT_BLK)], ones], axis=0)
                acc = acc + _dot(vt, jnp.exp2(x - m).astype(BF16))
            oa_ref[0, sl, :] = (acc[:HEAD_DIM] / acc[HEAD_DIM:HEAD_DIM + 1]).astype(BF16)

        for h in range(HEADS + HEADS_AHEAD):
            if h < HEADS:
                band_scores(h)
            if h >= HEADS_AHEAD:
                band_finish(h - HEADS_AHEAD)

    def head_of_step(clipped):
        band(clipped)
        fox_setup()

    pl.when(i >= BAND_BLOCKS - 1)(lambda: head_of_step(clipped=False))
    pl.when(i < BAND_BLOCKS - 1)(lambda: head_of_step(clipped=True))

    def step(j, cur, masked=False, prefetch=True):
        for h in range(HEADS + HEADS_AHEAD):
            if prefetch and h < HEADS:
                sb_ref[1 - cur, h] = scores(j + 1, h)
            if h >= HEADS_AHEAD:
                update(j, h - HEADS_AHEAD, sb_ref[cur, h - HEADS_AHEAD], masked)

    def pair(t, carry):
        step(2 * t, 0)
        step(2 * t + 1, 1)
        return carry

    lax.fori_loop(0, i // 2, pair, 0)

    @pl.when(i % 2 == 1)
    def _():
        step(i - 1, 0)
        step(i, 1, masked=True, prefetch=False)

    @pl.when(i % 2 == 0)
    def _():
        step(i, 0, masked=True, prefetch=False)

    for h in range(HEADS):
        sl = slice(h * HEAD_DIM, (h + 1) * HEAD_DIM)
        out = acc_ref[h, :HEAD_DIM, :] / acc_ref[h, HEAD_DIM:HEAD_DIM + 1, :]
        ob_ref[0, sl, :] = out.astype(BF16)


def _prompt_attn(qa, ka, vat, bias_kq, qb, kb, vbt, ccol):
    bsz, s, _ = qa.shape
    qblk = pl.BlockSpec((1, ATT_BLK, WIDTH), lambda b, i: (b, i, 0))
    rows = pl.BlockSpec((1, s, WIDTH), lambda b, i: (b, 0, 0))
    chan = pl.BlockSpec((1, WIDTH, s), lambda b, i: (b, 0, 0))
    oblk = pl.BlockSpec((1, WIDTH, ATT_BLK), lambda b, i: (b, 0, i))
    return pl.pallas_call(
        _prompt_attn_kernel,
        grid=(bsz, s // ATT_BLK),
        in_specs=[qblk, rows, chan, _resident(bias_kq.shape, lambda b, i: (0, 0, 0)),
                  qblk, rows, chan, pl.BlockSpec((1, s, HEADS), lambda b, i: (b, 0, 0))],
        out_specs=[oblk, oblk],
        out_shape=[jax.ShapeDtypeStruct((bsz, WIDTH, s), BF16)] * 2,
        scratch_shapes=[pltpu.VMEM((HEADS_AHEAD + 1, BAND_BLOCKS, ATT_BLK, ATT_BLK), F32),
                        pltpu.VMEM((s, HEADS * PAD_DIM), BF16),
                        pltpu.VMEM((ATT_BLK, HEADS * PAD_DIM), BF16),
                        pltpu.VMEM((2, HEADS, ATT_BLK, ATT_BLK), F32),
                        pltpu.VMEM((HEADS, 1, ATT_BLK), F32),
                        pltpu.VMEM((HEADS, HEAD_DIM + SUM_ROWS, ATT_BLK), F32)],
        compiler_params=_params("parallel", "arbitrary"),
        name="prompt_attn",
    )(qa, ka, vat, bias_kq, qb, kb, vbt, ccol)


def _sample_attention(s_past, s_new, vct, vnt):
    m = jnp.maximum(jnp.max(s_past, axis=-1, keepdims=True), jnp.max(s_new, axis=-1, keepdims=True))
    p_past = jnp.exp2(s_past - m)
    p_new = jnp.exp2(s_new - m)
    l = jnp.sum(p_past, axis=-1, keepdims=True) + jnp.sum(p_new, axis=-1, keepdims=True)
    acc = _dot_nt(p_past.astype(BF16), vct) + _dot_nt(p_new.astype(BF16), vnt)
    return (acc / l).astype(BF16)


def _sample_attn_kernel(qa_ref, kcat_ref, vcat_ref, kna_ref, vnat_ref, bias_ref,
                        qb_ref, kcbt_ref, vcbt_ref, knb_ref, vnbt_ref, ccol_ref, crow_ref, oa_ref, ob_ref):
    t = qa_ref.shape[1]
    past_a = kcat_ref.shape[2]
    past_b = kcbt_ref.shape[2]
    row = lax.broadcasted_iota(jnp.int32, (t, t), 0)
    col = lax.broadcasted_iota(jnp.int32, (t, t), 1)
    causal = col <= row

    def scores(idx):
        h = idx // 2
        sl = slice(h * HEAD_DIM, (h + 1) * HEAD_DIM)
        if idx % 2 == 0:
            q = qa_ref[0, :, sl]
            s_past = _dot(q, kcat_ref[0, sl, :].astype(BF16)) + bias_ref[h, 0:t, BAND - past_a:BAND]
            s_new = _dot_nt(q, kna_ref[0, :, sl]) + bias_ref[h, 0:t, BAND:BAND + t]
            return s_past, s_new
        q = qb_ref[0, :, sl]
        cq = ccol_ref[0, :, h:h + 1]
        s_past = _dot(q, kcbt_ref[0, sl, :].astype(BF16)) + (cq - crow_ref[0, h:h + 1, 0:past_b]) * LOG2E
        s_new = _dot_nt(q, knb_ref[0, :, sl]) + (cq - crow_ref[0, h:h + 1, past_b:past_b + t]) * LOG2E
        return s_past, jnp.where(causal, s_new, NEG)

    def finish(idx, s):
        h = idx // 2
        sl = slice(h * HEAD_DIM, (h + 1) * HEAD_DIM)
        if idx % 2 == 0:
            oa_ref[0, :, sl] = _sample_attention(*s, vcat_ref[0, sl, :].astype(BF16), vnat_ref[0, sl, :])
        else:
            ob_ref[0, :, sl] = _sample_attention(*s, vcbt_ref[0, sl, :].astype(BF16), vnbt_ref[0, sl, :])

    pending = []
    for idx in range(2 * HEADS + SAMPLE_AHEAD):
        if idx < 2 * HEADS:
            pending.append(scores(idx))
        if idx >= SAMPLE_AHEAD:
            finish(idx - SAMPLE_AHEAD, pending.pop(0))


def _sample_attn(qa, kcat, vcat, kna, vnat, bias, qb, kcbt, vcbt, knb, vnbt, ccol, crow):
    bsz, t, _ = qa.shape
    new = pl.BlockSpec((1, t, WIDTH), lambda b: (b, 0, 0))
    new_t = pl.BlockSpec((1, WIDTH, t), lambda b: (b, 0, 0))
    old = lambda a: pl.BlockSpec((1, WIDTH, a.shape[2]), lambda b: (b, 0, 0))
    return pl.pallas_call(
        _sample_attn_kernel,
        grid=(bsz,),
        in_specs=[new, old(kcat), old(vcat), new, new_t, _resident(bias.shape, lambda b: (0, 0, 0)),
                  new, old(kcbt), old(vcbt), new, new_t,
                  pl.BlockSpec((1, t, HEADS), lambda b: (b, 0, 0)),
                  pl.BlockSpec((1, HEADS, crow.shape[2]), lambda b: (b, 0, 0))],
        out_specs=[new, new],
        out_shape=[jax.ShapeDtypeStruct(qa.shape, BF16)] * 2,
        compiler_params=_params("parallel"),
        name="sample_attn",
    )(qa, kcat, vcat, kna, vnat, bias, qb, kcbt, vcbt, knb, vnbt, ccol, crow)


def _postmix_kernel(x_ref, oa_ref, ob_ref, gpre_ref, gpost_ref, wt_ref, wpa_ref, wpb_ref, wout_ref, y_ref,
                    *, channel_major):
    tm, d = x_ref.shape
    planes = tm // SUBLANES
    gates = wt_ref.shape[0] - 2 * d
    halves = [slice(k * tm // 2, (k + 1) * tm // 2) for k in range(2)]

    def mix(rows):
        h = _rmsnorm(x_ref[rows, :], gpre_ref[...]).astype(BF16)
        gate_a = jax.nn.sigmoid(_dot_nt(h, wt_ref[gates:gates + d, :].astype(BF16)))
        gate_b = jax.nn.sigmoid(_dot_nt(h, wt_ref[gates + d:, :].astype(BF16)))
        if channel_major:
            proj = lambda o_ref, w_ref: lax.dot_general(o_ref[0, :, rows], w_ref[...], (((0,), (0,)), ((), ())),
                                                        preferred_element_type=F32)
        else:
            proj = lambda o_ref, w_ref: _dot(o_ref[rows, :], w_ref[...])
        return (gate_a * proj(oa_ref, wpa_ref) + gate_b * proj(ob_ref, wpb_ref)).astype(BF16)

    merged = [mix(rows) for rows in halves]
    outs = [_dot(m, wout_ref[...]) for m in merged]
    for k, rows in enumerate(halves):
        y = x_ref[rows, :] + _rmsnorm(outs[k], gpost_ref[...])
        subs = slice(k * SUBLANES // 2, (k + 1) * SUBLANES // 2)
        y_ref[:, subs, :] = jnp.swapaxes(y.reshape(SUBLANES // 2, planes, d), 0, 1)


def _deinterleave(planes):
    p, s, d = planes.shape
    return jnp.swapaxes(planes, 0, 1).reshape(s * p, d)


def _postmix(x, oa, ob, g_pre, g_post, w_in_t, w_pa, w_pb, w_out, seq_len):
    n, d = x.shape
    tm = _mix_tile(seq_len)
    row = lambda i: (i, 0)
    fixed = lambda i: (0, 0)
    channel_major = oa.ndim == 3
    if channel_major:
        tiles_per_seq = oa.shape[2] // tm
        o_spec = pl.BlockSpec((1, WIDTH, tm), lambda i: (i // tiles_per_seq, 0, i % tiles_per_seq))
    else:
        o_spec = pl.BlockSpec((tm, WIDTH), row)
    return pl.pallas_call(
        functools.partial(_postmix_kernel, channel_major=channel_major),
        grid=(n // tm,),
        in_specs=[pl.BlockSpec((tm, d), row),
                  o_spec,
                  o_spec,
                  pl.BlockSpec((1, d), fixed),
                  pl.BlockSpec((1, d), fixed),
                  _resident(w_in_t.shape, fixed),
                  _resident(w_pa.shape, fixed),
                  _resident(w_pb.shape, fixed),
                  _resident(w_out.shape, fixed)],
        out_specs=pl.BlockSpec((tm // SUBLANES, SUBLANES, d), lambda i: (i, 0, 0)),
        out_shape=jax.ShapeDtypeStruct((n // SUBLANES, SUBLANES, d), F32),
        compiler_params=_params("parallel"),
        name="postmix",
    )(x, oa, ob, g_pre, g_post, w_in_t, w_pa, w_pb, w_out)


def _ffn_kernel(x_ref, st_ref, gpre_ref, gpost_ref, wup_ref, cw_ref, cb_ref, wdn_ref,
                y_ref, nst_ref, hist_ref, ext_ref, h_ref, f_ref, *, nseg):
    @pl.when(pl.program_id(1) == 0)
    def _():
        hist_ref[...] = st_ref[...]

    planes, _, d = x_ref.shape
    tm = planes * SUBLANES
    d_ff = wdn_ref.shape[0]
    h_ref[...] = _rmsnorm(x_ref[...].reshape(tm, d), gpre_ref[...]).astype(BF16)
    n_chunks = d_ff // FFN_COLS
    first_sublane = lax.broadcasted_iota(jnp.int32, (SUBLANES, FFN_COLS), 0) == 0

    def parts(c):
        for part in range(2):
            yield (slice(part * d_ff + c * FFN_COLS, part * d_ff + (c + 1) * FFN_COLS),
                   slice(part * FFN_COLS, (part + 1) * FFN_COLS))

    def up(c):
        ext = ext_ref.at[c % 2]
        for cols, dst in parts(c):
            u = _dot(h_ref[...], wup_ref[:, cols]).reshape(planes, SUBLANES, FFN_COLS)
            ext[CONV_W - 1:, :, dst] = u
            for k in range(CONV_W - 1):
                last = u[planes - (CONV_W - 1) + k]
                if nseg == 1:
                    ext[k, :, dst] = jnp.where(first_sublane, hist_ref[0, k:k + 1, cols], pltpu.roll(last, 1, 0))
                    hist_ref[0, k:k + 1, cols] = last[SUBLANES - 1:, :]
                else:
                    ext[k, :, dst] = hist_ref[:, k, cols]
                    hist_ref[:, k, cols] = last

    def activate(c):
        ext = ext_ref.at[c % 2]
        halves = []
        for cols, dst in parts(c):
            y = cb_ref[:, cols]
            for tap in range(CONV_W):
                y = y + ext[tap:tap + planes, :, dst] * cw_ref[tap:tap + 1, cols]
            halves.append(y)
        act = (jax.nn.gelu(halves[0]) * halves[1]).reshape(tm, FFN_COLS)
        f_ref[:, c * FFN_COLS:(c + 1) * FFN_COLS] = act.astype(BF16)

    up(0)
    for c in range(n_chunks):
        if c + 1 < n_chunks:
            up(c + 1)
        activate(c)
    f = _dot(f_ref[...], wdn_ref[...])
    y = x_ref[...].reshape(tm, d) + _rmsnorm(f, gpost_ref[...])
    y_ref[...] = _deinterleave(y.reshape(planes, SUBLANES, d))
    nst_ref[...] = hist_ref[...]


def _ffn(x, bsz, s, state, g_pre, g_post, w_up, conv_w, conv_b, w_down):
    d = x.shape[-1]
    up = w_up.shape[1]
    tm = _mix_tile(s)
    planes = tm // SUBLANES
    if s >= tm:
        nseg, tiles = 1, s // tm
    else:
        nseg, tiles = tm // s, 1
        assert nseg == SUBLANES
    outer = bsz // nseg
    row = lambda o, t: (o * tiles + t, 0)
    fixed = lambda o, t: (0, 0)
    st_spec = pl.BlockSpec((nseg, CONV_W - 1, up), lambda o, t: (o, 0, 0))
    y, new_state = pl.pallas_call(
        functools.partial(_ffn_kernel, nseg=nseg),
        grid=(outer, tiles),
        in_specs=[pl.BlockSpec((planes, SUBLANES, d), lambda o, t: (o * tiles + t, 0, 0)),
                  st_spec,
                  pl.BlockSpec((1, d), fixed),
                  pl.BlockSpec((1, d), fixed),
                  _resident(w_up.shape, fixed),
                  pl.BlockSpec(conv_w.shape, fixed),
                  pl.BlockSpec((1, up), fixed),
                  _resident(w_down.shape, fixed)],
        out_specs=[pl.BlockSpec((tm, d), row), st_spec],
        out_shape=[jax.ShapeDtypeStruct((bsz * s, d), F32),
                   jax.ShapeDtypeStruct(state.shape, F32)],
        scratch_shapes=[pltpu.VMEM((nseg, CONV_W - 1, up), F32),
                        pltpu.VMEM((2, CONV_W - 1 + planes, SUBLANES, 2 * FFN_COLS), F32),
                        pltpu.VMEM((tm, d), BF16),
                        pltpu.VMEM((tm, w_down.shape[0]), BF16)],
        compiler_params=_params("arbitrary", "arbitrary"),
        name="conv_ffn",
    )(x, state, g_pre, g_post, w_up, conv_w, conv_b, w_down)
    return y.reshape(bsz, s, d), new_state


def _layer(x, caches, conv_state, bias, w):
    bsz, s, d = x.shape
    n = bsz * s
    x2 = x.reshape(n, d)
    keep = min(BAND, s)
    (qa, qb, ka16, kb16, vat16, vbt16, kat, vat, kbt, vbt, lft) = _inproj(
        x2, w["g_pre_mix"], w["w_in_t"], w["b_f"], seq_len=s, band_keep=keep)
    seq = lambda a: a.reshape(bsz, s, a.shape[-1])
    if caches is None:
        ccol, _ = _cumsum(lft)
        oa, ob = _prompt_attn(seq(qa), seq(ka16), vat16, bias[1], seq(qb), seq(kb16), vbt16, ccol)
    else:
        ckat, cvat, ckbt, cvbt, clft = caches
        ccol, crow = _cumsum(clft, lft, col_from=clft.shape[2])
        oa, ob = _sample_attn(seq(qa), ckat, cvat, seq(ka16), vat16, bias[0],
                              seq(qb), ckbt, cvbt, seq(kb16), vbt16, ccol, crow)
        oa, ob = oa.reshape(n, WIDTH), ob.reshape(n, WIDTH)
    x1 = _postmix(x2, oa, ob, w["g_pre_mix"], w["g_post_mix"], w["w_in_t"], w["w_pa"], w["w_pb"], w["w_out"],
                  seq_len=s)
    y, new_conv = _ffn(x1, bsz, s, conv_state, w["g_pre_ffn"], w["g_post_ffn"],
                       w["w_up"], w["conv_w"], w["conv_b"], w["w_down"])
    heads = lambda a: a.reshape(bsz, HEADS, HEAD_DIM, a.shape[-1]).transpose(0, 3, 1, 2)
    return y, (heads(kat), heads(vat), heads(kbt), heads(vbt), lft.transpose(0, 2, 1), new_conv)


def _channel_major(cache):
    bsz, past = cache.shape[:2]
    return cache.transpose(0, 2, 3, 1).reshape(bsz, WIDTH, past)


def kernel(x_prompt, x_sample, cache_k_a, cache_v_a, cache_k_b, cache_v_b, cache_logf_b, state_conv_ffn,
           g_pre_mix, g_post_mix, g_pre_ffn, g_post_ffn, w_in, b_f, rel_table, w_proj_a, w_proj_b, w_out,
           w_up, conv_w, conv_b, w_down):
    depth = w_in.shape[0]
    up = w_up.shape[-1]
    x_p, x_s = x_prompt, x_sample
    p_states, s_states = [], []
    for l in range(depth):
        w = {
            "g_pre_mix": g_pre_mix[l][None], "g_post_mix": g_post_mix[l][None],
            "g_pre_ffn": g_pre_ffn[l][None], "g_post_ffn": g_post_ffn[l][None],
            "w_in_t": w_in[l].T,
            "b_f": b_f[l][:, None],
            "w_pa": w_proj_a[l].astype(BF16), "w_pb": w_proj_b[l].astype(BF16),
            "w_out": w_out[l].astype(BF16), "w_up": w_up[l].astype(BF16),
            "conv_w": conv_w[l], "conv_b": conv_b[l][None], "w_down": w_down[l].astype(BF16),
        }
        bias = _band_bias(rel_table[l])
        zero_state = jnp.zeros((x_p.shape[0], CONV_W - 1, up), F32)
        x_p, (ka, va, kb, vb, lf, cv) = _layer(x_p, None, zero_state, bias, w)
        p_states.append((ka, va, kb, vb, lf, cv))
        caches = (_channel_major(cache_k_a[l]), _channel_major(cache_v_a[l]), _channel_major(cache_k_b[l]),
                  _channel_major(cache_v_b[l]), cache_logf_b[l].transpose(0, 2, 1))
        x_s, st = _layer(x_s, caches, state_conv_ffn[l], bias, w)
        s_states.append(st)
    stack = lambda states: [jnp.stack(s) for s in zip(*states)]
    return (x_p, x_s, *stack(p_states), *stack(s_states))
```
